```python
import jax, jax.numpy as jnp
from jax import lax
import numpy as np

D_MODEL = 1024
BATCH = 8
SEQ = 2048
DEPTH = 1
DEC_BATCH = 128
DEC_SEQ = 1
PAST_LEN = 16384
PAGE_SIZE = 128

MIX_W = D_MODEL
CONV_CH = MIX_W // 2
RW = MIX_W - CONV_CH
HEAD_DIM = 64
N_HEADS = RW // HEAD_DIM
CONV_K = 3
LORA_W = 64
LORA_A = 64
D_PLE = 256
SHIFT_W = 3 * RW + LORA_W + LORA_A
IN_W = 4 * CONV_CH + SHIFT_W + RW
RMS_EPS = 1e-6
GN_EPS = 64e-5

kernel_name = "hymba_conv_rwkv7_step"


def _rmsnorm(x, g):
    xf = x.astype(jnp.float32)
    y = xf * lax.rsqrt(jnp.mean(xf * xf, axis=-1, keepdims=True) + RMS_EPS)
    return (y * g.astype(jnp.float32)).astype(x.dtype)


def _wkv7(r, decay, k, v, kk, a, s0):
    seq = tuple(jnp.moveaxis(t, 1, 0) for t in (r, decay, k, v, kk, a))

    def step(s, inp):
        r_t, w_t, k_t, v_t, kk_t, a_t = inp
        sa = jnp.einsum('bhvk,bhk->bhv', s, -kk_t)
        s = (s * w_t[:, :, None, :]
             + sa[..., None] * (kk_t * a_t)[:, :, None, :]
             + v_t[..., None] * k_t[:, :, None, :])
        y = jnp.einsum('bhvk,bhk->bhv', s, r_t)
        return s, y

    s_T, ys = lax.scan(step, s0, seq)
    return jnp.moveaxis(ys, 0, 1), s_T


def _layer(h, p, conv_buf, shift_buf, wkv_state, norm_g, w_in, conv_w, mu_shift, w0, w_up,
           a0, a_up, k_k, k_a, r_k, ln_w, ln_b, w_out, w_pg, w_pp):
    bsz, t_len, _ = h.shape
    f32 = jnp.float32
    xn = _rmsnorm(h, norm_g)
    z = xn @ w_in
    zB, zC, zh, gA, zs, gR = jnp.split(
        z, [CONV_CH, 2 * CONV_CH, 3 * CONV_CH, 4 * CONV_CH, 4 * CONV_CH + SHIFT_W], axis=-1)

    u = zC * zh
    u_ext = jnp.concatenate([conv_buf.astype(u.dtype), u], axis=1)
    conv = (conv_w[0] * u_ext[:, :t_len] + conv_w[1] * u_ext[:, 1:t_len + 1]
            + conv_w[2] * u_ext[:, 2:t_len + 2])
    y_a = zB * conv * jax.nn.silu(gA)
    new_conv = u_ext[:, -(CONV_K - 1):]

    zs_prev = jnp.concatenate([shift_buf.astype(zs.dtype)[:, None], zs[:, :-1]], axis=1)
    zm = zs + (zs_prev - zs) * mu_shift
    new_shift = zs[:, -1]
    r, k, v, wd, ad = jnp.split(zm, [RW, 2 * RW, 3 * RW, 3 * RW + LORA_W], axis=-1)
    r, k, v, wd, ad = (t.astype(f32) for t in (r, k, v, wd, ad))
    wlog = -jax.nn.softplus(-(w0.astype(f32) + jnp.tanh(wd) @ w_up.astype(f32))) - 0.5
    decay = jnp.exp(-jnp.exp(wlog))
    a = jax.nn.sigmoid(a0.astype(f32) + ad @ a_up.astype(f32))
    kk = k * k_k.astype(f32)
    k = k * (1.0 + (a - 1.0) * k_a.astype(f32))

    def hd(t):
        return t.reshape(bsz, t_len, N_HEADS, HEAD_DIM)

    r, decay, k, v, kk, a = (hd(t) for t in (r, decay, k, v, kk, a))
    kk = kk / jnp.maximum(jnp.sqrt(jnp.sum(kk * kk, axis=-1, keepdims=True)), 1e-12)
    o, s_T = _wkv7(r, decay, k, v, kk, a, wkv_state.astype(f32))
    mean = jnp.mean(o, axis=-1, keepdims=True)
    var = jnp.mean(jnp.square(o - mean), axis=-1, keepdims=True)
    o = (o - mean) * lax.rsqrt(var + GN_EPS)
    o = o.reshape(bsz, t_len, RW) * ln_w.astype(f32) + ln_b.astype(f32)
    bonus = jnp.sum(r * k * r_k.astype(f32), axis=-1, keepdims=True) * v
    o = o + bonus.reshape(bsz, t_len, RW)
    y_r = o.astype(h.dtype) * jax.nn.silu(gR)

    h = h + jnp.concatenate([y_a, y_r], axis=-1) @ w_out
    h = h + jax.nn.sigmoid(h @ w_pg) * (p @ w_pp)
    return h, new_conv, new_shift, s_T


def setup_inputs(seed: int = 0) -> dict:
    key = jax.random.key(seed)
    ks = jax.random.split(key, 24)

    def nrm(k, shape, scale):
        return jax.random.normal(k, shape, jnp.float32) * scale

    return {
        "x_prompt": nrm(ks[0], (BATCH, SEQ, D_MODEL), 1.0),
        "x_sample": nrm(ks[1], (DEC_BATCH, DEC_SEQ, D_MODEL), 1.0),
        "p_prompt": nrm(ks[2], (DEPTH, BATCH, SEQ, D_PLE), 1.0),
        "p_sample": nrm(ks[3], (DEPTH, DEC_BATCH, DEC_SEQ, D_PLE), 1.0),
        "state_conv": nrm(ks[4], (DEPTH, DEC_BATCH, CONV_K - 1, CONV_CH), 1.0),
        "state_shift": nrm(ks[5], (DEPTH, DEC_BATCH, SHIFT_W), 1.0),
        "state_wkv": nrm(ks[6], (DEPTH, DEC_BATCH, N_HEADS, HEAD_DIM, HEAD_DIM), 0.3),
        "norm_g": 1.0 + nrm(ks[7], (DEPTH, D_MODEL), 0.02),
        "w_in": nrm(ks[8], (DEPTH, D_MODEL, IN_W), D_MODEL ** -0.5),
        "conv_w": nrm(ks[9], (DEPTH, CONV_K, CONV_CH), CONV_K ** -0.5),
        "mu_shift": jax.random.uniform(ks[10], (DEPTH, SHIFT_W), jnp.float32, 0.0, 1.0),
        "w0": jax.random.uniform(ks[11], (DEPTH, RW), jnp.float32, -4.0, -0.5),
        "w_up": nrm(ks[12], (DEPTH, LORA_W, RW), 0.5 * LORA_W ** -0.5),
        "a0": nrm(ks[13], (DEPTH, RW), 0.1),
        "a_up": nrm(ks[14], (DEPTH, LORA_A, RW), LORA_A ** -0.5),
        "k_k": 0.85 + nrm(ks[15], (DEPTH, RW), 0.05),
        "k_a": 1.0 + nrm(ks[16], (DEPTH, RW), 0.05),
        "r_k": nrm(ks[17], (DEPTH, N_HEADS, HEAD_DIM), 0.1),
        "ln_w": 1.0 + nrm(ks[18], (DEPTH, RW), 0.02),
        "ln_b": nrm(ks[19], (DEPTH, RW), 0.02),
        "w_out": nrm(ks[20], (DEPTH, MIX_W, D_MODEL), MIX_W ** -0.5),
        "w_pg": nrm(ks[21], (DEPTH, D_MODEL, D_MODEL), D_MODEL ** -0.5),
        "w_pp": nrm(ks[22], (DEPTH, D_PLE, D_MODEL), D_PLE ** -0.5),
        "final_g": 1.0 + nrm(ks[23], (D_MODEL,), 0.02),
    }


def reference(x_prompt, x_sample, p_prompt, p_sample, state_conv, state_shift, state_wkv,
              norm_g, w_in, conv_w, mu_shift, w0, w_up, a0, a_up, k_k, k_a, r_k, ln_w, ln_b,
              w_out, w_pg, w_pp, final_g):
    hp, hs = x_prompt, x_sample
    bp = x_prompt.shape[0]
    cps, sps, wps, css, sss, wss = [], [], [], [], [], []
    for i in range(DEPTH):
        wi = (norm_g[i], w_in[i], conv_w[i], mu_shift[i], w0[i], w_up[i], a0[i], a_up[i],
              k_k[i], k_a[i], r_k[i], ln_w[i], ln_b[i], w_out[i], w_pg[i], w_pp[i])
        zc = jnp.zeros((bp, CONV_K - 1, CONV_CH), hp.dtype)
        zsh = jnp.zeros((bp, SHIFT_W), hp.dtype)
        zw = jnp.zeros((bp, N_HEADS, HEAD_DIM, HEAD_DIM), jnp.float32)
        hp, cp, sp, wp = _layer(hp, p_prompt[i], zc, zsh, zw, *wi)
        hs, cs, ss, ws = _layer(hs, p_sample[i], state_conv[i], state_shift[i], state_wkv[i], *wi)
        cps.append(cp); sps.append(sp); wps.append(wp)
        css.append(cs); sss.append(ss); wss.append(ws)
    y_prompt = _rmsnorm(hp, final_g)
    y_sample = _rmsnorm(hs, final_g)
    return (y_prompt, y_sample, jnp.stack(cps), jnp.stack(sps), jnp.stack(wps),
            jnp.stack(css), jnp.stack(sss), jnp.stack(wss))
```

```python
import functools
import math

import jax
import jax.numpy as jnp
from jax import lax
from jax.experimental import pallas as pl
from jax.experimental.pallas import tpu as pltpu

F32 = jnp.float32
BF16 = jnp.bfloat16

D_MODEL = 1024
CONV_CH = 512
RW = 512
HEAD_DIM = 64
N_HEADS = 8
N_PAIRS = N_HEADS // 2
LORA = 64
D_PLE = 256
SHIFT_W = 3 * RW + 2 * LORA
IN_W = 4 * CONV_CH + SHIFT_W + RW
OFF_ZS = 4 * CONV_CH
OFF_GR = OFF_ZS + SHIFT_W
RMS_EPS = 1e-6
GN_EPS = 64e-5
DECAY_SCALE = math.exp(-0.5)

LANES = 128
CHUNK = 64
CARRY_ROWS = 8
VMEM_LIMIT = 56 * 1024 * 1024


def _dot(a, b):
    return jnp.dot(a, b, preferred_element_type=F32)


def _dot_nt(a, b):
    return lax.dot_general(a, b, (((1,), (1,)), ((), ())), preferred_element_type=F32)


def _dot_tn(a, b):
    return lax.dot_general(a, b, (((0,), (0,)), ((), ())), preferred_element_type=F32)


def _split(x, parts):
    out = []
    rem = x
    for i in range(parts):
        t = rem.astype(BF16)
        out.append(t)
        if i + 1 < parts:
            rem = rem - t.astype(F32)
    return out


def _const_dot(c_bf16, x, parts):
    acc = None
    for t in _split(x, parts):
        d = _dot(c_bf16, t)
        acc = d if acc is None else acc + d
    return acc


def _x_dot_const(x, c_bf16, parts):
    acc = None
    for t in _split(x, parts):
        d = _dot(t, c_bf16)
        acc = d if acc is None else acc + d
    return acc


def _mm(a, b, passes):
    if passes == 1:
        return _dot(a.astype(BF16), b.astype(BF16))
    ah, al = _split(a, 2)
    bh, bl = _split(b, 2)
    return _dot(ah, bh) + _dot(al, bh) + _dot(ah, bl)


def _sigmoid(x):
    return 1.0 / (1.0 + jnp.exp(-x))


def _silu(x):
    return x * _sigmoid(x)


def _inproj_kernel(x_ref, g_ref, w_ref, z_ref):
    x = x_ref[...]
    ms = jnp.mean(x * x, axis=-1, keepdims=True)
    xn = x * lax.rsqrt(ms + RMS_EPS) * g_ref[...]
    z_ref[...] = _dot(xn.astype(BF16), w_ref[...])


def _inproj(x2d, norm_g, w_in_bf16, tm):
    n = x2d.shape[0]
    return pl.pallas_call(
        _inproj_kernel,
        grid=(n // tm,),
        in_specs=[
            pl.BlockSpec((tm, D_MODEL), lambda i: (i, 0)),
            pl.BlockSpec((1, D_MODEL), lambda i: (0, 0)),
            pl.BlockSpec((D_MODEL, IN_W), lambda i: (0, 0)),
        ],
        out_specs=pl.BlockSpec((tm, IN_W), lambda i: (i, 0)),
        out_shape=jax.ShapeDtypeStruct((n, IN_W), F32),
        compiler_params=pltpu.CompilerParams(
            dimension_semantics=("arbitrary",), vmem_limit_bytes=VMEM_LIMIT),
        name="inproj",
    )(x2d, norm_g.reshape(1, D_MODEL), w_in_bf16)


def _out_kernel(h_ref, y_ref, p_ref, wo_ref, wg_ref, wp_ref, fg_ref, o_ref):
    h1 = h_ref[...] + _dot(y_ref[...].astype(BF16), wo_ref[...])
    gate = _sigmoid(_dot(h1.astype(BF16), wg_ref[...]))
    pe = _dot(p_ref[...].astype(BF16), wp_ref[...])
    h2 = h1 + gate * pe
    ms = jnp.mean(h2 * h2, axis=-1, keepdims=True)
    o_ref[...] = h2 * lax.rsqrt(ms + RMS_EPS) * fg_ref[...]


def _out_stage(h2d, ycat, p2d, wo, wg, wp, final_g, tm):
    n = h2d.shape[0]
    return pl.pallas_call(
        _out_kernel,
        grid=(n // tm,),
        in_specs=[
            pl.BlockSpec((tm, D_MODEL), lambda i: (i, 0)),
            pl.BlockSpec((tm, D_MODEL), lambda i: (i, 0)),
            pl.BlockSpec((tm, D_PLE), lambda i: (i, 0)),
            pl.BlockSpec((D_MODEL, D_MODEL), lambda i: (0, 0)),
            pl.BlockSpec((D_MODEL, D_MODEL), lambda i: (0, 0)),
            pl.BlockSpec((D_PLE, D_MODEL), lambda i: (0, 0)),
            pl.BlockSpec((1, D_MODEL), lambda i: (0, 0)),
        ],
        out_specs=pl.BlockSpec((tm, D_MODEL), lambda i: (i, 0)),
        out_shape=jax.ShapeDtypeStruct((n, D_MODEL), F32),
        compiler_params=pltpu.CompilerParams(
            dimension_semantics=("arbitrary",), vmem_limit_bytes=VMEM_LIMIT),
        name="outstage",
    )(h2d, ycat, p2d, wo, wg, wp, final_g.reshape(1, D_MODEL))


def _rwkv_tokens(r, k, v, wa, prm, headones):
    n = r.shape[0]
    lane = lax.broadcasted_iota(jnp.int32, (n, LANES), 1)
    th = jnp.where(lane < LORA, jnp.tanh(wa), wa)
    lora = _mm(th, prm["wlora"], 3)
    lw = -DECAY_SCALE * _sigmoid(prm["w0"] + lora[:, :RW])
    a = _sigmoid(prm["a0"] + lora[:, RW:])
    kk = k * prm["k_k"]
    ss = _x_dot_const(kk * kk, headones, 2)
    kk = kk / jnp.maximum(jnp.sqrt(ss), 1e-12)
    kmod = k * (1.0 + (a - 1.0) * prm["k_a"])
    bonus = _x_dot_const(r * kmod * prm["r_k"], headones, 2) * v
    return lw, a, kk, kmod, bonus


def _rwkv_post(o, bonus, g_r, prm, headones):
    mean = _x_dot_const(o, headones, 2) * (1.0 / HEAD_DIM)
    d = o - mean
    var = _x_dot_const(d * d, headones, 2) * (1.0 / HEAD_DIM)
    on = d * lax.rsqrt(var + GN_EPS)
    on = on * prm["ln_w"] + prm["ln_b"]
    return (on + bonus) * _silu(g_r)


_PRM_NAMES = ("conv_w", "mu", "w0", "a0", "wlora", "k_k", "k_a", "r_k", "ln_w", "ln_b")


def _prm_arrays(conv_w, mu_shift, w0, w_up, a0, a_up, k_k, k_a, r_k, ln_w, ln_b):
    wlora = jnp.zeros((2 * LORA, 2 * RW), F32)
    wlora = wlora.at[:LORA, :RW].set(w_up).at[LORA:, RW:].set(a_up)
    row = lambda x: x.reshape(1, -1)
    return (conv_w, row(mu_shift), row(w0), row(a0), wlora, row(k_k), row(k_a),
            row(r_k), row(ln_w), row(ln_b))


def _prm_specs(grid_rank):
    zero = (lambda *_: (0, 0))
    shapes = ((3, CONV_CH), (1, SHIFT_W), (1, RW), (1, RW), (2 * LORA, 2 * RW),
              (1, RW), (1, RW), (1, RW), (1, RW), (1, RW))
    return [pl.BlockSpec(s, zero) for s in shapes]


def _load_prm(refs):
    return {n: r[...] for n, r in zip(_PRM_NAMES, refs)}


def _stack_heads(x):
    lane = lax.broadcasted_iota(jnp.int32, x.shape, 1)
    lo = jnp.where(lane < HEAD_DIM, x, 0.0)
    hi = jnp.where(lane >= HEAD_DIM, x, 0.0)
    return jnp.concatenate([lo, hi], axis=0)


def _unit_lower_inverse_minus_eye(a, inv_passes):
    n = a.shape[0]
    row = lax.broadcasted_iota(jnp.int32, (n, n), 0)
    col = lax.broadcasted_iota(jnp.int32, (n, n), 1)
    x = None
    size = 2
    while size <= CHUNK:
        half = size // 2
        sel = ((row // size) == (col // size)) & ((row % size) >= half) & ((col % size) < half)
        al = jnp.where(sel, a, 0.0)
        if x is None:
            x = al
        else:
            p = al + _mm(x, al, inv_passes)
            x = x + p + _mm(p, x, inv_passes)
        size *= 2
    return x


def _mixer_prompt_kernel(z_ref, *refs, tc, inv_passes):
    prm_refs = refs[:len(_PRM_NAMES)]
    headones_ref, tri_ref = refs[len(_PRM_NAMES):len(_PRM_NAMES) + 2]
    (ycat_ref, conv_out_ref, shift_out_ref, wkv_out_ref) = refs[len(_PRM_NAMES) + 2:len(_PRM_NAMES) + 6]
    (ubuf, zsbuf, s_ref, la_s, lr_s, rb_s, rk_s, bt_s, kt_s, v_s, gam_s, o_s) = refs[len(_PRM_NAMES) + 6:]

    t = pl.program_id(1)
    n_t = pl.num_programs(1)
    prm = _load_prm(prm_refs)
    headones = headones_ref[...]

    @pl.when(t == 0)
    def _():
        ubuf[0:CARRY_ROWS, :] = jnp.zeros((CARRY_ROWS, CONV_CH), F32)
        zsbuf[0:CARRY_ROWS, :] = jnp.zeros((CARRY_ROWS, SHIFT_W), F32)
        s_ref[...] = jnp.zeros(s_ref.shape, F32)

    u = z_ref[:, CONV_CH:2 * CONV_CH] * z_ref[:, 2 * CONV_CH:3 * CONV_CH]
    ubuf[CARRY_ROWS:CARRY_ROWS + tc, :] = u
    um1 = ubuf[CARRY_ROWS - 1:CARRY_ROWS - 1 + tc, :]
    um2 = ubuf[CARRY_ROWS - 2:CARRY_ROWS - 2 + tc, :]
    cw = prm["conv_w"]
    conv = cw[0:1, :] * um2 + cw[1:2, :] * um1 + cw[2:3, :] * u
    ycat_ref[:, 0:CONV_CH] = z_ref[:, 0:CONV_CH] * conv * _silu(z_ref[:, 3 * CONV_CH:4 * CONV_CH])
    last_u = ubuf[CARRY_ROWS + tc - 2:CARRY_ROWS + tc, :]
    ubuf[CARRY_ROWS - 2:CARRY_ROWS, :] = last_u

    @pl.when(t == n_t - 1)
    def _():
        conv_out_ref[0] = last_u

    zs = z_ref[:, OFF_ZS:OFF_ZS + SHIFT_W]
    zsbuf[CARRY_ROWS:CARRY_ROWS + tc, :] = zs
    zprev = zsbuf[CARRY_ROWS - 1:CARRY_ROWS - 1 + tc, :]
    zm = zs + (zprev - zs) * prm["mu"]
    last_zs = zsbuf[CARRY_ROWS + tc - 1:CARRY_ROWS + tc, :]
    zsbuf[CARRY_ROWS - 1:CARRY_ROWS, :] = last_zs

    @pl.when(t == n_t - 1)
    def _():
        shift_out_ref[0] = last_zs

    r = zm[:, 0:RW]
    k = zm[:, RW:2 * RW]
    v = zm[:, 2 * RW:3 * RW]
    wa = zm[:, 3 * RW:3 * RW + 2 * LORA]
    lw, a, kk, kmod, bonus = _rwkv_tokens(r, k, v, wa, prm, headones)

    gg = _const_dot(tri_ref[...], lw, 3)
    g = gg[0:tc]
    gc = gg[tc:2 * tc]
    eng = jnp.exp(-g)
    etail = jnp.exp(gc - g)
    b = kk * a
    la_s[...] = -kk * jnp.exp(g - lw)
    lr_s[...] = r * jnp.exp(g)
    rb_s[...] = b * eng
    rk_s[...] = kmod * eng
    bt_s[...] = b * etail
    kt_s[...] = kmod * etail
    v_s[...] = v
    gam_s[...] = jnp.exp(gc)

    n2 = 2 * CHUNK
    row = lax.broadcasted_iota(jnp.int32, (n2, n2), 0)
    col = lax.broadcasted_iota(jnp.int32, (n2, n2), 1)
    same = (row // CHUNK) == (col // CHUNK)
    strict = same & (row > col)
    incl = same & (row >= col)

    def chunk_body(c, carry):
        c0 = pl.multiple_of(c * CHUNK, CHUNK)
        rows = pl.ds(c0, CHUNK)
        for j in range(N_PAIRS):
            lanes = slice(j * LANES, (j + 1) * LANES)
            la = la_s[rows, lanes]
            lr = lr_s[rows, lanes]
            lhs1 = jnp.concatenate([_stack_heads(la), _stack_heads(lr)], axis=0).astype(BF16)
            rhs1 = jnp.concatenate([_stack_heads(rb_s[rows, lanes]),
                                    _stack_heads(rk_s[rows, lanes])], axis=0).astype(BF16)
            amat = _dot_nt(lhs1, rhs1)
            a_ab = jnp.where(strict, amat[0:n2, 0:n2], 0.0)
            a_ak = jnp.where(strict, amat[0:n2, n2:2 * n2], 0.0)
            a_rb = jnp.where(incl, amat[n2:2 * n2, 0:n2], 0.0)
            a_rk = jnp.where(incl, amat[n2:2 * n2, n2:2 * n2], 0.0)
            tm1 = _unit_lower_inverse_minus_eye(a_ab, inv_passes)

            s_old = s_ref[j]
            lh = _dot_nt(jnp.concatenate([la, lr], axis=0).astype(BF16), s_old.astype(BF16))
            vst = _stack_heads(v_s[rows, lanes]).astype(BF16)
            rhs = _stack_heads(lh[0:CHUNK]) + _dot(a_ak.astype(BF16), vst)
            ust = rhs + _mm(tm1, rhs, inv_passes)
            uv = jnp.concatenate([ust.astype(BF16), vst], axis=0)
            yst = _dot(jnp.concatenate([a_rb, a_rk], axis=1).astype(BF16), uv)
            o_s[rows, lanes] = lh[CHUNK:n2] + yst[0:CHUNK] + yst[CHUNK:n2]
            bk = jnp.concatenate([_stack_heads(bt_s[rows, lanes]),
                                  _stack_heads(kt_s[rows, lanes])], axis=0).astype(BF16)
            s_ref[j] = s_old * gam_s[pl.ds(c0, 1), lanes] + _dot_tn(uv, bk)
        return carry

    lax.fori_loop(0, tc // CHUNK, chunk_body, 0)

    y_r = _rwkv_post(o_s[...], bonus, z_ref[:, OFF_GR:OFF_GR + RW], prm, headones)
    ycat_ref[:, CONV_CH:2 * CONV_CH] = y_r

    @pl.when(t == n_t - 1)
    def _():
        for h in range(N_HEADS):
            j, i = divmod(h, 2)
            blk = s_ref[j]
            wkv_out_ref[0, h] = blk[i * HEAD_DIM:(i + 1) * HEAD_DIM, i * HEAD_DIM:(i + 1) * HEAD_DIM]


def _mixer_prompt(z, prm_arrays, headones, bsz, t_len, tc, inv_passes=3):
    n_t = t_len // tc
    blockones = (jnp.arange(tc)[:, None] // CHUNK) == (jnp.arange(tc)[None, :] // CHUNK)
    lower = jnp.arange(tc)[:, None] >= jnp.arange(tc)[None, :]
    tri = jnp.concatenate([blockones & lower, blockones], axis=0).astype(BF16)
    kern = functools.partial(_mixer_prompt_kernel, tc=tc, inv_passes=inv_passes)
    big = lambda: pltpu.VMEM((tc, RW), F32)
    return pl.pallas_call(
        kern,
        grid=(bsz, n_t),
        in_specs=[pl.BlockSpec((tc, IN_W), lambda b, t: (b * n_t + t, 0))]
        + _prm_specs(2)
        + [pl.BlockSpec((RW, RW), lambda b, t: (0, 0)),
           pl.BlockSpec((2 * tc, tc), lambda b, t: (0, 0))],
        out_specs=[
            pl.BlockSpec((tc, D_MODEL), lambda b, t: (b * n_t + t, 0)),
            pl.BlockSpec((1, 2, CONV_CH), lambda b, t: (b, 0, 0)),
            pl.BlockSpec((1, 1, SHIFT_W), lambda b, t: (b, 0, 0)),
            pl.BlockSpec((1, N_HEADS, HEAD_DIM, HEAD_DIM), lambda b, t: (b, 0, 0, 0)),
        ],
        out_shape=[
            jax.ShapeDtypeStruct((bsz * t_len, D_MODEL), F32),
            jax.ShapeDtypeStruct((bsz, 2, CONV_CH), F32),
            jax.ShapeDtypeStruct((bsz, 1, SHIFT_W), F32),
            jax.ShapeDtypeStruct((bsz, N_HEADS, HEAD_DIM, HEAD_DIM), F32),
        ],
        scratch_shapes=[
            pltpu.VMEM((CARRY_ROWS + tc, CONV_CH), F32),
            pltpu.VMEM((CARRY_ROWS + tc, SHIFT_W), F32),
            pltpu.VMEM((N_PAIRS, LANES, LANES), F32),
            big(), big(), big(), big(), big(), big(), big(), big(), big(),
        ],
        compiler_params=pltpu.CompilerParams(
            dimension_semantics=("arbitrary", "arbitrary"), vmem_limit_bytes=VMEM_LIMIT),
        name="mixer_prompt",
    )(z, *prm_arrays, headones, tri)


def _sample_pre_kernel(z_ref, cb_ref, sb_ref, *refs):
    prm_refs = refs[:len(_PRM_NAMES)]
    headones_ref = refs[len(_PRM_NAMES)]
    (ya_ref, conv_out_ref, nkk_ref, w_ref, b_ref, km_ref, v_ref, r_ref, bonus_ref) = refs[len(_PRM_NAMES) + 1:]
    prm = _load_prm(prm_refs)
    headones = headones_ref[...]

    u = z_ref[:, CONV_CH:2 * CONV_CH] * z_ref[:, 2 * CONV_CH:3 * CONV_CH]
    cb0 = cb_ref[:, 0:CONV_CH]
    cb1 = cb_ref[:, CONV_CH:2 * CONV_CH]
    cw = prm["conv_w"]
    conv = cw[0:1, :] * cb0 + cw[1:2, :] * cb1 + cw[2:3, :] * u
    ya_ref[...] = z_ref[:, 0:CONV_CH] * conv * _silu(z_ref[:, 3 * CONV_CH:4 * CONV_CH])
    conv_out_ref[:, 0:CONV_CH] = cb1
    conv_out_ref[:, CONV_CH:2 * CONV_CH] = u

    zs = z_ref[:, OFF_ZS:OFF_ZS + SHIFT_W]
    zm = zs + (sb_ref[...] - zs) * prm["mu"]
    r = zm[:, 0:RW]
    k = zm[:, RW:2 * RW]
    v = zm[:, 2 * RW:3 * RW]
    wa = zm[:, 3 * RW:3 * RW + 2 * LORA]
    lw, a, kk, kmod, bonus = _rwkv_tokens(r, k, v, wa, prm, headones)
    nkk_ref[...] = -kk
    w_ref[...] = jnp.exp(lw)
    b_ref[...] = kk * a
    km_ref[...] = kmod
    v_ref[...] = v
    r_ref[...] = r
    bonus_ref[...] = bonus


def _sample_step_kernel(s_ref, nkk_ref, w_ref, b_ref, km_ref, v_ref, r_ref, s1_ref, y_ref):
    s0 = s_ref[...]
    eye = (lax.broadcasted_iota(jnp.int32, (HEAD_DIM, HEAD_DIM), 0)
           == lax.broadcasted_iota(jnp.int32, (HEAD_DIM, HEAD_DIM), 1))
    sa = jnp.sum(s0 * nkk_ref[...], axis=-1, keepdims=True)
    vcol = jnp.sum(jnp.where(eye, v_ref[...], 0.0), axis=-1, keepdims=True)
    s1 = s0 * w_ref[...] + sa * b_ref[...] + vcol * km_ref[...]
    s1_ref[...] = s1
    ycol = jnp.sum(s1 * r_ref[...], axis=-1, keepdims=True)
    y_ref[...] = jnp.sum(jnp.where(eye, ycol, 0.0), axis=-2, keepdims=True)


def _sample_post_kernel(o_ref, bonus_ref, z_ref, ya_ref, *refs):
    prm_refs = refs[:len(_PRM_NAMES)]
    headones_ref = refs[len(_PRM_NAMES)]
    ycat_ref = refs[len(_PRM_NAMES) + 1]
    prm = _load_prm(prm_refs)
    ycat_ref[:, 0:CONV_CH] = ya_ref[...]
    ycat_ref[:, CONV_CH:2 * CONV_CH] = _rwkv_post(
        o_ref[...], bonus_ref[...], z_ref[:, OFF_GR:OFF_GR + RW], prm, headones_ref[...])


def _full(shape):
    nd = len(shape)
    return pl.BlockSpec(shape, lambda *_: (0,) * nd)


def _mixer_sample(z, state_conv, state_shift, state_wkv, prm_arrays, headones, bb=8):
    n = z.shape[0]
    vec = jax.ShapeDtypeStruct((n, RW), F32)
    cb = state_conv.reshape(n, 2 * CONV_CH)
    outs = pl.pallas_call(
        _sample_pre_kernel,
        grid=(1,),
        in_specs=[_full((n, IN_W)), _full((n, 2 * CONV_CH)), _full((n, SHIFT_W))]
        + _prm_specs(1) + [_full((RW, RW))],
        out_specs=[_full((n, CONV_CH)), _full((n, 2 * CONV_CH))] + [_full((n, RW))] * 7,
        out_shape=[jax.ShapeDtypeStruct((n, CONV_CH), F32),
                   jax.ShapeDtypeStruct((n, 2 * CONV_CH), F32)] + [vec] * 7,
        compiler_params=pltpu.CompilerParams(
            dimension_semantics=("arbitrary",), vmem_limit_bytes=VMEM_LIMIT),
        name="sample_pre",
    )(z, cb, state_shift, *prm_arrays, headones)
    ya, conv_new, nkk, w, b, km, v, r, bonus = outs

    hv = lambda x: x.reshape(n, N_HEADS, 1, HEAD_DIM)
    vspec = pl.BlockSpec((bb, N_HEADS, 1, HEAD_DIM), lambda i: (i, 0, 0, 0))
    sspec = pl.BlockSpec((bb, N_HEADS, HEAD_DIM, HEAD_DIM), lambda i: (i, 0, 0, 0))
    s1, y = pl.pallas_call(
        _sample_step_kernel,
        grid=(n // bb,),
        in_specs=[sspec] + [vspec] * 6,
        out_specs=[sspec, vspec],
        out_shape=[jax.ShapeDtypeStruct((n, N_HEADS, HEAD_DIM, HEAD_DIM), F32),
                   jax.ShapeDtypeStruct((n, N_HEADS, 1, HEAD_DIM), F32)],
        compiler_params=pltpu.CompilerParams(
            dimension_semantics=("arbitrary",), vmem_limit_bytes=VMEM_LIMIT),
        name="sample_step",
    )(state_wkv, hv(nkk), hv(w), hv(b), hv(km), hv(v), hv(r))

    ycat = pl.pallas_call(
        _sample_post_kernel,
        grid=(1,),
        in_specs=[_full((n, RW)), _full((n, RW)), _full((n, IN_W)), _full((n, CONV_CH))]
        + _prm_specs(1) + [_full((RW, RW))],
        out_specs=_full((n, D_MODEL)),
        out_shape=jax.ShapeDtypeStruct((n, D_MODEL), F32),
        compiler_params=pltpu.CompilerParams(
            dimension_semantics=("arbitrary",), vmem_limit_bytes=VMEM_LIMIT),
        name="sample_post",
    )(y.reshape(n, RW), bonus, z, ya, *prm_arrays, headones)
    return ycat, conv_new.reshape(n, 2, CONV_CH), s1


def _layer_prompt(x, p, w, tc=256, tm=256):
    bsz, t_len, _ = x.shape
    x2d = x.reshape(bsz * t_len, D_MODEL)
    z = _inproj(x2d, w["norm_g"], w["w_in"], tm)
    ycat, conv_new, shift_new, wkv_new = _mixer_prompt(z, w["prm"], w["headones"], bsz, t_len, tc)
    y = _out_stage(x2d, ycat, p.reshape(bsz * t_len, D_PLE), w["w_out"], w["w_pg"], w["w_pp"],
                   w["final_g"], tm)
    return y.reshape(bsz, t_len, D_MODEL), conv_new, shift_new.reshape(bsz, SHIFT_W), wkv_new


def _layer_sample(x, p, state_conv, state_shift, state_wkv, w):
    n = x.shape[0]
    x2d = x.reshape(n, D_MODEL)
    z = _inproj(x2d, w["norm_g"], w["w_in"], n)
    ycat, conv_new, wkv_new = _mixer_sample(z, state_conv, state_shift, state_wkv, w["prm"], w["headones"])
    y = _out_stage(x2d, ycat, p.reshape(n, D_PLE), w["w_out"], w["w_pg"], w["w_pp"], w["final_g"], n)
    shift_new = z[:, OFF_ZS:OFF_ZS + SHIFT_W]
    return y.reshape(n, 1, D_MODEL), conv_new, shift_new, wkv_new


def kernel(x_prompt, x_sample, p_prompt, p_sample, state_conv, state_shift, state_wkv, norm_g, w_in, conv_w, mu_shift, w0, w_up, a0, a_up, k_k, k_a, r_k, ln_w, ln_b, w_out, w_pg, w_pp, final_g):
    depth = norm_g.shape[0]
    assert depth == 1
    i = 0
    head_id = jnp.arange(RW) // HEAD_DIM
    w = {
        "norm_g": norm_g[i],
        "w_in": w_in[i].astype(BF16),
        "w_out": w_out[i].astype(BF16),
        "w_pg": w_pg[i].astype(BF16),
        "w_pp": w_pp[i].astype(BF16),
        "final_g": final_g,
        "prm": _prm_arrays(conv_w[i], mu_shift[i], w0[i], w_up[i], a0[i], a_up[i], k_k[i], k_a[i],
                           r_k[i].reshape(RW), ln_w[i], ln_b[i]),
        "headones": (head_id[:, None] == head_id[None, :]).astype(BF16),
    }
    yp, cp, sp, wp = _layer_prompt(x_prompt, p_prompt[i], w)
    ys, cs, ss, ws = _layer_sample(x_sample, p_sample[i], state_conv[i], state_shift[i], state_wkv[i], w)
    return (yp, ys, cp[None], sp[None], wp[None], cs[None], ss[None], ws[None])
```

```python
import functools
import math

import jax
import jax.numpy as jnp
from jax import lax
from jax.experimental import pallas as pl
from jax.experimental.pallas import tpu as pltpu

F32 = jnp.float32
BF16 = jnp.bfloat16

D_MODEL = 1024
CONV_CH = 512
RW = 512
HEAD_DIM = 64
N_HEADS = 8
N_PAIRS = N_HEADS // 2
LORA = 64
D_PLE = 256
SHIFT_W = 3 * RW + 2 * LORA
IN_W = 4 * CONV_CH + SHIFT_W + RW
OFF_ZS = 4 * CONV_CH
OFF_GR = OFF_ZS + SHIFT_W
RMS_EPS = 1e-6
GN_EPS = 64e-5
DECAY_SCALE = math.exp(-0.5)

LANES = 128
CHUNK = 64
CARRY_ROWS = 8
VMEM_LIMIT = 56 * 1024 * 1024


def _dot(a, b):
    return jnp.dot(a, b, preferred_element_type=F32)


def _dot_nt(a, b):
    return lax.dot_general(a, b, (((1,), (1,)), ((), ())), preferred_element_type=F32)


def _dot_tn(a, b):
    return lax.dot_general(a, b, (((0,), (0,)), ((), ())), preferred_element_type=F32)


def _split(x, parts):
    out = []
    rem = x
    for i in range(parts):
        t = rem.astype(BF16)
        out.append(t)
        if i + 1 < parts:
            rem = rem - t.astype(F32)
    return out


def _const_dot(c_bf16, x, parts):
    acc = None
    for t in _split(x, parts):
        d = _dot(c_bf16, t)
        acc = d if acc is None else acc + d
    return acc


def _x_dot_const(x, c_bf16, parts):
    acc = None
    for t in _split(x, parts):
        d = _dot(t, c_bf16)
        acc = d if acc is None else acc + d
    return acc


def _mm(a, b, passes):
    if passes == 1:
        return _dot(a.astype(BF16), b.astype(BF16))
    ah, al = _split(a, 2)
    bh, bl = _split(b, 2)
    return _dot(ah, bh) + _dot(al, bh) + _dot(ah, bl)


def _sigmoid(x):
    return 1.0 / (1.0 + jnp.exp(-x))


def _silu(x):
    return x * _sigmoid(x)


def _inproj_kernel(x_ref, g_ref, w_ref, z_ref):
    x = x_ref[...]
    ms = jnp.mean(x * x, axis=-1, keepdims=True)
    xn = x * lax.rsqrt(ms + RMS_EPS) * g_ref[...]
    z_ref[...] = _dot(xn.astype(BF16), w_ref[...])


def _inproj(x2d, norm_g, w_in_bf16, tm):
    n = x2d.shape[0]
    return pl.pallas_call(
        _inproj_kernel,
        grid=(n // tm,),
        in_specs=[
            pl.BlockSpec((tm, D_MODEL), lambda i: (i, 0)),
            pl.BlockSpec((1, D_MODEL), lambda i: (0, 0)),
            pl.BlockSpec((D_MODEL, IN_W), lambda i: (0, 0)),
        ],
        out_specs=pl.BlockSpec((tm, IN_W), lambda i: (i, 0)),
        out_shape=jax.ShapeDtypeStruct((n, IN_W), F32),
        compiler_params=pltpu.CompilerParams(
            dimension_semantics=("arbitrary",), vmem_limit_bytes=VMEM_LIMIT),
        name="inproj",
    )(x2d, norm_g.reshape(1, D_MODEL), w_in_bf16)


def _out_kernel(h_ref, y_ref, p_ref, wo_ref, wg_ref, wp_ref, fg_ref, o_ref):
    h1 = h_ref[...] + _dot(y_ref[...].astype(BF16), wo_ref[...])
    gate = _sigmoid(_dot(h1.astype(BF16), wg_ref[...]))
    pe = _dot(p_ref[...].astype(BF16), wp_ref[...])
    h2 = h1 + gate * pe
    ms = jnp.mean(h2 * h2, axis=-1, keepdims=True)
    o_ref[...] = h2 * lax.rsqrt(ms + RMS_EPS) * fg_ref[...]


def _out_stage(h2d, ycat, p2d, wo, wg, wp, final_g, tm):
    n = h2d.shape[0]
    return pl.pallas_call(
        _out_kernel,
        grid=(n // tm,),
        in_specs=[
            pl.BlockSpec((tm, D_MODEL), lambda i: (i, 0)),
            pl.BlockSpec((tm, D_MODEL), lambda i: (i, 0)),
            pl.BlockSpec((tm, D_PLE), lambda i: (i, 0)),
            pl.BlockSpec((D_MODEL, D_MODEL), lambda i: (0, 0)),
            pl.BlockSpec((D_MODEL, D_MODEL), lambda i: (0, 0)),
            pl.BlockSpec((D_PLE, D_MODEL), lambda i: (0, 0)),
            pl.BlockSpec((1, D_MODEL), lambda i: (0, 0)),
        ],
        out_specs=pl.BlockSpec((tm, D_MODEL), lambda i: (i, 0)),
        out_shape=jax.ShapeDtypeStruct((n, D_MODEL), F32),
        compiler_params=pltpu.CompilerParams(
            dimension_semantics=("arbitrary",), vmem_limit_bytes=VMEM_LIMIT),
        name="outstage",
    )(h2d, ycat, p2d, wo, wg, wp, final_g.reshape(1, D_MODEL))


def _rwkv_tokens(r, k, v, wa, prm, headones):
    n = r.shape[0]
    lane = lax.broadcasted_iota(jnp.int32, (n, LANES), 1)
    th = jnp.where(lane < LORA, jnp.tanh(wa), wa)
    lora = _mm(th, prm["wlora"], 3)
    lw = -DECAY_SCALE * _sigmoid(prm["w0"] + lora[:, :RW])
    a = _sigmoid(prm["a0"] + lora[:, RW:])
    kk = k * prm["k_k"]
    ss = _x_dot_const(kk * kk, headones, 2)
    kk = kk / jnp.maximum(jnp.sqrt(ss), 1e-12)
    kmod = k * (1.0 + (a - 1.0) * prm["k_a"])
    bonus = _x_dot_const(r * kmod * prm["r_k"], headones, 2) * v
    return lw, a, kk, kmod, bonus


def _rwkv_post(o, bonus, g_r, prm, headones):
    mean = _x_dot_const(o, headones, 2) * (1.0 / HEAD_DIM)
    d = o - mean
    var = _x_dot_const(d * d, headones, 2) * (1.0 / HEAD_DIM)
    on = d * lax.rsqrt(var + GN_EPS)
    on = on * prm["ln_w"] + prm["ln_b"]
    return (on + bonus) * _silu(g_r)


_PRM_NAMES = ("conv_w", "mu", "w0", "a0", "wlora", "k_k", "k_a", "r_k", "ln_w", "ln_b")


def _prm_arrays(conv_w, mu_shift, w0, w_up, a0, a_up, k_k, k_a, r_k, ln_w, ln_b):
    wlora = jnp.zeros((2 * LORA, 2 * RW), F32)
    wlora = wlora.at[:LORA, :RW].set(w_up).at[LORA:, RW:].set(a_up)
    row = lambda x: x.reshape(1, -1)
    return (conv_w, row(mu_shift), row(w0), row(a0), wlora, row(k_k), row(k_a),
            row(r_k), row(ln_w), row(ln_b))


def _prm_specs(grid_rank):
    zero = (lambda *_: (0, 0))
    shapes = ((3, CONV_CH), (1, SHIFT_W), (1, RW), (1, RW), (2 * LORA, 2 * RW),
              (1, RW), (1, RW), (1, RW), (1, RW), (1, RW))
    return [pl.BlockSpec(s, zero) for s in shapes]


def _load_prm(refs):
    return {n: r[...] for n, r in zip(_PRM_NAMES, refs)}


def _stack_heads(x):
    lane = lax.broadcasted_iota(jnp.int32, x.shape, 1)
    lo = jnp.where(lane < HEAD_DIM, x, 0.0)
    hi = jnp.where(lane >= HEAD_DIM, x, 0.0)
    return jnp.concatenate([lo, hi], axis=0)


def _bdot(a, b):
    return _dot(a.astype(BF16), b.astype(BF16))


def _unit_lower_inverse_minus_eye(mats):
    n = mats[0].shape[0]
    row = lax.broadcasted_iota(jnp.int32, (n, n), 0)
    col = lax.broadcasted_iota(jnp.int32, (n, n), 1)
    xs = None
    size = 2
    while size <= CHUNK:
        half = size // 2
        sel = ((row // size) == (col // size)) & ((row % size) >= half) & ((col % size) < half)
        als = [jnp.where(sel, a, 0.0) for a in mats]
        if xs is None:
            xs = als
        else:
            ps = [al + _bdot(x, al) for x, al in zip(xs, als)]
            xs = [x + p + _bdot(p, x) for x, p in zip(xs, ps)]
        size *= 2
    return xs


def _chunk_operators(ld, strict, incl):
    cat0 = lambda xs: jnp.concatenate(xs, axis=0)
    la = [_stack_heads(x) for x in ld("la")]
    lr = [_stack_heads(x) for x in ld("lr")]
    rb = [_stack_heads(x) for x in ld("rb")]
    rk = [_stack_heads(x) for x in ld("rk")]
    n2 = 2 * CHUNK
    amat = [_dot_nt(cat0([a, b]).astype(BF16), cat0([c, d]).astype(BF16))
            for a, b, c, d in zip(la, lr, rb, rk)]
    a_ab = [jnp.where(strict, m[0:n2, 0:n2], 0.0) for m in amat]
    xs = _unit_lower_inverse_minus_eye(a_ab)
    vst = [_stack_heads(x) for x in ld("v")]
    av = [_bdot(jnp.where(strict, m[0:n2, n2:2 * n2], 0.0), v) for m, v in zip(amat, vst)]
    uv0 = [y + _bdot(x, y) for x, y in zip(xs, av)]
    tla = [y + _bdot(x, y) for x, y in zip(xs, la)]
    a_rb = [jnp.where(incl, m[n2:2 * n2, 0:n2], 0.0) for m in amat]
    a_rk = [jnp.where(incl, m[n2:2 * n2, n2:2 * n2], 0.0) for m in amat]
    gop = [y + _bdot(a, t) for y, a, t in zip(lr, a_rb, tla)]
    uv = [cat0([u, v]).astype(BF16) for u, v in zip(uv0, vst)]
    y0 = [_dot(jnp.concatenate([a, b], axis=1).astype(BF16), w) for a, b, w in zip(a_rb, a_rk, uv)]
    bt = [_stack_heads(x) for x in ld("bt")]
    kt = [_stack_heads(x) for x in ld("kt")]
    pop = [_dot_tn(t.astype(BF16), b.astype(BF16)) for t, b in zip(tla, bt)]
    qop = [_dot_tn(w, cat0([b, k]).astype(BF16)) for w, b, k in zip(uv, bt, kt)]
    return gop, y0, pop, qop


def _mixer_prompt_kernel(z_ref, *refs, tc, group_chunks):
    prm_refs = refs[:len(_PRM_NAMES)]
    headones_ref, tri_ref = refs[len(_PRM_NAMES):len(_PRM_NAMES) + 2]
    (ycat_ref, conv_out_ref, shift_out_ref, wkv_out_ref) = refs[len(_PRM_NAMES) + 2:len(_PRM_NAMES) + 6]
    (ubuf, zsbuf, s_ref, la_s, lr_s, rb_s, rk_s, bt_s, kt_s, v_s, gam_s, o_s,
     g_sc, y0_sc, p_sc, q_sc) = refs[len(_PRM_NAMES) + 6:]

    t = pl.program_id(1)
    n_t = pl.num_programs(1)
    prm = _load_prm(prm_refs)
    headones = headones_ref[...]

    @pl.when(t == 0)
    def _():
        ubuf[0:CARRY_ROWS, :] = jnp.zeros((CARRY_ROWS, CONV_CH), F32)
        zsbuf[0:CARRY_ROWS, :] = jnp.zeros((CARRY_ROWS, SHIFT_W), F32)
        s_ref[...] = jnp.zeros(s_ref.shape, F32)

    u = z_ref[:, CONV_CH:2 * CONV_CH] * z_ref[:, 2 * CONV_CH:3 * CONV_CH]
    ubuf[CARRY_ROWS:CARRY_ROWS + tc, :] = u
    um1 = ubuf[CARRY_ROWS - 1:CARRY_ROWS - 1 + tc, :]
    um2 = ubuf[CARRY_ROWS - 2:CARRY_ROWS - 2 + tc, :]
    cw = prm["conv_w"]
    conv = cw[0:1, :] * um2 + cw[1:2, :] * um1 + cw[2:3, :] * u
    ycat_ref[:, 0:CONV_CH] = z_ref[:, 0:CONV_CH] * conv * _silu(z_ref[:, 3 * CONV_CH:4 * CONV_CH])
    last_u = ubuf[CARRY_ROWS + tc - 2:CARRY_ROWS + tc, :]
    ubuf[CARRY_ROWS - 2:CARRY_ROWS, :] = last_u

    @pl.when(t == n_t - 1)
    def _():
        conv_out_ref[0] = last_u

    zs = z_ref[:, OFF_ZS:OFF_ZS + SHIFT_W]
    zsbuf[CARRY_ROWS:CARRY_ROWS + tc, :] = zs
    zprev = zsbuf[CARRY_ROWS - 1:CARRY_ROWS - 1 + tc, :]
    zm = zs + (zprev - zs) * prm["mu"]
    last_zs = zsbuf[CARRY_ROWS + tc - 1:CARRY_ROWS + tc, :]
    zsbuf[CARRY_ROWS - 1:CARRY_ROWS, :] = last_zs

    @pl.when(t == n_t - 1)
    def _():
        shift_out_ref[0] = last_zs

    r = zm[:, 0:RW]
    k = zm[:, RW:2 * RW]
    v = zm[:, 2 * RW:3 * RW]
    wa = zm[:, 3 * RW:3 * RW + 2 * LORA]
    lw, a, kk, kmod, bonus = _rwkv_tokens(r, k, v, wa, prm, headones)

    gg = _const_dot(tri_ref[...], lw, 3)
    g = gg[0:tc]
    gc = gg[tc:2 * tc]
    eng = jnp.exp(-g)
    etail = jnp.exp(gc - g)
    b = kk * a
    la_s[...] = -kk * jnp.exp(g - lw)
    lr_s[...] = r * jnp.exp(g)
    rb_s[...] = b * eng
    rk_s[...] = kmod * eng
    bt_s[...] = b * etail
    kt_s[...] = kmod * etail
    v_s[...] = v
    gam_s[...] = jnp.exp(gc)

    n2 = 2 * CHUNK
    row = lax.broadcasted_iota(jnp.int32, (n2, n2), 0)
    col = lax.broadcasted_iota(jnp.int32, (n2, n2), 1)
    same = (row // CHUNK) == (col // CHUNK)
    strict = same & (row > col)
    incl = same & (row >= col)

    n_chunks = tc // CHUNK
    srcs = {"la": la_s, "lr": lr_s, "rb": rb_s, "rk": rk_s, "bt": bt_s, "kt": kt_s, "v": v_s}
    for c_lo in range(0, n_chunks, group_chunks):
        members = [(c, j) for c in range(c_lo, c_lo + group_chunks) for j in range(N_PAIRS)]

        def ld(name, members=members):
            ref = srcs[name]
            return [ref[c * CHUNK:(c + 1) * CHUNK, j * LANES:(j + 1) * LANES] for c, j in members]

        gop, y0, pop, qop = _chunk_operators(ld, strict, incl)
        for (c, j), g_, y_, p_, q_ in zip(members, gop, y0, pop, qop):
            idx = c * N_PAIRS + j
            g_sc[idx] = g_.astype(BF16)
            y0_sc[idx] = y_
            p_sc[idx] = p_.astype(BF16)
            q_sc[idx] = q_

    for c in range(n_chunks):
        for j in range(N_PAIRS):
            idx = c * N_PAIRS + j
            lanes = slice(j * LANES, (j + 1) * LANES)
            s_old = s_ref[j]
            s_bf = s_old.astype(BF16)
            yst = _dot_nt(g_sc[idx], s_bf) + y0_sc[idx]
            o_s[c * CHUNK:(c + 1) * CHUNK, lanes] = yst[0:CHUNK] + yst[CHUNK:n2]
            gam = gam_s[c * CHUNK:c * CHUNK + 1, lanes]
            s_ref[j] = s_old * gam + _dot(s_bf, p_sc[idx]) + q_sc[idx]

    y_r = _rwkv_post(o_s[...], bonus, z_ref[:, OFF_GR:OFF_GR + RW], prm, headones)
    ycat_ref[:, CONV_CH:2 * CONV_CH] = y_r

    @pl.when(t == n_t - 1)
    def _():
        for h in range(N_HEADS):
            j, i = divmod(h, 2)
            blk = s_ref[j]
            wkv_out_ref[0, h] = blk[i * HEAD_DIM:(i + 1) * HEAD_DIM, i * HEAD_DIM:(i + 1) * HEAD_DIM]


def _mixer_prompt(z, prm_arrays, headones, bsz, t_len, tc, group_chunks=2):
    n_t = t_len // tc
    blockones = (jnp.arange(tc)[:, None] // CHUNK) == (jnp.arange(tc)[None, :] // CHUNK)
    lower = jnp.arange(tc)[:, None] >= jnp.arange(tc)[None, :]
    tri = jnp.concatenate([blockones & lower, blockones], axis=0).astype(BF16)
    kern = functools.partial(_mixer_prompt_kernel, tc=tc, group_chunks=group_chunks)
    big = lambda: pltpu.VMEM((tc, RW), F32)
    n_blk = (tc // CHUNK) * N_PAIRS
    op = lambda dt: pltpu.VMEM((n_blk, LANES, LANES), dt)
    return pl.pallas_call(
        kern,
        grid=(bsz, n_t),
        in_specs=[pl.BlockSpec((tc, IN_W), lambda b, t: (b * n_t + t, 0))]
        + _prm_specs(2)
        + [pl.BlockSpec((RW, RW), lambda b, t: (0, 0)),
           pl.BlockSpec((2 * tc, tc), lambda b, t: (0, 0))],
        out_specs=[
            pl.BlockSpec((tc, D_MODEL), lambda b, t: (b * n_t + t, 0)),
            pl.BlockSpec((1, 2, CONV_CH), lambda b, t: (b, 0, 0)),
            pl.BlockSpec((1, 1, SHIFT_W), lambda b, t: (b, 0, 0)),
            pl.BlockSpec((1, N_HEADS, HEAD_DIM, HEAD_DIM), lambda b, t: (b, 0, 0, 0)),
        ],
        out_shape=[
            jax.ShapeDtypeStruct((bsz * t_len, D_MODEL), F32),
            jax.ShapeDtypeStruct((bsz, 2, CONV_CH), F32),
            jax.ShapeDtypeStruct((bsz, 1, SHIFT_W), F32),
            jax.ShapeDtypeStruct((bsz, N_HEADS, HEAD_DIM, HEAD_DIM), F32),
        ],
        scratch_shapes=[
            pltpu.VMEM((CARRY_ROWS + tc, CONV_CH), F32),
            pltpu.VMEM((CARRY_ROWS + tc, SHIFT_W), F32),
            pltpu.VMEM((N_PAIRS, LANES, LANES), F32),
            big(), big(), big(), big(), big(), big(), big(), big(), big(),
            op(BF16), op(F32), op(BF16), op(F32),
        ],
        compiler_params=pltpu.CompilerParams(
            dimension_semantics=("arbitrary", "arbitrary"), vmem_limit_bytes=VMEM_LIMIT),
        name="mixer_prompt",
    )(z, *prm_arrays, headones, tri)


def _sample_pre_kernel(z_ref, cb_ref, sb_ref, *refs):
    prm_refs = refs[:len(_PRM_NAMES)]
    headones_ref = refs[len(_PRM_NAMES)]
    (ya_ref, conv_out_ref, nkk_ref, w_ref, b_ref, km_ref, v_ref, r_ref, bonus_ref) = refs[len(_PRM_NAMES) + 1:]
    prm = _load_prm(prm_refs)
    headones = headones_ref[...]

    u = z_ref[:, CONV_CH:2 * CONV_CH] * z_ref[:, 2 * CONV_CH:3 * CONV_CH]
    cb0 = cb_ref[:, 0:CONV_CH]
    cb1 = cb_ref[:, CONV_CH:2 * CONV_CH]
    cw = prm["conv_w"]
    conv = cw[0:1, :] * cb0 + cw[1:2, :] * cb1 + cw[2:3, :] * u
    ya_ref[...] = z_ref[:, 0:CONV_CH] * conv * _silu(z_ref[:, 3 * CONV_CH:4 * CONV_CH])
    conv_out_ref[:, 0:CONV_CH] = cb1
    conv_out_ref[:, CONV_CH:2 * CONV_CH] = u

    zs = z_ref[:, OFF_ZS:OFF_ZS + SHIFT_W]
    zm = zs + (sb_ref[...] - zs) * prm["mu"]
    r = zm[:, 0:RW]
    k = zm[:, RW:2 * RW]
    v = zm[:, 2 * RW:3 * RW]
    wa = zm[:, 3 * RW:3 * RW + 2 * LORA]
    lw, a, kk, kmod, bonus = _rwkv_tokens(r, k, v, wa, prm, headones)
    nkk_ref[...] = -kk
    w_ref[...] = jnp.exp(lw)
    b_ref[...] = kk * a
    km_ref[...] = kmod
    v_ref[...] = v
    r_ref[...] = r
    bonus_ref[...] = bonus


def _sample_step_kernel(s_ref, nkk_ref, w_ref, b_ref, km_ref, v_ref, r_ref, s1_ref, y_ref):
    s0 = s_ref[...]
    eye = (lax.broadcasted_iota(jnp.int32, (HEAD_DIM, HEAD_DIM), 0)
           == lax.broadcasted_iota(jnp.int32, (HEAD_DIM, HEAD_DIM), 1))
    sa = jnp.sum(s0 * nkk_ref[...], axis=-1, keepdims=True)
    vcol = jnp.sum(jnp.where(eye, v_ref[...], 0.0), axis=-1, keepdims=True)
    s1 = s0 * w_ref[...] + sa * b_ref[...] + vcol * km_ref[...]
    s1_ref[...] = s1
    ycol = jnp.sum(s1 * r_ref[...], axis=-1, keepdims=True)
    y_ref[...] = jnp.sum(jnp.where(eye, ycol, 0.0), axis=-2, keepdims=True)


def _sample_post_kernel(o_ref, bonus_ref, z_ref, ya_ref, *refs):
    prm_refs = refs[:len(_PRM_NAMES)]
    headones_ref = refs[len(_PRM_NAMES)]
    ycat_ref = refs[len(_PRM_NAMES) + 1]
    prm = _load_prm(prm_refs)
    ycat_ref[:, 0:CONV_CH] = ya_ref[...]
    ycat_ref[:, CONV_CH:2 * CONV_CH] = _rwkv_post(
        o_ref[...], bonus_ref[...], z_ref[:, OFF_GR:OFF_GR + RW], prm, headones_ref[...])


def _full(shape):
    nd = len(shape)
    return pl.BlockSpec(shape, lambda *_: (0,) * nd)


def _mixer_sample(z, state_conv, state_shift, state_wkv, prm_arrays, headones, bb=8):
    n = z.shape[0]
    vec = jax.ShapeDtypeStruct((n, RW), F32)
    cb = state_conv.reshape(n, 2 * CONV_CH)
    outs = pl.pallas_call(
        _sample_pre_kernel,
        grid=(1,),
        in_specs=[_full((n, IN_W)), _full((n, 2 * CONV_CH)), _full((n, SHIFT_W))]
        + _prm_specs(1) + [_full((RW, RW))],
        out_specs=[_full((n, CONV_CH)), _full((n, 2 * CONV_CH))] + [_full((n, RW))] * 7,
        out_shape=[jax.ShapeDtypeStruct((n, CONV_CH), F32),
                   jax.ShapeDtypeStruct((n, 2 * CONV_CH), F32)] + [vec] * 7,
        compiler_params=pltpu.CompilerParams(
            dimension_semantics=("arbitrary",), vmem_limit_bytes=VMEM_LIMIT),
        name="sample_pre",
    )(z, cb, state_shift, *prm_arrays, headones)
    ya, conv_new, nkk, w, b, km, v, r, bonus = outs

    hv = lambda x: x.reshape(n, N_HEADS, 1, HEAD_DIM)
    vspec = pl.BlockSpec((bb, N_HEADS, 1, HEAD_DIM), lambda i: (i, 0, 0, 0))
    sspec = pl.BlockSpec((bb, N_HEADS, HEAD_DIM, HEAD_DIM), lambda i: (i, 0, 0, 0))
    s1, y = pl.pallas_call(
        _sample_step_kernel,
        grid=(n // bb,),
        in_specs=[sspec] + [vspec] * 6,
        out_specs=[sspec, vspec],
        out_shape=[jax.ShapeDtypeStruct((n, N_HEADS, HEAD_DIM, HEAD_DIM), F32),
                   jax.ShapeDtypeStruct((n, N_HEADS, 1, HEAD_DIM), F32)],
        compiler_params=pltpu.CompilerParams(
            dimension_semantics=("arbitrary",), vmem_limit_bytes=VMEM_LIMIT),
        name="sample_step",
    )(state_wkv, hv(nkk), hv(w), hv(b), hv(km), hv(v), hv(r))

    ycat = pl.pallas_call(
        _sample_post_kernel,
        grid=(1,),
        in_specs=[_full((n, RW)), _full((n, RW)), _full((n, IN_W)), _full((n, CONV_CH))]
        + _prm_specs(1) + [_full((RW, RW))],
        out_specs=_full((n, D_MODEL)),
        out_shape=jax.ShapeDtypeStruct((n, D_MODEL), F32),
        compiler_params=pltpu.CompilerParams(
            dimension_semantics=("arbitrary",), vmem_limit_bytes=VMEM_LIMIT),
        name="sample_post",
    )(y.reshape(n, RW), bonus, z, ya, *prm_arrays, headones)
    return ycat, conv_new.reshape(n, 2, CONV_CH), s1


def _layer_prompt(x, p, w, tc=256, tm=256):
    bsz, t_len, _ = x.shape
    x2d = x.reshape(bsz * t_len, D_MODEL)
    z = _inproj(x2d, w["norm_g"], w["w_in"], tm)
    ycat, conv_new, shift_new, wkv_new = _mixer_prompt(z, w["prm"], w["headones"], bsz, t_len, tc)
    y = _out_stage(x2d, ycat, p.reshape(bsz * t_len, D_PLE), w["w_out"], w["w_pg"], w["w_pp"],
                   w["final_g"], tm)
    return y.reshape(bsz, t_len, D_MODEL), conv_new, shift_new.reshape(bsz, SHIFT_W), wkv_new


def _layer_sample(x, p, state_conv, state_shift, state_wkv, w):
    n = x.shape[0]
    x2d = x.reshape(n, D_MODEL)
    z = _inproj(x2d, w["norm_g"], w["w_in"], n)
    ycat, conv_new, wkv_new = _mixer_sample(z, state_conv, state_shift, state_wkv, w["prm"], w["headones"])
    y = _out_stage(x2d, ycat, p.reshape(n, D_PLE), w["w_out"], w["w_pg"], w["w_pp"], w["final_g"], n)
    shift_new = z[:, OFF_ZS:OFF_ZS + SHIFT_W]
    return y.reshape(n, 1, D_MODEL), conv_new, shift_new, wkv_new


def kernel(x_prompt, x_sample, p_prompt, p_sample, state_conv, state_shift, state_wkv, norm_g, w_in, conv_w, mu_shift, w0, w_up, a0, a_up, k_k, k_a, r_k, ln_w, ln_b, w_out, w_pg, w_pp, final_g):
    depth = norm_g.shape[0]
    assert depth == 1
    i = 0
    head_id = jnp.arange(RW) // HEAD_DIM
    w = {
        "norm_g": norm_g[i],
        "w_in": w_in[i].astype(BF16),
        "w_out": w_out[i].astype(BF16),
        "w_pg": w_pg[i].astype(BF16),
        "w_pp": w_pp[i].astype(BF16),
        "final_g": final_g,
        "prm": _prm_arrays(conv_w[i], mu_shift[i], w0[i], w_up[i], a0[i], a_up[i], k_k[i], k_a[i],
                           r_k[i].reshape(RW), ln_w[i], ln_b[i]),
        "headones": (head_id[:, None] == head_id[None, :]).astype(BF16),
    }
    yp, cp, sp, wp = _layer_prompt(x_prompt, p_prompt[i], w)
    ys, cs, ss, ws = _layer_sample(x_sample, p_sample[i], state_conv[i], state_shift[i], state_wkv[i], w)
    return (yp, ys, cp[None], sp[None], wp[None], cs[None], ss[None], ws[None])
```

```python
import functools
import math

import jax
import jax.numpy as jnp
from jax import lax
from jax.experimental import pallas as pl
from jax.experimental.pallas import tpu as pltpu

F32 = jnp.float32
BF16 = jnp.bfloat16

D_MODEL = 1024
CONV_CH = 512
RW = 512
HEAD_DIM = 64
N_HEADS = 8
N_PAIRS = N_HEADS // 2
LORA = 64
D_PLE = 256
SHIFT_W = 3 * RW + 2 * LORA
IN_W = 4 * CONV_CH + SHIFT_W + RW
OFF_ZS = 4 * CONV_CH
OFF_GR = OFF_ZS + SHIFT_W
RMS_EPS = 1e-6
GN_EPS = 64e-5
DECAY_SCALE = math.exp(-0.5)

LANES = 128
MXU_DIM = 256
HEADSUM_W = MXU_DIM
SUM_PARTS = 1
CUM_PARTS = 2
LORA_PASSES = 1
CHUNK = 64
CARRY_ROWS = 8
VMEM_LIMIT = 56 * 1024 * 1024


def _dot(a, b):
    return jnp.dot(a, b, preferred_element_type=F32)


def _dot_nt(a, b):
    return lax.dot_general(a, b, (((1,), (1,)), ((), ())), preferred_element_type=F32)


def _dot_tn(a, b):
    return lax.dot_general(a, b, (((0,), (0,)), ((), ())), preferred_element_type=F32)


def _split(x, parts):
    out = []
    rem = x
    for i in range(parts):
        t = rem.astype(BF16)
        out.append(t)
        if i + 1 < parts:
            rem = rem - t.astype(F32)
    return out


def _const_dot(c_bf16, x, parts):
    acc = None
    for t in _split(x, parts):
        d = _dot(c_bf16, t)
        acc = d if acc is None else acc + d
    return acc


def _x_dot_const(x, c_bf16, parts):
    acc = None
    for t in _split(x, parts):
        d = _dot(t, c_bf16)
        acc = d if acc is None else acc + d
    return acc


def _head_sum(x, headones):
    w = headones.shape[0]
    halves = [_x_dot_const(x[:, i * w:(i + 1) * w], headones, SUM_PARTS) for i in range(x.shape[1] // w)]
    return jnp.concatenate(halves, axis=1)


def _mm(a, b, passes):
    if passes == 1:
        return _dot(a.astype(BF16), b.astype(BF16))
    ah, al = _split(a, 2)
    bh, bl = _split(b, 2)
    return _dot(ah, bh) + _dot(al, bh) + _dot(ah, bl)


def _sigmoid(x):
    return 1.0 / (1.0 + jnp.exp(-x))


def _silu(x):
    return x * _sigmoid(x)


def _inproj_kernel(x_ref, g_ref, w_ref, z_ref):
    x = x_ref[...]
    ms = jnp.mean(x * x, axis=-1, keepdims=True)
    xn = x * lax.rsqrt(ms + RMS_EPS) * g_ref[...]
    z_ref[...] = _dot(xn.astype(BF16), w_ref[...])


def _inproj(x2d, norm_g, w_in_bf16, tm):
    n = x2d.shape[0]
    return pl.pallas_call(
        _inproj_kernel,
        grid=(n // tm,),
        in_specs=[
            pl.BlockSpec((tm, D_MODEL), lambda i: (i, 0)),
            pl.BlockSpec((1, D_MODEL), lambda i: (0, 0)),
            pl.BlockSpec((D_MODEL, IN_W), lambda i: (0, 0)),
        ],
        out_specs=pl.BlockSpec((tm, IN_W), lambda i: (i, 0)),
        out_shape=jax.ShapeDtypeStruct((n, IN_W), F32),
        compiler_params=pltpu.CompilerParams(
            dimension_semantics=("arbitrary",), vmem_limit_bytes=VMEM_LIMIT),
        name="inproj",
    )(x2d, norm_g.reshape(1, D_MODEL), w_in_bf16)


def _out_kernel(h_ref, y_ref, p_ref, wo_ref, wg_ref, wp_ref, fg_ref, o_ref):
    h1 = h_ref[...] + _dot(y_ref[...].astype(BF16), wo_ref[...])
    gate = _sigmoid(_dot(h1.astype(BF16), wg_ref[...]))
    pe = _dot(p_ref[...].astype(BF16), wp_ref[...])
    h2 = h1 + gate * pe
    ms = jnp.mean(h2 * h2, axis=-1, keepdims=True)
    o_ref[...] = h2 * lax.rsqrt(ms + RMS_EPS) * fg_ref[...]


def _out_stage(h2d, ycat, p2d, wo, wg, wp, final_g, tm):
    n = h2d.shape[0]
    return pl.pallas_call(
        _out_kernel,
        grid=(n // tm,),
        in_specs=[
            pl.BlockSpec((tm, D_MODEL), lambda i: (i, 0)),
            pl.BlockSpec((tm, D_MODEL), lambda i: (i, 0)),
            pl.BlockSpec((tm, D_PLE), lambda i: (i, 0)),
            pl.BlockSpec((D_MODEL, D_MODEL), lambda i: (0, 0)),
            pl.BlockSpec((D_MODEL, D_MODEL), lambda i: (0, 0)),
            pl.BlockSpec((D_PLE, D_MODEL), lambda i: (0, 0)),
            pl.BlockSpec((1, D_MODEL), lambda i: (0, 0)),
        ],
        out_specs=pl.BlockSpec((tm, D_MODEL), lambda i: (i, 0)),
        out_shape=jax.ShapeDtypeStruct((n, D_MODEL), F32),
        compiler_params=pltpu.CompilerParams(
            dimension_semantics=("arbitrary",), vmem_limit_bytes=VMEM_LIMIT),
        name="outstage",
    )(h2d, ycat, p2d, wo, wg, wp, final_g.reshape(1, D_MODEL))


def _rwkv_tokens(r, k, v, wa, prm, headones):
    n = r.shape[0]
    lane = lax.broadcasted_iota(jnp.int32, (n, LANES), 1)
    th = jnp.where(lane < LORA, jnp.tanh(wa), wa)
    lora = _mm(th, prm["wlora"], LORA_PASSES)
    lw = -DECAY_SCALE * _sigmoid(prm["w0"] + lora[:, :RW])
    a = _sigmoid(prm["a0"] + lora[:, RW:])
    kk = k * prm["k_k"]
    ss = _head_sum(kk * kk, headones)
    kk = kk / jnp.maximum(jnp.sqrt(ss), 1e-12)
    kmod = k * (1.0 + (a - 1.0) * prm["k_a"])
    bonus = _head_sum(r * kmod * prm["r_k"], headones) * v
    return lw, a, kk, kmod, bonus


def _rwkv_post(o, bonus, g_r, prm, headones):
    mean = _head_sum(o, headones) * (1.0 / HEAD_DIM)
    d = o - mean
    var = _head_sum(d * d, headones) * (1.0 / HEAD_DIM)
    on = d * lax.rsqrt(var + GN_EPS)
    on = on * prm["ln_w"] + prm["ln_b"]
    return (on + bonus) * _silu(g_r)


_PRM_NAMES = ("conv_w", "mu", "w0", "a0", "wlora", "k_k", "k_a", "r_k", "ln_w", "ln_b")


def _prm_arrays(conv_w, mu_shift, w0, w_up, a0, a_up, k_k, k_a, r_k, ln_w, ln_b):
    wlora = jnp.zeros((2 * LORA, 2 * RW), F32)
    wlora = wlora.at[:LORA, :RW].set(w_up).at[LORA:, RW:].set(a_up)
    row = lambda x: x.reshape(1, -1)
    return (conv_w, row(mu_shift), row(w0), row(a0), wlora, row(k_k), row(k_a),
            row(r_k), row(ln_w), row(ln_b))


def _prm_specs(grid_rank):
    zero = (lambda *_: (0, 0))
    shapes = ((3, CONV_CH), (1, SHIFT_W), (1, RW), (1, RW), (2 * LORA, 2 * RW),
              (1, RW), (1, RW), (1, RW), (1, RW), (1, RW))
    return [pl.BlockSpec(s, zero) for s in shapes]


def _load_prm(refs):
    return {n: r[...] for n, r in zip(_PRM_NAMES, refs)}


def _stack_heads(x):
    lane = lax.broadcasted_iota(jnp.int32, x.shape, 1)
    lo = jnp.where(lane < HEAD_DIM, x, 0.0)
    hi = jnp.where(lane >= HEAD_DIM, x, 0.0)
    return jnp.concatenate([lo, hi], axis=0)


def _bdot(a, b):
    return _dot(a.astype(BF16), b.astype(BF16))


def _unit_lower_inverse_minus_eye(mats):
    n = mats[0].shape[0]
    row = lax.broadcasted_iota(jnp.int32, (n, n), 0)
    col = lax.broadcasted_iota(jnp.int32, (n, n), 1)
    xs = None
    size = 2
    while size <= CHUNK:
        half = size // 2
        sel = ((row // size) == (col // size)) & ((row % size) >= half) & ((col % size) < half)
        als = [jnp.where(sel, a, 0.0) for a in mats]
        if xs is None:
            xs = als
        else:
            ps = [al + _bdot(x, al) for x, al in zip(xs, als)]
            xs = [x + p + _bdot(p, x) for x, p in zip(xs, ps)]
        size *= 2
    return xs


def _chunk_operators(ld, strict, incl):
    cat0 = lambda xs: jnp.concatenate(xs, axis=0)
    la = [_stack_heads(x) for x in ld("la")]
    lr = [_stack_heads(x) for x in ld("lr")]
    rb = [_stack_heads(x) for x in ld("rb")]
    rk = [_stack_heads(x) for x in ld("rk")]
    n2 = 2 * CHUNK
    amat = [_dot_nt(cat0([a, b]).astype(BF16), cat0([c, d]).astype(BF16))
            for a, b, c, d in zip(la, lr, rb, rk)]
    a_ab = [jnp.where(strict, m[0:n2, 0:n2], 0.0) for m in amat]
    xs = _unit_lower_inverse_minus_eye(a_ab)
    vst = [_stack_heads(x) for x in ld("v")]
    av = [_bdot(jnp.where(strict, m[0:n2, n2:2 * n2], 0.0), v) for m, v in zip(amat, vst)]
    uv0 = [y + _bdot(x, y) for x, y in zip(xs, av)]
    tla = [y + _bdot(x, y) for x, y in zip(xs, la)]
    a_rb = [jnp.where(incl, m[n2:2 * n2, 0:n2], 0.0) for m in amat]
    a_rk = [jnp.where(incl, m[n2:2 * n2, n2:2 * n2], 0.0) for m in amat]
    gop = [y + _bdot(a, t) for y, a, t in zip(lr, a_rb, tla)]
    uv = [cat0([u, v]).astype(BF16) for u, v in zip(uv0, vst)]
    y0 = [_dot(jnp.concatenate([a, b], axis=1).astype(BF16), w) for a, b, w in zip(a_rb, a_rk, uv)]
    bt = [_stack_heads(x) for x in ld("bt")]
    kt = [_stack_heads(x) for x in ld("kt")]
    pop = [_dot_tn(t.astype(BF16), b.astype(BF16)) for t, b in zip(tla, bt)]
    qop = [_dot_tn(w, cat0([b, k]).astype(BF16)) for w, b, k in zip(uv, bt, kt)]
    return gop, y0, pop, qop


def _mixer_prompt_kernel(z_ref, *refs, tc, group_chunks):
    prm_refs = refs[:len(_PRM_NAMES)]
    headones_ref, tri_ref = refs[len(_PRM_NAMES):len(_PRM_NAMES) + 2]
    (ycat_ref, conv_out_ref, shift_out_ref, wkv_out_ref) = refs[len(_PRM_NAMES) + 2:len(_PRM_NAMES) + 6]
    (ubuf, zsbuf, s_ref, la_s, lr_s, rb_s, rk_s, bt_s, kt_s, v_s, gam_s, o_s,
     g_sc, y0_sc, p_sc, q_sc) = refs[len(_PRM_NAMES) + 6:]

    t = pl.program_id(1)
    n_t = pl.num_programs(1)
    prm = _load_prm(prm_refs)
    headones = headones_ref[...]

    @pl.when(t == 0)
    def _():
        ubuf[0:CARRY_ROWS, :] = jnp.zeros((CARRY_ROWS, CONV_CH), F32)
        zsbuf[0:CARRY_ROWS, :] = jnp.zeros((CARRY_ROWS, SHIFT_W), F32)
        s_ref[...] = jnp.zeros(s_ref.shape, F32)

    u = z_ref[:, CONV_CH:2 * CONV_CH] * z_ref[:, 2 * CONV_CH:3 * CONV_CH]
    ubuf[CARRY_ROWS:CARRY_ROWS + tc, :] = u
    um1 = ubuf[CARRY_ROWS - 1:CARRY_ROWS - 1 + tc, :]
    um2 = ubuf[CARRY_ROWS - 2:CARRY_ROWS - 2 + tc, :]
    cw = prm["conv_w"]
    conv = cw[0:1, :] * um2 + cw[1:2, :] * um1 + cw[2:3, :] * u
    ycat_ref[:, 0:CONV_CH] = z_ref[:, 0:CONV_CH] * conv * _silu(z_ref[:, 3 * CONV_CH:4 * CONV_CH])
    last_u = ubuf[CARRY_ROWS + tc - 2:CARRY_ROWS + tc, :]
    ubuf[CARRY_ROWS - 2:CARRY_ROWS, :] = last_u

    @pl.when(t == n_t - 1)
    def _():
        conv_out_ref[0] = last_u

    zs = z_ref[:, OFF_ZS:OFF_ZS + SHIFT_W]
    zsbuf[CARRY_ROWS:CARRY_ROWS + tc, :] = zs
    zprev = zsbuf[CARRY_ROWS - 1:CARRY_ROWS - 1 + tc, :]
    zm = zs + (zprev - zs) * prm["mu"]
    last_zs = zsbuf[CARRY_ROWS + tc - 1:CARRY_ROWS + tc, :]
    zsbuf[CARRY_ROWS - 1:CARRY_ROWS, :] = last_zs

    @pl.when(t == n_t - 1)
    def _():
        shift_out_ref[0] = last_zs

    r = zm[:, 0:RW]
    k = zm[:, RW:2 * RW]
    v = zm[:, 2 * RW:3 * RW]
    wa = zm[:, 3 * RW:3 * RW + 2 * LORA]
    lw, a, kk, kmod, bonus = _rwkv_tokens(r, k, v, wa, prm, headones)

    g = _const_dot(tri_ref[...], lw, CUM_PARTS)
    gc = jnp.concatenate(
        [jnp.broadcast_to(g[(c + 1) * CHUNK - 1:(c + 1) * CHUNK, :], (CHUNK, RW)) for c in range(tc // CHUNK)],
        axis=0)
    eng = jnp.exp(-g)
    etail = jnp.exp(gc - g)
    b = kk * a
    la_s[...] = -kk * jnp.exp(g - lw)
    lr_s[...] = r * jnp.exp(g)
    rb_s[...] = b * eng
    rk_s[...] = kmod * eng
    bt_s[...] = b * etail
    kt_s[...] = kmod * etail
    v_s[...] = v
    gam_s[...] = jnp.exp(gc)

    n2 = 2 * CHUNK
    row = lax.broadcasted_iota(jnp.int32, (n2, n2), 0)
    col = lax.broadcasted_iota(jnp.int32, (n2, n2), 1)
    same = (row // CHUNK) == (col // CHUNK)
    strict = same & (row > col)
    incl = same & (row >= col)

    n_chunks = tc // CHUNK
    srcs = {"la": la_s, "lr": lr_s, "rb": rb_s, "rk": rk_s, "bt": bt_s, "kt": kt_s, "v": v_s}
    for c_lo in range(0, n_chunks, group_chunks):
        members = [(c, j) for c in range(c_lo, c_lo + group_chunks) for j in range(N_PAIRS)]

        def ld(name, members=members):
            ref = srcs[name]
            return [ref[c * CHUNK:(c + 1) * CHUNK, j * LANES:(j + 1) * LANES] for c, j in members]

        gop, y0, pop, qop = _chunk_operators(ld, strict, incl)
        for (c, j), g_, y_, p_, q_ in zip(members, gop, y0, pop, qop):
            idx = c * N_PAIRS + j
            g_sc[idx] = g_.astype(BF16)
            y0_sc[idx] = y_
            p_sc[idx] = p_.astype(BF16)
            q_sc[idx] = q_

    for c in range(n_chunks):
        for j in range(N_PAIRS):
            idx = c * N_PAIRS + j
            lanes = slice(j * LANES, (j + 1) * LANES)
            s_old = s_ref[j]
            s_bf = s_old.astype(BF16)
            yst = _dot_nt(g_sc[idx], s_bf) + y0_sc[idx]
            o_s[c * CHUNK:(c + 1) * CHUNK, lanes] = yst[0:CHUNK] + yst[CHUNK:n2]
            gam = gam_s[c * CHUNK:c * CHUNK + 1, lanes]
            s_ref[j] = s_old * gam + _dot(s_bf, p_sc[idx]) + q_sc[idx]

    y_r = _rwkv_post(o_s[...], bonus, z_ref[:, OFF_GR:OFF_GR + RW], prm, headones)
    ycat_ref[:, CONV_CH:2 * CONV_CH] = y_r

    @pl.when(t == n_t - 1)
    def _():
        for h in range(N_HEADS):
            j, i = divmod(h, 2)
            blk = s_ref[j]
            wkv_out_ref[0, h] = blk[i * HEAD_DIM:(i + 1) * HEAD_DIM, i * HEAD_DIM:(i + 1) * HEAD_DIM]


def _mixer_prompt(z, prm_arrays, headones, bsz, t_len, tc, group_chunks=4):
    n_t = t_len // tc
    blockones = (jnp.arange(tc)[:, None] // CHUNK) == (jnp.arange(tc)[None, :] // CHUNK)
    lower = jnp.arange(tc)[:, None] >= jnp.arange(tc)[None, :]
    tri = (blockones & lower).astype(BF16)
    kern = functools.partial(_mixer_prompt_kernel, tc=tc, group_chunks=group_chunks)
    big = lambda: pltpu.VMEM((tc, RW), F32)
    n_blk = (tc // CHUNK) * N_PAIRS
    op = lambda dt: pltpu.VMEM((n_blk, LANES, LANES), dt)
    return pl.pallas_call(
        kern,
        grid=(bsz, n_t),
        in_specs=[pl.BlockSpec((tc, IN_W), lambda b, t: (b * n_t + t, 0))]
        + _prm_specs(2)
        + [pl.BlockSpec((HEADSUM_W, HEADSUM_W), lambda b, t: (0, 0)),
           pl.BlockSpec((tc, tc), lambda b, t: (0, 0))],
        out_specs=[
            pl.BlockSpec((tc, D_MODEL), lambda b, t: (b * n_t + t, 0)),
            pl.BlockSpec((1, 2, CONV_CH), lambda b, t: (b, 0, 0)),
            pl.BlockSpec((1, 1, SHIFT_W), lambda b, t: (b, 0, 0)),
            pl.BlockSpec((1, N_HEADS, HEAD_DIM, HEAD_DIM), lambda b, t: (b, 0, 0, 0)),
        ],
        out_shape=[
            jax.ShapeDtypeStruct((bsz * t_len, D_MODEL), F32),
            jax.ShapeDtypeStruct((bsz, 2, CONV_CH), F32),
            jax.ShapeDtypeStruct((bsz, 1, SHIFT_W), F32),
            jax.ShapeDtypeStruct((bsz, N_HEADS, HEAD_DIM, HEAD_DIM), F32),
        ],
        scratch_shapes=[
            pltpu.VMEM((CARRY_ROWS + tc, CONV_CH), F32),
            pltpu.VMEM((CARRY_ROWS + tc, SHIFT_W), F32),
            pltpu.VMEM((N_PAIRS, LANES, LANES), F32),
            big(), big(), big(), big(), big(), big(), big(), big(), big(),
            op(BF16), op(F32), op(BF16), op(F32),
        ],
        compiler_params=pltpu.CompilerParams(
            dimension_semantics=("arbitrary", "arbitrary"), vmem_limit_bytes=VMEM_LIMIT),
        name="mixer_prompt",
    )(z, *prm_arrays, headones, tri)


def _sample_pre_kernel(z_ref, cb_ref, sb_ref, *refs):
    prm_refs = refs[:len(_PRM_NAMES)]
    headones_ref = refs[len(_PRM_NAMES)]
    (ya_ref, conv_out_ref, nkk_ref, w_ref, b_ref, km_ref, v_ref, r_ref, bonus_ref) = refs[len(_PRM_NAMES) + 1:]
    prm = _load_prm(prm_refs)
    headones = headones_ref[...]

    u = z_ref[:, CONV_CH:2 * CONV_CH] * z_ref[:, 2 * CONV_CH:3 * CONV_CH]
    cb0 = cb_ref[:, 0:CONV_CH]
    cb1 = cb_ref[:, CONV_CH:2 * CONV_CH]
    cw = prm["conv_w"]
    conv = cw[0:1, :] * cb0 + cw[1:2, :] * cb1 + cw[2:3, :] * u
    ya_ref[...] = z_ref[:, 0:CONV_CH] * conv * _silu(z_ref[:, 3 * CONV_CH:4 * CONV_CH])
    conv_out_ref[:, 0:CONV_CH] = cb1
    conv_out_ref[:, CONV_CH:2 * CONV_CH] = u

    zs = z_ref[:, OFF_ZS:OFF_ZS + SHIFT_W]
    zm = zs + (sb_ref[...] - zs) * prm["mu"]
    r = zm[:, 0:RW]
    k = zm[:, RW:2 * RW]
    v = zm[:, 2 * RW:3 * RW]
    wa = zm[:, 3 * RW:3 * RW + 2 * LORA]
    lw, a, kk, kmod, bonus = _rwkv_tokens(r, k, v, wa, prm, headones)
    nkk_ref[...] = -kk
    w_ref[...] = jnp.exp(lw)
    b_ref[...] = kk * a
    km_ref[...] = kmod
    v_ref[...] = v
    r_ref[...] = r
    bonus_ref[...] = bonus


def _sample_step_kernel(s_ref, nkk_ref, w_ref, b_ref, km_ref, v_ref, r_ref, s1_ref, y_ref):
    s0 = s_ref[...]
    eye = (lax.broadcasted_iota(jnp.int32, (HEAD_DIM, HEAD_DIM), 0)
           == lax.broadcasted_iota(jnp.int32, (HEAD_DIM, HEAD_DIM), 1))
    sa = jnp.sum(s0 * nkk_ref[...], axis=-1, keepdims=True)
    vcol = jnp.sum(jnp.where(eye, v_ref[...], 0.0), axis=-1, keepdims=True)
    s1 = s0 * w_ref[...] + sa * b_ref[...] + vcol * km_ref[...]
    s1_ref[...] = s1
    ycol = jnp.sum(s1 * r_ref[...], axis=-1, keepdims=True)
    y_ref[...] = jnp.sum(jnp.where(eye, ycol, 0.0), axis=-2, keepdims=True)


def _sample_post_kernel(o_ref, bonus_ref, z_ref, ya_ref, *refs):
    prm_refs = refs[:len(_PRM_NAMES)]
    headones_ref = refs[len(_PRM_NAMES)]
    ycat_ref = refs[len(_PRM_NAMES) + 1]
    prm = _load_prm(prm_refs)
    ycat_ref[:, 0:CONV_CH] = ya_ref[...]
    ycat_ref[:, CONV_CH:2 * CONV_CH] = _rwkv_post(
        o_ref[...], bonus_ref[...], z_ref[:, OFF_GR:OFF_GR + RW], prm, headones_ref[...])


def _full(shape):
    nd = len(shape)
    return pl.BlockSpec(shape, lambda *_: (0,) * nd)


def _mixer_sample(z, state_conv, state_shift, state_wkv, prm_arrays, headones, bb=8):
    n = z.shape[0]
    vec = jax.ShapeDtypeStruct((n, RW), F32)
    cb = state_conv.reshape(n, 2 * CONV_CH)
    outs = pl.pallas_call(
        _sample_pre_kernel,
        grid=(1,),
        in_specs=[_full((n, IN_W)), _full((n, 2 * CONV_CH)), _full((n, SHIFT_W))]
        + _prm_specs(1) + [_full((HEADSUM_W, HEADSUM_W))],
        out_specs=[_full((n, CONV_CH)), _full((n, 2 * CONV_CH))] + [_full((n, RW))] * 7,
        out_shape=[jax.ShapeDtypeStruct((n, CONV_CH), F32),
                   jax.ShapeDtypeStruct((n, 2 * CONV_CH), F32)] + [vec] * 7,
        compiler_params=pltpu.CompilerParams(
            dimension_semantics=("arbitrary",), vmem_limit_bytes=VMEM_LIMIT),
        name="sample_pre",
    )(z, cb, state_shift, *prm_arrays, headones)
    ya, conv_new, nkk, w, b, km, v, r, bonus = outs

    hv = lambda x: x.reshape(n, N_HEADS, 1, HEAD_DIM)
    vspec = pl.BlockSpec((bb, N_HEADS, 1, HEAD_DIM), lambda i: (i, 0, 0, 0))
    sspec = pl.BlockSpec((bb, N_HEADS, HEAD_DIM, HEAD_DIM), lambda i: (i, 0, 0, 0))
    s1, y = pl.pallas_call(
        _sample_step_kernel,
        grid=(n // bb,),
        in_specs=[sspec] + [vspec] * 6,
        out_specs=[sspec, vspec],
        out_shape=[jax.ShapeDtypeStruct((n, N_HEADS, HEAD_DIM, HEAD_DIM), F32),
                   jax.ShapeDtypeStruct((n, N_HEADS, 1, HEAD_DIM), F32)],
        compiler_params=pltpu.CompilerParams(
            dimension_semantics=("arbitrary",), vmem_limit_bytes=VMEM_LIMIT),
        name="sample_step",
    )(state_wkv, hv(nkk), hv(w), hv(b), hv(km), hv(v), hv(r))

    ycat = pl.pallas_call(
        _sample_post_kernel,
        grid=(1,),
        in_specs=[_full((n, RW)), _full((n, RW)), _full((n, IN_W)), _full((n, CONV_CH))]
        + _prm_specs(1) + [_full((HEADSUM_W, HEADSUM_W))],
        out_specs=_full((n, D_MODEL)),
        out_shape=jax.ShapeDtypeStruct((n, D_MODEL), F32),
        compiler_params=pltpu.CompilerParams(
            dimension_semantics=("arbitrary",), vmem_limit_bytes=VMEM_LIMIT),
        name="sample_post",
    )(y.reshape(n, RW), bonus, z, ya, *prm_arrays, headones)
    return ycat, conv_new.reshape(n, 2, CONV_CH), s1


def _layer_prompt(x, p, w, tc=256, tm=256):
    bsz, t_len, _ = x.shape
    x2d = x.reshape(bsz * t_len, D_MODEL)
    z = _inproj(x2d, w["norm_g"], w["w_in"], tm)
    ycat, conv_new, shift_new, wkv_new = _mixer_prompt(z, w["prm"], w["headones"], bsz, t_len, tc)
    y = _out_stage(x2d, ycat, p.reshape(bsz * t_len, D_PLE), w["w_out"], w["w_pg"], w["w_pp"],
                   w["final_g"], tm)
    return y.reshape(bsz, t_len, D_MODEL), conv_new, shift_new.reshape(bsz, SHIFT_W), wkv_new


def _layer_sample(x, p, state_conv, state_shift, state_wkv, w):
    n = x.shape[0]
    x2d = x.reshape(n, D_MODEL)
    z = _inproj(x2d, w["norm_g"], w["w_in"], n)
    ycat, conv_new, wkv_new = _mixer_sample(z, state_conv, state_shift, state_wkv, w["prm"], w["headones"])
    y = _out_stage(x2d, ycat, p.reshape(n, D_PLE), w["w_out"], w["w_pg"], w["w_pp"], w["final_g"], n)
    shift_new = z[:, OFF_ZS:OFF_ZS + SHIFT_W]
    return y.reshape(n, 1, D_MODEL), conv_new, shift_new, wkv_new


def kernel(x_prompt, x_sample, p_prompt, p_sample, state_conv, state_shift, state_wkv, norm_g, w_in, conv_w, mu_shift, w0, w_up, a0, a_up, k_k, k_a, r_k, ln_w, ln_b, w_out, w_pg, w_pp, final_g):
    depth = norm_g.shape[0]
    assert depth == 1
    i = 0
    head_id = jnp.arange(HEADSUM_W) // HEAD_DIM
    w = {
        "norm_g": norm_g[i],
        "w_in": w_in[i].astype(BF16),
        "w_out": w_out[i].astype(BF16),
        "w_pg": w_pg[i].astype(BF16),
        "w_pp": w_pp[i].astype(BF16),
        "final_g": final_g,
        "prm": _prm_arrays(conv_w[i], mu_shift[i], w0[i], w_up[i], a0[i], a_up[i], k_k[i], k_a[i],
                           r_k[i].reshape(RW), ln_w[i], ln_b[i]),
        "headones": (head_id[:, None] == head_id[None, :]).astype(BF16),
    }
    yp, cp, sp, wp = _layer_prompt(x_prompt, p_prompt[i], w)
    ys, cs, ss, ws = _layer_sample(x_sample, p_sample[i], state_conv[i], state_shift[i], state_wkv[i], w)
    return (yp, ys, cp[None], sp[None], wp[None], cs[None], ss[None], ws[None])
```

```python
import functools
import math

import jax
import jax.numpy as jnp
from jax import lax
from jax.experimental import pallas as pl
from jax.experimental.pallas import tpu as pltpu

F32 = jnp.float32
BF16 = jnp.bfloat16

D_MODEL = 1024
CONV_CH = 512
RW = 512
HEAD_DIM = 64
N_HEADS = 8
N_PAIRS = N_HEADS // 2
LORA = 64
D_PLE = 256
SHIFT_W = 3 * RW + 2 * LORA
IN_W = 4 * CONV_CH + SHIFT_W + RW
OFF_ZS = 4 * CONV_CH
OFF_GR = OFF_ZS + SHIFT_W
RMS_EPS = 1e-6
GN_EPS = 64e-5
DECAY_SCALE = math.exp(-0.5)

LANES = 128
MXU_DIM = 256
HEADSUM_W = MXU_DIM
SUM_PARTS = 1
CUM_PARTS = 2
LORA_PASSES = 1
CHUNK = 64
CARRY_ROWS = 8
VMEM_LIMIT = 56 * 1024 * 1024


def _dot(a, b):
    return jnp.dot(a, b, preferred_element_type=F32)


def _dot_nt(a, b):
    return lax.dot_general(a, b, (((1,), (1,)), ((), ())), preferred_element_type=F32)


def _dot_tn(a, b):
    return lax.dot_general(a, b, (((0,), (0,)), ((), ())), preferred_element_type=F32)


def _split(x, parts):
    out = []
    rem = x
    for i in range(parts):
        t = rem.astype(BF16)
        out.append(t)
        if i + 1 < parts:
            rem = rem - t.astype(F32)
    return out


def _const_dot(c_bf16, x, parts):
    acc = None
    for t in _split(x, parts):
        d = _dot(c_bf16, t)
        acc = d if acc is None else acc + d
    return acc


def _x_dot_const(x, c_bf16, parts):
    acc = None
    for t in _split(x, parts):
        d = _dot(t, c_bf16)
        acc = d if acc is None else acc + d
    return acc


def _head_sum(x, headones):
    w = headones.shape[0]
    halves = [_x_dot_const(x[:, i * w:(i + 1) * w], headones, SUM_PARTS) for i in range(x.shape[1] // w)]
    return jnp.concatenate(halves, axis=1)


def _mm(a, b, passes):
    if passes == 1:
        return _dot(a.astype(BF16), b.astype(BF16))
    ah, al = _split(a, 2)
    bh, bl = _split(b, 2)
    return _dot(ah, bh) + _dot(al, bh) + _dot(ah, bl)


def _sigmoid(x):
    return 1.0 / (1.0 + jnp.exp(-x))


def _silu(x):
    return x * _sigmoid(x)


def _inproj_kernel(x_ref, g_ref, w_ref, z_ref):
    x = x_ref[...]
    ms = jnp.mean(x * x, axis=-1, keepdims=True)
    xn = x * lax.rsqrt(ms + RMS_EPS) * g_ref[...]
    z_ref[...] = _dot(xn.astype(BF16), w_ref[...])


def _inproj(x2d, norm_g, w_in_bf16, tm):
    n = x2d.shape[0]
    return pl.pallas_call(
        _inproj_kernel,
        grid=(n // tm,),
        in_specs=[
            pl.BlockSpec((tm, D_MODEL), lambda i: (i, 0)),
            pl.BlockSpec((1, D_MODEL), lambda i: (0, 0)),
            pl.BlockSpec((D_MODEL, IN_W), lambda i: (0, 0)),
        ],
        out_specs=pl.BlockSpec((tm, IN_W), lambda i: (i, 0)),
        out_shape=jax.ShapeDtypeStruct((n, IN_W), F32),
        compiler_params=pltpu.CompilerParams(
            dimension_semantics=("arbitrary",), vmem_limit_bytes=VMEM_LIMIT),
        name="inproj",
    )(x2d, norm_g.reshape(1, D_MODEL), w_in_bf16)


def _out_kernel(h_ref, y_ref, p_ref, wo_ref, wg_ref, wp_ref, fg_ref, o_ref):
    h1 = h_ref[...] + _dot(y_ref[...].astype(BF16), wo_ref[...])
    gate = _sigmoid(_dot(h1.astype(BF16), wg_ref[...]))
    pe = _dot(p_ref[...].astype(BF16), wp_ref[...])
    h2 = h1 + gate * pe
    ms = jnp.mean(h2 * h2, axis=-1, keepdims=True)
    o_ref[...] = h2 * lax.rsqrt(ms + RMS_EPS) * fg_ref[...]


def _out_stage(h2d, ycat, p2d, wo, wg, wp, final_g, tm):
    n = h2d.shape[0]
    return pl.pallas_call(
        _out_kernel,
        grid=(n // tm,),
        in_specs=[
            pl.BlockSpec((tm, D_MODEL), lambda i: (i, 0)),
            pl.BlockSpec((tm, D_MODEL), lambda i: (i, 0)),
            pl.BlockSpec((tm, D_PLE), lambda i: (i, 0)),
            pl.BlockSpec((D_MODEL, D_MODEL), lambda i: (0, 0)),
            pl.BlockSpec((D_MODEL, D_MODEL), lambda i: (0, 0)),
            pl.BlockSpec((D_PLE, D_MODEL), lambda i: (0, 0)),
            pl.BlockSpec((1, D_MODEL), lambda i: (0, 0)),
        ],
        out_specs=pl.BlockSpec((tm, D_MODEL), lambda i: (i, 0)),
        out_shape=jax.ShapeDtypeStruct((n, D_MODEL), F32),
        compiler_params=pltpu.CompilerParams(
            dimension_semantics=("arbitrary",), vmem_limit_bytes=VMEM_LIMIT),
        name="outstage",
    )(h2d, ycat, p2d, wo, wg, wp, final_g.reshape(1, D_MODEL))


def _rwkv_tokens(r, k, v, wa, prm, headones, between_steps=lambda: None):
    n = r.shape[0]
    lane = lax.broadcasted_iota(jnp.int32, (n, LANES), 1)
    th = jnp.where(lane < LORA, jnp.tanh(wa), wa)
    lora = _mm(th, prm["wlora"], LORA_PASSES)
    lw = -DECAY_SCALE * _sigmoid(prm["w0"] + lora[:, :RW])
    a = _sigmoid(prm["a0"] + lora[:, RW:])
    between_steps()
    kk = k * prm["k_k"]
    ss = _head_sum(kk * kk, headones)
    kk = kk / jnp.maximum(jnp.sqrt(ss), 1e-12)
    between_steps()
    kmod = k * (1.0 + (a - 1.0) * prm["k_a"])
    bonus = _head_sum(r * kmod * prm["r_k"], headones) * v
    between_steps()
    return lw, a, kk, kmod, bonus


def _rwkv_post(o, bonus, g_r, prm, headones):
    mean = _head_sum(o, headones) * (1.0 / HEAD_DIM)
    d = o - mean
    var = _head_sum(d * d, headones) * (1.0 / HEAD_DIM)
    on = d * lax.rsqrt(var + GN_EPS)
    on = on * prm["ln_w"] + prm["ln_b"]
    return (on + bonus) * _silu(g_r)


_PRM_NAMES = ("conv_w", "mu", "w0", "a0", "wlora", "k_k", "k_a", "r_k", "ln_w", "ln_b")


def _prm_arrays(conv_w, mu_shift, w0, w_up, a0, a_up, k_k, k_a, r_k, ln_w, ln_b):
    wlora = jnp.zeros((2 * LORA, 2 * RW), F32)
    wlora = wlora.at[:LORA, :RW].set(w_up).at[LORA:, RW:].set(a_up)
    row = lambda x: x.reshape(1, -1)
    return (conv_w, row(mu_shift), row(w0), row(a0), wlora, row(k_k), row(k_a),
            row(r_k), row(ln_w), row(ln_b))


def _prm_specs(grid_rank):
    zero = (lambda *_: (0, 0))
    shapes = ((3, CONV_CH), (1, SHIFT_W), (1, RW), (1, RW), (2 * LORA, 2 * RW),
              (1, RW), (1, RW), (1, RW), (1, RW), (1, RW))
    return [pl.BlockSpec(s, zero) for s in shapes]


def _load_prm(refs):
    return {n: r[...] for n, r in zip(_PRM_NAMES, refs)}


def _stack_heads(x):
    lane = lax.broadcasted_iota(jnp.int32, x.shape, 1)
    lo = jnp.where(lane < HEAD_DIM, x, 0.0)
    hi = jnp.where(lane >= HEAD_DIM, x, 0.0)
    return jnp.concatenate([lo, hi], axis=0)


def _bdot(a, b):
    return _dot(a.astype(BF16), b.astype(BF16))


def _unit_lower_inverse_minus_eye(mats, between_levels=lambda: None):
    n = mats[0].shape[0]
    row = lax.broadcasted_iota(jnp.int32, (n, n), 0)
    col = lax.broadcasted_iota(jnp.int32, (n, n), 1)
    xs = None
    size = 2
    while size <= CHUNK:
        half = size // 2
        sel = ((row // size) == (col // size)) & ((row % size) >= half) & ((col % size) < half)
        als = [jnp.where(sel, a, 0.0) for a in mats]
        if xs is None:
            xs = als
        else:
            ps = [al + _bdot(x, al) for x, al in zip(xs, als)]
            xs = [x + p + _bdot(p, x) for x, p in zip(xs, ps)]
            between_levels()
        size *= 2
    return xs


def _chunk_operators(ld, strict, incl, side_work=()):
    side_work = list(side_work)

    def breathe():
        if side_work:
            side_work.pop(0)()

    cat0 = lambda xs: jnp.concatenate(xs, axis=0)
    la = [_stack_heads(x) for x in ld("la")]
    lr = [_stack_heads(x) for x in ld("lr")]
    rb = [_stack_heads(x) for x in ld("rb")]
    rk = [_stack_heads(x) for x in ld("rk")]
    n2 = 2 * CHUNK
    amat = [_dot_nt(cat0([a, b]).astype(BF16), cat0([c, d]).astype(BF16))
            for a, b, c, d in zip(la, lr, rb, rk)]
    a_ab = [jnp.where(strict, m[0:n2, 0:n2], 0.0) for m in amat]
    xs = _unit_lower_inverse_minus_eye(a_ab, breathe)
    vst = [_stack_heads(x) for x in ld("v")]
    av = [_bdot(jnp.where(strict, m[0:n2, n2:2 * n2], 0.0), v) for m, v in zip(amat, vst)]
    breathe()
    uv0 = [y + _bdot(x, y) for x, y in zip(xs, av)]
    tla = [y + _bdot(x, y) for x, y in zip(xs, la)]
    breathe()
    a_rb = [jnp.where(incl, m[n2:2 * n2, 0:n2], 0.0) for m in amat]
    a_rk = [jnp.where(incl, m[n2:2 * n2, n2:2 * n2], 0.0) for m in amat]
    gop = [y + _bdot(a, t) for y, a, t in zip(lr, a_rb, tla)]
    uv = [cat0([u, v]).astype(BF16) for u, v in zip(uv0, vst)]
    y0 = [_dot(jnp.concatenate([a, b], axis=1).astype(BF16), w) for a, b, w in zip(a_rb, a_rk, uv)]
    breathe()
    bt = [_stack_heads(x) for x in ld("bt")]
    kt = [_stack_heads(x) for x in ld("kt")]
    pop = [_dot_tn(t.astype(BF16), b.astype(BF16)) for t, b in zip(tla, bt)]
    qop = [_dot_tn(w, cat0([b, k]).astype(BF16)) for w, b, k in zip(uv, bt, kt)]
    while side_work:
        breathe()
    return gop, y0, pop, qop


def _prompt_kernel(x_ref, p_ref, ng_ref, win_ref, wo_ref, wg_ref, wp_ref, fg_ref, *refs, tc):
    prm_refs = refs[:len(_PRM_NAMES)]
    headones_ref, tri_ref = refs[len(_PRM_NAMES):len(_PRM_NAMES) + 2]
    (y_ref, conv_out_ref, shift_out_ref, wkv_out_ref) = refs[len(_PRM_NAMES) + 2:len(_PRM_NAMES) + 6]
    (z_s, ycat_s, ubuf, zsbuf, s_ref, la_s, lr_s, rb_s, rk_s, bt_s, kt_s, v_s, gam_s, o_s,
     g_sc, y0_sc, p_sc, q_sc) = refs[len(_PRM_NAMES) + 6:]

    t = pl.program_id(1)
    n_t = pl.num_programs(1)
    prm = _load_prm(prm_refs)
    headones = headones_ref[...]

    @pl.when(t == 0)
    def _():
        ubuf[0:CARRY_ROWS, :] = jnp.zeros((CARRY_ROWS, CONV_CH), F32)
        zsbuf[0:CARRY_ROWS, :] = jnp.zeros((CARRY_ROWS, SHIFT_W), F32)
        s_ref[...] = jnp.zeros(s_ref.shape, F32)

    x = x_ref[...]
    xn = (x * lax.rsqrt(jnp.mean(x * x, axis=-1, keepdims=True) + RMS_EPS) * ng_ref[...]).astype(BF16)

    def project(c0, c1):
        z_s[:, c0:c1] = _dot(xn, win_ref[:, c0:c1])

    later_cols = [(c0, c0 + CONV_CH) for c0 in (OFF_GR, 0, CONV_CH, 2 * CONV_CH, 3 * CONV_CH)]

    def project_next():
        if later_cols:
            project(*later_cols.pop(0))

    project(OFF_ZS, OFF_ZS + SHIFT_W)

    zs = z_s[:, OFF_ZS:OFF_ZS + SHIFT_W]
    zsbuf[CARRY_ROWS:CARRY_ROWS + tc, :] = zs
    zprev = zsbuf[CARRY_ROWS - 1:CARRY_ROWS - 1 + tc, :]
    zm = zs + (zprev - zs) * prm["mu"]
    last_zs = zsbuf[CARRY_ROWS + tc - 1:CARRY_ROWS + tc, :]
    zsbuf[CARRY_ROWS - 1:CARRY_ROWS, :] = last_zs

    @pl.when(t == n_t - 1)
    def _():
        shift_out_ref[0] = last_zs

    project_next()
    r = zm[:, 0:RW]
    k = zm[:, RW:2 * RW]
    v = zm[:, 2 * RW:3 * RW]
    wa = zm[:, 3 * RW:3 * RW + 2 * LORA]
    lw, a, kk, kmod, bonus = _rwkv_tokens(r, k, v, wa, prm, headones, project_next)

    g = _const_dot(tri_ref[...], lw, CUM_PARTS)
    gc = jnp.concatenate(
        [jnp.broadcast_to(g[(c + 1) * CHUNK - 1:(c + 1) * CHUNK, :], (CHUNK, RW)) for c in range(tc // CHUNK)],
        axis=0)
    project_next()
    eng = jnp.exp(-g)
    etail = jnp.exp(gc - g)
    b = kk * a
    la_s[...] = -kk * jnp.exp(g - lw)
    lr_s[...] = r * jnp.exp(g)
    project_next()
    rb_s[...] = b * eng
    rk_s[...] = kmod * eng
    bt_s[...] = b * etail
    kt_s[...] = kmod * etail
    v_s[...] = v
    gam_s[...] = jnp.exp(gc)
    while later_cols:
        project_next()

    n2 = 2 * CHUNK
    row = lax.broadcasted_iota(jnp.int32, (n2, n2), 0)
    col = lax.broadcasted_iota(jnp.int32, (n2, n2), 1)
    same = (row // CHUNK) == (col // CHUNK)
    strict = same & (row > col)
    incl = same & (row >= col)

    cw = prm["conv_w"]

    def conv_rows(r0):
        def run():
            rows = slice(r0, r0 + CHUNK)
            u = z_s[rows, CONV_CH:2 * CONV_CH] * z_s[rows, 2 * CONV_CH:3 * CONV_CH]
            ubuf[CARRY_ROWS + r0:CARRY_ROWS + r0 + CHUNK, :] = u
            um1 = ubuf[CARRY_ROWS - 1 + r0:CARRY_ROWS - 1 + r0 + CHUNK, :]
            um2 = ubuf[CARRY_ROWS - 2 + r0:CARRY_ROWS - 2 + r0 + CHUNK, :]
            conv = cw[0:1, :] * um2 + cw[1:2, :] * um1 + cw[2:3, :] * u
            ycat_s[rows, 0:CONV_CH] = z_s[rows, 0:CONV_CH] * conv * _silu(z_s[rows, 3 * CONV_CH:4 * CONV_CH])
        return run

    n_chunks = tc // CHUNK
    srcs = {"la": la_s, "lr": lr_s, "rb": rb_s, "rk": rk_s, "bt": bt_s, "kt": kt_s, "v": v_s}
    members = [(c, j) for c in range(n_chunks) for j in range(N_PAIRS)]

    def ld(name):
        ref = srcs[name]
        return [ref[c * CHUNK:(c + 1) * CHUNK, j * LANES:(j + 1) * LANES] for c, j in members]

    gop, y0, pop, qop = _chunk_operators(ld, strict, incl, [conv_rows(c * CHUNK) for c in range(n_chunks)])
    for (c, j), g_, y_, p_, q_ in zip(members, gop, y0, pop, qop):
        idx = c * N_PAIRS + j
        g_sc[idx] = g_.astype(BF16)
        y0_sc[idx] = y_
        p_sc[idx] = p_.astype(BF16)
        q_sc[idx] = q_

    last_u = ubuf[CARRY_ROWS + tc - 2:CARRY_ROWS + tc, :]
    ubuf[CARRY_ROWS - 2:CARRY_ROWS, :] = last_u

    @pl.when(t == n_t - 1)
    def _():
        conv_out_ref[0] = last_u

    for c in range(n_chunks):
        for j in range(N_PAIRS):
            idx = c * N_PAIRS + j
            lanes = slice(j * LANES, (j + 1) * LANES)
            s_old = s_ref[j]
            s_bf = s_old.astype(BF16)
            yst = _dot_nt(g_sc[idx], s_bf) + y0_sc[idx]
            o_s[c * CHUNK:(c + 1) * CHUNK, lanes] = yst[0:CHUNK] + yst[CHUNK:n2]
            gam = gam_s[c * CHUNK:c * CHUNK + 1, lanes]
            s_ref[j] = s_old * gam + _dot(s_bf, p_sc[idx]) + q_sc[idx]

    ycat_s[:, CONV_CH:2 * CONV_CH] = _rwkv_post(o_s[...], bonus, z_s[:, OFF_GR:OFF_GR + RW], prm, headones)

    @pl.when(t == n_t - 1)
    def _():
        for h in range(N_HEADS):
            j, i = divmod(h, 2)
            blk = s_ref[j]
            wkv_out_ref[0, h] = blk[i * HEAD_DIM:(i + 1) * HEAD_DIM, i * HEAD_DIM:(i + 1) * HEAD_DIM]

    h1 = x_ref[...] + _dot(ycat_s[...].astype(BF16), wo_ref[...])
    gate = _sigmoid(_dot(h1.astype(BF16), wg_ref[...]))
    pe = _dot(p_ref[...].astype(BF16), wp_ref[...])
    h2 = h1 + gate * pe
    ms = jnp.mean(h2 * h2, axis=-1, keepdims=True)
    y_ref[...] = h2 * lax.rsqrt(ms + RMS_EPS) * fg_ref[...]


def _prompt_layer(x2d, p2d, w, bsz, t_len, tc):
    n_t = t_len // tc
    blockones = (jnp.arange(tc)[:, None] // CHUNK) == (jnp.arange(tc)[None, :] // CHUNK)
    lower = jnp.arange(tc)[:, None] >= jnp.arange(tc)[None, :]
    tri = (blockones & lower).astype(BF16)
    kern = functools.partial(_prompt_kernel, tc=tc)
    big = lambda: pltpu.VMEM((tc, RW), F32)
    n_blk = (tc // CHUNK) * N_PAIRS
    op = lambda dt: pltpu.VMEM((n_blk, LANES, LANES), dt)
    tile = lambda width: pl.BlockSpec((tc, width), lambda b, t: (b * n_t + t, 0))
    const = lambda shape: pl.BlockSpec(shape, lambda b, t: (0, 0), pipeline_mode=pl.Buffered(1))
    return pl.pallas_call(
        kern,
        grid=(bsz, n_t),
        in_specs=[tile(D_MODEL), tile(D_PLE), const((1, D_MODEL)), const((D_MODEL, IN_W)),
                  const((D_MODEL, D_MODEL)), const((D_MODEL, D_MODEL)), const((D_PLE, D_MODEL)),
                  const((1, D_MODEL))]
        + _prm_specs(2)
        + [const((HEADSUM_W, HEADSUM_W)), const((tc, tc))],
        out_specs=[
            tile(D_MODEL),
            pl.BlockSpec((1, 2, CONV_CH), lambda b, t: (b, 0, 0)),
            pl.BlockSpec((1, 1, SHIFT_W), lambda b, t: (b, 0, 0)),
            pl.BlockSpec((1, N_HEADS, HEAD_DIM, HEAD_DIM), lambda b, t: (b, 0, 0, 0)),
        ],
        out_shape=[
            jax.ShapeDtypeStruct((bsz * t_len, D_MODEL), F32),
            jax.ShapeDtypeStruct((bsz, 2, CONV_CH), F32),
            jax.ShapeDtypeStruct((bsz, 1, SHIFT_W), F32),
            jax.ShapeDtypeStruct((bsz, N_HEADS, HEAD_DIM, HEAD_DIM), F32),
        ],
        scratch_shapes=[
            pltpu.VMEM((tc, IN_W), F32),
            pltpu.VMEM((tc, D_MODEL), F32),
            pltpu.VMEM((CARRY_ROWS + tc, CONV_CH), F32),
            pltpu.VMEM((CARRY_ROWS + tc, SHIFT_W), F32),
            pltpu.VMEM((N_PAIRS, LANES, LANES), F32),
            big(), big(), big(), big(), big(), big(), big(), big(), big(),
            op(BF16), op(F32), op(BF16), op(F32),
        ],
        compiler_params=pltpu.CompilerParams(
            dimension_semantics=("arbitrary", "arbitrary"), vmem_limit_bytes=VMEM_LIMIT),
        name="prompt_layer",
    )(x2d, p2d, w["norm_g"].reshape(1, D_MODEL), w["w_in"], w["w_out"], w["w_pg"], w["w_pp"],
      w["final_g"].reshape(1, D_MODEL), *w["prm"], w["headones"], tri)


def _sample_pre_kernel(z_ref, cb_ref, sb_ref, *refs):
    prm_refs = refs[:len(_PRM_NAMES)]
    headones_ref = refs[len(_PRM_NAMES)]
    (ya_ref, conv_out_ref, nkk_ref, w_ref, b_ref, km_ref, v_ref, r_ref, bonus_ref) = refs[len(_PRM_NAMES) + 1:]
    prm = _load_prm(prm_refs)
    headones = headones_ref[...]

    u = z_ref[:, CONV_CH:2 * CONV_CH] * z_ref[:, 2 * CONV_CH:3 * CONV_CH]
    cb0 = cb_ref[:, 0:CONV_CH]
    cb1 = cb_ref[:, CONV_CH:2 * CONV_CH]
    cw = prm["conv_w"]
    conv = cw[0:1, :] * cb0 + cw[1:2, :] * cb1 + cw[2:3, :] * u
    ya_ref[...] = z_ref[:, 0:CONV_CH] * conv * _silu(z_ref[:, 3 * CONV_CH:4 * CONV_CH])
    conv_out_ref[:, 0:CONV_CH] = cb1
    conv_out_ref[:, CONV_CH:2 * CONV_CH] = u

    zs = z_ref[:, OFF_ZS:OFF_ZS + SHIFT_W]
    zm = zs + (sb_ref[...] - zs) * prm["mu"]
    r = zm[:, 0:RW]
    k = zm[:, RW:2 * RW]
    v = zm[:, 2 * RW:3 * RW]
    wa = zm[:, 3 * RW:3 * RW + 2 * LORA]
    lw, a, kk, kmod, bonus = _rwkv_tokens(r, k, v, wa, prm, headones)
    nkk_ref[...] = -kk
    w_ref[...] = jnp.exp(lw)
    b_ref[...] = kk * a
    km_ref[...] = kmod
    v_ref[...] = v
    r_ref[...] = r
    bonus_ref[...] = bonus


def _sample_step_kernel(s_ref, nkk_ref, w_ref, b_ref, km_ref, v_ref, r_ref, s1_ref, y_ref):
    s0 = s_ref[...]
    eye = (lax.broadcasted_iota(jnp.int32, (HEAD_DIM, HEAD_DIM), 0)
           == lax.broadcasted_iota(jnp.int32, (HEAD_DIM, HEAD_DIM), 1))
    sa = jnp.sum(s0 * nkk_ref[...], axis=-1, keepdims=True)
    vcol = jnp.sum(jnp.where(eye, v_ref[...], 0.0), axis=-1, keepdims=True)
    s1 = s0 * w_ref[...] + sa * b_ref[...] + vcol * km_ref[...]
    s1_ref[...] = s1
    ycol = jnp.sum(s1 * r_ref[...], axis=-1, keepdims=True)
    y_ref[...] = jnp.sum(jnp.where(eye, ycol, 0.0), axis=-2, keepdims=True)


def _sample_post_kernel(o_ref, bonus_ref, z_ref, ya_ref, *refs):
    prm_refs = refs[:len(_PRM_NAMES)]
    headones_ref = refs[len(_PRM_NAMES)]
    ycat_ref = refs[len(_PRM_NAMES) + 1]
    prm = _load_prm(prm_refs)
    ycat_ref[:, 0:CONV_CH] = ya_ref[...]
    ycat_ref[:, CONV_CH:2 * CONV_CH] = _rwkv_post(
        o_ref[...], bonus_ref[...], z_ref[:, OFF_GR:OFF_GR + RW], prm, headones_ref[...])


def _full(shape):
    nd = len(shape)
    return pl.BlockSpec(shape, lambda *_: (0,) * nd)


def _mixer_sample(z, state_conv, state_shift, state_wkv, prm_arrays, headones, bb=8):
    n = z.shape[0]
    vec = jax.ShapeDtypeStruct((n, RW), F32)
    cb = state_conv.reshape(n, 2 * CONV_CH)
    outs = pl.pallas_call(
        _sample_pre_kernel,
        grid=(1,),
        in_specs=[_full((n, IN_W)), _full((n, 2 * CONV_CH)), _full((n, SHIFT_W))]
        + _prm_specs(1) + [_full((HEADSUM_W, HEADSUM_W))],
        out_specs=[_full((n, CONV_CH)), _full((n, 2 * CONV_CH))] + [_full((n, RW))] * 7,
        out_shape=[jax.ShapeDtypeStruct((n, CONV_CH), F32),
                   jax.ShapeDtypeStruct((n, 2 * CONV_CH), F32)] + [vec] * 7,
        compiler_params=pltpu.CompilerParams(
            dimension_semantics=("arbitrary",), vmem_limit_bytes=VMEM_LIMIT),
        name="sample_pre",
    )(z, cb, state_shift, *prm_arrays, headones)
    ya, conv_new, nkk, w, b, km, v, r, bonus = outs

    hv = lambda x: x.reshape(n, N_HEADS, 1, HEAD_DIM)
    vspec = pl.BlockSpec((bb, N_HEADS, 1, HEAD_DIM), lambda i: (i, 0, 0, 0))
    sspec = pl.BlockSpec((bb, N_HEADS, HEAD_DIM, HEAD_DIM), lambda i: (i, 0, 0, 0))
    s1, y = pl.pallas_call(
        _sample_step_kernel,
        grid=(n // bb,),
        in_specs=[sspec] + [vspec] * 6,
        out_specs=[sspec, vspec],
        out_shape=[jax.ShapeDtypeStruct((n, N_HEADS, HEAD_DIM, HEAD_DIM), F32),
                   jax.ShapeDtypeStruct((n, N_HEADS, 1, HEAD_DIM), F32)],
        compiler_params=pltpu.CompilerParams(
            dimension_semantics=("arbitrary",), vmem_limit_bytes=VMEM_LIMIT),
        name="sample_step",
    )(state_wkv, hv(nkk), hv(w), hv(b), hv(km), hv(v), hv(r))

    ycat = pl.pallas_call(
        _sample_post_kernel,
        grid=(1,),
        in_specs=[_full((n, RW)), _full((n, RW)), _full((n, IN_W)), _full((n, CONV_CH))]
        + _prm_specs(1) + [_full((HEADSUM_W, HEADSUM_W))],
        out_specs=_full((n, D_MODEL)),
        out_shape=jax.ShapeDtypeStruct((n, D_MODEL), F32),
        compiler_params=pltpu.CompilerParams(
            dimension_semantics=("arbitrary",), vmem_limit_bytes=VMEM_LIMIT),
        name="sample_post",
    )(y.reshape(n, RW), bonus, z, ya, *prm_arrays, headones)
    return ycat, conv_new.reshape(n, 2, CONV_CH), s1


def _layer_prompt(x, p, w, tc=256, tm=256):
    bsz, t_len, _ = x.shape
    x2d = x.reshape(bsz * t_len, D_MODEL)
    y, conv_new, shift_new, wkv_new = _prompt_layer(x2d, p.reshape(bsz * t_len, D_PLE), w, bsz, t_len, tc)
    return y.reshape(bsz, t_len, D_MODEL), conv_new, shift_new.reshape(bsz, SHIFT_W), wkv_new


def _layer_sample(x, p, state_conv, state_shift, state_wkv, w):
    n = x.shape[0]
    x2d = x.reshape(n, D_MODEL)
    z = _inproj(x2d, w["norm_g"], w["w_in"], n)
    ycat, conv_new, wkv_new = _mixer_sample(z, state_conv, state_shift, state_wkv, w["prm"], w["headones"])
    y = _out_stage(x2d, ycat, p.reshape(n, D_PLE), w["w_out"], w["w_pg"], w["w_pp"], w["final_g"], n)
    shift_new = z[:, OFF_ZS:OFF_ZS + SHIFT_W]
    return y.reshape(n, 1, D_MODEL), conv_new, shift_new, wkv_new


def kernel(x_prompt, x_sample, p_prompt, p_sample, state_conv, state_shift, state_wkv, norm_g, w_in, conv_w, mu_shift, w0, w_up, a0, a_up, k_k, k_a, r_k, ln_w, ln_b, w_out, w_pg, w_pp, final_g):
    depth = norm_g.shape[0]
    assert depth == 1
    i = 0
    head_id = jnp.arange(HEADSUM_W) // HEAD_DIM
    w = {
        "norm_g": norm_g[i],
        "w_in": w_in[i].astype(BF16),
        "w_out": w_out[i].astype(BF16),
        "w_pg": w_pg[i].astype(BF16),
        "w_pp": w_pp[i].astype(BF16),
        "final_g": final_g,
        "prm": _prm_arrays(conv_w[i], mu_shift[i], w0[i], w_up[i], a0[i], a_up[i], k_k[i], k_a[i],
                           r_k[i].reshape(RW), ln_w[i], ln_b[i]),
        "headones": (head_id[:, None] == head_id[None, :]).astype(BF16),
    }
    yp, cp, sp, wp = _layer_prompt(x_prompt, p_prompt[i], w)
    ys, cs, ss, ws = _layer_sample(x_sample, p_sample[i], state_conv[i], state_shift[i], state_wkv[i], w)
    return (yp, ys, cp[None], sp[None], wp[None], cs[None], ss[None], ws[None])
```

```python
import functools
import math

import jax
import jax.numpy as jnp
from jax import lax
from jax.experimental import pallas as pl
from jax.experimental.pallas import tpu as pltpu

F32 = jnp.float32
BF16 = jnp.bfloat16

D_MODEL = 1024
CONV_CH = 512
RW = 512
HEAD_DIM = 64
N_HEADS = 8
N_PAIRS = N_HEADS // 2
LORA = 64
D_PLE = 256
SHIFT_W = 3 * RW + 2 * LORA
IN_W = 4 * CONV_CH + SHIFT_W + RW
OFF_ZS = 4 * CONV_CH
OFF_GR = OFF_ZS + SHIFT_W
RMS_EPS = 1e-6
GN_EPS = 64e-5
DECAY_SCALE = math.exp(-0.5)

LANES = 128
MXU_DIM = 256
HEADSUM_W = MXU_DIM
SUM_PARTS = 1
CUM_PARTS = 2
LORA_PASSES = 1
CHUNK = 64
CARRY_ROWS = 8
VMEM_LIMIT = 56 * 1024 * 1024


def _dot(a, b):
    return jnp.dot(a, b, preferred_element_type=F32)


def _dot_nt(a, b):
    return lax.dot_general(a, b, (((1,), (1,)), ((), ())), preferred_element_type=F32)


def _dot_tn(a, b):
    return lax.dot_general(a, b, (((0,), (0,)), ((), ())), preferred_element_type=F32)


def _split(x, parts):
    out = []
    rem = x
    for i in range(parts):
        t = rem.astype(BF16)
        out.append(t)
        if i + 1 < parts:
            rem = rem - t.astype(F32)
    return out


def _const_dot(c_bf16, x, parts):
    acc = None
    for t in _split(x, parts):
        d = _dot(c_bf16, t)
        acc = d if acc is None else acc + d
    return acc


def _x_dot_const(x, c_bf16, parts):
    acc = None
    for t in _split(x, parts):
        d = _dot(t, c_bf16)
        acc = d if acc is None else acc + d
    return acc


def _head_sum(x, headones):
    w = headones.shape[0]
    halves = [_x_dot_const(x[:, i * w:(i + 1) * w], headones, SUM_PARTS) for i in range(x.shape[1] // w)]
    return jnp.concatenate(halves, axis=1)


def _mm(a, b, passes):
    if passes == 1:
        return _dot(a.astype(BF16), b.astype(BF16))
    ah, al = _split(a, 2)
    bh, bl = _split(b, 2)
    return _dot(ah, bh) + _dot(al, bh) + _dot(ah, bl)


def _sigmoid(x):
    return 1.0 / (1.0 + jnp.exp(-x))


def _silu(x):
    return x * _sigmoid(x)


def _rwkv_tokens(r, k, v, wa, prm, headones, between_steps=lambda: None):
    n = r.shape[0]
    lane = lax.broadcasted_iota(jnp.int32, (n, LANES), 1)
    th = jnp.where(lane < LORA, jnp.tanh(wa), wa)
    lora = _mm(th, prm["wlora"], LORA_PASSES)
    lw = -DECAY_SCALE * _sigmoid(prm["w0"] + lora[:, :RW])
    a = _sigmoid(prm["a0"] + lora[:, RW:])
    between_steps()
    kk = k * prm["k_k"]
    ss = _head_sum(kk * kk, headones)
    kk = kk / jnp.maximum(jnp.sqrt(ss), 1e-12)
    between_steps()
    kmod = k * (1.0 + (a - 1.0) * prm["k_a"])
    bonus = _head_sum(r * kmod * prm["r_k"], headones) * v
    between_steps()
    return lw, a, kk, kmod, bonus


def _rwkv_post(o, bonus, g_r, prm, headones):
    mean = _head_sum(o, headones) * (1.0 / HEAD_DIM)
    d = o - mean
    var = _head_sum(d * d, headones) * (1.0 / HEAD_DIM)
    on = d * lax.rsqrt(var + GN_EPS)
    on = on * prm["ln_w"] + prm["ln_b"]
    return (on + bonus) * _silu(g_r)


_PRM_NAMES = ("conv_w", "mu", "w0", "a0", "wlora", "k_k", "k_a", "r_k", "ln_w", "ln_b")


def _prm_arrays(conv_w, mu_shift, w0, w_up, a0, a_up, k_k, k_a, r_k, ln_w, ln_b):
    wlora = jnp.zeros((2 * LORA, 2 * RW), F32)
    wlora = wlora.at[:LORA, :RW].set(w_up).at[LORA:, RW:].set(a_up)
    row = lambda x: x.reshape(1, -1)
    return (conv_w, row(mu_shift), row(w0), row(a0), wlora, row(k_k), row(k_a),
            row(r_k), row(ln_w), row(ln_b))


def _prm_specs(grid_rank):
    zero = (lambda *_: (0, 0))
    shapes = ((3, CONV_CH), (1, SHIFT_W), (1, RW), (1, RW), (2 * LORA, 2 * RW),
              (1, RW), (1, RW), (1, RW), (1, RW), (1, RW))
    return [pl.BlockSpec(s, zero) for s in shapes]


def _load_prm(refs):
    return {n: r[...] for n, r in zip(_PRM_NAMES, refs)}


def _stack_heads(x):
    lane = lax.broadcasted_iota(jnp.int32, x.shape, 1)
    lo = jnp.where(lane < HEAD_DIM, x, 0.0)
    hi = jnp.where(lane >= HEAD_DIM, x, 0.0)
    return jnp.concatenate([lo, hi], axis=0)


def _bdot(a, b):
    return _dot(a.astype(BF16), b.astype(BF16))


def _unit_lower_inverse_minus_eye(mats, between_levels=lambda: None):
    n = mats[0].shape[0]
    row = lax.broadcasted_iota(jnp.int32, (n, n), 0)
    col = lax.broadcasted_iota(jnp.int32, (n, n), 1)
    xs = None
    size = 2
    while size <= CHUNK:
        half = size // 2
        sel = ((row // size) == (col // size)) & ((row % size) >= half) & ((col % size) < half)
        als = [jnp.where(sel, a, 0.0) for a in mats]
        if xs is None:
            xs = als
        else:
            ps = [al + _bdot(x, al) for x, al in zip(xs, als)]
            xs = [x + p + _bdot(p, x) for x, p in zip(xs, ps)]
            between_levels()
        size *= 2
    return xs


def _chunk_operators(ld, strict, incl, side_work=()):
    side_work = list(side_work)

    def breathe():
        if side_work:
            side_work.pop(0)()

    cat0 = lambda xs: jnp.concatenate(xs, axis=0)
    la = [_stack_heads(x) for x in ld("la")]
    lr = [_stack_heads(x) for x in ld("lr")]
    rb = [_stack_heads(x) for x in ld("rb")]
    rk = [_stack_heads(x) for x in ld("rk")]
    n2 = 2 * CHUNK
    amat = [_dot_nt(cat0([a, b]).astype(BF16), cat0([c, d]).astype(BF16))
            for a, b, c, d in zip(la, lr, rb, rk)]
    a_ab = [jnp.where(strict, m[0:n2, 0:n2], 0.0) for m in amat]
    xs = _unit_lower_inverse_minus_eye(a_ab, breathe)
    vst = [_stack_heads(x) for x in ld("v")]
    av = [_bdot(jnp.where(strict, m[0:n2, n2:2 * n2], 0.0), v) for m, v in zip(amat, vst)]
    breathe()
    uv0 = [y + _bdot(x, y) for x, y in zip(xs, av)]
    tla = [y + _bdot(x, y) for x, y in zip(xs, la)]
    breathe()
    a_rb = [jnp.where(incl, m[n2:2 * n2, 0:n2], 0.0) for m in amat]
    a_rk = [jnp.where(incl, m[n2:2 * n2, n2:2 * n2], 0.0) for m in amat]
    gop = [y + _bdot(a, t) for y, a, t in zip(lr, a_rb, tla)]
    uv = [cat0([u, v]).astype(BF16) for u, v in zip(uv0, vst)]
    y0 = [_dot(jnp.concatenate([a, b], axis=1).astype(BF16), w) for a, b, w in zip(a_rb, a_rk, uv)]
    breathe()
    bt = [_stack_heads(x) for x in ld("bt")]
    kt = [_stack_heads(x) for x in ld("kt")]
    pop = [_dot_tn(t.astype(BF16), b.astype(BF16)) for t, b in zip(tla, bt)]
    qop = [_dot_tn(w, cat0([b, k]).astype(BF16)) for w, b, k in zip(uv, bt, kt)]
    while side_work:
        breathe()
    return gop, y0, pop, qop


def _prompt_kernel(x_ref, p_ref, ng_ref, win_ref, wo_ref, wg_ref, wp_ref, fg_ref, *refs, tc):
    prm_refs = refs[:len(_PRM_NAMES)]
    headones_ref, tri_ref = refs[len(_PRM_NAMES):len(_PRM_NAMES) + 2]
    (y_ref, conv_out_ref, shift_out_ref, wkv_out_ref) = refs[len(_PRM_NAMES) + 2:len(_PRM_NAMES) + 6]
    (z_s, ycat_s, ubuf, zsbuf, s_ref, la_s, lr_s, rb_s, rk_s, bt_s, kt_s, v_s, gam_s, o_s,
     g_sc, y0_sc, p_sc, q_sc) = refs[len(_PRM_NAMES) + 6:]

    t = pl.program_id(1)
    n_t = pl.num_programs(1)
    prm = _load_prm(prm_refs)
    headones = headones_ref[...]

    @pl.when(t == 0)
    def _():
        ubuf[0:CARRY_ROWS, :] = jnp.zeros((CARRY_ROWS, CONV_CH), F32)
        zsbuf[0:CARRY_ROWS, :] = jnp.zeros((CARRY_ROWS, SHIFT_W), F32)
        s_ref[...] = jnp.zeros(s_ref.shape, F32)

    x = x_ref[...]
    xn = (x * lax.rsqrt(jnp.mean(x * x, axis=-1, keepdims=True) + RMS_EPS) * ng_ref[...]).astype(BF16)

    def project(c0, c1):
        z_s[:, c0:c1] = _dot(xn, win_ref[:, c0:c1])

    later_cols = [(c0, c0 + CONV_CH) for c0 in (OFF_GR, 0, CONV_CH, 2 * CONV_CH, 3 * CONV_CH)]

    def project_next():
        if later_cols:
            project(*later_cols.pop(0))

    project(OFF_ZS, OFF_ZS + SHIFT_W)

    zs = z_s[:, OFF_ZS:OFF_ZS + SHIFT_W]
    zsbuf[CARRY_ROWS:CARRY_ROWS + tc, :] = zs
    zprev = zsbuf[CARRY_ROWS - 1:CARRY_ROWS - 1 + tc, :]
    zm = zs + (zprev - zs) * prm["mu"]
    last_zs = zsbuf[CARRY_ROWS + tc - 1:CARRY_ROWS + tc, :]
    zsbuf[CARRY_ROWS - 1:CARRY_ROWS, :] = last_zs

    @pl.when(t == n_t - 1)
    def _():
        shift_out_ref[0] = last_zs

    project_next()
    r = zm[:, 0:RW]
    k = zm[:, RW:2 * RW]
    v = zm[:, 2 * RW:3 * RW]
    wa = zm[:, 3 * RW:3 * RW + 2 * LORA]
    lw, a, kk, kmod, bonus = _rwkv_tokens(r, k, v, wa, prm, headones, project_next)

    g = _const_dot(tri_ref[...], lw, CUM_PARTS)
    gc = jnp.concatenate(
        [jnp.broadcast_to(g[(c + 1) * CHUNK - 1:(c + 1) * CHUNK, :], (CHUNK, RW)) for c in range(tc // CHUNK)],
        axis=0)
    project_next()
    eng = jnp.exp(-g)
    etail = jnp.exp(gc - g)
    b = kk * a
    la_s[...] = -kk * jnp.exp(g - lw)
    lr_s[...] = r * jnp.exp(g)
    project_next()
    rb_s[...] = b * eng
    rk_s[...] = kmod * eng
    bt_s[...] = b * etail
    kt_s[...] = kmod * etail
    v_s[...] = v
    gam_s[...] = jnp.exp(gc)
    while later_cols:
        project_next()

    n2 = 2 * CHUNK
    row = lax.broadcasted_iota(jnp.int32, (n2, n2), 0)
    col = lax.broadcasted_iota(jnp.int32, (n2, n2), 1)
    same = (row // CHUNK) == (col // CHUNK)
    strict = same & (row > col)
    incl = same & (row >= col)

    cw = prm["conv_w"]

    def conv_rows(r0):
        def run():
            rows = slice(r0, r0 + CHUNK)
            u = z_s[rows, CONV_CH:2 * CONV_CH] * z_s[rows, 2 * CONV_CH:3 * CONV_CH]
            ubuf[CARRY_ROWS + r0:CARRY_ROWS + r0 + CHUNK, :] = u
            um1 = ubuf[CARRY_ROWS - 1 + r0:CARRY_ROWS - 1 + r0 + CHUNK, :]
            um2 = ubuf[CARRY_ROWS - 2 + r0:CARRY_ROWS - 2 + r0 + CHUNK, :]
            conv = cw[0:1, :] * um2 + cw[1:2, :] * um1 + cw[2:3, :] * u
            ycat_s[rows, 0:CONV_CH] = z_s[rows, 0:CONV_CH] * conv * _silu(z_s[rows, 3 * CONV_CH:4 * CONV_CH])
        return run

    n_chunks = tc // CHUNK
    srcs = {"la": la_s, "lr": lr_s, "rb": rb_s, "rk": rk_s, "bt": bt_s, "kt": kt_s, "v": v_s}
    members = [(c, j) for c in range(n_chunks) for j in range(N_PAIRS)]

    def ld(name):
        ref = srcs[name]
        return [ref[c * CHUNK:(c + 1) * CHUNK, j * LANES:(j + 1) * LANES] for c, j in members]

    gop, y0, pop, qop = _chunk_operators(ld, strict, incl, [conv_rows(c * CHUNK) for c in range(n_chunks)])
    for (c, j), g_, y_, p_, q_ in zip(members, gop, y0, pop, qop):
        idx = c * N_PAIRS + j
        g_sc[idx] = g_.astype(BF16)
        y0_sc[idx] = y_
        p_sc[idx] = p_.astype(BF16)
        q_sc[idx] = q_

    last_u = ubuf[CARRY_ROWS + tc - 2:CARRY_ROWS + tc, :]
    ubuf[CARRY_ROWS - 2:CARRY_ROWS, :] = last_u

    @pl.when(t == n_t - 1)
    def _():
        conv_out_ref[0] = last_u

    for c in range(n_chunks):
        for j in range(N_PAIRS):
            idx = c * N_PAIRS + j
            lanes = slice(j * LANES, (j + 1) * LANES)
            s_old = s_ref[j]
            s_bf = s_old.astype(BF16)
            yst = _dot_nt(g_sc[idx], s_bf) + y0_sc[idx]
            o_s[c * CHUNK:(c + 1) * CHUNK, lanes] = yst[0:CHUNK] + yst[CHUNK:n2]
            gam = gam_s[c * CHUNK:c * CHUNK + 1, lanes]
            s_ref[j] = s_old * gam + _dot(s_bf, p_sc[idx]) + q_sc[idx]

    ycat_s[:, CONV_CH:2 * CONV_CH] = _rwkv_post(o_s[...], bonus, z_s[:, OFF_GR:OFF_GR + RW], prm, headones)

    @pl.when(t == n_t - 1)
    def _():
        for h in range(N_HEADS):
            j, i = divmod(h, 2)
            blk = s_ref[j]
            wkv_out_ref[0, h] = blk[i * HEAD_DIM:(i + 1) * HEAD_DIM, i * HEAD_DIM:(i + 1) * HEAD_DIM]

    h1 = x_ref[...] + _dot(ycat_s[...].astype(BF16), wo_ref[...])
    gate = _sigmoid(_dot(h1.astype(BF16), wg_ref[...]))
    pe = _dot(p_ref[...].astype(BF16), wp_ref[...])
    h2 = h1 + gate * pe
    ms = jnp.mean(h2 * h2, axis=-1, keepdims=True)
    y_ref[...] = h2 * lax.rsqrt(ms + RMS_EPS) * fg_ref[...]


def _prompt_layer(x2d, p2d, w, bsz, t_len, tc):
    n_t = t_len // tc
    blockones = (jnp.arange(tc)[:, None] // CHUNK) == (jnp.arange(tc)[None, :] // CHUNK)
    lower = jnp.arange(tc)[:, None] >= jnp.arange(tc)[None, :]
    tri = (blockones & lower).astype(BF16)
    kern = functools.partial(_prompt_kernel, tc=tc)
    big = lambda: pltpu.VMEM((tc, RW), F32)
    n_blk = (tc // CHUNK) * N_PAIRS
    op = lambda dt: pltpu.VMEM((n_blk, LANES, LANES), dt)
    tile = lambda width: pl.BlockSpec((tc, width), lambda b, t: (b * n_t + t, 0))
    const = lambda shape: pl.BlockSpec(shape, lambda b, t: (0, 0), pipeline_mode=pl.Buffered(1))
    return pl.pallas_call(
        kern,
        grid=(bsz, n_t),
        in_specs=[tile(D_MODEL), tile(D_PLE), const((1, D_MODEL)), const((D_MODEL, IN_W)),
                  const((D_MODEL, D_MODEL)), const((D_MODEL, D_MODEL)), const((D_PLE, D_MODEL)),
                  const((1, D_MODEL))]
        + _prm_specs(2)
        + [const((HEADSUM_W, HEADSUM_W)), const((tc, tc))],
        out_specs=[
            tile(D_MODEL),
            pl.BlockSpec((1, 2, CONV_CH), lambda b, t: (b, 0, 0)),
            pl.BlockSpec((1, 1, SHIFT_W), lambda b, t: (b, 0, 0)),
            pl.BlockSpec((1, N_HEADS, HEAD_DIM, HEAD_DIM), lambda b, t: (b, 0, 0, 0)),
        ],
        out_shape=[
            jax.ShapeDtypeStruct((bsz * t_len, D_MODEL), F32),
            jax.ShapeDtypeStruct((bsz, 2, CONV_CH), F32),
            jax.ShapeDtypeStruct((bsz, 1, SHIFT_W), F32),
            jax.ShapeDtypeStruct((bsz, N_HEADS, HEAD_DIM, HEAD_DIM), F32),
        ],
        scratch_shapes=[
            pltpu.VMEM((tc, IN_W), F32),
            pltpu.VMEM((tc, D_MODEL), F32),
            pltpu.VMEM((CARRY_ROWS + tc, CONV_CH), F32),
            pltpu.VMEM((CARRY_ROWS + tc, SHIFT_W), F32),
            pltpu.VMEM((N_PAIRS, LANES, LANES), F32),
            big(), big(), big(), big(), big(), big(), big(), big(), big(),
            op(BF16), op(F32), op(BF16), op(F32),
        ],
        compiler_params=pltpu.CompilerParams(
            dimension_semantics=("arbitrary", "arbitrary"), vmem_limit_bytes=VMEM_LIMIT),
        name="prompt_layer",
    )(x2d, p2d, w["norm_g"].reshape(1, D_MODEL), w["w_in"], w["w_out"], w["w_pg"], w["w_pp"],
      w["final_g"].reshape(1, D_MODEL), *w["prm"], w["headones"], tri)


def _sample_front_kernel(x_ref, ng_ref, win_ref, cb_ref, sb_ref, *refs):
    prm_refs = refs[:len(_PRM_NAMES)]
    headones_ref = refs[len(_PRM_NAMES)]
    (ya_ref, conv_out_ref, shift_out_ref, gr_ref, bonus_ref,
     nkk_ref, w_ref, b_ref, km_ref, v_ref, r_ref) = refs[len(_PRM_NAMES) + 1:]
    prm = _load_prm(prm_refs)
    headones = headones_ref[...]

    x = x_ref[...]
    xn = (x * lax.rsqrt(jnp.mean(x * x, axis=-1, keepdims=True) + RMS_EPS) * ng_ref[...]).astype(BF16)
    z = _dot(xn, win_ref[...])

    u = z[:, CONV_CH:2 * CONV_CH] * z[:, 2 * CONV_CH:3 * CONV_CH]
    cb0 = cb_ref[:, 0:CONV_CH]
    cb1 = cb_ref[:, CONV_CH:2 * CONV_CH]
    cw = prm["conv_w"]
    conv = cw[0:1, :] * cb0 + cw[1:2, :] * cb1 + cw[2:3, :] * u
    ya_ref[...] = z[:, 0:CONV_CH] * conv * _silu(z[:, 3 * CONV_CH:4 * CONV_CH])
    conv_out_ref[:, 0:CONV_CH] = cb1
    conv_out_ref[:, CONV_CH:2 * CONV_CH] = u

    zs = z[:, OFF_ZS:OFF_ZS + SHIFT_W]
    shift_out_ref[...] = zs
    gr_ref[...] = z[:, OFF_GR:OFF_GR + RW]
    zm = zs + (sb_ref[...] - zs) * prm["mu"]
    r = zm[:, 0:RW]
    k = zm[:, RW:2 * RW]
    v = zm[:, 2 * RW:3 * RW]
    wa = zm[:, 3 * RW:3 * RW + 2 * LORA]
    lw, a, kk, kmod, bonus = _rwkv_tokens(r, k, v, wa, prm, headones)
    bonus_ref[...] = bonus
    nkk_ref[...] = (-kk).T
    w_ref[...] = jnp.exp(lw).T
    b_ref[...] = (kk * a).T
    km_ref[...] = kmod.T
    v_ref[...] = v.T
    r_ref[...] = r.T


def _sample_step_kernel(s_ref, nkk_ref, w_ref, b_ref, km_ref, v_ref, r_ref, s1_ref, y_ref):
    nkk = nkk_ref[0]
    w = w_ref[0]
    b = b_ref[0]
    km = km_ref[0]
    r = r_ref[0]
    for vi in range(HEAD_DIM):
        s0 = s_ref[0, vi]
        sa = jnp.sum(s0 * nkk, axis=0, keepdims=True)
        s1 = s0 * w + sa * b + v_ref[0, vi:vi + 1, :] * km
        s1_ref[0, vi] = s1
        y_ref[0, vi:vi + 1, :] = jnp.sum(s1 * r, axis=0, keepdims=True)


def _sample_back_kernel(o_ref, bonus_ref, gr_ref, ya_ref, x_ref, p_ref, wo_ref, wg_ref, wp_ref, fg_ref, *refs):
    prm_refs = refs[:len(_PRM_NAMES)]
    headones_ref = refs[len(_PRM_NAMES)]
    y_ref = refs[len(_PRM_NAMES) + 1]
    prm = _load_prm(prm_refs)
    y_r = _rwkv_post(o_ref[...].T, bonus_ref[...], gr_ref[...], prm, headones_ref[...])
    ycat = jnp.concatenate([ya_ref[...], y_r], axis=1).astype(BF16)
    h1 = x_ref[...] + _dot(ycat, wo_ref[...])
    gate = _sigmoid(_dot(h1.astype(BF16), wg_ref[...]))
    pe = _dot(p_ref[...].astype(BF16), wp_ref[...])
    h2 = h1 + gate * pe
    ms = jnp.mean(h2 * h2, axis=-1, keepdims=True)
    y_ref[...] = h2 * lax.rsqrt(ms + RMS_EPS) * fg_ref[...]


def _full(shape):
    nd = len(shape)
    return pl.BlockSpec(shape, lambda *_: (0,) * nd)


def _sample_layer(x2d, p2d, state_conv, state_shift, state_hvkb, w):
    n = x2d.shape[0]
    params = pltpu.CompilerParams(dimension_semantics=("arbitrary",), vmem_limit_bytes=VMEM_LIMIT)
    row = jax.ShapeDtypeStruct((n, RW), F32)
    col = jax.ShapeDtypeStruct((RW, n), F32)
    cb = state_conv.reshape(n, 2 * CONV_CH)
    front = pl.pallas_call(
        _sample_front_kernel,
        grid=(1,),
        in_specs=[_full((n, D_MODEL)), _full((1, D_MODEL)), _full((D_MODEL, IN_W)),
                  _full((n, 2 * CONV_CH)), _full((n, SHIFT_W))]
        + _prm_specs(1) + [_full((HEADSUM_W, HEADSUM_W))],
        out_specs=[_full((n, CONV_CH)), _full((n, 2 * CONV_CH)), _full((n, SHIFT_W)),
                   _full((n, RW)), _full((n, RW))] + [_full((RW, n))] * 6,
        out_shape=[jax.ShapeDtypeStruct((n, CONV_CH), F32), jax.ShapeDtypeStruct((n, 2 * CONV_CH), F32),
                   jax.ShapeDtypeStruct((n, SHIFT_W), F32), row, row] + [col] * 6,
        compiler_params=params,
        name="sample_front",
    )(x2d, w["norm_g"].reshape(1, D_MODEL), w["w_in"], cb, state_shift, *w["prm"], w["headones"])
    ya, conv_new, shift_new, g_r, bonus, nkk, dec, b, km, v, r = front

    hk = lambda a: a.reshape(N_HEADS, HEAD_DIM, n)
    vspec = pl.BlockSpec((1, HEAD_DIM, n), lambda h: (h, 0, 0))
    sspec = pl.BlockSpec((1, HEAD_DIM, HEAD_DIM, n), lambda h: (h, 0, 0, 0))
    s1, y = pl.pallas_call(
        _sample_step_kernel,
        grid=(N_HEADS,),
        in_specs=[sspec] + [vspec] * 6,
        out_specs=[sspec, vspec],
        out_shape=[jax.ShapeDtypeStruct((N_HEADS, HEAD_DIM, HEAD_DIM, n), F32),
                   jax.ShapeDtypeStruct((N_HEADS, HEAD_DIM, n), F32)],
        compiler_params=params,
        name="sample_step",
    )(state_hvkb, hk(nkk), hk(dec), hk(b), hk(km), hk(v), hk(r))

    y_out = pl.pallas_call(
        _sample_back_kernel,
        grid=(1,),
        in_specs=[_full((RW, n)), _full((n, RW)), _full((n, RW)), _full((n, CONV_CH)),
                  _full((n, D_MODEL)), _full((n, D_PLE)), _full((D_MODEL, D_MODEL)),
                  _full((D_MODEL, D_MODEL)), _full((D_PLE, D_MODEL)), _full((1, D_MODEL))]
        + _prm_specs(1) + [_full((HEADSUM_W, HEADSUM_W))],
        out_specs=_full((n, D_MODEL)),
        out_shape=jax.ShapeDtypeStruct((n, D_MODEL), F32),
        compiler_params=params,
        name="sample_back",
    )(y.reshape(RW, n), bonus, g_r, ya, x2d, p2d, w["w_out"], w["w_pg"], w["w_pp"],
      w["final_g"].reshape(1, D_MODEL), *w["prm"], w["headones"])
    return y_out, conv_new.reshape(n, 2, CONV_CH), shift_new, s1


def _layer_prompt(x, p, w, tc=256, tm=256):
    bsz, t_len, _ = x.shape
    x2d = x.reshape(bsz * t_len, D_MODEL)
    y, conv_new, shift_new, wkv_new = _prompt_layer(x2d, p.reshape(bsz * t_len, D_PLE), w, bsz, t_len, tc)
    return y.reshape(bsz, t_len, D_MODEL), conv_new, shift_new.reshape(bsz, SHIFT_W), wkv_new


def _layer_sample(x, p, state_conv, state_shift, state_wkv, w):
    n = x.shape[0]
    s_hvkb = jnp.transpose(state_wkv, (1, 2, 3, 0))
    y, conv_new, shift_new, s1 = _sample_layer(x.reshape(n, D_MODEL), p.reshape(n, D_PLE), state_conv,
                                               state_shift, s_hvkb, w)
    return y.reshape(n, 1, D_MODEL), conv_new, shift_new, jnp.transpose(s1, (3, 0, 1, 2))


def kernel(x_prompt, x_sample, p_prompt, p_sample, state_conv, state_shift, state_wkv, norm_g, w_in, conv_w, mu_shift, w0, w_up, a0, a_up, k_k, k_a, r_k, ln_w, ln_b, w_out, w_pg, w_pp, final_g):
    depth = norm_g.shape[0]
    assert depth == 1
    i = 0
    head_id = jnp.arange(HEADSUM_W) // HEAD_DIM
    w = {
        "norm_g": norm_g[i],
        "w_in": w_in[i].astype(BF16),
        "w_out": w_out[i].astype(BF16),
        "w_pg": w_pg[i].astype(BF16),
        "w_pp": w_pp[i].astype(BF16),
        "final_g": final_g,
        "prm": _prm_arrays(conv_w[i], mu_shift[i], w0[i], w_up[i], a0[i], a_up[i], k_k[i], k_a[i],
                           r_k[i].reshape(RW), ln_w[i], ln_b[i]),
        "headones": (head_id[:, None] == head_id[None, :]).astype(BF16),
    }
    yp, cp, sp, wp = _layer_prompt(x_prompt, p_prompt[i], w)
    ys, cs, ss, ws = _layer_sample(x_sample, p_sample[i], state_conv[i], state_shift[i], state_wkv[i], w)
    return (yp, ys, cp[None], sp[None], wp[None], cs[None], ss[None], ws[None])
```

```python
import functools
import math

import jax
import jax.numpy as jnp
from jax import lax
from jax.experimental import pallas as pl
from jax.experimental.pallas import tpu as pltpu

F32 = jnp.float32
BF16 = jnp.bfloat16

D_MODEL = 1024
CONV_CH = 512
RW = 512
HEAD_DIM = 64
N_HEADS = 8
N_PAIRS = N_HEADS // 2
LORA = 64
D_PLE = 256
SHIFT_W = 3 * RW + 2 * LORA
IN_W = 4 * CONV_CH + SHIFT_W + RW
OFF_ZS = 4 * CONV_CH
OFF_GR = OFF_ZS + SHIFT_W
RMS_EPS = 1e-6
GN_EPS = 64e-5
DECAY_SCALE = math.exp(-0.5)

LANES = 128
MXU_DIM = 256
HEADSUM_W = MXU_DIM
SUM_PARTS = 1
CUM_PARTS = 2
LORA_PASSES = 1
CHUNK = 64
CARRY_ROWS = 8
VMEM_LIMIT = 56 * 1024 * 1024


def _dot(a, b):
    return jnp.dot(a, b, preferred_element_type=F32)


def _dot_nt(a, b):
    return lax.dot_general(a, b, (((1,), (1,)), ((), ())), preferred_element_type=F32)


def _dot_tn(a, b):
    return lax.dot_general(a, b, (((0,), (0,)), ((), ())), preferred_element_type=F32)


def _split(x, parts):
    out = []
    rem = x
    for i in range(parts):
        t = rem.astype(BF16)
        out.append(t)
        if i + 1 < parts:
            rem = rem - t.astype(F32)
    return out


def _const_dot(c_bf16, x, parts):
    acc = None
    for t in _split(x, parts):
        d = _dot(c_bf16, t)
        acc = d if acc is None else acc + d
    return acc


def _x_dot_const(x, c_bf16, parts):
    acc = None
    for t in _split(x, parts):
        d = _dot(t, c_bf16)
        acc = d if acc is None else acc + d
    return acc


def _head_sum(x, headones):
    w = headones.shape[0]
    halves = [_x_dot_const(x[:, i * w:(i + 1) * w], headones, SUM_PARTS) for i in range(x.shape[1] // w)]
    return jnp.concatenate(halves, axis=1)


def _mm(a, b, passes):
    if passes == 1:
        return _dot(a.astype(BF16), b.astype(BF16))
    ah, al = _split(a, 2)
    bh, bl = _split(b, 2)
    return _dot(ah, bh) + _dot(al, bh) + _dot(ah, bl)


def _sigmoid(x):
    return 1.0 / (1.0 + jnp.exp(-x))


def _silu(x):
    return x * _sigmoid(x)


def _rwkv_tokens(r, k, v, wa, prm, headones, between_steps=lambda: None):
    n = r.shape[0]
    lane = lax.broadcasted_iota(jnp.int32, (n, LANES), 1)
    th = jnp.where(lane < LORA, jnp.tanh(wa), wa)
    lora = _mm(th, prm["wlora"], LORA_PASSES)
    lw = -DECAY_SCALE * _sigmoid(prm["w0"] + lora[:, :RW])
    a = _sigmoid(prm["a0"] + lora[:, RW:])
    between_steps()
    kk = k * prm["k_k"]
    ss = _head_sum(kk * kk, headones)
    kk = kk / jnp.maximum(jnp.sqrt(ss), 1e-12)
    between_steps()
    kmod = k * (1.0 + (a - 1.0) * prm["k_a"])
    bonus = _head_sum(r * kmod * prm["r_k"], headones) * v
    between_steps()
    return lw, a, kk, kmod, bonus


def _rwkv_post(o, bonus, g_r, prm, headones):
    mean = _head_sum(o, headones) * (1.0 / HEAD_DIM)
    d = o - mean
    var = _head_sum(d * d, headones) * (1.0 / HEAD_DIM)
    on = d * lax.rsqrt(var + GN_EPS)
    on = on * prm["ln_w"] + prm["ln_b"]
    return (on + bonus) * _silu(g_r)


_PRM_NAMES = ("conv_w", "mu", "w0", "a0", "wlora", "k_k", "k_a", "r_k", "ln_w", "ln_b")


def _prm_arrays(conv_w, mu_shift, w0, w_up, a0, a_up, k_k, k_a, r_k, ln_w, ln_b):
    wlora = jnp.zeros((2 * LORA, 2 * RW), F32)
    wlora = wlora.at[:LORA, :RW].set(w_up).at[LORA:, RW:].set(a_up)
    row = lambda x: x.reshape(1, -1)
    return (conv_w, row(mu_shift), row(w0), row(a0), wlora, row(k_k), row(k_a),
            row(r_k), row(ln_w), row(ln_b))


def _prm_specs(grid_rank):
    zero = (lambda *_: (0, 0))
    shapes = ((3, CONV_CH), (1, SHIFT_W), (1, RW), (1, RW), (2 * LORA, 2 * RW),
              (1, RW), (1, RW), (1, RW), (1, RW), (1, RW))
    return [pl.BlockSpec(s, zero) for s in shapes]


def _load_prm(refs):
    return {n: r[...] for n, r in zip(_PRM_NAMES, refs)}


def _stack_heads(x):
    lane = lax.broadcasted_iota(jnp.int32, x.shape, 1)
    lo = jnp.where(lane < HEAD_DIM, x, 0.0)
    hi = jnp.where(lane >= HEAD_DIM, x, 0.0)
    return jnp.concatenate([lo, hi], axis=0)


def _bdot(a, b):
    return _dot(a.astype(BF16), b.astype(BF16))


def _unit_lower_inverse_minus_eye(mats, between_levels=lambda: None):
    n = mats[0].shape[0]
    row = lax.broadcasted_iota(jnp.int32, (n, n), 0)
    col = lax.broadcasted_iota(jnp.int32, (n, n), 1)
    xs = None
    size = 2
    while size <= CHUNK:
        half = size // 2
        sel = ((row // size) == (col // size)) & ((row % size) >= half) & ((col % size) < half)
        als = [jnp.where(sel, a, 0.0) for a in mats]
        if xs is None:
            xs = als
        else:
            ps = [al + _bdot(x, al) for x, al in zip(xs, als)]
            xs = [x + p + _bdot(p, x) for x, p in zip(xs, ps)]
            between_levels()
        size *= 2
    return xs


def _chunk_operators(ld, strict, incl, side_work=()):
    side_work = list(side_work)

    def breathe():
        if side_work:
            side_work.pop(0)()

    cat0 = lambda xs: jnp.concatenate(xs, axis=0)
    la = [_stack_heads(x) for x in ld("la")]
    lr = [_stack_heads(x) for x in ld("lr")]
    rb = [_stack_heads(x) for x in ld("rb")]
    rk = [_stack_heads(x) for x in ld("rk")]
    n2 = 2 * CHUNK
    amat = [_dot_nt(cat0([a, b]).astype(BF16), cat0([c, d]).astype(BF16))
            for a, b, c, d in zip(la, lr, rb, rk)]
    a_ab = [jnp.where(strict, m[0:n2, 0:n2], 0.0) for m in amat]
    xs = _unit_lower_inverse_minus_eye(a_ab, breathe)
    vst = [_stack_heads(x) for x in ld("v")]
    av = [_bdot(jnp.where(strict, m[0:n2, n2:2 * n2], 0.0), v) for m, v in zip(amat, vst)]
    breathe()
    uv0 = [y + _bdot(x, y) for x, y in zip(xs, av)]
    tla = [y + _bdot(x, y) for x, y in zip(xs, la)]
    breathe()
    a_rb = [jnp.where(incl, m[n2:2 * n2, 0:n2], 0.0) for m in amat]
    a_rk = [jnp.where(incl, m[n2:2 * n2, n2:2 * n2], 0.0) for m in amat]
    gop = [y + _bdot(a, t) for y, a, t in zip(lr, a_rb, tla)]
    uv = [cat0([u, v]).astype(BF16) for u, v in zip(uv0, vst)]
    y0 = [_dot(jnp.concatenate([a, b], axis=1).astype(BF16), w) for a, b, w in zip(a_rb, a_rk, uv)]
    breathe()
    bt = [_stack_heads(x) for x in ld("bt")]
    kt = [_stack_heads(x) for x in ld("kt")]
    pop = [_dot_tn(t.astype(BF16), b.astype(BF16)) for t, b in zip(tla, bt)]
    qop = [_dot_tn(w, cat0([b, k]).astype(BF16)) for w, b, k in zip(uv, bt, kt)]
    while side_work:
        breathe()
    return gop, y0, pop, qop


def _prompt_kernel(xh_ref, xt_ref, pt_ref, ng_ref, win_ref, wo_ref, wg_ref, wp_ref, fg_ref, *refs,
                   tc, n_t, n_tiles):
    prm_refs = refs[:len(_PRM_NAMES)]
    headones_ref, tri_ref = refs[len(_PRM_NAMES):len(_PRM_NAMES) + 2]
    (y_ref, conv_out_ref, shift_out_ref, wkv_out_ref) = refs[len(_PRM_NAMES) + 2:len(_PRM_NAMES) + 6]
    (z_s, ycat_s, ubuf, zsbuf, s_ref, la_s, lr_s, rb_s, rk_s, bt_s, kt_s, v_s, gam_s, o_s,
     g_sc, y0_sc, p_sc, q_sc, bonus_p, gr_p) = refs[len(_PRM_NAMES) + 6:]

    s = pl.program_id(0)
    head_t = lax.rem(s, jnp.int32(n_t))
    tail_t = lax.rem(s + (n_t - 1), jnp.int32(n_t))
    prm = _load_prm(prm_refs)
    headones = headones_ref[...]
    n2 = 2 * CHUNK
    n_chunks = tc // CHUNK

    @pl.when(s == 0)
    def _():
        for ref in (g_sc, y0_sc, p_sc, q_sc, gam_s, bonus_p, gr_p, ycat_s, s_ref):
            ref[...] = jnp.zeros(ref.shape, ref.dtype)

    @pl.when(head_t == 0)
    def _():
        ubuf[0:CARRY_ROWS, :] = jnp.zeros((CARRY_ROWS, CONV_CH), F32)
        zsbuf[0:CARRY_ROWS, :] = jnp.zeros((CARRY_ROWS, SHIFT_W), F32)

    @pl.when(tail_t == 0)
    def _():
        s_ref[...] = jnp.zeros(s_ref.shape, F32)

    tail = {}

    def tail_state_pass(c):
        def run():
            for j in range(N_PAIRS):
                idx = c * N_PAIRS + j
                lanes = slice(j * LANES, (j + 1) * LANES)
                s_old = s_ref[j]
                s_bf = s_old.astype(BF16)
                yst = _dot_nt(g_sc[idx], s_bf) + y0_sc[idx]
                o_s[c * CHUNK:(c + 1) * CHUNK, lanes] = yst[0:CHUNK] + yst[CHUNK:n2]
                gam = gam_s[c * CHUNK:c * CHUNK + 1, lanes]
                s_ref[j] = s_old * gam + _dot(s_bf, p_sc[idx]) + q_sc[idx]
        return run

    def tail_post():
        ycat_s[:, CONV_CH:2 * CONV_CH] = _rwkv_post(o_s[...], bonus_p[...], gr_p[...], prm, headones)

    def tail_out_proj():
        tail["h1"] = xt_ref[...] + _dot(ycat_s[...].astype(BF16), wo_ref[...])

    def tail_gate():
        tail["gate"] = _sigmoid(_dot(tail["h1"].astype(BF16), wg_ref[...]))
        tail["pe"] = _dot(pt_ref[...].astype(BF16), wp_ref[...])

    def tail_norm():
        h2 = tail["h1"] + tail["gate"] * tail["pe"]
        ms = jnp.mean(h2 * h2, axis=-1, keepdims=True)
        y_ref[...] = h2 * lax.rsqrt(ms + RMS_EPS) * fg_ref[...]

    x = xh_ref[...]
    xn = (x * lax.rsqrt(jnp.mean(x * x, axis=-1, keepdims=True) + RMS_EPS) * ng_ref[...]).astype(BF16)

    def project(c0):
        def run():
            z_s[:, c0:c0 + CONV_CH] = _dot(xn, win_ref[:, c0:c0 + CONV_CH])
        return run

    side_queue = [tail_state_pass(0), project(OFF_GR), tail_state_pass(1), project(0),
                  tail_state_pass(2), project(CONV_CH), tail_state_pass(3), project(2 * CONV_CH),
                  tail_post, project(3 * CONV_CH), tail_out_proj, tail_gate, tail_norm]
    assert n_chunks == 4

    def project_next(count=2):
        for _ in range(count):
            if side_queue:
                side_queue.pop(0)()

    z_s[:, OFF_ZS:OFF_ZS + SHIFT_W] = _dot(xn, win_ref[:, OFF_ZS:OFF_ZS + SHIFT_W])

    zs = z_s[:, OFF_ZS:OFF_ZS + SHIFT_W]
    zsbuf[CARRY_ROWS:CARRY_ROWS + tc, :] = zs
    zprev = zsbuf[CARRY_ROWS - 1:CARRY_ROWS - 1 + tc, :]
    zm = zs + (zprev - zs) * prm["mu"]
    zsbuf[CARRY_ROWS - 1:CARRY_ROWS, :] = zsbuf[CARRY_ROWS + tc - 1:CARRY_ROWS + tc, :]

    project_next()
    r = zm[:, 0:RW]
    k = zm[:, RW:2 * RW]
    v = zm[:, 2 * RW:3 * RW]
    wa = zm[:, 3 * RW:3 * RW + 2 * LORA]
    lw, a, kk, kmod, bonus = _rwkv_tokens(r, k, v, wa, prm, headones, project_next)

    g = _const_dot(tri_ref[...], lw, CUM_PARTS)
    gc = jnp.concatenate(
        [jnp.broadcast_to(g[(c + 1) * CHUNK - 1:(c + 1) * CHUNK, :], (CHUNK, RW)) for c in range(tc // CHUNK)],
        axis=0)
    project_next()
    eng = jnp.exp(-g)
    etail = jnp.exp(gc - g)
    b = kk * a
    la_s[...] = -kk * jnp.exp(g - lw)
    lr_s[...] = r * jnp.exp(g)
    project_next()
    rb_s[...] = b * eng
    rk_s[...] = kmod * eng
    bt_s[...] = b * etail
    kt_s[...] = kmod * etail
    v_s[...] = v
    while side_queue:
        project_next()
    gam_s[...] = jnp.exp(gc)

    row = lax.broadcasted_iota(jnp.int32, (n2, n2), 0)
    col = lax.broadcasted_iota(jnp.int32, (n2, n2), 1)
    same = (row // CHUNK) == (col // CHUNK)
    strict = same & (row > col)
    incl = same & (row >= col)

    cw = prm["conv_w"]

    def conv_rows(r0):
        def run():
            rows = slice(r0, r0 + CHUNK)
            u = z_s[rows, CONV_CH:2 * CONV_CH] * z_s[rows, 2 * CONV_CH:3 * CONV_CH]
            ubuf[CARRY_ROWS + r0:CARRY_ROWS + r0 + CHUNK, :] = u
            um1 = ubuf[CARRY_ROWS - 1 + r0:CARRY_ROWS - 1 + r0 + CHUNK, :]
            um2 = ubuf[CARRY_ROWS - 2 + r0:CARRY_ROWS - 2 + r0 + CHUNK, :]
            conv = cw[0:1, :] * um2 + cw[1:2, :] * um1 + cw[2:3, :] * u
            ycat_s[rows, 0:CONV_CH] = z_s[rows, 0:CONV_CH] * conv * _silu(z_s[rows, 3 * CONV_CH:4 * CONV_CH])
        return run

    srcs = {"la": la_s, "lr": lr_s, "rb": rb_s, "rk": rk_s, "bt": bt_s, "kt": kt_s, "v": v_s}
    members = [(c, j) for c in range(n_chunks) for j in range(N_PAIRS)]

    def ld(name):
        ref = srcs[name]
        return [ref[c * CHUNK:(c + 1) * CHUNK, j * LANES:(j + 1) * LANES] for c, j in members]

    gop, y0, pop, qop = _chunk_operators(ld, strict, incl, [conv_rows(c * CHUNK) for c in range(n_chunks)])
    for (c, j), g_, y_, p_, q_ in zip(members, gop, y0, pop, qop):
        idx = c * N_PAIRS + j
        g_sc[idx] = g_.astype(BF16)
        y0_sc[idx] = y_
        p_sc[idx] = p_.astype(BF16)
        q_sc[idx] = q_

    ubuf[CARRY_ROWS - 2:CARRY_ROWS, :] = ubuf[CARRY_ROWS + tc - 2:CARRY_ROWS + tc, :]

    bonus_p[...] = bonus
    gr_p[...] = z_s[:, OFF_GR:OFF_GR + RW]

    @pl.when((head_t == n_t - 1) & (s < n_tiles))
    def _():
        shift_out_ref[0] = zsbuf[CARRY_ROWS - 1:CARRY_ROWS, :]
        conv_out_ref[0] = ubuf[CARRY_ROWS - 2:CARRY_ROWS, :]

    @pl.when((tail_t == n_t - 1) & (s > 0))
    def _():
        for h in range(N_HEADS):
            j, i = divmod(h, 2)
            blk = s_ref[j]
            wkv_out_ref[0, h] = blk[i * HEAD_DIM:(i + 1) * HEAD_DIM, i * HEAD_DIM:(i + 1) * HEAD_DIM]


def _prompt_layer(x2d, p2d, w, bsz, t_len, tc):
    n_t = t_len // tc
    blockones = (jnp.arange(tc)[:, None] // CHUNK) == (jnp.arange(tc)[None, :] // CHUNK)
    lower = jnp.arange(tc)[:, None] >= jnp.arange(tc)[None, :]
    tri = (blockones & lower).astype(BF16)
    n_tiles = bsz * n_t
    kern = functools.partial(_prompt_kernel, tc=tc, n_t=n_t, n_tiles=n_tiles)
    big = lambda: pltpu.VMEM((tc, RW), F32)
    n_blk = (tc // CHUNK) * N_PAIRS
    op = lambda dt: pltpu.VMEM((n_blk, LANES, LANES), dt)
    head = lambda s: jnp.minimum(s, n_tiles - 1)
    tail = lambda s: jnp.maximum(s - 1, 0)
    head_tile = lambda width: pl.BlockSpec((tc, width), lambda s: (head(s), 0))
    tail_tile = lambda width: pl.BlockSpec((tc, width), lambda s: (tail(s), 0))
    const = lambda shape: pl.BlockSpec(shape, lambda s: (0, 0), pipeline_mode=pl.Buffered(1))
    return pl.pallas_call(
        kern,
        grid=(n_tiles + 1,),
        in_specs=[head_tile(D_MODEL), tail_tile(D_MODEL), tail_tile(D_PLE), const((1, D_MODEL)),
                  const((D_MODEL, IN_W)), const((D_MODEL, D_MODEL)), const((D_MODEL, D_MODEL)),
                  const((D_PLE, D_MODEL)), const((1, D_MODEL))]
        + _prm_specs(1)
        + [const((HEADSUM_W, HEADSUM_W)), const((tc, tc))],
        out_specs=[
            tail_tile(D_MODEL),
            pl.BlockSpec((1, 2, CONV_CH), lambda s: (head(s) // n_t, 0, 0)),
            pl.BlockSpec((1, 1, SHIFT_W), lambda s: (head(s) // n_t, 0, 0)),
            pl.BlockSpec((1, N_HEADS, HEAD_DIM, HEAD_DIM), lambda s: (tail(s) // n_t, 0, 0, 0)),
        ],
        out_shape=[
            jax.ShapeDtypeStruct((bsz * t_len, D_MODEL), F32),
            jax.ShapeDtypeStruct((bsz, 2, CONV_CH), F32),
            jax.ShapeDtypeStruct((bsz, 1, SHIFT_W), F32),
            jax.ShapeDtypeStruct((bsz, N_HEADS, HEAD_DIM, HEAD_DIM), F32),
        ],
        scratch_shapes=[
            pltpu.VMEM((tc, IN_W), F32),
            pltpu.VMEM((tc, D_MODEL), F32),
            pltpu.VMEM((CARRY_ROWS + tc, CONV_CH), F32),
            pltpu.VMEM((CARRY_ROWS + tc, SHIFT_W), F32),
            pltpu.VMEM((N_PAIRS, LANES, LANES), F32),
            big(), big(), big(), big(), big(), big(), big(), big(), big(),
            op(BF16), op(F32), op(BF16), op(F32),
            big(), big(),
        ],
        compiler_params=pltpu.CompilerParams(
            dimension_semantics=("arbitrary",), vmem_limit_bytes=VMEM_LIMIT),
        name="prompt_layer",
    )(x2d, x2d, p2d, w["norm_g"].reshape(1, D_MODEL), w["w_in"], w["w_out"], w["w_pg"], w["w_pp"],
      w["final_g"].reshape(1, D_MODEL), *w["prm"], w["headones"], tri)


def _sample_front_kernel(x_ref, ng_ref, win_ref, cb_ref, sb_ref, *refs):
    prm_refs = refs[:len(_PRM_NAMES)]
    headones_ref = refs[len(_PRM_NAMES)]
    (ya_ref, conv_out_ref, shift_out_ref, gr_ref, bonus_ref,
     nkk_ref, w_ref, b_ref, km_ref, v_ref, r_ref) = refs[len(_PRM_NAMES) + 1:]
    prm = _load_prm(prm_refs)
    headones = headones_ref[...]

    x = x_ref[...]
    xn = (x * lax.rsqrt(jnp.mean(x * x, axis=-1, keepdims=True) + RMS_EPS) * ng_ref[...]).astype(BF16)
    z = _dot(xn, win_ref[...])

    u = z[:, CONV_CH:2 * CONV_CH] * z[:, 2 * CONV_CH:3 * CONV_CH]
    cb0 = cb_ref[:, 0:CONV_CH]
    cb1 = cb_ref[:, CONV_CH:2 * CONV_CH]
    cw = prm["conv_w"]
    conv = cw[0:1, :] * cb0 + cw[1:2, :] * cb1 + cw[2:3, :] * u
    ya_ref[...] = z[:, 0:CONV_CH] * conv * _silu(z[:, 3 * CONV_CH:4 * CONV_CH])
    conv_out_ref[:, 0:CONV_CH] = cb1
    conv_out_ref[:, CONV_CH:2 * CONV_CH] = u

    zs = z[:, OFF_ZS:OFF_ZS + SHIFT_W]
    shift_out_ref[...] = zs
    gr_ref[...] = z[:, OFF_GR:OFF_GR + RW]
    zm = zs + (sb_ref[...] - zs) * prm["mu"]
    r = zm[:, 0:RW]
    k = zm[:, RW:2 * RW]
    v = zm[:, 2 * RW:3 * RW]
    wa = zm[:, 3 * RW:3 * RW + 2 * LORA]
    lw, a, kk, kmod, bonus = _rwkv_tokens(r, k, v, wa, prm, headones)
    bonus_ref[...] = bonus
    nkk_ref[...] = (-kk).T
    w_ref[...] = jnp.exp(lw).T
    b_ref[...] = (kk * a).T
    km_ref[...] = kmod.T
    v_ref[...] = v.T
    r_ref[...] = r.T


def _sample_step_kernel(s_ref, nkk_ref, w_ref, b_ref, km_ref, v_ref, r_ref, s1_ref, y_ref):
    nkk = nkk_ref[0]
    w = w_ref[0]
    b = b_ref[0]
    km = km_ref[0]
    r = r_ref[0]
    for vi in range(HEAD_DIM):
        s0 = s_ref[0, vi]
        sa = jnp.sum(s0 * nkk, axis=0, keepdims=True)
        s1 = s0 * w + sa * b + v_ref[0, vi:vi + 1, :] * km
        s1_ref[0, vi] = s1
        y_ref[0, vi:vi + 1, :] = jnp.sum(s1 * r, axis=0, keepdims=True)


def _sample_back_kernel(o_ref, bonus_ref, gr_ref, ya_ref, x_ref, p_ref, wo_ref, wg_ref, wp_ref, fg_ref, *refs):
    prm_refs = refs[:len(_PRM_NAMES)]
    headones_ref = refs[len(_PRM_NAMES)]
    y_ref = refs[len(_PRM_NAMES) + 1]
    prm = _load_prm(prm_refs)
    y_r = _rwkv_post(o_ref[...].T, bonus_ref[...], gr_ref[...], prm, headones_ref[...])
    ycat = jnp.concatenate([ya_ref[...], y_r], axis=1).astype(BF16)
    h1 = x_ref[...] + _dot(ycat, wo_ref[...])
    gate = _sigmoid(_dot(h1.astype(BF16), wg_ref[...]))
    pe = _dot(p_ref[...].astype(BF16), wp_ref[...])
    h2 = h1 + gate * pe
    ms = jnp.mean(h2 * h2, axis=-1, keepdims=True)
    y_ref[...] = h2 * lax.rsqrt(ms + RMS_EPS) * fg_ref[...]


def _full(shape):
    nd = len(shape)
    return pl.BlockSpec(shape, lambda *_: (0,) * nd)


def _sample_layer(x2d, p2d, state_conv, state_shift, state_hvkb, w):
    n = x2d.shape[0]
    params = pltpu.CompilerParams(dimension_semantics=("arbitrary",), vmem_limit_bytes=VMEM_LIMIT)
    row = jax.ShapeDtypeStruct((n, RW), F32)
    col = jax.ShapeDtypeStruct((RW, n), F32)
    cb = state_conv.reshape(n, 2 * CONV_CH)
    front = pl.pallas_call(
        _sample_front_kernel,
        grid=(1,),
        in_specs=[_full((n, D_MODEL)), _full((1, D_MODEL)), _full((D_MODEL, IN_W)),
                  _full((n, 2 * CONV_CH)), _full((n, SHIFT_W))]
        + _prm_specs(1) + [_full((HEADSUM_W, HEADSUM_W))],
        out_specs=[_full((n, CONV_CH)), _full((n, 2 * CONV_CH)), _full((n, SHIFT_W)),
                   _full((n, RW)), _full((n, RW))] + [_full((RW, n))] * 6,
        out_shape=[jax.ShapeDtypeStruct((n, CONV_CH), F32), jax.ShapeDtypeStruct((n, 2 * CONV_CH), F32),
                   jax.ShapeDtypeStruct((n, SHIFT_W), F32), row, row] + [col] * 6,
        compiler_params=params,
        name="sample_front",
    )(x2d, w["norm_g"].reshape(1, D_MODEL), w["w_in"], cb, state_shift, *w["prm"], w["headones"])
    ya, conv_new, shift_new, g_r, bonus, nkk, dec, b, km, v, r = front

    hk = lambda a: a.reshape(N_HEADS, HEAD_DIM, n)
    vspec = pl.BlockSpec((1, HEAD_DIM, n), lambda h: (h, 0, 0))
    sspec = pl.BlockSpec((1, HEAD_DIM, HEAD_DIM, n), lambda h: (h, 0, 0, 0))
    s1, y = pl.pallas_call(
        _sample_step_kernel,
        grid=(N_HEADS,),
        in_specs=[sspec] + [vspec] * 6,
        out_specs=[sspec, vspec],
        out_shape=[jax.ShapeDtypeStruct((N_HEADS, HEAD_DIM, HEAD_DIM, n), F32),
                   jax.ShapeDtypeStruct((N_HEADS, HEAD_DIM, n), F32)],
        compiler_params=params,
        name="sample_step",
    )(state_hvkb, hk(nkk), hk(dec), hk(b), hk(km), hk(v), hk(r))

    y_out = pl.pallas_call(
        _sample_back_kernel,
        grid=(1,),
        in_specs=[_full((RW, n)), _full((n, RW)), _full((n, RW)), _full((n, CONV_CH)),
                  _full((n, D_MODEL)), _full((n, D_PLE)), _full((D_MODEL, D_MODEL)),
                  _full((D_MODEL, D_MODEL)), _full((D_PLE, D_MODEL)), _full((1, D_MODEL))]
        + _prm_specs(1) + [_full((HEADSUM_W, HEADSUM_W))],
        out_specs=_full((n, D_MODEL)),
        out_shape=jax.ShapeDtypeStruct((n, D_MODEL), F32),
        compiler_params=params,
        name="sample_back",
    )(y.reshape(RW, n), bonus, g_r, ya, x2d, p2d, w["w_out"], w["w_pg"], w["w_pp"],
      w["final_g"].reshape(1, D_MODEL), *w["prm"], w["headones"])
    return y_out, conv_new.reshape(n, 2, CONV_CH), shift_new, s1


def _layer_prompt(x, p, w, tc=256, tm=256):
    bsz, t_len, _ = x.shape
    x2d = x.reshape(bsz * t_len, D_MODEL)
    y, conv_new, shift_new, wkv_new = _prompt_layer(x2d, p.reshape(bsz * t_len, D_PLE), w, bsz, t_len, tc)
    return y.reshape(bsz, t_len, D_MODEL), conv_new, shift_new.reshape(bsz, SHIFT_W), wkv_new


def _layer_sample(x, p, state_conv, state_shift, state_wkv, w):
    n = x.shape[0]
    s_hvkb = jnp.transpose(state_wkv, (1, 2, 3, 0))
    y, conv_new, shift_new, s1 = _sample_layer(x.reshape(n, D_MODEL), p.reshape(n, D_PLE), state_conv,
                                               state_shift, s_hvkb, w)
    return y.reshape(n, 1, D_MODEL), conv_new, shift_new, jnp.transpose(s1, (3, 0, 1, 2))


def kernel(x_prompt, x_sample, p_prompt, p_sample, state_conv, state_shift, state_wkv, norm_g, w_in, conv_w, mu_shift, w0, w_up, a0, a_up, k_k, k_a, r_k, ln_w, ln_b, w_out, w_pg, w_pp, final_g):
    depth = norm_g.shape[0]
    assert depth == 1
    i = 0
    head_id = jnp.arange(HEADSUM_W) // HEAD_DIM
    w = {
        "norm_g": norm_g[i],
        "w_in": w_in[i].astype(BF16),
        "w_out": w_out[i].astype(BF16),
        "w_pg": w_pg[i].astype(BF16),
        "w_pp": w_pp[i].astype(BF16),
        "final_g": final_g,
        "prm": _prm_arrays(conv_w[i], mu_shift[i], w0[i], w_up[i], a0[i], a_up[i], k_k[i], k_a[i],
                           r_k[i].reshape(RW), ln_w[i], ln_b[i]),
        "headones": (head_id[:, None] == head_id[None, :]).astype(BF16),
    }
    yp, cp, sp, wp = _layer_prompt(x_prompt, p_prompt[i], w)
    ys, cs, ss, ws = _layer_sample(x_sample, p_sample[i], state_conv[i], state_shift[i], state_wkv[i], w)
    return (yp, ys, cp[None], sp[None], wp[None], cs[None], ss[None], ws[None])
```

```python
import functools
import math

import jax
import jax.numpy as jnp
from jax import lax
from jax.experimental import pallas as pl
from jax.experimental.pallas import tpu as pltpu

F32 = jnp.float32
BF16 = jnp.bfloat16

D_MODEL = 1024
CONV_CH = 512
RW = 512
HEAD_DIM = 64
N_HEADS = 8
N_PAIRS = N_HEADS // 2
LORA = 64
D_PLE = 256
SHIFT_W = 3 * RW + 2 * LORA
IN_W = 4 * CONV_CH + SHIFT_W + RW
OFF_ZS = 4 * CONV_CH
OFF_GR = OFF_ZS + SHIFT_W
RMS_EPS = 1e-6
GN_EPS = 64e-5
DECAY_SCALE = math.exp(-0.5)

LANES = 128
SUBLANES = 8
MXU_DIM = 256
HEADSUM_W = MXU_DIM
SUM_PARTS = 1
CUM_PARTS = 2
LORA_PASSES = 1
CHUNK = 64
CARRY_ROWS = 8
VMEM_LIMIT = 56 * 1024 * 1024


def _dot(a, b):
    return jnp.dot(a, b, preferred_element_type=F32)


def _dot_nt(a, b):
    return lax.dot_general(a, b, (((1,), (1,)), ((), ())), preferred_element_type=F32)


def _dot_tn(a, b):
    return lax.dot_general(a, b, (((0,), (0,)), ((), ())), preferred_element_type=F32)


def _split(x, parts):
    out = []
    rem = x
    for i in range(parts):
        t = rem.astype(BF16)
        out.append(t)
        if i + 1 < parts:
            rem = rem - t.astype(F32)
    return out


def _const_dot(c_bf16, x, parts):
    acc = None
    for t in _split(x, parts):
        d = _dot(c_bf16, t)
        acc = d if acc is None else acc + d
    return acc


def _x_dot_const(x, c_bf16, parts):
    acc = None
    for t in _split(x, parts):
        d = _dot(t, c_bf16)
        acc = d if acc is None else acc + d
    return acc


def _head_sum(x, headones):
    w = headones.shape[0]
    halves = [_x_dot_const(x[:, i * w:(i + 1) * w], headones, SUM_PARTS) for i in range(x.shape[1] // w)]
    return jnp.concatenate(halves, axis=1)


def _mm(a, b, passes):
    if passes == 1:
        return _dot(a.astype(BF16), b.astype(BF16))
    ah, al = _split(a, 2)
    bh, bl = _split(b, 2)
    return _dot(ah, bh) + _dot(al, bh) + _dot(ah, bl)


def _sigmoid(x):
    return 1.0 / (1.0 + jnp.exp(-x))


def _silu(x):
    return x * _sigmoid(x)


def _rwkv_tokens(r, k, v, wa, prm, headones, between_steps=lambda: None):
    n = r.shape[0]
    lane = lax.broadcasted_iota(jnp.int32, (n, LANES), 1)
    th = jnp.where(lane < LORA, jnp.tanh(wa), wa)
    lora = _mm(th, prm["wlora"], LORA_PASSES)
    lw = -DECAY_SCALE * _sigmoid(prm["w0"] + lora[:, :RW])
    a = _sigmoid(prm["a0"] + lora[:, RW:])
    between_steps()
    kk = k * prm["k_k"]
    ss = _head_sum(kk * kk, headones)
    kk = kk / jnp.maximum(jnp.sqrt(ss), 1e-12)
    between_steps()
    kmod = k * (1.0 + (a - 1.0) * prm["k_a"])
    bonus = _head_sum(r * kmod * prm["r_k"], headones) * v
    between_steps()
    return lw, a, kk, kmod, bonus


def _rwkv_post(o, bonus, g_r, prm, headones):
    mean = _head_sum(o, headones) * (1.0 / HEAD_DIM)
    d = o - mean
    var = _head_sum(d * d, headones) * (1.0 / HEAD_DIM)
    on = d * lax.rsqrt(var + GN_EPS)
    on = on * prm["ln_w"] + prm["ln_b"]
    return (on + bonus) * _silu(g_r)


_PRM_NAMES = ("conv_w", "mu", "w0", "a0", "wlora", "k_k", "k_a", "r_k", "ln_w", "ln_b")


def _prm_arrays(conv_w, mu_shift, w0, w_up, a0, a_up, k_k, k_a, r_k, ln_w, ln_b):
    wlora = jnp.zeros((2 * LORA, 2 * RW), F32)
    wlora = wlora.at[:LORA, :RW].set(w_up).at[LORA:, RW:].set(a_up)
    row = lambda x: x.reshape(1, -1)
    return (conv_w, row(mu_shift), row(w0), row(a0), wlora, row(k_k), row(k_a),
            row(r_k), row(ln_w), row(ln_b))


def _prm_specs(grid_rank):
    zero = (lambda *_: (0, 0))
    shapes = ((3, CONV_CH), (1, SHIFT_W), (1, RW), (1, RW), (2 * LORA, 2 * RW),
              (1, RW), (1, RW), (1, RW), (1, RW), (1, RW))
    return [pl.BlockSpec(s, zero) for s in shapes]


def _load_prm(refs):
    return {n: r[...] for n, r in zip(_PRM_NAMES, refs)}


def _stack_heads(x):
    lane = lax.broadcasted_iota(jnp.int32, x.shape, 1)
    lo = jnp.where(lane < HEAD_DIM, x, 0.0)
    hi = jnp.where(lane >= HEAD_DIM, x, 0.0)
    return jnp.concatenate([lo, hi], axis=0)


def _bdot(a, b):
    return _dot(a.astype(BF16), b.astype(BF16))


def _diag_block_inverses_minus_eye(mats, lane_bcast):
    n = mats[0].shape[0]
    nb = n // SUBLANES
    row = lax.broadcasted_iota(jnp.int32, (n, n), 0)
    col = lax.broadcasted_iota(jnp.int32, (n, n), 1)
    in_diag = ((row // SUBLANES) == (col // SUBLANES)) & (row > col)
    packed = []
    for a in mats:
        d = jnp.where(in_diag, a, 0.0)
        acc = d[0:SUBLANES]
        for m in range(1, nb):
            acc = acc + d[m * SUBLANES:(m + 1) * SUBLANES]
        packed.append(acc)
    d_all = jnp.concatenate(packed, axis=0).astype(BF16)
    sub = lax.broadcasted_iota(jnp.int32, (SUBLANES, n), 0)
    lane = lax.broadcasted_iota(jnp.int32, (SUBLANES, n), 1)
    eye8 = jnp.where(sub == lane % SUBLANES, 1.0, 0.0)
    ts = [eye8 for _ in mats]
    for j in range(SUBLANES - 1):
        spread = _dot(d_all, lane_bcast[j])
        ts = [t + spread[i * SUBLANES:(i + 1) * SUBLANES] * jnp.broadcast_to(t[j:j + 1, :], (SUBLANES, n))
              for i, t in enumerate(ts)]
    blk = lax.broadcasted_iota(jnp.int32, (SUBLANES, n), 1) // SUBLANES
    out = []
    for t in ts:
        tm1 = t - eye8
        out.append(jnp.concatenate([jnp.where(blk == m, tm1, 0.0) for m in range(nb)], axis=0))
    return out


def _unit_lower_inverse_minus_eye(mats, lane_bcast, between_levels=lambda: None):
    n = mats[0].shape[0]
    row = lax.broadcasted_iota(jnp.int32, (n, n), 0)
    col = lax.broadcasted_iota(jnp.int32, (n, n), 1)
    xs = _diag_block_inverses_minus_eye(mats, lane_bcast)
    size = 2 * SUBLANES
    while size <= CHUNK:
        half = size // 2
        sel = ((row // size) == (col // size)) & ((row % size) >= half) & ((col % size) < half)
        als = [jnp.where(sel, a, 0.0) for a in mats]
        ps = [al + _bdot(x, al) for x, al in zip(xs, als)]
        xs = [x + p + _bdot(p, x) for x, p in zip(xs, ps)]
        between_levels()
        size *= 2
    return xs


def _chunk_operators(ld, strict, incl, lane_bcast, side_work=()):
    side_work = list(side_work)

    def breathe():
        if side_work:
            side_work.pop(0)()

    cat0 = lambda xs: jnp.concatenate(xs, axis=0)
    la = [_stack_heads(x) for x in ld("la")]
    lr = [_stack_heads(x) for x in ld("lr")]
    n2 = 2 * CHUNK
    amat = [_dot_nt(cat0([a, b]).astype(BF16), cat0([c, d]).astype(BF16))
            for a, b, c, d in zip(la, lr, ld("rb"), ld("rk"))]

    def block_diag_pair(m):
        swapped = pltpu.roll(m, CHUNK, axis=1)
        first = cat0([m[0:CHUNK], swapped[CHUNK:n2]])
        second = cat0([swapped[0:CHUNK], m[CHUNK:n2]])
        return first, second

    top = [block_diag_pair(m[0:n2]) for m in amat]
    a_ab = [jnp.where(strict, t[0], 0.0) for t in top]
    xs = _unit_lower_inverse_minus_eye(a_ab, lane_bcast, breathe)
    vst = [_stack_heads(x) for x in ld("v")]
    av = [_bdot(jnp.where(strict, t[1], 0.0), v) for t, v in zip(top, vst)]
    breathe()
    uv0 = [y + _bdot(x, y) for x, y in zip(xs, av)]
    tla = [y + _bdot(x, y) for x, y in zip(xs, la)]
    breathe()
    bottom = [block_diag_pair(m[n2:2 * n2]) for m in amat]
    a_rb = [jnp.where(incl, t[0], 0.0) for t in bottom]
    a_rk = [jnp.where(incl, t[1], 0.0) for t in bottom]
    gop = [y + _bdot(a, t) for y, a, t in zip(lr, a_rb, tla)]
    uv = [cat0([u, v]).astype(BF16) for u, v in zip(uv0, vst)]
    y0 = [_dot(jnp.concatenate([a, b], axis=1).astype(BF16), w) for a, b, w in zip(a_rb, a_rk, uv)]
    breathe()
    bt = [_stack_heads(x) for x in ld("bt")]
    kt = [_stack_heads(x) for x in ld("kt")]
    pop = [_dot_tn(t.astype(BF16), b.astype(BF16)) for t, b in zip(tla, bt)]
    qop = [_dot_tn(w, cat0([b, k]).astype(BF16)) for w, b, k in zip(uv, bt, kt)]
    while side_work:
        breathe()
    return gop, y0, pop, qop


def _prompt_kernel(xh_ref, xt_ref, pt_ref, ng_ref, win_ref, wo_ref, wg_ref, wp_ref, fg_ref, *refs,
                   tc, n_t, n_tiles):
    prm_refs = refs[:len(_PRM_NAMES)]
    headones_ref, tri_ref, bcast_ref = refs[len(_PRM_NAMES):len(_PRM_NAMES) + 3]
    (y_ref, conv_out_ref, shift_out_ref, wkv_out_ref) = refs[len(_PRM_NAMES) + 3:len(_PRM_NAMES) + 7]
    (z_s, ycat_s, ubuf, zsbuf, s_ref, la_s, lr_s, rb_s, rk_s, bt_s, kt_s, v_s, gam_s, o_s,
     g_sc, y0_sc, p_sc, q_sc, bonus_p, gr_p) = refs[len(_PRM_NAMES) + 7:]

    s = pl.program_id(0)
    head_t = lax.rem(s, jnp.int32(n_t))
    tail_t = lax.rem(s + (n_t - 1), jnp.int32(n_t))
    prm = _load_prm(prm_refs)
    headones = headones_ref[...]
    n2 = 2 * CHUNK
    n_chunks = tc // CHUNK

    @pl.when(s == 0)
    def _():
        for ref in (g_sc, y0_sc, p_sc, q_sc, gam_s, bonus_p, gr_p, ycat_s, s_ref):
            ref[...] = jnp.zeros(ref.shape, ref.dtype)

    @pl.when(head_t == 0)
    def _():
        ubuf[0:CARRY_ROWS, :] = jnp.zeros((CARRY_ROWS, CONV_CH), F32)
        zsbuf[0:CARRY_ROWS, :] = jnp.zeros((CARRY_ROWS, SHIFT_W), F32)

    @pl.when(tail_t == 0)
    def _():
        s_ref[...] = jnp.zeros(s_ref.shape, F32)

    tail = {}

    def tail_state_pass(c):
        def run():
            for j in range(N_PAIRS):
                idx = c * N_PAIRS + j
                lanes = slice(j * LANES, (j + 1) * LANES)
                s_old = s_ref[j]
                s_bf = s_old.astype(BF16)
                yst = _dot_nt(g_sc[idx], s_bf) + y0_sc[idx]
                o_s[c * CHUNK:(c + 1) * CHUNK, lanes] = yst[0:CHUNK] + yst[CHUNK:n2]
                gam = gam_s[c * CHUNK:c * CHUNK + 1, lanes]
                s_ref[j] = s_old * gam + _dot(s_bf, p_sc[idx]) + q_sc[idx]
        return run

    def tail_post():
        ycat_s[:, CONV_CH:2 * CONV_CH] = _rwkv_post(o_s[...], bonus_p[...], gr_p[...], prm, headones)

    def tail_out_proj():
        tail["h1"] = xt_ref[...] + _dot(ycat_s[...].astype(BF16), wo_ref[...])

    def tail_gate():
        tail["gate"] = _sigmoid(_dot(tail["h1"].astype(BF16), wg_ref[...]))
        tail["pe"] = _dot(pt_ref[...].astype(BF16), wp_ref[...])

    def tail_norm():
        h2 = tail["h1"] + tail["gate"] * tail["pe"]
        ms = jnp.mean(h2 * h2, axis=-1, keepdims=True)
        y_ref[...] = h2 * lax.rsqrt(ms + RMS_EPS) * fg_ref[...]

    x = xh_ref[...]
    xn = (x * lax.rsqrt(jnp.mean(x * x, axis=-1, keepdims=True) + RMS_EPS) * ng_ref[...]).astype(BF16)

    def project(c0):
        def run():
            z_s[:, c0:c0 + CONV_CH] = _dot(xn, win_ref[:, c0:c0 + CONV_CH])
        return run

    side_queue = [tail_state_pass(0), project(OFF_GR), tail_state_pass(1), project(0),
                  tail_state_pass(2), project(CONV_CH), tail_state_pass(3), project(2 * CONV_CH),
                  tail_post, project(3 * CONV_CH), tail_out_proj, tail_gate, tail_norm]
    assert n_chunks == 4

    def project_next(count=2):
        for _ in range(count):
            if side_queue:
                side_queue.pop(0)()

    z_s[:, OFF_ZS:OFF_ZS + SHIFT_W] = _dot(xn, win_ref[:, OFF_ZS:OFF_ZS + SHIFT_W])

    zs = z_s[:, OFF_ZS:OFF_ZS + SHIFT_W]
    zsbuf[CARRY_ROWS:CARRY_ROWS + tc, :] = zs
    zprev = zsbuf[CARRY_ROWS - 1:CARRY_ROWS - 1 + tc, :]
    zm = zs + (zprev - zs) * prm["mu"]
    zsbuf[CARRY_ROWS - 1:CARRY_ROWS, :] = zsbuf[CARRY_ROWS + tc - 1:CARRY_ROWS + tc, :]

    project_next()
    r = zm[:, 0:RW]
    k = zm[:, RW:2 * RW]
    v = zm[:, 2 * RW:3 * RW]
    wa = zm[:, 3 * RW:3 * RW + 2 * LORA]
    lw, a, kk, kmod, bonus = _rwkv_tokens(r, k, v, wa, prm, headones, project_next)

    g = _const_dot(tri_ref[...], lw, CUM_PARTS)
    gc = jnp.concatenate(
        [jnp.broadcast_to(g[(c + 1) * CHUNK - 1:(c + 1) * CHUNK, :], (CHUNK, RW)) for c in range(tc // CHUNK)],
        axis=0)
    project_next()
    eng = jnp.exp(-g)
    etail = jnp.exp(gc - g)
    b = kk * a
    la_s[...] = -kk * jnp.exp(g - lw)
    lr_s[...] = r * jnp.exp(g)
    project_next()
    rb_s[...] = b * eng
    rk_s[...] = kmod * eng
    bt_s[...] = b * etail
    kt_s[...] = kmod * etail
    v_s[...] = v
    while side_queue:
        project_next()
    gam_s[...] = jnp.exp(gc)

    row = lax.broadcasted_iota(jnp.int32, (n2, n2), 0)
    col = lax.broadcasted_iota(jnp.int32, (n2, n2), 1)
    same = (row // CHUNK) == (col // CHUNK)
    strict = same & (row > col)
    incl = same & (row >= col)

    cw = prm["conv_w"]

    def conv_rows(r0):
        def run():
            rows = slice(r0, r0 + CHUNK)
            u = z_s[rows, CONV_CH:2 * CONV_CH] * z_s[rows, 2 * CONV_CH:3 * CONV_CH]
            ubuf[CARRY_ROWS + r0:CARRY_ROWS + r0 + CHUNK, :] = u
            um1 = ubuf[CARRY_ROWS - 1 + r0:CARRY_ROWS - 1 + r0 + CHUNK, :]
            um2 = ubuf[CARRY_ROWS - 2 + r0:CARRY_ROWS - 2 + r0 + CHUNK, :]
            conv = cw[0:1, :] * um2 + cw[1:2, :] * um1 + cw[2:3, :] * u
            ycat_s[rows, 0:CONV_CH] = z_s[rows, 0:CONV_CH] * conv * _silu(z_s[rows, 3 * CONV_CH:4 * CONV_CH])
        return run

    srcs = {"la": la_s, "lr": lr_s, "rb": rb_s, "rk": rk_s, "bt": bt_s, "kt": kt_s, "v": v_s}
    members = [(c, j) for c in range(n_chunks) for j in range(N_PAIRS)]

    def ld(name):
        ref = srcs[name]
        return [ref[c * CHUNK:(c + 1) * CHUNK, j * LANES:(j + 1) * LANES] for c, j in members]

    lane_bcast = [bcast_ref[j] for j in range(SUBLANES - 1)]
    gop, y0, pop, qop = _chunk_operators(ld, strict, incl, lane_bcast,
                                         [conv_rows(c * CHUNK) for c in range(n_chunks)])
    for (c, j), g_, y_, p_, q_ in zip(members, gop, y0, pop, qop):
        idx = c * N_PAIRS + j
        g_sc[idx] = g_.astype(BF16)
        y0_sc[idx] = y_
        p_sc[idx] = p_.astype(BF16)
        q_sc[idx] = q_

    ubuf[CARRY_ROWS - 2:CARRY_ROWS, :] = ubuf[CARRY_ROWS + tc - 2:CARRY_ROWS + tc, :]

    bonus_p[...] = bonus
    gr_p[...] = z_s[:, OFF_GR:OFF_GR + RW]

    @pl.when((head_t == n_t - 1) & (s < n_tiles))
    def _():
        shift_out_ref[0] = zsbuf[CARRY_ROWS - 1:CARRY_ROWS, :]
        conv_out_ref[0] = ubuf[CARRY_ROWS - 2:CARRY_ROWS, :]

    @pl.when((tail_t == n_t - 1) & (s > 0))
    def _():
        for h in range(N_HEADS):
            j, i = divmod(h, 2)
            blk = s_ref[j]
            wkv_out_ref[0, h] = blk[i * HEAD_DIM:(i + 1) * HEAD_DIM, i * HEAD_DIM:(i + 1) * HEAD_DIM]


def _prompt_layer(x2d, p2d, w, bsz, t_len, tc):
    n_t = t_len // tc
    blockones = (jnp.arange(tc)[:, None] // CHUNK) == (jnp.arange(tc)[None, :] // CHUNK)
    lower = jnp.arange(tc)[:, None] >= jnp.arange(tc)[None, :]
    tri = (blockones & lower).astype(BF16)
    n_tiles = bsz * n_t
    n2 = 2 * CHUNK
    idx = jnp.arange(n2)
    same_blk = (idx[:, None] // SUBLANES) == (idx[None, :] // SUBLANES)
    lane_bcast = jnp.stack([same_blk & (idx[:, None] % SUBLANES == j) for j in range(SUBLANES - 1)]).astype(BF16)
    kern = functools.partial(_prompt_kernel, tc=tc, n_t=n_t, n_tiles=n_tiles)
    big = lambda: pltpu.VMEM((tc, RW), F32)
    n_blk = (tc // CHUNK) * N_PAIRS
    op = lambda dt: pltpu.VMEM((n_blk, LANES, LANES), dt)
    head = lambda s: jnp.minimum(s, n_tiles - 1)
    tail = lambda s: jnp.maximum(s - 1, 0)
    head_tile = lambda width: pl.BlockSpec((tc, width), lambda s: (head(s), 0))
    tail_tile = lambda width: pl.BlockSpec((tc, width), lambda s: (tail(s), 0))
    const = lambda shape: pl.BlockSpec(shape, lambda s: (0, 0), pipeline_mode=pl.Buffered(1))
    return pl.pallas_call(
        kern,
        grid=(n_tiles + 1,),
        in_specs=[head_tile(D_MODEL), tail_tile(D_MODEL), tail_tile(D_PLE), const((1, D_MODEL)),
                  const((D_MODEL, IN_W)), const((D_MODEL, D_MODEL)), const((D_MODEL, D_MODEL)),
                  const((D_PLE, D_MODEL)), const((1, D_MODEL))]
        + _prm_specs(1)
        + [const((HEADSUM_W, HEADSUM_W)), const((tc, tc)),
           pl.BlockSpec((SUBLANES - 1, n2, n2), lambda s: (0, 0, 0), pipeline_mode=pl.Buffered(1))],
        out_specs=[
            tail_tile(D_MODEL),
            pl.BlockSpec((1, 2, CONV_CH), lambda s: (head(s) // n_t, 0, 0)),
            pl.BlockSpec((1, 1, SHIFT_W), lambda s: (head(s) // n_t, 0, 0)),
            pl.BlockSpec((1, N_HEADS, HEAD_DIM, HEAD_DIM), lambda s: (tail(s) // n_t, 0, 0, 0)),
        ],
        out_shape=[
            jax.ShapeDtypeStruct((bsz * t_len, D_MODEL), F32),
            jax.ShapeDtypeStruct((bsz, 2, CONV_CH), F32),
            jax.ShapeDtypeStruct((bsz, 1, SHIFT_W), F32),
            jax.ShapeDtypeStruct((bsz, N_HEADS, HEAD_DIM, HEAD_DIM), F32),
        ],
        scratch_shapes=[
            pltpu.VMEM((tc, IN_W), F32),
            pltpu.VMEM((tc, D_MODEL), F32),
            pltpu.VMEM((CARRY_ROWS + tc, CONV_CH), F32),
            pltpu.VMEM((CARRY_ROWS + tc, SHIFT_W), F32),
            pltpu.VMEM((N_PAIRS, LANES, LANES), F32),
            big(), big(), big(), big(), big(), big(), big(), big(), big(),
            op(BF16), op(F32), op(BF16), op(F32),
            big(), big(),
        ],
        compiler_params=pltpu.CompilerParams(
            dimension_semantics=("arbitrary",), vmem_limit_bytes=VMEM_LIMIT),
        name="prompt_layer",
    )(x2d, x2d, p2d, w["norm_g"].reshape(1, D_MODEL), w["w_in"], w["w_out"], w["w_pg"], w["w_pp"],
      w["final_g"].reshape(1, D_MODEL), *w["prm"], w["headones"], tri, lane_bcast)


def _sample_front_kernel(x_ref, ng_ref, win_ref, cb_ref, sb_ref, *refs):
    prm_refs = refs[:len(_PRM_NAMES)]
    headones_ref = refs[len(_PRM_NAMES)]
    (ya_ref, conv_out_ref, shift_out_ref, gr_ref, bonus_ref,
     nkk_ref, w_ref, b_ref, km_ref, v_ref, r_ref) = refs[len(_PRM_NAMES) + 1:]
    prm = _load_prm(prm_refs)
    headones = headones_ref[...]

    x = x_ref[...]
    xn = (x * lax.rsqrt(jnp.mean(x * x, axis=-1, keepdims=True) + RMS_EPS) * ng_ref[...]).astype(BF16)
    z = _dot(xn, win_ref[...])

    u = z[:, CONV_CH:2 * CONV_CH] * z[:, 2 * CONV_CH:3 * CONV_CH]
    cb0 = cb_ref[:, 0:CONV_CH]
    cb1 = cb_ref[:, CONV_CH:2 * CONV_CH]
    cw = prm["conv_w"]
    conv = cw[0:1, :] * cb0 + cw[1:2, :] * cb1 + cw[2:3, :] * u
    ya_ref[...] = z[:, 0:CONV_CH] * conv * _silu(z[:, 3 * CONV_CH:4 * CONV_CH])
    conv_out_ref[:, 0:CONV_CH] = cb1
    conv_out_ref[:, CONV_CH:2 * CONV_CH] = u

    zs = z[:, OFF_ZS:OFF_ZS + SHIFT_W]
    shift_out_ref[...] = zs
    gr_ref[...] = z[:, OFF_GR:OFF_GR + RW]
    zm = zs + (sb_ref[...] - zs) * prm["mu"]
    r = zm[:, 0:RW]
    k = zm[:, RW:2 * RW]
    v = zm[:, 2 * RW:3 * RW]
    wa = zm[:, 3 * RW:3 * RW + 2 * LORA]
    lw, a, kk, kmod, bonus = _rwkv_tokens(r, k, v, wa, prm, headones)
    bonus_ref[...] = bonus
    nkk_ref[...] = (-kk).T
    w_ref[...] = jnp.exp(lw).T
    b_ref[...] = (kk * a).T
    km_ref[...] = kmod.T
    v_ref[...] = v.T
    r_ref[...] = r.T


def _sample_step_kernel(s_ref, nkk_ref, w_ref, b_ref, km_ref, v_ref, r_ref, s1_ref, y_ref):
    nkk = nkk_ref[0]
    w = w_ref[0]
    b = b_ref[0]
    km = km_ref[0]
    r = r_ref[0]
    for vi in range(HEAD_DIM):
        s0 = s_ref[0, vi]
        sa = jnp.sum(s0 * nkk, axis=0, keepdims=True)
        s1 = s0 * w + sa * b + v_ref[0, vi:vi + 1, :] * km
        s1_ref[0, vi] = s1
        y_ref[0, vi:vi + 1, :] = jnp.sum(s1 * r, axis=0, keepdims=True)


def _sample_back_kernel(o_ref, bonus_ref, gr_ref, ya_ref, x_ref, p_ref, wo_ref, wg_ref, wp_ref, fg_ref, *refs):
    prm_refs = refs[:len(_PRM_NAMES)]
    headones_ref = refs[len(_PRM_NAMES)]
    y_ref = refs[len(_PRM_NAMES) + 1]
    prm = _load_prm(prm_refs)
    y_r = _rwkv_post(o_ref[...].T, bonus_ref[...], gr_ref[...], prm, headones_ref[...])
    ycat = jnp.concatenate([ya_ref[...], y_r], axis=1).astype(BF16)
    h1 = x_ref[...] + _dot(ycat, wo_ref[...])
    gate = _sigmoid(_dot(h1.astype(BF16), wg_ref[...]))
    pe = _dot(p_ref[...].astype(BF16), wp_ref[...])
    h2 = h1 + gate * pe
    ms = jnp.mean(h2 * h2, axis=-1, keepdims=True)
    y_ref[...] = h2 * lax.rsqrt(ms + RMS_EPS) * fg_ref[...]


def _full(shape):
    nd = len(shape)
    return pl.BlockSpec(shape, lambda *_: (0,) * nd)


def _sample_layer(x2d, p2d, state_conv, state_shift, state_hvkb, w):
    n = x2d.shape[0]
    params = pltpu.CompilerParams(dimension_semantics=("arbitrary",), vmem_limit_bytes=VMEM_LIMIT)
    row = jax.ShapeDtypeStruct((n, RW), F32)
    col = jax.ShapeDtypeStruct((RW, n), F32)
    cb = state_conv.reshape(n, 2 * CONV_CH)
    front = pl.pallas_call(
        _sample_front_kernel,
        grid=(1,),
        in_specs=[_full((n, D_MODEL)), _full((1, D_MODEL)), _full((D_MODEL, IN_W)),
                  _full((n, 2 * CONV_CH)), _full((n, SHIFT_W))]
        + _prm_specs(1) + [_full((HEADSUM_W, HEADSUM_W))],
        out_specs=[_full((n, CONV_CH)), _full((n, 2 * CONV_CH)), _full((n, SHIFT_W)),
                   _full((n, RW)), _full((n, RW))] + [_full((RW, n))] * 6,
        out_shape=[jax.ShapeDtypeStruct((n, CONV_CH), F32), jax.ShapeDtypeStruct((n, 2 * CONV_CH), F32),
                   jax.ShapeDtypeStruct((n, SHIFT_W), F32), row, row] + [col] * 6,
        compiler_params=params,
        name="sample_front",
    )(x2d, w["norm_g"].reshape(1, D_MODEL), w["w_in"], cb, state_shift, *w["prm"], w["headones"])
    ya, conv_new, shift_new, g_r, bonus, nkk, dec, b, km, v, r = front

    hk = lambda a: a.reshape(N_HEADS, HEAD_DIM, n)
    vspec = pl.BlockSpec((1, HEAD_DIM, n), lambda h: (h, 0, 0))
    sspec = pl.BlockSpec((1, HEAD_DIM, HEAD_DIM, n), lambda h: (h, 0, 0, 0))
    s1, y = pl.pallas_call(
        _sample_step_kernel,
        grid=(N_HEADS,),
        in_specs=[sspec] + [vspec] * 6,
        out_specs=[sspec, vspec],
        out_shape=[jax.ShapeDtypeStruct((N_HEADS, HEAD_DIM, HEAD_DIM, n), F32),
                   jax.ShapeDtypeStruct((N_HEADS, HEAD_DIM, n), F32)],
        compiler_params=params,
        name="sample_step",
    )(state_hvkb, hk(nkk), hk(dec), hk(b), hk(km), hk(v), hk(r))

    y_out = pl.pallas_call(
        _sample_back_kernel,
        grid=(1,),
        in_specs=[_full((RW, n)), _full((n, RW)), _full((n, RW)), _full((n, CONV_CH)),
                  _full((n, D_MODEL)), _full((n, D_PLE)), _full((D_MODEL, D_MODEL)),
                  _full((D_MODEL, D_MODEL)), _full((D_PLE, D_MODEL)), _full((1, D_MODEL))]
        + _prm_specs(1) + [_full((HEADSUM_W, HEADSUM_W))],
        out_specs=_full((n, D_MODEL)),
        out_shape=jax.ShapeDtypeStruct((n, D_MODEL), F32),
        compiler_params=params,
        name="sample_back",
    )(y.reshape(RW, n), bonus, g_r, ya, x2d, p2d, w["w_out"], w["w_pg"], w["w_pp"],
      w["final_g"].reshape(1, D_MODEL), *w["prm"], w["headones"])
    return y_out, conv_new.reshape(n, 2, CONV_CH), shift_new, s1


def _layer_prompt(x, p, w, tc=256, tm=256):
    bsz, t_len, _ = x.shape
    x2d = x.reshape(bsz * t_len, D_MODEL)
    y, conv_new, shift_new, wkv_new = _prompt_layer(x2d, p.reshape(bsz * t_len, D_PLE), w, bsz, t_len, tc)
    return y.reshape(bsz, t_len, D_MODEL), conv_new, shift_new.reshape(bsz, SHIFT_W), wkv_new


def _layer_sample(x, p, state_conv, state_shift, state_wkv, w):
    n = x.shape[0]
    s_hvkb = jnp.transpose(state_wkv, (1, 2, 3, 0))
    y, conv_new, shift_new, s1 = _sample_layer(x.reshape(n, D_MODEL), p.reshape(n, D_PLE), state_conv,
                                               state_shift, s_hvkb, w)
    return y.reshape(n, 1, D_MODEL), conv_new, shift_new, jnp.transpose(s1, (3, 0, 1, 2))


def kernel(x_prompt, x_sample, p_prompt, p_sample, state_conv, state_shift, state_wkv, norm_g, w_in, conv_w, mu_shift, w0, w_up, a0, a_up, k_k, k_a, r_k, ln_w, ln_b, w_out, w_pg, w_pp, final_g):
    depth = norm_g.shape[0]
    assert depth == 1
    i = 0
    head_id = jnp.arange(HEADSUM_W) // HEAD_DIM
    w = {
        "norm_g": norm_g[i],
        "w_in": w_in[i].astype(BF16),
        "w_out": w_out[i].astype(BF16),
        "w_pg": w_pg[i].astype(BF16),
        "w_pp": w_pp[i].astype(BF16),
        "final_g": final_g,
        "prm": _prm_arrays(conv_w[i], mu_shift[i], w0[i], w_up[i], a0[i], a_up[i], k_k[i], k_a[i],
                           r_k[i].reshape(RW), ln_w[i], ln_b[i]),
        "headones": (head_id[:, None] == head_id[None, :]).astype(BF16),
    }
    yp, cp, sp, wp = _layer_prompt(x_prompt, p_prompt[i], w)
    ys, cs, ss, ws = _layer_sample(x_sample, p_sample[i], state_conv[i], state_shift[i], state_wkv[i], w)
    return (yp, ys, cp[None], sp[None], wp[None], cs[None], ss[None], ws[None])
```

```python
import functools
import math

import jax
import jax.numpy as jnp
from jax import lax
from jax.experimental import pallas as pl
from jax.experimental.pallas import tpu as pltpu

F32 = jnp.float32
BF16 = jnp.bfloat16

D_MODEL = 1024
CONV_CH = 512
RW = 512
HEAD_DIM = 64
N_HEADS = 8
N_PAIRS = N_HEADS // 2
LORA = 64
D_PLE = 256
SHIFT_W = 3 * RW + 2 * LORA
IN_W = 4 * CONV_CH + SHIFT_W + RW
OFF_ZS = 4 * CONV_CH
OFF_GR = OFF_ZS + SHIFT_W
RMS_EPS = 1e-6
GN_EPS = 64e-5
DECAY_SCALE = math.exp(-0.5)

LANES = 128
SUBLANES = 8
MXU_DIM = 256
HEADSUM_W = MXU_DIM
SUM_PARTS = 1
CUM_PARTS = 2
LORA_PASSES = 1
CHUNK = 64
CARRY_ROWS = 8
VMEM_LIMIT = 56 * 1024 * 1024


def _dot(a, b):
    return jnp.dot(a, b, preferred_element_type=F32)


def _dot_nt(a, b):
    return lax.dot_general(a, b, (((1,), (1,)), ((), ())), preferred_element_type=F32)


def _dot_tn(a, b):
    return lax.dot_general(a, b, (((0,), (0,)), ((), ())), preferred_element_type=F32)


def _split(x, parts):
    out = []
    rem = x
    for i in range(parts):
        t = rem.astype(BF16)
        out.append(t)
        if i + 1 < parts:
            rem = rem - t.astype(F32)
    return out


def _const_dot(c_bf16, x, parts):
    acc = None
    for t in _split(x, parts):
        d = _dot(c_bf16, t)
        acc = d if acc is None else acc + d
    return acc


def _x_dot_const(x, c_bf16, parts):
    acc = None
    for t in _split(x, parts):
        d = _dot(t, c_bf16)
        acc = d if acc is None else acc + d
    return acc


def _head_sum(x, headones):
    w = headones.shape[0]
    halves = [_x_dot_const(x[:, i * w:(i + 1) * w], headones, SUM_PARTS) for i in range(x.shape[1] // w)]
    return jnp.concatenate(halves, axis=1)


def _mm(a, b, passes):
    if passes == 1:
        return _dot(a.astype(BF16), b.astype(BF16))
    ah, al = _split(a, 2)
    bh, bl = _split(b, 2)
    return _dot(ah, bh) + _dot(al, bh) + _dot(ah, bl)


def _sigmoid(x):
    return jax.nn.sigmoid(x)


def _silu(x):
    return x * _sigmoid(x)


def _rwkv_tokens(r, k, v, wa, prm, headones, between_steps=lambda: None):
    n = r.shape[0]
    lane = lax.broadcasted_iota(jnp.int32, (n, LANES), 1)
    th = jnp.where(lane < LORA, jnp.tanh(wa), wa)
    lora = _mm(th, prm["wlora"], LORA_PASSES)
    lw = -DECAY_SCALE * _sigmoid(prm["w0"] + lora[:, :RW])
    a = _sigmoid(prm["a0"] + lora[:, RW:])
    between_steps()
    kk = k * prm["k_k"]
    ss = _head_sum(kk * kk, headones)
    kk = kk / jnp.maximum(jnp.sqrt(ss), 1e-12)
    between_steps()
    kmod = k * (1.0 + (a - 1.0) * prm["k_a"])
    bonus = _head_sum(r * kmod * prm["r_k"], headones) * v
    between_steps()
    return lw, a, kk, kmod, bonus


def _rwkv_post(o, bonus, g_r, prm, headones):
    mean = _head_sum(o, headones) * (1.0 / HEAD_DIM)
    d = o - mean
    var = _head_sum(d * d, headones) * (1.0 / HEAD_DIM)
    on = d * lax.rsqrt(var + GN_EPS)
    on = on * prm["ln_w"] + prm["ln_b"]
    return (on + bonus) * _silu(g_r)


_PRM_NAMES = ("conv_w", "mu", "w0", "a0", "wlora", "k_k", "k_a", "r_k", "ln_w", "ln_b")


def _prm_arrays(conv_w, mu_shift, w0, w_up, a0, a_up, k_k, k_a, r_k, ln_w, ln_b):
    wlora = jnp.zeros((2 * LORA, 2 * RW), F32)
    wlora = wlora.at[:LORA, :RW].set(w_up).at[LORA:, RW:].set(a_up)
    row = lambda x: x.reshape(1, -1)
    return (conv_w, row(mu_shift), row(w0), row(a0), wlora, row(k_k), row(k_a),
            row(r_k), row(ln_w), row(ln_b))


def _prm_specs(grid_rank):
    zero = (lambda *_: (0, 0))
    shapes = ((3, CONV_CH), (1, SHIFT_W), (1, RW), (1, RW), (2 * LORA, 2 * RW),
              (1, RW), (1, RW), (1, RW), (1, RW), (1, RW))
    return [pl.BlockSpec(s, zero) for s in shapes]


def _load_prm(refs):
    return {n: r[...] for n, r in zip(_PRM_NAMES, refs)}


def _stack_heads(x):
    lane = lax.broadcasted_iota(jnp.int32, x.shape, 1)
    lo = jnp.where(lane < HEAD_DIM, x, 0.0)
    hi = jnp.where(lane >= HEAD_DIM, x, 0.0)
    return jnp.concatenate([lo, hi], axis=0)


def _bdot(a, b):
    return _dot(a.astype(BF16), b.astype(BF16))


def _diag_block_inverses_minus_eye(mats, lane_bcast):
    n = mats[0].shape[0]
    nb = n // SUBLANES
    row = lax.broadcasted_iota(jnp.int32, (n, n), 0)
    col = lax.broadcasted_iota(jnp.int32, (n, n), 1)
    in_diag = ((row // SUBLANES) == (col // SUBLANES)) & (row > col)
    packed = []
    for a in mats:
        d = jnp.where(in_diag, a, 0.0)
        acc = d[0:SUBLANES]
        for m in range(1, nb):
            acc = acc + d[m * SUBLANES:(m + 1) * SUBLANES]
        packed.append(acc)
    d_all = jnp.concatenate(packed, axis=0).astype(BF16)
    sub = lax.broadcasted_iota(jnp.int32, (SUBLANES, n), 0)
    lane = lax.broadcasted_iota(jnp.int32, (SUBLANES, n), 1)
    eye8 = jnp.where(sub == lane % SUBLANES, 1.0, 0.0)
    ts = [eye8 for _ in mats]
    for j in range(SUBLANES - 1):
        spread = _dot(d_all, lane_bcast[j])
        ts = [t + spread[i * SUBLANES:(i + 1) * SUBLANES] * jnp.broadcast_to(t[j:j + 1, :], (SUBLANES, n))
              for i, t in enumerate(ts)]
    blk = lax.broadcasted_iota(jnp.int32, (SUBLANES, n), 1) // SUBLANES
    out = []
    for t in ts:
        tm1 = t - eye8
        out.append(jnp.concatenate([jnp.where(blk == m, tm1, 0.0) for m in range(nb)], axis=0))
    return out


def _unit_lower_inverse_minus_eye(mats, lane_bcast, between_levels=lambda: None):
    n = mats[0].shape[0]
    row = lax.broadcasted_iota(jnp.int32, (n, n), 0)
    col = lax.broadcasted_iota(jnp.int32, (n, n), 1)
    xs = _diag_block_inverses_minus_eye(mats, lane_bcast)
    size = 2 * SUBLANES
    while size <= CHUNK:
        half = size // 2
        sel = ((row // size) == (col // size)) & ((row % size) >= half) & ((col % size) < half)
        als = [jnp.where(sel, a, 0.0) for a in mats]
        ps = [al + _bdot(x, al) for x, al in zip(xs, als)]
        xs = [x + p + _bdot(p, x) for x, p in zip(xs, ps)]
        between_levels()
        size *= 2
    return xs


def _chunk_operators(ld, strict, incl, lane_bcast, side_work=()):
    side_work = list(side_work)

    def breathe():
        if side_work:
            side_work.pop(0)()

    cat0 = lambda xs: jnp.concatenate(xs, axis=0)
    la = [_stack_heads(x) for x in ld("la")]
    lr = [_stack_heads(x) for x in ld("lr")]
    n2 = 2 * CHUNK
    amat = [_dot_nt(cat0([a, b]).astype(BF16), cat0([c, d]).astype(BF16))
            for a, b, c, d in zip(la, lr, ld("rb"), ld("rk"))]

    def block_diag_pair(m):
        swapped = pltpu.roll(m, CHUNK, axis=1)
        first = cat0([m[0:CHUNK], swapped[CHUNK:n2]])
        second = cat0([swapped[0:CHUNK], m[CHUNK:n2]])
        return first, second

    top = [block_diag_pair(m[0:n2]) for m in amat]
    a_ab = [jnp.where(strict, t[0], 0.0) for t in top]
    xs = _unit_lower_inverse_minus_eye(a_ab, lane_bcast, breathe)
    vst = [_stack_heads(x) for x in ld("v")]
    av = [_bdot(jnp.where(strict, t[1], 0.0), v) for t, v in zip(top, vst)]
    breathe()
    uv0 = [y + _bdot(x, y) for x, y in zip(xs, av)]
    tla = [y + _bdot(x, y) for x, y in zip(xs, la)]
    breathe()
    bottom = [block_diag_pair(m[n2:2 * n2]) for m in amat]
    a_rb = [jnp.where(incl, t[0], 0.0) for t in bottom]
    a_rk = [jnp.where(incl, t[1], 0.0) for t in bottom]
    gop = [y + _bdot(a, t) for y, a, t in zip(lr, a_rb, tla)]
    uv = [cat0([u, v]).astype(BF16) for u, v in zip(uv0, vst)]
    y0 = [_dot(jnp.concatenate([a, b], axis=1).astype(BF16), w) for a, b, w in zip(a_rb, a_rk, uv)]
    breathe()
    bt = [_stack_heads(x) for x in ld("bt")]
    kt = [_stack_heads(x) for x in ld("kt")]
    pop = [_dot_tn(t.astype(BF16), b.astype(BF16)) for t, b in zip(tla, bt)]
    qop = [_dot_tn(w, cat0([b, k]).astype(BF16)) for w, b, k in zip(uv, bt, kt)]
    while side_work:
        breathe()
    return gop, y0, pop, qop


def _prompt_kernel(xn_ref, xt_ref, pt_ref, ng_ref, win_ref, wo_ref, wg_ref, wp_ref, fg_ref, *refs,
                   tc, n_t, n_tiles):
    prm_refs = refs[:len(_PRM_NAMES)]
    headones_ref, tri_ref, bcast_ref = refs[len(_PRM_NAMES):len(_PRM_NAMES) + 3]
    (y_ref, conv_out_ref, shift_out_ref, wkv_out_ref) = refs[len(_PRM_NAMES) + 3:len(_PRM_NAMES) + 7]
    (z_s, ycat_s, ubuf, zsbuf, s_ref, la_s, lr_s, rb_s, rk_s, bt_s, kt_s, v_s, gam_s, o_s,
     g_sc, y0_sc, p_sc, q_sc, bonus_p, gr_p, xn_s) = refs[len(_PRM_NAMES) + 7:]

    s = pl.program_id(0)
    head_t = lax.rem(s, jnp.int32(n_t))
    tail_t = lax.rem(s + (n_t - 1), jnp.int32(n_t))
    prm = _load_prm(prm_refs)
    headones = headones_ref[...]
    n2 = 2 * CHUNK
    n_chunks = tc // CHUNK

    @pl.when(s == 0)
    def _():
        for ref in (g_sc, y0_sc, p_sc, q_sc, gam_s, bonus_p, gr_p, ycat_s, s_ref):
            ref[...] = jnp.zeros(ref.shape, ref.dtype)

    @pl.when(head_t == 0)
    def _():
        ubuf[0:CARRY_ROWS, :] = jnp.zeros((CARRY_ROWS, CONV_CH), F32)
        zsbuf[0:CARRY_ROWS, :] = jnp.zeros((CARRY_ROWS, SHIFT_W), F32)

    @pl.when(tail_t == 0)
    def _():
        s_ref[...] = jnp.zeros(s_ref.shape, F32)

    tail = {}

    def tail_state_pass(c):
        def run():
            for j in range(N_PAIRS):
                idx = c * N_PAIRS + j
                lanes = slice(j * LANES, (j + 1) * LANES)
                s_old = s_ref[j]
                s_bf = s_old.astype(BF16)
                yst = _dot_nt(g_sc[idx], s_bf) + y0_sc[idx]
                o_s[c * CHUNK:(c + 1) * CHUNK, lanes] = yst[0:CHUNK] + yst[CHUNK:n2]
                gam = gam_s[c * CHUNK:c * CHUNK + 1, lanes]
                s_ref[j] = s_old * gam + _dot(s_bf, p_sc[idx]) + q_sc[idx]
        return run

    def tail_post():
        ycat_s[:, CONV_CH:2 * CONV_CH] = _rwkv_post(o_s[...], bonus_p[...], gr_p[...], prm, headones)

    def tail_out_proj():
        tail["h1"] = xt_ref[...] + _dot(ycat_s[...].astype(BF16), wo_ref[...])

    def tail_gate():
        tail["gate"] = _sigmoid(_dot(tail["h1"].astype(BF16), wg_ref[...]))
        tail["pe"] = _dot(pt_ref[...].astype(BF16), wp_ref[...])

    def tail_norm():
        h2 = tail["h1"] + tail["gate"] * tail["pe"]
        ms = jnp.mean(h2 * h2, axis=-1, keepdims=True)
        y_ref[...] = h2 * lax.rsqrt(ms + RMS_EPS) * fg_ref[...]

    def start_projection(x_ref):
        x = x_ref[...]
        xn = (x * lax.rsqrt(jnp.mean(x * x, axis=-1, keepdims=True) + RMS_EPS) * ng_ref[...]).astype(BF16)
        xn_s[...] = xn
        z_s[:, OFF_ZS:OFF_ZS + SHIFT_W] = _dot(xn, win_ref[:, OFF_ZS:OFF_ZS + SHIFT_W])

    @pl.when(s == 0)
    def _():
        start_projection(xt_ref)

    def project(c0):
        def run():
            z_s[:, c0:c0 + CONV_CH] = _dot(xn_s[...], win_ref[:, c0:c0 + CONV_CH])
        return run

    side_queue = [tail_state_pass(0), project(OFF_GR), tail_state_pass(1), project(0),
                  tail_state_pass(2), project(CONV_CH), tail_state_pass(3), project(2 * CONV_CH),
                  tail_post, project(3 * CONV_CH), tail_out_proj, tail_gate, tail_norm]
    assert n_chunks == 4

    def project_next(count=2):
        for _ in range(count):
            if side_queue:
                side_queue.pop(0)()

    zs = z_s[:, OFF_ZS:OFF_ZS + SHIFT_W]
    zsbuf[CARRY_ROWS:CARRY_ROWS + tc, :] = zs
    zprev = zsbuf[CARRY_ROWS - 1:CARRY_ROWS - 1 + tc, :]
    zm = zs + (zprev - zs) * prm["mu"]
    zsbuf[CARRY_ROWS - 1:CARRY_ROWS, :] = zsbuf[CARRY_ROWS + tc - 1:CARRY_ROWS + tc, :]

    project_next()
    r = zm[:, 0:RW]
    k = zm[:, RW:2 * RW]
    v = zm[:, 2 * RW:3 * RW]
    wa = zm[:, 3 * RW:3 * RW + 2 * LORA]
    lw, a, kk, kmod, bonus = _rwkv_tokens(r, k, v, wa, prm, headones, project_next)

    g = _const_dot(tri_ref[...], lw, CUM_PARTS)
    gc = jnp.concatenate(
        [jnp.broadcast_to(g[(c + 1) * CHUNK - 1:(c + 1) * CHUNK, :], (CHUNK, RW)) for c in range(tc // CHUNK)],
        axis=0)
    project_next()
    eng = jnp.exp(-g)
    etail = jnp.exp(gc - g)
    b = kk * a
    la_s[...] = -kk * jnp.exp(g - lw)
    lr_s[...] = r * jnp.exp(g)
    project_next()
    rb_s[...] = b * eng
    rk_s[...] = kmod * eng
    bt_s[...] = b * etail
    kt_s[...] = kmod * etail
    v_s[...] = v
    while side_queue:
        project_next()
    gam_s[...] = jnp.exp(gc)

    row = lax.broadcasted_iota(jnp.int32, (n2, n2), 0)
    col = lax.broadcasted_iota(jnp.int32, (n2, n2), 1)
    same = (row // CHUNK) == (col // CHUNK)
    strict = same & (row > col)
    incl = same & (row >= col)

    cw = prm["conv_w"]

    def conv_rows(r0):
        def run():
            rows = slice(r0, r0 + CHUNK)
            u = z_s[rows, CONV_CH:2 * CONV_CH] * z_s[rows, 2 * CONV_CH:3 * CONV_CH]
            ubuf[CARRY_ROWS + r0:CARRY_ROWS + r0 + CHUNK, :] = u
            um1 = ubuf[CARRY_ROWS - 1 + r0:CARRY_ROWS - 1 + r0 + CHUNK, :]
            um2 = ubuf[CARRY_ROWS - 2 + r0:CARRY_ROWS - 2 + r0 + CHUNK, :]
            conv = cw[0:1, :] * um2 + cw[1:2, :] * um1 + cw[2:3, :] * u
            ycat_s[rows, 0:CONV_CH] = z_s[rows, 0:CONV_CH] * conv * _silu(z_s[rows, 3 * CONV_CH:4 * CONV_CH])
        return run

    srcs = {"la": la_s, "lr": lr_s, "rb": rb_s, "rk": rk_s, "bt": bt_s, "kt": kt_s, "v": v_s}
    members = [(c, j) for c in range(n_chunks) for j in range(N_PAIRS)]

    def ld(name):
        ref = srcs[name]
        return [ref[c * CHUNK:(c + 1) * CHUNK, j * LANES:(j + 1) * LANES] for c, j in members]

    lane_bcast = [bcast_ref[j] for j in range(SUBLANES - 1)]
    gop, y0, pop, qop = _chunk_operators(
        ld, strict, incl, lane_bcast,
        [lambda: start_projection(xn_ref)] + [conv_rows(c * CHUNK) for c in range(n_chunks)])
    for (c, j), g_, y_, p_, q_ in zip(members, gop, y0, pop, qop):
        idx = c * N_PAIRS + j
        g_sc[idx] = g_.astype(BF16)
        y0_sc[idx] = y_
        p_sc[idx] = p_.astype(BF16)
        q_sc[idx] = q_

    ubuf[CARRY_ROWS - 2:CARRY_ROWS, :] = ubuf[CARRY_ROWS + tc - 2:CARRY_ROWS + tc, :]

    bonus_p[...] = bonus
    gr_p[...] = z_s[:, OFF_GR:OFF_GR + RW]

    @pl.when((head_t == n_t - 1) & (s < n_tiles))
    def _():
        shift_out_ref[0] = zsbuf[CARRY_ROWS - 1:CARRY_ROWS, :]
        conv_out_ref[0] = ubuf[CARRY_ROWS - 2:CARRY_ROWS, :]

    @pl.when((tail_t == n_t - 1) & (s > 0))
    def _():
        for h in range(N_HEADS):
            j, i = divmod(h, 2)
            blk = s_ref[j]
            wkv_out_ref[0, h] = blk[i * HEAD_DIM:(i + 1) * HEAD_DIM, i * HEAD_DIM:(i + 1) * HEAD_DIM]


def _prompt_layer(x2d, p2d, w, bsz, t_len, tc):
    n_t = t_len // tc
    blockones = (jnp.arange(tc)[:, None] // CHUNK) == (jnp.arange(tc)[None, :] // CHUNK)
    lower = jnp.arange(tc)[:, None] >= jnp.arange(tc)[None, :]
    tri = (blockones & lower).astype(BF16)
    n_tiles = bsz * n_t
    n2 = 2 * CHUNK
    idx = jnp.arange(n2)
    same_blk = (idx[:, None] // SUBLANES) == (idx[None, :] // SUBLANES)
    lane_bcast = jnp.stack([same_blk & (idx[:, None] % SUBLANES == j) for j in range(SUBLANES - 1)]).astype(BF16)
    kern = functools.partial(_prompt_kernel, tc=tc, n_t=n_t, n_tiles=n_tiles)
    big = lambda: pltpu.VMEM((tc, RW), F32)
    n_blk = (tc // CHUNK) * N_PAIRS
    op = lambda dt: pltpu.VMEM((n_blk, LANES, LANES), dt)
    head = lambda s: jnp.minimum(s, n_tiles - 1)
    tail = lambda s: jnp.maximum(s - 1, 0)
    next_tile = lambda width: pl.BlockSpec((tc, width), lambda s: (jnp.minimum(s + 1, n_tiles - 1), 0))
    tail_tile = lambda width: pl.BlockSpec((tc, width), lambda s: (tail(s), 0))
    const = lambda shape: pl.BlockSpec(shape, lambda s: (0, 0), pipeline_mode=pl.Buffered(1))
    return pl.pallas_call(
        kern,
        grid=(n_tiles + 1,),
        in_specs=[next_tile(D_MODEL), tail_tile(D_MODEL), tail_tile(D_PLE), const((1, D_MODEL)),
                  const((D_MODEL, IN_W)), const((D_MODEL, D_MODEL)), const((D_MODEL, D_MODEL)),
                  const((D_PLE, D_MODEL)), const((1, D_MODEL))]
        + _prm_specs(1)
        + [const((HEADSUM_W, HEADSUM_W)), const((tc, tc)),
           pl.BlockSpec((SUBLANES - 1, n2, n2), lambda s: (0, 0, 0), pipeline_mode=pl.Buffered(1))],
        out_specs=[
            tail_tile(D_MODEL),
            pl.BlockSpec((1, 2, CONV_CH), lambda s: (head(s) // n_t, 0, 0)),
            pl.BlockSpec((1, 1, SHIFT_W), lambda s: (head(s) // n_t, 0, 0)),
            pl.BlockSpec((1, N_HEADS, HEAD_DIM, HEAD_DIM), lambda s: (tail(s) // n_t, 0, 0, 0)),
        ],
        out_shape=[
            jax.ShapeDtypeStruct((bsz * t_len, D_MODEL), F32),
            jax.ShapeDtypeStruct((bsz, 2, CONV_CH), F32),
            jax.ShapeDtypeStruct((bsz, 1, SHIFT_W), F32),
            jax.ShapeDtypeStruct((bsz, N_HEADS, HEAD_DIM, HEAD_DIM), F32),
        ],
        scratch_shapes=[
            pltpu.VMEM((tc, IN_W), F32),
            pltpu.VMEM((tc, D_MODEL), F32),
            pltpu.VMEM((CARRY_ROWS + tc, CONV_CH), F32),
            pltpu.VMEM((CARRY_ROWS + tc, SHIFT_W), F32),
            pltpu.VMEM((N_PAIRS, LANES, LANES), F32),
            big(), big(), big(), big(), big(), big(), big(), big(), big(),
            op(BF16), op(F32), op(BF16), op(F32),
            big(), big(),
            pltpu.VMEM((tc, D_MODEL), BF16),
        ],
        compiler_params=pltpu.CompilerParams(
            dimension_semantics=("arbitrary",), vmem_limit_bytes=VMEM_LIMIT),
        name="prompt_layer",
    )(x2d, x2d, p2d, w["norm_g"].reshape(1, D_MODEL), w["w_in"], w["w_out"], w["w_pg"], w["w_pp"],
      w["final_g"].reshape(1, D_MODEL), *w["prm"], w["headones"], tri, lane_bcast)


def _sample_front_kernel(x_ref, ng_ref, win_ref, cb_ref, sb_ref, *refs):
    prm_refs = refs[:len(_PRM_NAMES)]
    headones_ref = refs[len(_PRM_NAMES)]
    (ya_ref, conv_out_ref, shift_out_ref, gr_ref, bonus_ref,
     nkk_ref, w_ref, b_ref, km_ref, v_ref, r_ref) = refs[len(_PRM_NAMES) + 1:]
    prm = _load_prm(prm_refs)
    headones = headones_ref[...]

    x = x_ref[...]
    xn = (x * lax.rsqrt(jnp.mean(x * x, axis=-1, keepdims=True) + RMS_EPS) * ng_ref[...]).astype(BF16)
    z = _dot(xn, win_ref[...])

    u = z[:, CONV_CH:2 * CONV_CH] * z[:, 2 * CONV_CH:3 * CONV_CH]
    cb0 = cb_ref[:, 0:CONV_CH]
    cb1 = cb_ref[:, CONV_CH:2 * CONV_CH]
    cw = prm["conv_w"]
    conv = cw[0:1, :] * cb0 + cw[1:2, :] * cb1 + cw[2:3, :] * u
    ya_ref[...] = z[:, 0:CONV_CH] * conv * _silu(z[:, 3 * CONV_CH:4 * CONV_CH])
    conv_out_ref[:, 0:CONV_CH] = cb1
    conv_out_ref[:, CONV_CH:2 * CONV_CH] = u

    zs = z[:, OFF_ZS:OFF_ZS + SHIFT_W]
    shift_out_ref[...] = zs
    gr_ref[...] = z[:, OFF_GR:OFF_GR + RW]
    zm = zs + (sb_ref[...] - zs) * prm["mu"]
    r = zm[:, 0:RW]
    k = zm[:, RW:2 * RW]
    v = zm[:, 2 * RW:3 * RW]
    wa = zm[:, 3 * RW:3 * RW + 2 * LORA]
    lw, a, kk, kmod, bonus = _rwkv_tokens(r, k, v, wa, prm, headones)
    bonus_ref[...] = bonus
    nkk_ref[...] = (-kk).T
    w_ref[...] = jnp.exp(lw).T
    b_ref[...] = (kk * a).T
    km_ref[...] = kmod.T
    v_ref[...] = v.T
    r_ref[...] = r.T


def _sample_step_kernel(s_ref, nkk_ref, w_ref, b_ref, km_ref, v_ref, r_ref, s1_ref, y_ref):
    nkk = nkk_ref[0]
    w = w_ref[0]
    b = b_ref[0]
    km = km_ref[0]
    r = r_ref[0]
    for vi in range(HEAD_DIM):
        s0 = s_ref[0, vi]
        sa = jnp.sum(s0 * nkk, axis=0, keepdims=True)
        s1 = s0 * w + sa * b + v_ref[0, vi:vi + 1, :] * km
        s1_ref[0, vi] = s1
        y_ref[0, vi:vi + 1, :] = jnp.sum(s1 * r, axis=0, keepdims=True)


def _sample_back_kernel(o_ref, bonus_ref, gr_ref, ya_ref, x_ref, p_ref, wo_ref, wg_ref, wp_ref, fg_ref, *refs):
    prm_refs = refs[:len(_PRM_NAMES)]
    headones_ref = refs[len(_PRM_NAMES)]
    y_ref = refs[len(_PRM_NAMES) + 1]
    prm = _load_prm(prm_refs)
    y_r = _rwkv_post(o_ref[...].T, bonus_ref[...], gr_ref[...], prm, headones_ref[...])
    ycat = jnp.concatenate([ya_ref[...], y_r], axis=1).astype(BF16)
    h1 = x_ref[...] + _dot(ycat, wo_ref[...])
    gate = _sigmoid(_dot(h1.astype(BF16), wg_ref[...]))
    pe = _dot(p_ref[...].astype(BF16), wp_ref[...])
    h2 = h1 + gate * pe
    ms = jnp.mean(h2 * h2, axis=-1, keepdims=True)
    y_ref[...] = h2 * lax.rsqrt(ms + RMS_EPS) * fg_ref[...]


def _full(shape):
    nd = len(shape)
    return pl.BlockSpec(shape, lambda *_: (0,) * nd)


def _sample_layer(x2d, p2d, state_conv, state_shift, state_hvkb, w):
    n = x2d.shape[0]
    params = pltpu.CompilerParams(dimension_semantics=("arbitrary",), vmem_limit_bytes=VMEM_LIMIT)
    row = jax.ShapeDtypeStruct((n, RW), F32)
    col = jax.ShapeDtypeStruct((RW, n), F32)
    cb = state_conv.reshape(n, 2 * CONV_CH)
    front = pl.pallas_call(
        _sample_front_kernel,
        grid=(1,),
        in_specs=[_full((n, D_MODEL)), _full((1, D_MODEL)), _full((D_MODEL, IN_W)),
                  _full((n, 2 * CONV_CH)), _full((n, SHIFT_W))]
        + _prm_specs(1) + [_full((HEADSUM_W, HEADSUM_W))],
        out_specs=[_full((n, CONV_CH)), _full((n, 2 * CONV_CH)), _full((n, SHIFT_W)),
                   _full((n, RW)), _full((n, RW))] + [_full((RW, n))] * 6,
        out_shape=[jax.ShapeDtypeStruct((n, CONV_CH), F32), jax.ShapeDtypeStruct((n, 2 * CONV_CH), F32),
                   jax.ShapeDtypeStruct((n, SHIFT_W), F32), row, row] + [col] * 6,
        compiler_params=params,
        name="sample_front",
    )(x2d, w["norm_g"].reshape(1, D_MODEL), w["w_in"], cb, state_shift, *w["prm"], w["headones"])
    ya, conv_new, shift_new, g_r, bonus, nkk, dec, b, km, v, r = front

    hk = lambda a: a.reshape(N_HEADS, HEAD_DIM, n)
    vspec = pl.BlockSpec((1, HEAD_DIM, n), lambda h: (h, 0, 0))
    sspec = pl.BlockSpec((1, HEAD_DIM, HEAD_DIM, n), lambda h: (h, 0, 0, 0))
    s1, y = pl.pallas_call(
        _sample_step_kernel,
        grid=(N_HEADS,),
        in_specs=[sspec] + [vspec] * 6,
        out_specs=[sspec, vspec],
        out_shape=[jax.ShapeDtypeStruct((N_HEADS, HEAD_DIM, HEAD_DIM, n), F32),
                   jax.ShapeDtypeStruct((N_HEADS, HEAD_DIM, n), F32)],
        compiler_params=params,
        name="sample_step",
    )(state_hvkb, hk(nkk), hk(dec), hk(b), hk(km), hk(v), hk(r))

    y_out = pl.pallas_call(
        _sample_back_kernel,
        grid=(1,),
        in_specs=[_full((RW, n)), _full((n, RW)), _full((n, RW)), _full((n, CONV_CH)),
                  _full((n, D_MODEL)), _full((n, D_PLE)), _full((D_MODEL, D_MODEL)),
                  _full((D_MODEL, D_MODEL)), _full((D_PLE, D_MODEL)), _full((1, D_MODEL))]
        + _prm_specs(1) + [_full((HEADSUM_W, HEADSUM_W))],
        out_specs=_full((n, D_MODEL)),
        out_shape=jax.ShapeDtypeStruct((n, D_MODEL), F32),
        compiler_params=params,
        name="sample_back",
    )(y.reshape(RW, n), bonus, g_r, ya, x2d, p2d, w["w_out"], w["w_pg"], w["w_pp"],
      w["final_g"].reshape(1, D_MODEL), *w["prm"], w["headones"])
    return y_out, conv_new.reshape(n, 2, CONV_CH), shift_new, s1


def _layer_prompt(x, p, w, tc=256, tm=256):
    bsz, t_len, _ = x.shape
    x2d = x.reshape(bsz * t_len, D_MODEL)
    y, conv_new, shift_new, wkv_new = _prompt_layer(x2d, p.reshape(bsz * t_len, D_PLE), w, bsz, t_len, tc)
    return y.reshape(bsz, t_len, D_MODEL), conv_new, shift_new.reshape(bsz, SHIFT_W), wkv_new


def _layer_sample(x, p, state_conv, state_shift, state_wkv, w):
    n = x.shape[0]
    s_hvkb = jnp.transpose(state_wkv, (1, 2, 3, 0))
    y, conv_new, shift_new, s1 = _sample_layer(x.reshape(n, D_MODEL), p.reshape(n, D_PLE), state_conv,
                                               state_shift, s_hvkb, w)
    return y.reshape(n, 1, D_MODEL), conv_new, shift_new, jnp.transpose(s1, (3, 0, 1, 2))


def kernel(x_prompt, x_sample, p_prompt, p_sample, state_conv, state_shift, state_wkv, norm_g, w_in, conv_w, mu_shift, w0, w_up, a0, a_up, k_k, k_a, r_k, ln_w, ln_b, w_out, w_pg, w_pp, final_g):
    depth = norm_g.shape[0]
    assert depth == 1
    i = 0
    head_id = jnp.arange(HEADSUM_W) // HEAD_DIM
    w = {
        "norm_g": norm_g[i],
        "w_in": w_in[i].astype(BF16),
        "w_out": w_out[i].astype(BF16),
        "w_pg": w_pg[i].astype(BF16),
        "w_pp": w_pp[i].astype(BF16),
        "final_g": final_g,
        "prm": _prm_arrays(conv_w[i], mu_shift[i], w0[i], w_up[i], a0[i], a_up[i], k_k[i], k_a[i],
                           r_k[i].reshape(RW), ln_w[i], ln_b[i]),
        "headones": (head_id[:, None] == head_id[None, :]).astype(BF16),
    }
    yp, cp, sp, wp = _layer_prompt(x_prompt, p_prompt[i], w)
    ys, cs, ss, ws = _layer_sample(x_sample, p_sample[i], state_conv[i], state_shift[i], state_wkv[i], w)
    return (yp, ys, cp[None], sp[None], wp[None], cs[None], ss[None], ws[None])
```

```python
import functools
import math

import jax
import jax.numpy as jnp
from jax import lax
from jax.experimental import pallas as pl
from jax.experimental.pallas import tpu as pltpu

F32 = jnp.float32
BF16 = jnp.bfloat16

D_MODEL = 1024
CONV_CH = 512
RW = 512
HEAD_DIM = 64
N_HEADS = 8
N_PAIRS = N_HEADS // 2
LORA = 64
D_PLE = 256
SHIFT_W = 3 * RW + 2 * LORA
IN_W = 4 * CONV_CH + SHIFT_W + RW
OFF_ZS = 4 * CONV_CH
OFF_GR = OFF_ZS + SHIFT_W
RMS_EPS = 1e-6
GN_EPS = 64e-5
DECAY_SCALE = math.exp(-0.5)

LANES = 128
SUBLANES = 8
MXU_DIM = 256
HEADSUM_W = MXU_DIM
SUM_PARTS = 1
CUM_PARTS = 2
LORA_PASSES = 1
CHUNK = 64
CARRY_ROWS = 8
VMEM_LIMIT = 56 * 1024 * 1024


def _dot(a, b):
    return jnp.dot(a, b, preferred_element_type=F32)


def _dot_nt(a, b):
    return lax.dot_general(a, b, (((1,), (1,)), ((), ())), preferred_element_type=F32)


def _dot_tn(a, b):
    return lax.dot_general(a, b, (((0,), (0,)), ((), ())), preferred_element_type=F32)


def _split(x, parts):
    out = []
    rem = x
    for i in range(parts):
        t = rem.astype(BF16)
        out.append(t)
        if i + 1 < parts:
            rem = rem - t.astype(F32)
    return out


def _const_dot(c_bf16, x, parts):
    acc = None
    for t in _split(x, parts):
        d = _dot(c_bf16, t)
        acc = d if acc is None else acc + d
    return acc


def _x_dot_const(x, c_bf16, parts):
    acc = None
    for t in _split(x, parts):
        d = _dot(t, c_bf16)
        acc = d if acc is None else acc + d
    return acc


def _head_sum(x, headones):
    w = headones.shape[0]
    halves = [_x_dot_const(x[:, i * w:(i + 1) * w], headones, SUM_PARTS) for i in range(x.shape[1] // w)]
    return jnp.concatenate(halves, axis=1)


def _mm(a, b, passes):
    if passes == 1:
        return _dot(a.astype(BF16), b.astype(BF16))
    ah, al = _split(a, 2)
    bh, bl = _split(b, 2)
    return _dot(ah, bh) + _dot(al, bh) + _dot(ah, bl)


def _sigmoid(x):
    return 1.0 / (1.0 + jnp.exp(-x))


def _silu(x):
    return x * _sigmoid(x)


def _rwkv_tokens(r, k, v, wa, prm, headones, between_steps=lambda: None):
    n = r.shape[0]
    lane = lax.broadcasted_iota(jnp.int32, (n, LANES), 1)
    th = jnp.where(lane < LORA, jnp.tanh(wa), wa)
    lora = _mm(th, prm["wlora"], LORA_PASSES)
    lw = -DECAY_SCALE * _sigmoid(prm["w0"] + lora[:, :RW])
    a = _sigmoid(prm["a0"] + lora[:, RW:])
    between_steps()
    kk = k * prm["k_k"]
    ss = _head_sum(kk * kk, headones)
    kk = kk / jnp.maximum(jnp.sqrt(ss), 1e-12)
    between_steps()
    kmod = k * (1.0 + (a - 1.0) * prm["k_a"])
    bonus = _head_sum(r * kmod * prm["r_k"], headones) * v
    between_steps()
    return lw, a, kk, kmod, bonus


def _rwkv_post(o, bonus, g_r, prm, headones):
    mean = _head_sum(o, headones) * (1.0 / HEAD_DIM)
    d = o - mean
    var = _head_sum(d * d, headones) * (1.0 / HEAD_DIM)
    on = d * lax.rsqrt(var + GN_EPS)
    on = on * prm["ln_w"] + prm["ln_b"]
    return (on + bonus) * _silu(g_r)


_PRM_NAMES = ("conv_w", "mu", "w0", "a0", "wlora", "k_k", "k_a", "r_k", "ln_w", "ln_b")


def _prm_arrays(conv_w, mu_shift, w0, w_up, a0, a_up, k_k, k_a, r_k, ln_w, ln_b):
    wlora = jnp.zeros((2 * LORA, 2 * RW), F32)
    wlora = wlora.at[:LORA, :RW].set(w_up).at[LORA:, RW:].set(a_up)
    row = lambda x: x.reshape(1, -1)
    return (conv_w, row(mu_shift), row(w0), row(a0), wlora, row(k_k), row(k_a),
            row(r_k), row(ln_w), row(ln_b))


def _prm_specs(grid_rank):
    zero = (lambda *_: (0, 0))
    shapes = ((3, CONV_CH), (1, SHIFT_W), (1, RW), (1, RW), (2 * LORA, 2 * RW),
              (1, RW), (1, RW), (1, RW), (1, RW), (1, RW))
    return [pl.BlockSpec(s, zero) for s in shapes]


def _load_prm(refs):
    return {n: r[...] for n, r in zip(_PRM_NAMES, refs)}


def _stack_heads(x):
    lane = lax.broadcasted_iota(jnp.int32, x.shape, 1)
    lo = jnp.where(lane < HEAD_DIM, x, 0.0)
    hi = jnp.where(lane >= HEAD_DIM, x, 0.0)
    return jnp.concatenate([lo, hi], axis=0)


def _stack_heads_bf16(x):
    lane = lax.broadcasted_iota(jnp.int32, x.shape, 1)
    lo = jnp.where(lane < HEAD_DIM, 1.0, 0.0).astype(BF16)
    xb = x.astype(BF16)
    return jnp.concatenate([xb * lo, xb * (1.0 - lo)], axis=0)


def _bdot(a, b):
    return _dot(a.astype(BF16), b.astype(BF16))


def _diag_block_inverses_minus_eye(mats, lane_bcast):
    n = mats[0].shape[0]
    nb = n // SUBLANES
    row = lax.broadcasted_iota(jnp.int32, (n, n), 0)
    col = lax.broadcasted_iota(jnp.int32, (n, n), 1)
    in_diag = ((row // SUBLANES) == (col // SUBLANES)) & (row > col)
    packed = []
    for a in mats:
        d = jnp.where(in_diag, a, 0.0)
        acc = d[0:SUBLANES]
        for m in range(1, nb):
            acc = acc + d[m * SUBLANES:(m + 1) * SUBLANES]
        packed.append(acc)
    d_all = jnp.concatenate(packed, axis=0).astype(BF16)
    sub = lax.broadcasted_iota(jnp.int32, (SUBLANES, n), 0)
    lane = lax.broadcasted_iota(jnp.int32, (SUBLANES, n), 1)
    eye8 = jnp.where(sub == lane % SUBLANES, 1.0, 0.0)
    ts = [eye8 for _ in mats]
    for j in range(SUBLANES - 1):
        spread = _dot(d_all, lane_bcast[j])
        ts = [t + spread[i * SUBLANES:(i + 1) * SUBLANES] * jnp.broadcast_to(t[j:j + 1, :], (SUBLANES, n))
              for i, t in enumerate(ts)]
    blk = lax.broadcasted_iota(jnp.int32, (SUBLANES, n), 1) // SUBLANES
    out = []
    for t in ts:
        tm1 = t - eye8
        out.append(jnp.concatenate([jnp.where(blk == m, tm1, 0.0) for m in range(nb)], axis=0))
    return out


def _unit_lower_inverse_minus_eye(mats, lane_bcast, between_levels=lambda: None):
    n = mats[0].shape[0]
    row = lax.broadcasted_iota(jnp.int32, (n, n), 0)
    col = lax.broadcasted_iota(jnp.int32, (n, n), 1)
    xs = _diag_block_inverses_minus_eye(mats, lane_bcast)
    size = 2 * SUBLANES
    while size <= CHUNK:
        half = size // 2
        sel = ((row // size) == (col // size)) & ((row % size) >= half) & ((col % size) < half)
        als = [jnp.where(sel, a, 0.0) for a in mats]
        ps = [al + _bdot(x, al) for x, al in zip(xs, als)]
        xs = [x + p + _bdot(p, x) for x, p in zip(xs, ps)]
        between_levels()
        size *= 2
    return xs


def _chunk_operators(ld, strict, incl, lane_bcast, side_work=()):
    side_work = list(side_work)

    def breathe():
        if side_work:
            side_work.pop(0)()

    cat0 = lambda xs: jnp.concatenate(xs, axis=0)
    la = [_stack_heads(x) for x in ld("la")]
    lr = [_stack_heads(x) for x in ld("lr")]
    n2 = 2 * CHUNK
    amat = [_dot_nt(cat0([a, b]).astype(BF16), cat0([c, d]).astype(BF16))
            for a, b, c, d in zip(la, lr, ld("rb"), ld("rk"))]

    def block_diag_pair(m):
        swapped = pltpu.roll(m, CHUNK, axis=1)
        first = cat0([m[0:CHUNK], swapped[CHUNK:n2]])
        second = cat0([swapped[0:CHUNK], m[CHUNK:n2]])
        return first, second

    top = [block_diag_pair(m[0:n2]) for m in amat]
    a_ab = [jnp.where(strict, t[0], 0.0) for t in top]
    xs = _unit_lower_inverse_minus_eye(a_ab, lane_bcast, breathe)
    vst = [_stack_heads_bf16(x) for x in ld("v")]
    av = [_dot(jnp.where(strict, t[1], 0.0).astype(BF16), v) for t, v in zip(top, vst)]
    breathe()
    uv0 = [y + _bdot(x, y) for x, y in zip(xs, av)]
    tla = [y + _bdot(x, y) for x, y in zip(xs, la)]
    breathe()
    bottom = [block_diag_pair(m[n2:2 * n2]) for m in amat]
    a_rb = [jnp.where(incl, t[0], 0.0) for t in bottom]
    a_rk = [jnp.where(incl, t[1], 0.0) for t in bottom]
    gop = [y + _bdot(a, t) for y, a, t in zip(lr, a_rb, tla)]
    uv = [cat0([u.astype(BF16), v]) for u, v in zip(uv0, vst)]
    y0 = [_dot(jnp.concatenate([a, b], axis=1).astype(BF16), w) for a, b, w in zip(a_rb, a_rk, uv)]
    breathe()
    bt = [_stack_heads_bf16(x) for x in ld("bt")]
    kt = [_stack_heads_bf16(x) for x in ld("kt")]
    pop = [_dot_tn(t.astype(BF16), b) for t, b in zip(tla, bt)]
    qop = [_dot_tn(w, cat0([b, k])) for w, b, k in zip(uv, bt, kt)]
    while side_work:
        breathe()
    return gop, y0, pop, qop


def _prompt_kernel(xh_ref, xt_ref, pt_ref, ng_ref, win_ref, wo_ref, wg_ref, wp_ref, fg_ref, *refs,
                   tc, n_t, n_tiles):
    prm_refs = refs[:len(_PRM_NAMES)]
    headones_ref, tri_ref, bcast_ref = refs[len(_PRM_NAMES):len(_PRM_NAMES) + 3]
    (y_ref, conv_out_ref, shift_out_ref, wkv_out_ref) = refs[len(_PRM_NAMES) + 3:len(_PRM_NAMES) + 7]
    (z_s, ycat_s, ubuf, zsbuf, s_ref, la_s, lr_s, rb_s, rk_s, bt_s, kt_s, v_s, gam_s, o_s,
     g_sc, y0_sc, p_sc, q_sc, bonus_p, gr_p) = refs[len(_PRM_NAMES) + 7:]

    s = pl.program_id(0)
    head_t = lax.rem(s, jnp.int32(n_t))
    tail_t = lax.rem(s + (n_t - 1), jnp.int32(n_t))
    prm = _load_prm(prm_refs)
    headones = headones_ref[...]
    n2 = 2 * CHUNK
    n_chunks = tc // CHUNK

    @pl.when(s == 0)
    def _():
        for ref in (g_sc, y0_sc, p_sc, q_sc, gam_s, bonus_p, gr_p, ycat_s, s_ref):
            ref[...] = jnp.zeros(ref.shape, ref.dtype)

    @pl.when(head_t == 0)
    def _():
        ubuf[0:CARRY_ROWS, :] = jnp.zeros((CARRY_ROWS, CONV_CH), F32)
        zsbuf[0:CARRY_ROWS, :] = jnp.zeros((CARRY_ROWS, SHIFT_W), F32)

    @pl.when(tail_t == 0)
    def _():
        s_ref[...] = jnp.zeros(s_ref.shape, F32)

    tail = {}

    def tail_state_pass(c):
        def run():
            for j in range(N_PAIRS):
                idx = c * N_PAIRS + j
                lanes = slice(j * LANES, (j + 1) * LANES)
                s_old = s_ref[j]
                s_bf = s_old.astype(BF16)
                yst = _dot_nt(g_sc[idx], s_bf) + y0_sc[idx]
                o_s[c * CHUNK:(c + 1) * CHUNK, lanes] = yst[0:CHUNK] + yst[CHUNK:n2]
                gam = gam_s[c * CHUNK:c * CHUNK + 1, lanes]
                s_ref[j] = s_old * gam + _dot(s_bf, p_sc[idx]) + q_sc[idx]
        return run

    def tail_post():
        ycat_s[:, CONV_CH:2 * CONV_CH] = _rwkv_post(o_s[...], bonus_p[...], gr_p[...], prm, headones)

    def tail_out_proj():
        tail["h1"] = xt_ref[...] + _dot(ycat_s[...].astype(BF16), wo_ref[...])

    def tail_gate():
        tail["gate"] = _sigmoid(_dot(tail["h1"].astype(BF16), wg_ref[...]))
        tail["pe"] = _dot(pt_ref[...].astype(BF16), wp_ref[...])

    def tail_norm():
        h2 = tail["h1"] + tail["gate"] * tail["pe"]
        ms = jnp.mean(h2 * h2, axis=-1, keepdims=True)
        y_ref[...] = h2 * lax.rsqrt(ms + RMS_EPS) * fg_ref[...]

    x = xh_ref[...]
    xn = (x * lax.rsqrt(jnp.mean(x * x, axis=-1, keepdims=True) + RMS_EPS) * ng_ref[...]).astype(BF16)

    def project(c0):
        def run():
            z_s[:, c0:c0 + CONV_CH] = _dot(xn, win_ref[:, c0:c0 + CONV_CH])
        return run

    passes = [tail_state_pass(c) for c in range(n_chunks)]
    blocks = [project(c0) for c0 in (OFF_GR, 0, CONV_CH, 2 * CONV_CH, 3 * CONV_CH)]
    side_queue = []
    while passes or len(blocks) > 1:
        side_queue += passes[:1] + (blocks[:1] if len(blocks) > 1 else [])
        passes, blocks = passes[1:], (blocks[1:] if len(blocks) > 1 else blocks)
    side_queue += [tail_post, blocks[0], tail_out_proj, tail_gate, tail_norm]

    def project_next(count=2):
        for _ in range(count):
            if side_queue:
                side_queue.pop(0)()

    z_s[:, OFF_ZS:OFF_ZS + SHIFT_W] = _dot(xn, win_ref[:, OFF_ZS:OFF_ZS + SHIFT_W])

    zs = z_s[:, OFF_ZS:OFF_ZS + SHIFT_W]
    zsbuf[CARRY_ROWS:CARRY_ROWS + tc, :] = zs
    zprev = zsbuf[CARRY_ROWS - 1:CARRY_ROWS - 1 + tc, :]
    zm = zs + (zprev - zs) * prm["mu"]
    zsbuf[CARRY_ROWS - 1:CARRY_ROWS, :] = zsbuf[CARRY_ROWS + tc - 1:CARRY_ROWS + tc, :]

    project_next()
    r = zm[:, 0:RW]
    k = zm[:, RW:2 * RW]
    v = zm[:, 2 * RW:3 * RW]
    wa = zm[:, 3 * RW:3 * RW + 2 * LORA]
    lw, a, kk, kmod, bonus = _rwkv_tokens(r, k, v, wa, prm, headones, project_next)

    g = _const_dot(tri_ref[...], lw, CUM_PARTS)
    gc = jnp.concatenate(
        [jnp.broadcast_to(g[(c + 1) * CHUNK - 1:(c + 1) * CHUNK, :], (CHUNK, RW)) for c in range(tc // CHUNK)],
        axis=0)
    project_next()
    eng = jnp.exp(-g)
    etail = jnp.exp(gc - g)
    b = kk * a
    la_s[...] = -kk * jnp.exp(g - lw)
    lr_s[...] = r * jnp.exp(g)
    project_next()
    rb_s[...] = b * eng
    rk_s[...] = kmod * eng
    bt_s[...] = b * etail
    kt_s[...] = kmod * etail
    v_s[...] = v
    while side_queue:
        project_next()
    gam_s[...] = jnp.exp(gc)

    row = lax.broadcasted_iota(jnp.int32, (n2, n2), 0)
    col = lax.broadcasted_iota(jnp.int32, (n2, n2), 1)
    same = (row // CHUNK) == (col // CHUNK)
    strict = same & (row > col)
    incl = same & (row >= col)

    cw = prm["conv_w"]

    def conv_rows(r0):
        def run():
            rows = slice(r0, r0 + CHUNK)
            u = z_s[rows, CONV_CH:2 * CONV_CH] * z_s[rows, 2 * CONV_CH:3 * CONV_CH]
            ubuf[CARRY_ROWS + r0:CARRY_ROWS + r0 + CHUNK, :] = u
            um1 = ubuf[CARRY_ROWS - 1 + r0:CARRY_ROWS - 1 + r0 + CHUNK, :]
            um2 = ubuf[CARRY_ROWS - 2 + r0:CARRY_ROWS - 2 + r0 + CHUNK, :]
            conv = cw[0:1, :] * um2 + cw[1:2, :] * um1 + cw[2:3, :] * u
            ycat_s[rows, 0:CONV_CH] = z_s[rows, 0:CONV_CH] * conv * _silu(z_s[rows, 3 * CONV_CH:4 * CONV_CH])
        return run

    srcs = {"la": la_s, "lr": lr_s, "rb": rb_s, "rk": rk_s, "bt": bt_s, "kt": kt_s, "v": v_s}
    members = [(c, j) for c in range(n_chunks) for j in range(N_PAIRS)]

    def ld(name):
        ref = srcs[name]
        return [ref[c * CHUNK:(c + 1) * CHUNK, j * LANES:(j + 1) * LANES] for c, j in members]

    lane_bcast = [bcast_ref[j] for j in range(SUBLANES - 1)]
    gop, y0, pop, qop = _chunk_operators(ld, strict, incl, lane_bcast,
                                         [conv_rows(c * CHUNK) for c in range(n_chunks)])
    for (c, j), g_, y_, p_, q_ in zip(members, gop, y0, pop, qop):
        idx = c * N_PAIRS + j
        g_sc[idx] = g_.astype(BF16)
        y0_sc[idx] = y_
        p_sc[idx] = p_.astype(BF16)
        q_sc[idx] = q_

    ubuf[CARRY_ROWS - 2:CARRY_ROWS, :] = ubuf[CARRY_ROWS + tc - 2:CARRY_ROWS + tc, :]

    bonus_p[...] = bonus
    gr_p[...] = z_s[:, OFF_GR:OFF_GR + RW]

    @pl.when((head_t == n_t - 1) & (s < n_tiles))
    def _():
        shift_out_ref[0] = zsbuf[CARRY_ROWS - 1:CARRY_ROWS, :]
        conv_out_ref[0] = ubuf[CARRY_ROWS - 2:CARRY_ROWS, :]

    @pl.when((tail_t == n_t - 1) & (s > 0))
    def _():
        for h in range(N_HEADS):
            j, i = divmod(h, 2)
            blk = s_ref[j]
            wkv_out_ref[0, h] = blk[i * HEAD_DIM:(i + 1) * HEAD_DIM, i * HEAD_DIM:(i + 1) * HEAD_DIM]


def _prompt_layer(x2d, p2d, w, bsz, t_len, tc):
    n_t = t_len // tc
    blockones = (jnp.arange(tc)[:, None] // CHUNK) == (jnp.arange(tc)[None, :] // CHUNK)
    lower = jnp.arange(tc)[:, None] >= jnp.arange(tc)[None, :]
    tri = (blockones & lower).astype(BF16)
    n_tiles = bsz * n_t
    n2 = 2 * CHUNK
    idx = jnp.arange(n2)
    same_blk = (idx[:, None] // SUBLANES) == (idx[None, :] // SUBLANES)
    lane_bcast = jnp.stack([same_blk & (idx[:, None] % SUBLANES == j) for j in range(SUBLANES - 1)]).astype(BF16)
    kern = functools.partial(_prompt_kernel, tc=tc, n_t=n_t, n_tiles=n_tiles)
    big = lambda: pltpu.VMEM((tc, RW), F32)
    n_blk = (tc // CHUNK) * N_PAIRS
    op = lambda dt: pltpu.VMEM((n_blk, LANES, LANES), dt)
    head = lambda s: jnp.minimum(s, n_tiles - 1)
    tail = lambda s: jnp.maximum(s - 1, 0)
    head_tile = lambda width: pl.BlockSpec((tc, width), lambda s: (head(s), 0))
    tail_tile = lambda width: pl.BlockSpec((tc, width), lambda s: (tail(s), 0))
    const = lambda shape: pl.BlockSpec(shape, lambda s: (0, 0), pipeline_mode=pl.Buffered(1))
    return pl.pallas_call(
        kern,
        grid=(n_tiles + 1,),
        in_specs=[head_tile(D_MODEL), tail_tile(D_MODEL), tail_tile(D_PLE), const((1, D_MODEL)),
                  const((D_MODEL, IN_W)), const((D_MODEL, D_MODEL)), const((D_MODEL, D_MODEL)),
                  const((D_PLE, D_MODEL)), const((1, D_MODEL))]
        + _prm_specs(1)
        + [const((HEADSUM_W, HEADSUM_W)), const((tc, tc)),
           pl.BlockSpec((SUBLANES - 1, n2, n2), lambda s: (0, 0, 0), pipeline_mode=pl.Buffered(1))],
        out_specs=[
            tail_tile(D_MODEL),
            pl.BlockSpec((1, 2, CONV_CH), lambda s: (head(s) // n_t, 0, 0)),
            pl.BlockSpec((1, 1, SHIFT_W), lambda s: (head(s) // n_t, 0, 0)),
            pl.BlockSpec((1, N_HEADS, HEAD_DIM, HEAD_DIM), lambda s: (tail(s) // n_t, 0, 0, 0)),
        ],
        out_shape=[
            jax.ShapeDtypeStruct((bsz * t_len, D_MODEL), F32),
            jax.ShapeDtypeStruct((bsz, 2, CONV_CH), F32),
            jax.ShapeDtypeStruct((bsz, 1, SHIFT_W), F32),
            jax.ShapeDtypeStruct((bsz, N_HEADS, HEAD_DIM, HEAD_DIM), F32),
        ],
        scratch_shapes=[
            pltpu.VMEM((tc, IN_W), F32),
            pltpu.VMEM((tc, D_MODEL), F32),
            pltpu.VMEM((CARRY_ROWS + tc, CONV_CH), F32),
            pltpu.VMEM((CARRY_ROWS + tc, SHIFT_W), F32),
            pltpu.VMEM((N_PAIRS, LANES, LANES), F32),
            big(), big(), big(), big(), big(), big(), big(), big(), big(),
            op(BF16), op(F32), op(BF16), op(F32),
            big(), big(),
        ],
        compiler_params=pltpu.CompilerParams(
            dimension_semantics=("arbitrary",), vmem_limit_bytes=VMEM_LIMIT),
        name="prompt_layer",
    )(x2d, x2d, p2d, w["norm_g"].reshape(1, D_MODEL), w["w_in"], w["w_out"], w["w_pg"], w["w_pp"],
      w["final_g"].reshape(1, D_MODEL), *w["prm"], w["headones"], tri, lane_bcast)


def _sample_front_kernel(x_ref, ng_ref, win_ref, cb_ref, sb_ref, *refs):
    prm_refs = refs[:len(_PRM_NAMES)]
    headones_ref = refs[len(_PRM_NAMES)]
    (ya_ref, conv_out_ref, shift_out_ref, gr_ref, bonus_ref,
     nkk_ref, w_ref, b_ref, km_ref, v_ref, r_ref) = refs[len(_PRM_NAMES) + 1:]
    prm = _load_prm(prm_refs)
    headones = headones_ref[...]

    x = x_ref[...]
    xn = (x * lax.rsqrt(jnp.mean(x * x, axis=-1, keepdims=True) + RMS_EPS) * ng_ref[...]).astype(BF16)
    z = _dot(xn, win_ref[...])

    u = z[:, CONV_CH:2 * CONV_CH] * z[:, 2 * CONV_CH:3 * CONV_CH]
    cb0 = cb_ref[:, 0:CONV_CH]
    cb1 = cb_ref[:, CONV_CH:2 * CONV_CH]
    cw = prm["conv_w"]
    conv = cw[0:1, :] * cb0 + cw[1:2, :] * cb1 + cw[2:3, :] * u
    ya_ref[...] = z[:, 0:CONV_CH] * conv * _silu(z[:, 3 * CONV_CH:4 * CONV_CH])
    conv_out_ref[:, 0:CONV_CH] = cb1
    conv_out_ref[:, CONV_CH:2 * CONV_CH] = u

    zs = z[:, OFF_ZS:OFF_ZS + SHIFT_W]
    shift_out_ref[...] = zs
    gr_ref[...] = z[:, OFF_GR:OFF_GR + RW]
    zm = zs + (sb_ref[...] - zs) * prm["mu"]
    r = zm[:, 0:RW]
    k = zm[:, RW:2 * RW]
    v = zm[:, 2 * RW:3 * RW]
    wa = zm[:, 3 * RW:3 * RW + 2 * LORA]
    lw, a, kk, kmod, bonus = _rwkv_tokens(r, k, v, wa, prm, headones)
    bonus_ref[...] = bonus
    nkk_ref[...] = (-kk).T
    w_ref[...] = jnp.exp(lw).T
    b_ref[...] = (kk * a).T
    km_ref[...] = kmod.T
    v_ref[...] = v.T
    r_ref[...] = r.T


def _sample_step_kernel(s_ref, nkk_ref, w_ref, b_ref, km_ref, v_ref, r_ref, s1_ref, y_ref):
    nkk = nkk_ref[0]
    w = w_ref[0]
    b = b_ref[0]
    km = km_ref[0]
    r = r_ref[0]
    for vi in range(HEAD_DIM):
        s0 = s_ref[0, vi]
        sa = jnp.sum(s0 * nkk, axis=0, keepdims=True)
        s1 = s0 * w + sa * b + v_ref[0, vi:vi + 1, :] * km
        s1_ref[0, vi] = s1
        y_ref[0, vi:vi + 1, :] = jnp.sum(s1 * r, axis=0, keepdims=True)


def _sample_back_kernel(o_ref, bonus_ref, gr_ref, ya_ref, x_ref, p_ref, wo_ref, wg_ref, wp_ref, fg_ref, *refs):
    prm_refs = refs[:len(_PRM_NAMES)]
    headones_ref = refs[len(_PRM_NAMES)]
    y_ref = refs[len(_PRM_NAMES) + 1]
    prm = _load_prm(prm_refs)
    y_r = _rwkv_post(o_ref[...].T, bonus_ref[...], gr_ref[...], prm, headones_ref[...])
    ycat = jnp.concatenate([ya_ref[...], y_r], axis=1).astype(BF16)
    h1 = x_ref[...] + _dot(ycat, wo_ref[...])
    gate = _sigmoid(_dot(h1.astype(BF16), wg_ref[...]))
    pe = _dot(p_ref[...].astype(BF16), wp_ref[...])
    h2 = h1 + gate * pe
    ms = jnp.mean(h2 * h2, axis=-1, keepdims=True)
    y_ref[...] = h2 * lax.rsqrt(ms + RMS_EPS) * fg_ref[...]


def _full(shape):
    nd = len(shape)
    return pl.BlockSpec(shape, lambda *_: (0,) * nd)


def _sample_layer(x2d, p2d, state_conv, state_shift, state_hvkb, w):
    n = x2d.shape[0]
    params = pltpu.CompilerParams(dimension_semantics=("arbitrary",), vmem_limit_bytes=VMEM_LIMIT)
    row = jax.ShapeDtypeStruct((n, RW), F32)
    col = jax.ShapeDtypeStruct((RW, n), F32)
    cb = state_conv.reshape(n, 2 * CONV_CH)
    front = pl.pallas_call(
        _sample_front_kernel,
        grid=(1,),
        in_specs=[_full((n, D_MODEL)), _full((1, D_MODEL)), _full((D_MODEL, IN_W)),
                  _full((n, 2 * CONV_CH)), _full((n, SHIFT_W))]
        + _prm_specs(1) + [_full((HEADSUM_W, HEADSUM_W))],
        out_specs=[_full((n, CONV_CH)), _full((n, 2 * CONV_CH)), _full((n, SHIFT_W)),
                   _full((n, RW)), _full((n, RW))] + [_full((RW, n))] * 6,
        out_shape=[jax.ShapeDtypeStruct((n, CONV_CH), F32), jax.ShapeDtypeStruct((n, 2 * CONV_CH), F32),
                   jax.ShapeDtypeStruct((n, SHIFT_W), F32), row, row] + [col] * 6,
        compiler_params=params,
        name="sample_front",
    )(x2d, w["norm_g"].reshape(1, D_MODEL), w["w_in"], cb, state_shift, *w["prm"], w["headones"])
    ya, conv_new, shift_new, g_r, bonus, nkk, dec, b, km, v, r = front

    hk = lambda a: a.reshape(N_HEADS, HEAD_DIM, n)
    vspec = pl.BlockSpec((1, HEAD_DIM, n), lambda h: (h, 0, 0))
    sspec = pl.BlockSpec((1, HEAD_DIM, HEAD_DIM, n), lambda h: (h, 0, 0, 0))
    s1, y = pl.pallas_call(
        _sample_step_kernel,
        grid=(N_HEADS,),
        in_specs=[sspec] + [vspec] * 6,
        out_specs=[sspec, vspec],
        out_shape=[jax.ShapeDtypeStruct((N_HEADS, HEAD_DIM, HEAD_DIM, n), F32),
                   jax.ShapeDtypeStruct((N_HEADS, HEAD_DIM, n), F32)],
        compiler_params=params,
        name="sample_step",
    )(state_hvkb, hk(nkk), hk(dec), hk(b), hk(km), hk(v), hk(r))

    y_out = pl.pallas_call(
        _sample_back_kernel,
        grid=(1,),
        in_specs=[_full((RW, n)), _full((n, RW)), _full((n, RW)), _full((n, CONV_CH)),
                  _full((n, D_MODEL)), _full((n, D_PLE)), _full((D_MODEL, D_MODEL)),
                  _full((D_MODEL, D_MODEL)), _full((D_PLE, D_MODEL)), _full((1, D_MODEL))]
        + _prm_specs(1) + [_full((HEADSUM_W, HEADSUM_W))],
        out_specs=_full((n, D_MODEL)),
        out_shape=jax.ShapeDtypeStruct((n, D_MODEL), F32),
        compiler_params=params,
        name="sample_back",
    )(y.reshape(RW, n), bonus, g_r, ya, x2d, p2d, w["w_out"], w["w_pg"], w["w_pp"],
      w["final_g"].reshape(1, D_MODEL), *w["prm"], w["headones"])
    return y_out, conv_new.reshape(n, 2, CONV_CH), shift_new, s1


def _layer_prompt(x, p, w, tc=256):
    bsz, t_len, _ = x.shape
    x2d = x.reshape(bsz * t_len, D_MODEL)
    y, conv_new, shift_new, wkv_new = _prompt_layer(x2d, p.reshape(bsz * t_len, D_PLE), w, bsz, t_len, tc)
    return y.reshape(bsz, t_len, D_MODEL), conv_new, shift_new.reshape(bsz, SHIFT_W), wkv_new


def _layer_sample(x, p, state_conv, state_shift, state_wkv, w):
    n = x.shape[0]
    s_hvkb = jnp.transpose(state_wkv, (1, 2, 3, 0))
    y, conv_new, shift_new, s1 = _sample_layer(x.reshape(n, D_MODEL), p.reshape(n, D_PLE), state_conv,
                                               state_shift, s_hvkb, w)
    return y.reshape(n, 1, D_MODEL), conv_new, shift_new, jnp.transpose(s1, (3, 0, 1, 2))


def kernel(x_prompt, x_sample, p_prompt, p_sample, state_conv, state_shift, state_wkv, norm_g, w_in, conv_w, mu_shift, w0, w_up, a0, a_up, k_k, k_a, r_k, ln_w, ln_b, w_out, w_pg, w_pp, final_g):
    depth = norm_g.shape[0]
    assert depth == 1
    i = 0
    head_id = jnp.arange(HEADSUM_W) // HEAD_DIM
    w = {
        "norm_g": norm_g[i],
        "w_in": w_in[i].astype(BF16),
        "w_out": w_out[i].astype(BF16),
        "w_pg": w_pg[i].astype(BF16),
        "w_pp": w_pp[i].astype(BF16),
        "final_g": final_g,
        "prm": _prm_arrays(conv_w[i], mu_shift[i], w0[i], w_up[i], a0[i], a_up[i], k_k[i], k_a[i],
                           r_k[i].reshape(RW), ln_w[i], ln_b[i]),
        "headones": (head_id[:, None] == head_id[None, :]).astype(BF16),
    }
    yp, cp, sp, wp = _layer_prompt(x_prompt, p_prompt[i], w)
    ys, cs, ss, ws = _layer_sample(x_sample, p_sample[i], state_conv[i], state_shift[i], state_wkv[i], w)
    return (yp, ys, cp[None], sp[None], wp[None], cs[None], ss[None], ws[None])
```

```python
import functools
import math

import jax
import jax.numpy as jnp
import numpy as np
from jax import lax
from jax.experimental import pallas as pl
from jax.experimental.pallas import tpu as pltpu

F32 = jnp.float32
BF16 = jnp.bfloat16

D_MODEL = 1024
CONV_CH = 512
RW = 512
HEAD_DIM = 64
N_HEADS = 8
N_PAIRS = N_HEADS // 2
LORA = 64
D_PLE = 256
SHIFT_W = 3 * RW + 2 * LORA
IN_W = 4 * CONV_CH + SHIFT_W + RW
OFF_ZS = 4 * CONV_CH
OFF_GR = OFF_ZS + SHIFT_W
RMS_EPS = 1e-6
GN_EPS = 64e-5
DECAY_SCALE = math.exp(-0.5)

LANES = 128
SUBLANES = 8
MXU_DIM = 256
HEADSUM_W = MXU_DIM
SUM_PARTS = 1
CUM_PARTS = 1
LORA_PASSES = 1
CHUNK = 64
W_IN_BLOCKS = 3
CARRY_ROWS = 8
VMEM_LIMIT = 56 * 1024 * 1024


def _dot(a, b):
    return jnp.dot(a, b, preferred_element_type=F32)


def _dot_nt(a, b):
    return lax.dot_general(a, b, (((1,), (1,)), ((), ())), preferred_element_type=F32)


def _dot_tn(a, b):
    return lax.dot_general(a, b, (((0,), (0,)), ((), ())), preferred_element_type=F32)


def _split(x, parts):
    out = []
    rem = x
    for i in range(parts):
        t = rem.astype(BF16)
        out.append(t)
        if i + 1 < parts:
            rem = rem - t.astype(F32)
    return out


def _const_dot(c_bf16, x, parts):
    acc = None
    for t in _split(x, parts):
        d = _dot(c_bf16, t)
        acc = d if acc is None else acc + d
    return acc


def _x_dot_const(x, c_bf16, parts):
    acc = None
    for t in _split(x, parts):
        d = _dot(t, c_bf16)
        acc = d if acc is None else acc + d
    return acc


def _head_sum(x, headones):
    w = headones.shape[0]
    halves = [_x_dot_const(x[:, i * w:(i + 1) * w], headones, SUM_PARTS) for i in range(x.shape[1] // w)]
    return jnp.concatenate(halves, axis=1)


def _mm(a, b, passes):
    if passes == 1:
        return _dot(a.astype(BF16), b.astype(BF16))
    ah, al = _split(a, 2)
    bh, bl = _split(b, 2)
    return _dot(ah, bh) + _dot(al, bh) + _dot(ah, bl)


def _sigmoid(x):
    return 1.0 / (1.0 + jnp.exp(-x))


def _silu(x):
    return x * _sigmoid(x)


def _rwkv_tokens(r, k, v, wa, prm, headones, between_steps=lambda: None):
    n = r.shape[0]
    lane = lax.broadcasted_iota(jnp.int32, (n, LANES), 1)
    th = jnp.where(lane < LORA, jnp.tanh(wa), wa)
    lora = _mm(th, prm["wlora"], LORA_PASSES)
    lw = -DECAY_SCALE * _sigmoid(prm["w0"] + lora[:, :RW])
    a = _sigmoid(prm["a0"] + lora[:, RW:])
    between_steps()
    kk = k * prm["k_k"]
    ss = _head_sum(kk * kk, headones)
    kk = kk / jnp.maximum(jnp.sqrt(ss), 1e-12)
    between_steps()
    kmod = k * (1.0 + (a - 1.0) * prm["k_a"])
    bonus = _head_sum(r * kmod * prm["r_k"], headones) * v
    between_steps()
    return lw, a, kk, kmod, bonus


def _rwkv_post(o, bonus, g_r, prm, headones):
    mean = _head_sum(o, headones) * (1.0 / HEAD_DIM)
    d = o - mean
    var = _head_sum(d * d, headones) * (1.0 / HEAD_DIM)
    on = d * lax.rsqrt(var + GN_EPS)
    on = on * prm["ln_w"] + prm["ln_b"]
    return (on + bonus) * _silu(g_r)


_PRM_NAMES = ("conv_w", "mu", "w0", "a0", "wlora", "k_k", "k_a", "r_k", "ln_w", "ln_b")


def _prm_arrays(conv_w, mu_shift, w0, w_up, a0, a_up, k_k, k_a, r_k, ln_w, ln_b):
    wlora = jnp.zeros((2 * LORA, 2 * RW), F32)
    wlora = wlora.at[:LORA, :RW].set(w_up).at[LORA:, RW:].set(a_up)
    row = lambda x: x.reshape(1, -1)
    return (conv_w, row(mu_shift), row(w0), row(a0), wlora, row(k_k), row(k_a),
            row(r_k), row(ln_w), row(ln_b))


def _prm_specs(grid_rank):
    zero = (lambda *_: (0, 0))
    shapes = ((3, CONV_CH), (1, SHIFT_W), (1, RW), (1, RW), (2 * LORA, 2 * RW),
              (1, RW), (1, RW), (1, RW), (1, RW), (1, RW))
    return [pl.BlockSpec(s, zero) for s in shapes]


def _load_prm(refs):
    return {n: r[...] for n, r in zip(_PRM_NAMES, refs)}


def _stack_heads(x):
    lane = lax.broadcasted_iota(jnp.int32, x.shape, 1)
    lo = jnp.where(lane < HEAD_DIM, x, 0.0)
    hi = jnp.where(lane >= HEAD_DIM, x, 0.0)
    return jnp.concatenate([lo, hi], axis=0)


def _stack_heads_bf16(x):
    lane = lax.broadcasted_iota(jnp.int32, x.shape, 1)
    lo = jnp.where(lane < HEAD_DIM, 1.0, 0.0).astype(BF16)
    xb = x.astype(BF16)
    return jnp.concatenate([xb * lo, xb * (1.0 - lo)], axis=0)


def _bdot(a, b):
    return _dot(a.astype(BF16), b.astype(BF16))


def _diag_block_inverses_minus_eye(mats, lane_bcast):
    n = mats[0].shape[0]
    nb = n // SUBLANES
    row = lax.broadcasted_iota(jnp.int32, (n, n), 0)
    col = lax.broadcasted_iota(jnp.int32, (n, n), 1)
    in_diag = ((row // SUBLANES) == (col // SUBLANES)) & (row > col)
    packed = []
    for a in mats:
        d = jnp.where(in_diag, a, 0.0)
        acc = d[0:SUBLANES]
        for m in range(1, nb):
            acc = acc + d[m * SUBLANES:(m + 1) * SUBLANES]
        packed.append(acc)
    d_all = jnp.concatenate(packed, axis=0).astype(BF16)
    sub = lax.broadcasted_iota(jnp.int32, (SUBLANES, n), 0)
    lane = lax.broadcasted_iota(jnp.int32, (SUBLANES, n), 1)
    eye8 = jnp.where(sub == lane % SUBLANES, 1.0, 0.0)
    ts = [eye8 for _ in mats]
    for j in range(SUBLANES - 1):
        spread = _dot(d_all, lane_bcast[j])
        ts = [t + spread[i * SUBLANES:(i + 1) * SUBLANES] * jnp.broadcast_to(t[j:j + 1, :], (SUBLANES, n))
              for i, t in enumerate(ts)]
    blk = lax.broadcasted_iota(jnp.int32, (SUBLANES, n), 1) // SUBLANES
    out = []
    for t in ts:
        tm1 = t - eye8
        out.append(jnp.concatenate([jnp.where(blk == m, tm1, 0.0) for m in range(nb)], axis=0))
    return out


def _unit_lower_inverse_minus_eye(mats, lane_bcast, between_levels=lambda: None):
    n = mats[0].shape[0]
    row = lax.broadcasted_iota(jnp.int32, (n, n), 0)
    col = lax.broadcasted_iota(jnp.int32, (n, n), 1)
    xs = _diag_block_inverses_minus_eye(mats, lane_bcast)
    size = 2 * SUBLANES
    while size <= CHUNK:
        half = size // 2
        sel = ((row // size) == (col // size)) & ((row % size) >= half) & ((col % size) < half)
        als = [jnp.where(sel, a, 0.0) for a in mats]
        ps = [al + _bdot(x, al) for x, al in zip(xs, als)]
        xs = [x + p + _bdot(p, x) for x, p in zip(xs, ps)]
        between_levels()
        size *= 2
    return xs


def _chunk_operators(ld, strict, incl, lane_bcast, side_work=()):
    side_work = list(side_work)

    def breathe():
        if side_work:
            side_work.pop(0)()

    cat0 = lambda xs: jnp.concatenate(xs, axis=0)
    la = [_stack_heads(x) for x in ld("la")]
    lr = [_stack_heads(x) for x in ld("lr")]
    n2 = 2 * CHUNK
    amat = [_dot_nt(cat0([a, b]).astype(BF16), cat0([c, d]).astype(BF16))
            for a, b, c, d in zip(la, lr, ld("rb"), ld("rk"))]

    def block_diag_pair(m):
        swapped = pltpu.roll(m, CHUNK, axis=1)
        first = cat0([m[0:CHUNK], swapped[CHUNK:n2]])
        second = cat0([swapped[0:CHUNK], m[CHUNK:n2]])
        return first, second

    top = [block_diag_pair(m[0:n2]) for m in amat]
    a_ab = [jnp.where(strict, t[0], 0.0) for t in top]
    xs = _unit_lower_inverse_minus_eye(a_ab, lane_bcast, breathe)
    vst = [_stack_heads_bf16(x) for x in ld("v")]
    av = [_dot(jnp.where(strict, t[1], 0.0).astype(BF16), v) for t, v in zip(top, vst)]
    breathe()
    uv0 = [y + _bdot(x, y) for x, y in zip(xs, av)]
    tla = [y + _bdot(x, y) for x, y in zip(xs, la)]
    breathe()
    bottom = [block_diag_pair(m[n2:2 * n2]) for m in amat]
    a_rb = [jnp.where(incl, t[0], 0.0) for t in bottom]
    a_rk = [jnp.where(incl, t[1], 0.0) for t in bottom]
    gop = [y + _bdot(a, t) for y, a, t in zip(lr, a_rb, tla)]
    uv = [cat0([u.astype(BF16), v]) for u, v in zip(uv0, vst)]
    y0 = [_dot(jnp.concatenate([a, b], axis=1).astype(BF16), w) for a, b, w in zip(a_rb, a_rk, uv)]
    breathe()
    bt = [_stack_heads_bf16(x) for x in ld("bt")]
    kt = [_stack_heads_bf16(x) for x in ld("kt")]
    pop = [_dot_tn(t.astype(BF16), b) for t, b in zip(tla, bt)]
    qop = [_dot_tn(w, cat0([b, k])) for w, b, k in zip(uv, bt, kt)]
    while side_work:
        breathe()
    return gop, y0, pop, qop


def _prompt_kernel(xh_ref, xt_ref, pt_ref, ng_ref, win_ref, wo_ref, wg_ref, wp_ref, fg_ref, *refs,
                   tc, n_t, n_tiles):
    prm_refs = refs[:len(_PRM_NAMES)]
    headones_ref, tri_ref, bcast_ref = refs[len(_PRM_NAMES):len(_PRM_NAMES) + 3]
    (y_ref, conv_out_ref, shift_out_ref, wkv_out_ref) = refs[len(_PRM_NAMES) + 3:len(_PRM_NAMES) + 7]
    (z_s, ycat_s, ubuf, zsbuf, s_ref, la_s, lr_s, rb_s, rk_s, bt_s, kt_s, v_s, gam_s, o_s,
     g_sc, y0_sc, p_sc, q_sc, bonus_p, gr_p) = refs[len(_PRM_NAMES) + 7:]

    s = pl.program_id(0)
    head_t = lax.rem(s, jnp.int32(n_t))
    tail_t = lax.rem(s + (n_t - 1), jnp.int32(n_t))
    prm = _load_prm(prm_refs)
    headones = headones_ref[...]
    n2 = 2 * CHUNK
    n_chunks = tc // CHUNK

    @pl.when(s == 0)
    def _():
        for ref in (g_sc, y0_sc, p_sc, q_sc, gam_s, bonus_p, gr_p, ycat_s, s_ref):
            ref[...] = jnp.zeros(ref.shape, ref.dtype)

    @pl.when(head_t == 0)
    def _():
        ubuf[0:CARRY_ROWS, :] = jnp.zeros((CARRY_ROWS, CONV_CH), F32)
        zsbuf[0:CARRY_ROWS, :] = jnp.zeros((CARRY_ROWS, SHIFT_W), F32)

    @pl.when(tail_t == 0)
    def _():
        s_ref[...] = jnp.zeros(s_ref.shape, F32)

    tail = {}

    def tail_state_pass(c):
        def run():
            for j in range(N_PAIRS):
                idx = c * N_PAIRS + j
                lanes = slice(j * LANES, (j + 1) * LANES)
                s_old = s_ref[j]
                s_bf = s_old.astype(BF16)
                yst = _dot_nt(g_sc[idx], s_bf) + y0_sc[idx]
                o_s[c * CHUNK:(c + 1) * CHUNK, lanes] = yst[0:CHUNK] + yst[CHUNK:n2]
                gam = gam_s[c * CHUNK:c * CHUNK + 1, lanes]
                s_ref[j] = s_old * gam + _dot(s_bf, p_sc[idx]) + q_sc[idx]
        return run

    def tail_post():
        ycat_s[:, CONV_CH:2 * CONV_CH] = _rwkv_post(o_s[...], bonus_p[...], gr_p[...], prm, headones)

    def tail_out_proj():
        tail["h1"] = xt_ref[...] + _dot(ycat_s[...].astype(BF16), wo_ref[...])

    def tail_gate():
        tail["gate"] = _sigmoid(_dot(tail["h1"].astype(BF16), wg_ref[...]))
        tail["pe"] = _dot(pt_ref[...].astype(BF16), wp_ref[...])

    def tail_norm():
        h2 = tail["h1"] + tail["gate"] * tail["pe"]
        ms = jnp.mean(h2 * h2, axis=-1, keepdims=True)
        y_ref[...] = h2 * lax.rsqrt(ms + RMS_EPS) * fg_ref[...]

    x = xh_ref[...]
    xn = (x * lax.rsqrt(jnp.mean(x * x, axis=-1, keepdims=True) + RMS_EPS) * ng_ref[...]).astype(BF16)

    def project(c0):
        def run():
            z_s[:, c0:c0 + CONV_CH] = _dot(xn, win_ref[:, c0:c0 + CONV_CH])
        return run

    passes = [tail_state_pass(c) for c in range(n_chunks)]
    blocks = [project(c0) for c0 in (OFF_GR, 0, CONV_CH, 2 * CONV_CH, 3 * CONV_CH)]
    side_queue = []
    while passes or len(blocks) > 1:
        side_queue += passes[:1] + (blocks[:1] if len(blocks) > 1 else [])
        passes, blocks = passes[1:], (blocks[1:] if len(blocks) > 1 else blocks)
    side_queue += [tail_post, blocks[0], tail_out_proj, tail_gate, tail_norm]

    def project_next(count=2):
        for _ in range(count):
            if side_queue:
                side_queue.pop(0)()

    z_s[:, OFF_ZS:OFF_ZS + SHIFT_W] = _dot(xn, win_ref[:, OFF_ZS:OFF_ZS + SHIFT_W])

    zs = z_s[:, OFF_ZS:OFF_ZS + SHIFT_W]
    zsbuf[CARRY_ROWS:CARRY_ROWS + tc, :] = zs
    zprev = zsbuf[CARRY_ROWS - 1:CARRY_ROWS - 1 + tc, :]
    zm = zs + (zprev - zs) * prm["mu"]
    zsbuf[CARRY_ROWS - 1:CARRY_ROWS, :] = zsbuf[CARRY_ROWS + tc - 1:CARRY_ROWS + tc, :]

    project_next()
    r = zm[:, 0:RW]
    k = zm[:, RW:2 * RW]
    v = zm[:, 2 * RW:3 * RW]
    wa = zm[:, 3 * RW:3 * RW + 2 * LORA]
    lw, a, kk, kmod, bonus = _rwkv_tokens(r, k, v, wa, prm, headones, project_next)

    g = _const_dot(tri_ref[...], lw, CUM_PARTS)
    gc = jnp.concatenate(
        [jnp.broadcast_to(g[(c + 1) * CHUNK - 1:(c + 1) * CHUNK, :], (CHUNK, RW)) for c in range(tc // CHUNK)],
        axis=0)
    project_next()
    eng = jnp.exp(-g)
    etail = jnp.exp(gc - g)
    b = kk * a
    la_s[...] = -kk * jnp.exp(g - lw)
    lr_s[...] = r * jnp.exp(g)
    project_next()
    rb_s[...] = b * eng
    rk_s[...] = kmod * eng
    bt_s[...] = b * etail
    kt_s[...] = kmod * etail
    v_s[...] = v
    while side_queue:
        project_next()
    gam_s[...] = jnp.exp(gc)

    row = lax.broadcasted_iota(jnp.int32, (n2, n2), 0)
    col = lax.broadcasted_iota(jnp.int32, (n2, n2), 1)
    same = (row // CHUNK) == (col // CHUNK)
    strict = same & (row > col)
    incl = same & (row >= col)

    cw = prm["conv_w"]

    def conv_rows(r0):
        def run():
            rows = slice(r0, r0 + CHUNK)
            u = z_s[rows, CONV_CH:2 * CONV_CH] * z_s[rows, 2 * CONV_CH:3 * CONV_CH]
            ubuf[CARRY_ROWS + r0:CARRY_ROWS + r0 + CHUNK, :] = u
            um1 = ubuf[CARRY_ROWS - 1 + r0:CARRY_ROWS - 1 + r0 + CHUNK, :]
            um2 = ubuf[CARRY_ROWS - 2 + r0:CARRY_ROWS - 2 + r0 + CHUNK, :]
            conv = cw[0:1, :] * um2 + cw[1:2, :] * um1 + cw[2:3, :] * u
            ycat_s[rows, 0:CONV_CH] = z_s[rows, 0:CONV_CH] * conv * _silu(z_s[rows, 3 * CONV_CH:4 * CONV_CH])
        return run

    srcs = {"la": la_s, "lr": lr_s, "rb": rb_s, "rk": rk_s, "bt": bt_s, "kt": kt_s, "v": v_s}
    members = [(c, j) for c in range(n_chunks) for j in range(N_PAIRS)]

    def ld(name):
        ref = srcs[name]
        return [ref[c * CHUNK:(c + 1) * CHUNK, j * LANES:(j + 1) * LANES] for c, j in members]

    lane_bcast = [bcast_ref[j] for j in range(SUBLANES - 1)]
    gop, y0, pop, qop = _chunk_operators(ld, strict, incl, lane_bcast,
                                         [conv_rows(c * CHUNK) for c in range(n_chunks)])
    for (c, j), g_, y_, p_, q_ in zip(members, gop, y0, pop, qop):
        idx = c * N_PAIRS + j
        g_sc[idx] = g_.astype(BF16)
        y0_sc[idx] = y_
        p_sc[idx] = p_.astype(BF16)
        q_sc[idx] = q_

    ubuf[CARRY_ROWS - 2:CARRY_ROWS, :] = ubuf[CARRY_ROWS + tc - 2:CARRY_ROWS + tc, :]

    bonus_p[...] = bonus
    gr_p[...] = z_s[:, OFF_GR:OFF_GR + RW]

    @pl.when((head_t == n_t - 1) & (s < n_tiles))
    def _():
        shift_out_ref[0] = zsbuf[CARRY_ROWS - 1:CARRY_ROWS, :]
        conv_out_ref[0] = ubuf[CARRY_ROWS - 2:CARRY_ROWS, :]

    @pl.when((tail_t == n_t - 1) & (s > 0))
    def _():
        for h in range(N_HEADS):
            j, i = divmod(h, 2)
            blk = s_ref[j]
            wkv_out_ref[0, h] = blk[i * HEAD_DIM:(i + 1) * HEAD_DIM, i * HEAD_DIM:(i + 1) * HEAD_DIM]


def _prompt_layer(x2d, p2d, w, bsz, t_len, tc):
    n_t = t_len // tc
    tok = np.arange(tc)
    blockones = (tok[:, None] // CHUNK) == (tok[None, :] // CHUNK)
    tri = jnp.asarray(blockones & (tok[:, None] >= tok[None, :]), BF16)
    n_tiles = bsz * n_t
    n2 = 2 * CHUNK
    idx = np.arange(n2)
    same_blk = (idx[:, None] // SUBLANES) == (idx[None, :] // SUBLANES)
    lane_bcast = jnp.asarray(
        np.stack([same_blk & (idx[:, None] % SUBLANES == j) for j in range(SUBLANES - 1)]), BF16)
    kern = functools.partial(_prompt_kernel, tc=tc, n_t=n_t, n_tiles=n_tiles)
    big = lambda: pltpu.VMEM((tc, RW), F32)
    n_blk = (tc // CHUNK) * N_PAIRS
    op = lambda dt: pltpu.VMEM((n_blk, LANES, LANES), dt)
    head = lambda s: jnp.minimum(s, n_tiles - 1)
    tail = lambda s: jnp.maximum(s - 1, 0)
    head_tile = lambda width: pl.BlockSpec((tc, width), lambda s: (head(s), 0))
    tail_tile = lambda width: pl.BlockSpec((tc, width), lambda s: (tail(s), 0))
    const = lambda shape: pl.BlockSpec(shape, lambda s: (0, 0), pipeline_mode=pl.Buffered(1))
    return pl.pallas_call(
        kern,
        grid=(n_tiles + 1,),
        in_specs=[head_tile(D_MODEL), tail_tile(D_MODEL), tail_tile(D_PLE), const((1, D_MODEL)),
                  const((D_MODEL, IN_W)), const((D_MODEL, D_MODEL)), const((D_MODEL, D_MODEL)),
                  const((D_PLE, D_MODEL)), const((1, D_MODEL))]
        + _prm_specs(1)
        + [const((HEADSUM_W, HEADSUM_W)), const((tc, tc)),
           pl.BlockSpec((SUBLANES - 1, n2, n2), lambda s: (0, 0, 0), pipeline_mode=pl.Buffered(1))],
        out_specs=[
            tail_tile(D_MODEL),
            pl.BlockSpec((1, 2, CONV_CH), lambda s: (head(s) // n_t, 0, 0)),
            pl.BlockSpec((1, 1, SHIFT_W), lambda s: (head(s) // n_t, 0, 0)),
            pl.BlockSpec((1, N_HEADS, HEAD_DIM, HEAD_DIM), lambda s: (tail(s) // n_t, 0, 0, 0)),
        ],
        out_shape=[
            jax.ShapeDtypeStruct((bsz * t_len, D_MODEL), F32),
            jax.ShapeDtypeStruct((bsz, 2, CONV_CH), F32),
            jax.ShapeDtypeStruct((bsz, 1, SHIFT_W), F32),
            jax.ShapeDtypeStruct((bsz, N_HEADS, HEAD_DIM, HEAD_DIM), F32),
        ],
        scratch_shapes=[
            pltpu.VMEM((tc, IN_W), F32),
            pltpu.VMEM((tc, D_MODEL), F32),
            pltpu.VMEM((CARRY_ROWS + tc, CONV_CH), F32),
            pltpu.VMEM((CARRY_ROWS + tc, SHIFT_W), F32),
            pltpu.VMEM((N_PAIRS, LANES, LANES), F32),
            big(), big(), big(), big(), big(), big(), big(), big(), big(),
            op(BF16), op(F32), op(BF16), op(F32),
            big(), big(),
        ],
        compiler_params=pltpu.CompilerParams(
            dimension_semantics=("arbitrary",), vmem_limit_bytes=VMEM_LIMIT),
        name="prompt_layer",
    )(x2d, x2d, p2d, w["norm_g"].reshape(1, D_MODEL), w["w_in"], w["w_out"], w["w_pg"], w["w_pp"],
      w["final_g"].reshape(1, D_MODEL), *w["prm"], w["headones"], tri, lane_bcast)


def _sample_front_kernel(x_ref, ng_ref, win_ref, cb_ref, sb_ref, *refs):
    prm_refs = refs[:len(_PRM_NAMES)]
    headones_ref = refs[len(_PRM_NAMES)]
    (ya_ref, conv_out_ref, shift_out_ref, gr_ref, bonus_ref,
     nkk_ref, w_ref, b_ref, km_ref, v_ref, r_ref, wbf_ref, xn_s, z_s) = refs[len(_PRM_NAMES) + 1:]
    j = pl.program_id(0)

    @pl.when(j == 0)
    def _():
        x = x_ref[...]
        xn_s[...] = (x * lax.rsqrt(jnp.mean(x * x, axis=-1, keepdims=True) + RMS_EPS) * ng_ref[...]).astype(BF16)

    w_blk = win_ref[...].astype(BF16)
    wbf_ref[...] = w_blk
    z_s[j] = _dot(xn_s[...], w_blk)

    @pl.when(j == pl.num_programs(0) - 1)
    def _():
        _sample_front_tokens(z_s, cb_ref, sb_ref, prm_refs, headones_ref, ya_ref, conv_out_ref, shift_out_ref,
                             gr_ref, bonus_ref, nkk_ref, w_ref, b_ref, km_ref, v_ref, r_ref)


def _sample_front_tokens(z_s, cb_ref, sb_ref, prm_refs, headones_ref, ya_ref, conv_out_ref, shift_out_ref,
                         gr_ref, bonus_ref, nkk_ref, w_ref, b_ref, km_ref, v_ref, r_ref):
    prm = _load_prm(prm_refs)
    headones = headones_ref[...]
    z = jnp.concatenate([z_s[i] for i in range(z_s.shape[0])], axis=1)

    u = z[:, CONV_CH:2 * CONV_CH] * z[:, 2 * CONV_CH:3 * CONV_CH]
    cb0 = cb_ref[:, 0:CONV_CH]
    cb1 = cb_ref[:, CONV_CH:2 * CONV_CH]
    cw = prm["conv_w"]
    conv = cw[0:1, :] * cb0 + cw[1:2, :] * cb1 + cw[2:3, :] * u
    ya_ref[...] = z[:, 0:CONV_CH] * conv * _silu(z[:, 3 * CONV_CH:4 * CONV_CH])
    conv_out_ref[:, 0:CONV_CH] = cb1
    conv_out_ref[:, CONV_CH:2 * CONV_CH] = u

    zs = z[:, OFF_ZS:OFF_ZS + SHIFT_W]
    shift_out_ref[...] = zs
    gr_ref[...] = z[:, OFF_GR:OFF_GR + RW]
    zm = zs + (sb_ref[...] - zs) * prm["mu"]
    r = zm[:, 0:RW]
    k = zm[:, RW:2 * RW]
    v = zm[:, 2 * RW:3 * RW]
    wa = zm[:, 3 * RW:3 * RW + 2 * LORA]
    lw, a, kk, kmod, bonus = _rwkv_tokens(r, k, v, wa, prm, headones)
    bonus_ref[...] = bonus
    nkk_ref[...] = (-kk).T
    w_ref[...] = jnp.exp(lw).T
    b_ref[...] = (kk * a).T
    km_ref[...] = kmod.T
    v_ref[...] = v.T
    r_ref[...] = r.T


def _sample_step_kernel(s_ref, nkk_ref, w_ref, b_ref, km_ref, v_ref, r_ref, s1_ref, y_ref):
    nkk = nkk_ref[0]
    w = w_ref[0]
    b = b_ref[0]
    km = km_ref[0]
    r = r_ref[0]
    for vi in range(HEAD_DIM):
        s0 = s_ref[0, vi]
        sa = jnp.sum(s0 * nkk, axis=0, keepdims=True)
        s1 = s0 * w + sa * b + v_ref[0, vi:vi + 1, :] * km
        s1_ref[0, vi] = s1
        y_ref[0, vi:vi + 1, :] = jnp.sum(s1 * r, axis=0, keepdims=True)


def _sample_back_kernel(o_ref, bonus_ref, gr_ref, ya_ref, x_ref, p_ref, wo_ref, wg_ref, wp_ref, fg_ref, *refs):
    prm_refs = refs[:len(_PRM_NAMES)]
    headones_ref = refs[len(_PRM_NAMES)]
    y_ref = refs[len(_PRM_NAMES) + 1]
    prm = _load_prm(prm_refs)
    y_r = _rwkv_post(o_ref[...].T, bonus_ref[...], gr_ref[...], prm, headones_ref[...])
    ycat = jnp.concatenate([ya_ref[...], y_r], axis=1).astype(BF16)
    h1 = x_ref[...] + _dot(ycat, wo_ref[...])
    gate = _sigmoid(_dot(h1.astype(BF16), wg_ref[...]))
    pe = _dot(p_ref[...].astype(BF16), wp_ref[...])
    h2 = h1 + gate * pe
    ms = jnp.mean(h2 * h2, axis=-1, keepdims=True)
    y_ref[...] = h2 * lax.rsqrt(ms + RMS_EPS) * fg_ref[...]


def _full(shape):
    nd = len(shape)
    return pl.BlockSpec(shape, lambda *_: (0,) * nd)


def _sample_layer(x2d, p2d, state_conv, state_shift, state_hvkb, w):
    n = x2d.shape[0]
    params = pltpu.CompilerParams(dimension_semantics=("arbitrary",), vmem_limit_bytes=VMEM_LIMIT)
    row = jax.ShapeDtypeStruct((n, RW), F32)
    col = jax.ShapeDtypeStruct((RW, n), F32)
    cb = state_conv.reshape(n, 2 * CONV_CH)
    wblk = IN_W // W_IN_BLOCKS
    front = pl.pallas_call(
        _sample_front_kernel,
        grid=(W_IN_BLOCKS,),
        in_specs=[_full((n, D_MODEL)), _full((1, D_MODEL)), pl.BlockSpec((D_MODEL, wblk), lambda j: (0, j)),
                  _full((n, 2 * CONV_CH)), _full((n, SHIFT_W))]
        + _prm_specs(1) + [_full((HEADSUM_W, HEADSUM_W))],
        out_specs=[_full((n, CONV_CH)), _full((n, 2 * CONV_CH)), _full((n, SHIFT_W)),
                   _full((n, RW)), _full((n, RW))] + [_full((RW, n))] * 6
        + [pl.BlockSpec((D_MODEL, wblk), lambda j: (0, j))],
        out_shape=[jax.ShapeDtypeStruct((n, CONV_CH), F32), jax.ShapeDtypeStruct((n, 2 * CONV_CH), F32),
                   jax.ShapeDtypeStruct((n, SHIFT_W), F32), row, row] + [col] * 6
        + [jax.ShapeDtypeStruct((D_MODEL, IN_W), BF16)],
        scratch_shapes=[pltpu.VMEM((n, D_MODEL), BF16), pltpu.VMEM((W_IN_BLOCKS, n, wblk), F32)],
        compiler_params=params,
        name="sample_front",
    )(x2d, w["norm_g"].reshape(1, D_MODEL), w["w_in_f32"], cb, state_shift, *w["prm"], w["headones"])
    ya, conv_new, shift_new, g_r, bonus, nkk, dec, b, km, v, r, w_in_bf16 = front

    hk = lambda a: a.reshape(N_HEADS, HEAD_DIM, n)
    vspec = pl.BlockSpec((1, HEAD_DIM, n), lambda h: (h, 0, 0))
    sspec = pl.BlockSpec((1, HEAD_DIM, HEAD_DIM, n), lambda h: (h, 0, 0, 0))
    s1, y = pl.pallas_call(
        _sample_step_kernel,
        grid=(N_HEADS,),
        in_specs=[sspec] + [vspec] * 6,
        out_specs=[sspec, vspec],
        out_shape=[jax.ShapeDtypeStruct((N_HEADS, HEAD_DIM, HEAD_DIM, n), F32),
                   jax.ShapeDtypeStruct((N_HEADS, HEAD_DIM, n), F32)],
        compiler_params=params,
        name="sample_step",
    )(state_hvkb, hk(nkk), hk(dec), hk(b), hk(km), hk(v), hk(r))

    y_out = pl.pallas_call(
        _sample_back_kernel,
        grid=(1,),
        in_specs=[_full((RW, n)), _full((n, RW)), _full((n, RW)), _full((n, CONV_CH)),
                  _full((n, D_MODEL)), _full((n, D_PLE)), _full((D_MODEL, D_MODEL)),
                  _full((D_MODEL, D_MODEL)), _full((D_PLE, D_MODEL)), _full((1, D_MODEL))]
        + _prm_specs(1) + [_full((HEADSUM_W, HEADSUM_W))],
        out_specs=_full((n, D_MODEL)),
        out_shape=jax.ShapeDtypeStruct((n, D_MODEL), F32),
        compiler_params=params,
        name="sample_back",
    )(y.reshape(RW, n), bonus, g_r, ya, x2d, p2d, w["w_out"], w["w_pg"], w["w_pp"],
      w["final_g"].reshape(1, D_MODEL), *w["prm"], w["headones"])
    return y_out, conv_new.reshape(n, 2, CONV_CH), shift_new, s1, w_in_bf16


def _layer_prompt(x, p, w, tc=256):
    bsz, t_len, _ = x.shape
    x2d = x.reshape(bsz * t_len, D_MODEL)
    y, conv_new, shift_new, wkv_new = _prompt_layer(x2d, p.reshape(bsz * t_len, D_PLE), w, bsz, t_len, tc)
    return y.reshape(bsz, t_len, D_MODEL), conv_new, shift_new.reshape(bsz, SHIFT_W), wkv_new


def _layer_sample(x, p, state_conv, state_shift, state_wkv, w):
    n = x.shape[0]
    s_hvkb = jnp.transpose(state_wkv, (1, 2, 3, 0))
    y, conv_new, shift_new, s1, w_in_bf16 = _sample_layer(x.reshape(n, D_MODEL), p.reshape(n, D_PLE), state_conv,
                                                          state_shift, s_hvkb, w)
    return y.reshape(n, 1, D_MODEL), conv_new, shift_new, jnp.transpose(s1, (3, 0, 1, 2)), w_in_bf16


def kernel(x_prompt, x_sample, p_prompt, p_sample, state_conv, state_shift, state_wkv, norm_g, w_in, conv_w, mu_shift, w0, w_up, a0, a_up, k_k, k_a, r_k, ln_w, ln_b, w_out, w_pg, w_pp, final_g):
    depth = norm_g.shape[0]
    assert depth == 1
    i = 0
    head_id = np.arange(HEADSUM_W) // HEAD_DIM
    w = {
        "norm_g": norm_g[i],
        "w_in_f32": w_in[i],
        "w_out": w_out[i].astype(BF16),
        "w_pg": w_pg[i].astype(BF16),
        "w_pp": w_pp[i].astype(BF16),
        "final_g": final_g,
        "prm": _prm_arrays(conv_w[i], mu_shift[i], w0[i], w_up[i], a0[i], a_up[i], k_k[i], k_a[i],
                           r_k[i].reshape(RW), ln_w[i], ln_b[i]),
        "headones": jnp.asarray(head_id[:, None] == head_id[None, :], BF16),
    }
    ys, cs, ss, ws, w["w_in"] = _layer_sample(x_sample, p_sample[i], state_conv[i], state_shift[i],
                                              state_wkv[i], w)
    yp, cp, sp, wp = _layer_prompt(x_prompt, p_prompt[i], w)
    return (yp, ys, cp[None], sp[None], wp[None], cs[None], ss[None], ws[None])
```

```python
import functools
import math

import jax
import jax.numpy as jnp
import numpy as np
from jax import lax
from jax.experimental import pallas as pl
from jax.experimental.pallas import tpu as pltpu

F32 = jnp.float32
BF16 = jnp.bfloat16

D_MODEL = 1024
CONV_CH = 512
RW = 512
HEAD_DIM = 64
N_HEADS = 8
N_PAIRS = N_HEADS // 2
LORA = 64
D_PLE = 256
SHIFT_W = 3 * RW + 2 * LORA
IN_W = 4 * CONV_CH + SHIFT_W + RW
OFF_ZS = 4 * CONV_CH
OFF_GR = OFF_ZS + SHIFT_W
RMS_EPS = 1e-6
GN_EPS = 64e-5
DECAY_SCALE = math.exp(-0.5)

LANES = 128
SUBLANES = 8
MXU_DIM = 256
HEADSUM_W = MXU_DIM
SUM_PARTS = 1
CUM_PARTS = 1
LORA_PASSES = 1
CHUNK = 64
W_IN_BLOCKS = 3
CARRY_ROWS = 8
VMEM_LIMIT = 56 * 1024 * 1024


def _dot(a, b):
    return jnp.dot(a, b, preferred_element_type=F32)


def _dot_nt(a, b):
    return lax.dot_general(a, b, (((1,), (1,)), ((), ())), preferred_element_type=F32)


def _dot_tn(a, b):
    return lax.dot_general(a, b, (((0,), (0,)), ((), ())), preferred_element_type=F32)


def _split(x, parts):
    out = []
    rem = x
    for i in range(parts):
        t = rem.astype(BF16)
        out.append(t)
        if i + 1 < parts:
            rem = rem - t.astype(F32)
    return out


def _const_dot(c_bf16, x, parts):
    acc = None
    for t in _split(x, parts):
        d = _dot(c_bf16, t)
        acc = d if acc is None else acc + d
    return acc


def _x_dot_const(x, c_bf16, parts):
    acc = None
    for t in _split(x, parts):
        d = _dot(t, c_bf16)
        acc = d if acc is None else acc + d
    return acc


def _head_sum(x, headones):
    w = headones.shape[0]
    halves = [_x_dot_const(x[:, i * w:(i + 1) * w], headones, SUM_PARTS) for i in range(x.shape[1] // w)]
    return jnp.concatenate(halves, axis=1)


def _mm(a, b, passes):
    if passes == 1:
        return _dot(a.astype(BF16), b.astype(BF16))
    ah, al = _split(a, 2)
    bh, bl = _split(b, 2)
    return _dot(ah, bh) + _dot(al, bh) + _dot(ah, bl)


def _sigmoid(x):
    return 1.0 / (1.0 + jnp.exp(-x))


def _silu(x):
    return x * _sigmoid(x)


def _rwkv_tokens(r, k, v, wa, prm, headones, between_steps=lambda: None):
    n = r.shape[0]
    lane = lax.broadcasted_iota(jnp.int32, (n, LANES), 1)
    th = jnp.where(lane < LORA, jnp.tanh(wa), wa)
    lora = _mm(th, prm["wlora"], LORA_PASSES)
    lw = -DECAY_SCALE * _sigmoid(prm["w0"] + lora[:, :RW])
    a = _sigmoid(prm["a0"] + lora[:, RW:])
    between_steps()
    kk = k * prm["k_k"]
    ss = _head_sum(kk * kk, headones)
    kk = kk / jnp.maximum(jnp.sqrt(ss), 1e-12)
    between_steps()
    kmod = k * (1.0 + (a - 1.0) * prm["k_a"])
    bonus = _head_sum(r * kmod * prm["r_k"], headones) * v
    between_steps()
    return lw, a, kk, kmod, bonus


def _rwkv_post(o, bonus, g_r, prm, headones):
    mean = _head_sum(o, headones) * (1.0 / HEAD_DIM)
    d = o - mean
    var = _head_sum(d * d, headones) * (1.0 / HEAD_DIM)
    on = d * lax.rsqrt(var + GN_EPS)
    on = on * prm["ln_w"] + prm["ln_b"]
    return (on + bonus) * _silu(g_r)


_PRM_NAMES = ("conv_w", "mu", "w0", "a0", "wlora", "k_k", "k_a", "r_k", "ln_w", "ln_b")


def _prm_arrays(conv_w, mu_shift, w0, w_up, a0, a_up, k_k, k_a, r_k, ln_w, ln_b):
    wlora = jnp.zeros((2 * LORA, 2 * RW), F32)
    wlora = wlora.at[:LORA, :RW].set(w_up).at[LORA:, RW:].set(a_up)
    row = lambda x: x.reshape(1, -1)
    return (conv_w, row(mu_shift), row(w0), row(a0), wlora, row(k_k), row(k_a),
            row(r_k), row(ln_w), row(ln_b))


def _prm_specs(grid_rank):
    zero = (lambda *_: (0, 0))
    shapes = ((3, CONV_CH), (1, SHIFT_W), (1, RW), (1, RW), (2 * LORA, 2 * RW),
              (1, RW), (1, RW), (1, RW), (1, RW), (1, RW))
    return [pl.BlockSpec(s, zero) for s in shapes]


def _load_prm(refs):
    return {n: r[...] for n, r in zip(_PRM_NAMES, refs)}


def _stack_heads(x):
    lane = lax.broadcasted_iota(jnp.int32, x.shape, 1)
    lo = jnp.where(lane < HEAD_DIM, x, 0.0)
    hi = jnp.where(lane >= HEAD_DIM, x, 0.0)
    return jnp.concatenate([lo, hi], axis=0)


def _stack_heads_bf16(x):
    lane = lax.broadcasted_iota(jnp.int32, x.shape, 1)
    lo = jnp.where(lane < HEAD_DIM, 1.0, 0.0).astype(BF16)
    xb = x.astype(BF16)
    return jnp.concatenate([xb * lo, xb * (1.0 - lo)], axis=0)


def _bdot(a, b):
    return _dot(a.astype(BF16), b.astype(BF16))


def _diag_block_inverses_minus_eye(mats, lane_bcast):
    n = mats[0].shape[0]
    nb = n // SUBLANES
    row = lax.broadcasted_iota(jnp.int32, (n, n), 0)
    col = lax.broadcasted_iota(jnp.int32, (n, n), 1)
    in_diag = ((row // SUBLANES) == (col // SUBLANES)) & (row > col)
    packed = []
    for a in mats:
        d = jnp.where(in_diag, a, 0.0)
        acc = d[0:SUBLANES]
        for m in range(1, nb):
            acc = acc + d[m * SUBLANES:(m + 1) * SUBLANES]
        packed.append(acc)
    d_all = jnp.concatenate(packed, axis=0).astype(BF16)
    sub = lax.broadcasted_iota(jnp.int32, (SUBLANES, n), 0)
    lane = lax.broadcasted_iota(jnp.int32, (SUBLANES, n), 1)
    eye8 = jnp.where(sub == lane % SUBLANES, 1.0, 0.0)
    ts = [eye8 for _ in mats]
    for j in range(SUBLANES - 1):
        spread = _dot(d_all, lane_bcast[j])
        ts = [t + spread[i * SUBLANES:(i + 1) * SUBLANES] * jnp.broadcast_to(t[j:j + 1, :], (SUBLANES, n))
              for i, t in enumerate(ts)]
    blk = lax.broadcasted_iota(jnp.int32, (SUBLANES, n), 1) // SUBLANES
    out = []
    for t in ts:
        tm1 = t - eye8
        out.append(jnp.concatenate([jnp.where(blk == m, tm1, 0.0) for m in range(nb)], axis=0))
    return out


def _unit_lower_inverse_minus_eye(mats, lane_bcast, between_levels=lambda: None):
    n = mats[0].shape[0]
    row = lax.broadcasted_iota(jnp.int32, (n, n), 0)
    col = lax.broadcasted_iota(jnp.int32, (n, n), 1)
    xs = _diag_block_inverses_minus_eye(mats, lane_bcast)
    size = 2 * SUBLANES
    while size <= CHUNK:
        half = size // 2
        sel = ((row // size) == (col // size)) & ((row % size) >= half) & ((col % size) < half)
        als = [jnp.where(sel, a, 0.0) for a in mats]
        ps = [al + _bdot(x, al) for x, al in zip(xs, als)]
        xs = [x + p + _bdot(p, x) for x, p in zip(xs, ps)]
        between_levels()
        size *= 2
    return xs


def _chunk_operators(ld, strict, incl, lane_bcast, side_work=()):
    side_work = list(side_work)

    def breathe():
        if side_work:
            side_work.pop(0)()

    cat0 = lambda xs: jnp.concatenate(xs, axis=0)
    la = [_stack_heads(x) for x in ld("la")]
    lr = [_stack_heads(x) for x in ld("lr")]
    n2 = 2 * CHUNK
    amat = [_dot_nt(cat0([a, b]).astype(BF16), cat0([c, d]).astype(BF16))
            for a, b, c, d in zip(la, lr, ld("rb"), ld("rk"))]

    def block_diag_pair(m):
        swapped = pltpu.roll(m, CHUNK, axis=1)
        first = cat0([m[0:CHUNK], swapped[CHUNK:n2]])
        second = cat0([swapped[0:CHUNK], m[CHUNK:n2]])
        return first, second

    top = [block_diag_pair(m[0:n2]) for m in amat]
    a_ab = [jnp.where(strict, t[0], 0.0) for t in top]
    xs = _unit_lower_inverse_minus_eye(a_ab, lane_bcast, breathe)
    vst = [_stack_heads_bf16(x) for x in ld("v")]
    av = [_dot(jnp.where(strict, t[1], 0.0).astype(BF16), v) for t, v in zip(top, vst)]
    breathe()
    uv0 = [y + _bdot(x, y) for x, y in zip(xs, av)]
    tla = [y + _bdot(x, y) for x, y in zip(xs, la)]
    breathe()
    bottom = [block_diag_pair(m[n2:2 * n2]) for m in amat]
    a_rb = [jnp.where(incl, t[0], 0.0) for t in bottom]
    a_rk = [jnp.where(incl, t[1], 0.0) for t in bottom]
    gop = [y + _bdot(a, t) for y, a, t in zip(lr, a_rb, tla)]
    uv = [cat0([u.astype(BF16), v]) for u, v in zip(uv0, vst)]
    y0 = [_dot(jnp.concatenate([a, b], axis=1).astype(BF16), w) for a, b, w in zip(a_rb, a_rk, uv)]
    breathe()
    bt = [_stack_heads_bf16(x) for x in ld("bt")]
    kt = [_stack_heads_bf16(x) for x in ld("kt")]
    pop = [_dot_tn(t.astype(BF16), b) for t, b in zip(tla, bt)]
    qop = [_dot_tn(w, cat0([b, k])) for w, b, k in zip(uv, bt, kt)]
    while side_work:
        breathe()
    return gop, y0, pop, qop


def _prompt_kernel(xh_ref, xt_ref, pt_ref, ng_ref, win_ref, wo_ref, wg_ref, wp_ref, fg_ref, *refs,
                   tc, n_t, n_tiles):
    prm_refs = refs[:len(_PRM_NAMES)]
    headones_ref, tri_ref, bcast_ref = refs[len(_PRM_NAMES):len(_PRM_NAMES) + 3]
    (y_ref, conv_out_ref, shift_out_ref, wkv_out_ref) = refs[len(_PRM_NAMES) + 3:len(_PRM_NAMES) + 7]
    (z_s, ycat_s, ubuf, zsbuf, s_ref, la_s, lr_s, rb_s, rk_s, bt_s, kt_s, v_s, gam_s, o_s,
     g_sc, y0_sc, p_sc, q_sc, bonus_p, gr_p) = refs[len(_PRM_NAMES) + 7:]

    s = pl.program_id(0)
    head_t = lax.rem(s, jnp.int32(n_t))
    tail_t = lax.rem(s + (n_t - 1), jnp.int32(n_t))
    prm = _load_prm(prm_refs)
    headones = headones_ref[...]
    n2 = 2 * CHUNK
    n_chunks = tc // CHUNK

    @pl.when(s == 0)
    def _():
        for ref in (g_sc, y0_sc, p_sc, q_sc, gam_s, bonus_p, gr_p, ycat_s, s_ref):
            ref[...] = jnp.zeros(ref.shape, ref.dtype)

    @pl.when(head_t == 0)
    def _():
        ubuf[0:CARRY_ROWS, :] = jnp.zeros((CARRY_ROWS, CONV_CH), F32)
        zsbuf[0:CARRY_ROWS, :] = jnp.zeros((CARRY_ROWS, SHIFT_W), F32)

    @pl.when(tail_t == 0)
    def _():
        s_ref[...] = jnp.zeros(s_ref.shape, F32)

    tail = {}

    def tail_state_pass(c):
        def run():
            for j in range(N_PAIRS):
                idx = c * N_PAIRS + j
                lanes = slice(j * LANES, (j + 1) * LANES)
                s_old = s_ref[j]
                s_bf = s_old.astype(BF16)
                yst = _dot_nt(g_sc[idx], s_bf) + y0_sc[idx]
                o_s[c * CHUNK:(c + 1) * CHUNK, lanes] = yst[0:CHUNK] + yst[CHUNK:n2]
                gam = gam_s[c * CHUNK:c * CHUNK + 1, lanes]
                s_ref[j] = s_old * gam + _dot(s_bf, p_sc[idx]) + q_sc[idx]
        return run

    def tail_post():
        ycat_s[:, CONV_CH:2 * CONV_CH] = _rwkv_post(o_s[...], bonus_p[...], gr_p[...], prm, headones)

    def tail_out_proj():
        tail["h1"] = xt_ref[...] + _dot(ycat_s[...].astype(BF16), wo_ref[...])

    def tail_gate():
        tail["gate"] = _sigmoid(_dot(tail["h1"].astype(BF16), wg_ref[...]))
        tail["pe"] = _dot(pt_ref[...].astype(BF16), wp_ref[...])

    def tail_norm():
        h2 = tail["h1"] + tail["gate"] * tail["pe"]
        ms = jnp.mean(h2 * h2, axis=-1, keepdims=True)
        y_ref[...] = h2 * lax.rsqrt(ms + RMS_EPS) * fg_ref[...]

    x = xh_ref[...]
    xn = (x * lax.rsqrt(jnp.mean(x * x, axis=-1, keepdims=True) + RMS_EPS) * ng_ref[...]).astype(BF16)

    def project(c0):
        def run():
            z_s[:, c0:c0 + CONV_CH] = _dot(xn, win_ref[:, c0:c0 + CONV_CH])
        return run

    passes = [tail_state_pass(c) for c in range(n_chunks)]
    blocks = [project(c0) for c0 in (OFF_GR, 0, CONV_CH, 2 * CONV_CH, 3 * CONV_CH)]
    side_queue = []
    while passes or len(blocks) > 1:
        side_queue += passes[:1] + (blocks[:1] if len(blocks) > 1 else [])
        passes, blocks = passes[1:], (blocks[1:] if len(blocks) > 1 else blocks)
    side_queue += [tail_post, blocks[0], tail_out_proj, tail_gate, tail_norm]

    def project_next(count=2):
        for _ in range(count):
            if side_queue:
                side_queue.pop(0)()

    z_s[:, OFF_ZS:OFF_ZS + SHIFT_W] = _dot(xn, win_ref[:, OFF_ZS:OFF_ZS + SHIFT_W])

    zs = z_s[:, OFF_ZS:OFF_ZS + SHIFT_W]
    zsbuf[CARRY_ROWS:CARRY_ROWS + tc, :] = zs
    zprev = zsbuf[CARRY_ROWS - 1:CARRY_ROWS - 1 + tc, :]
    zm = zs + (zprev - zs) * prm["mu"]
    zsbuf[CARRY_ROWS - 1:CARRY_ROWS, :] = zsbuf[CARRY_ROWS + tc - 1:CARRY_ROWS + tc, :]

    project_next()
    r = zm[:, 0:RW]
    k = zm[:, RW:2 * RW]
    v = zm[:, 2 * RW:3 * RW]
    wa = zm[:, 3 * RW:3 * RW + 2 * LORA]
    lw, a, kk, kmod, bonus = _rwkv_tokens(r, k, v, wa, prm, headones, project_next)

    g = _const_dot(tri_ref[...], lw, CUM_PARTS)
    gc = jnp.concatenate(
        [jnp.broadcast_to(g[(c + 1) * CHUNK - 1:(c + 1) * CHUNK, :], (CHUNK, RW)) for c in range(tc // CHUNK)],
        axis=0)
    project_next()
    eng = jnp.exp(-g)
    etail = jnp.exp(gc - g)
    b = kk * a
    la_s[...] = -kk * jnp.exp(g - lw)
    lr_s[...] = r * jnp.exp(g)
    project_next()
    rb_s[...] = b * eng
    rk_s[...] = kmod * eng
    bt_s[...] = b * etail
    kt_s[...] = kmod * etail
    v_s[...] = v
    while side_queue:
        project_next()
    gam_s[...] = jnp.exp(gc)

    row = lax.broadcasted_iota(jnp.int32, (n2, n2), 0)
    col = lax.broadcasted_iota(jnp.int32, (n2, n2), 1)
    same = (row // CHUNK) == (col // CHUNK)
    strict = same & (row > col)
    incl = same & (row >= col)

    cw = prm["conv_w"]

    def conv_rows(r0):
        def run():
            rows = slice(r0, r0 + CHUNK)
            u = z_s[rows, CONV_CH:2 * CONV_CH] * z_s[rows, 2 * CONV_CH:3 * CONV_CH]
            ubuf[CARRY_ROWS + r0:CARRY_ROWS + r0 + CHUNK, :] = u
            um1 = ubuf[CARRY_ROWS - 1 + r0:CARRY_ROWS - 1 + r0 + CHUNK, :]
            um2 = ubuf[CARRY_ROWS - 2 + r0:CARRY_ROWS - 2 + r0 + CHUNK, :]
            conv = cw[0:1, :] * um2 + cw[1:2, :] * um1 + cw[2:3, :] * u
            ycat_s[rows, 0:CONV_CH] = z_s[rows, 0:CONV_CH] * conv * _silu(z_s[rows, 3 * CONV_CH:4 * CONV_CH])
        return run

    srcs = {"la": la_s, "lr": lr_s, "rb": rb_s, "rk": rk_s, "bt": bt_s, "kt": kt_s, "v": v_s}
    members = [(c, j) for c in range(n_chunks) for j in range(N_PAIRS)]

    def ld(name):
        ref = srcs[name]
        return [ref[c * CHUNK:(c + 1) * CHUNK, j * LANES:(j + 1) * LANES] for c, j in members]

    lane_bcast = [bcast_ref[j] for j in range(SUBLANES - 1)]
    gop, y0, pop, qop = _chunk_operators(ld, strict, incl, lane_bcast,
                                         [conv_rows(c * CHUNK) for c in range(n_chunks)])
    for (c, j), g_, y_, p_, q_ in zip(members, gop, y0, pop, qop):
        idx = c * N_PAIRS + j
        g_sc[idx] = g_.astype(BF16)
        y0_sc[idx] = y_
        p_sc[idx] = p_.astype(BF16)
        q_sc[idx] = q_

    ubuf[CARRY_ROWS - 2:CARRY_ROWS, :] = ubuf[CARRY_ROWS + tc - 2:CARRY_ROWS + tc, :]

    bonus_p[...] = bonus
    gr_p[...] = z_s[:, OFF_GR:OFF_GR + RW]

    @pl.when((head_t == n_t - 1) & (s < n_tiles))
    def _():
        shift_out_ref[0] = zsbuf[CARRY_ROWS - 1:CARRY_ROWS, :]
        conv_out_ref[0] = ubuf[CARRY_ROWS - 2:CARRY_ROWS, :]

    @pl.when((tail_t == n_t - 1) & (s > 0))
    def _():
        for h in range(N_HEADS):
            j, i = divmod(h, 2)
            blk = s_ref[j]
            wkv_out_ref[0, h] = blk[i * HEAD_DIM:(i + 1) * HEAD_DIM, i * HEAD_DIM:(i + 1) * HEAD_DIM]


def _prompt_layer(x2d, p2d, w, bsz, t_len, tc):
    n_t = t_len // tc
    tok = np.arange(tc)
    blockones = (tok[:, None] // CHUNK) == (tok[None, :] // CHUNK)
    tri = jnp.asarray(blockones & (tok[:, None] >= tok[None, :]), BF16)
    n_tiles = bsz * n_t
    n2 = 2 * CHUNK
    idx = np.arange(n2)
    same_blk = (idx[:, None] // SUBLANES) == (idx[None, :] // SUBLANES)
    lane_bcast = jnp.asarray(
        np.stack([same_blk & (idx[:, None] % SUBLANES == j) for j in range(SUBLANES - 1)]), BF16)
    kern = functools.partial(_prompt_kernel, tc=tc, n_t=n_t, n_tiles=n_tiles)
    big = lambda: pltpu.VMEM((tc, RW), F32)
    n_blk = (tc // CHUNK) * N_PAIRS
    op = lambda dt: pltpu.VMEM((n_blk, LANES, LANES), dt)
    head = lambda s: jnp.minimum(s, n_tiles - 1)
    tail = lambda s: jnp.maximum(s - 1, 0)
    head_tile = lambda width: pl.BlockSpec((tc, width), lambda s: (head(s), 0))
    tail_tile = lambda width: pl.BlockSpec((tc, width), lambda s: (tail(s), 0))
    const = lambda shape: pl.BlockSpec(shape, lambda s: (0, 0), pipeline_mode=pl.Buffered(1))
    return pl.pallas_call(
        kern,
        grid=(n_tiles + 1,),
        in_specs=[head_tile(D_MODEL), tail_tile(D_MODEL), tail_tile(D_PLE), const((1, D_MODEL)),
                  const((D_MODEL, IN_W)), const((D_MODEL, D_MODEL)), const((D_MODEL, D_MODEL)),
                  const((D_PLE, D_MODEL)), const((1, D_MODEL))]
        + _prm_specs(1)
        + [const((HEADSUM_W, HEADSUM_W)), const((tc, tc)),
           pl.BlockSpec((SUBLANES - 1, n2, n2), lambda s: (0, 0, 0), pipeline_mode=pl.Buffered(1))],
        out_specs=[
            tail_tile(D_MODEL),
            pl.BlockSpec((1, 2, CONV_CH), lambda s: (head(s) // n_t, 0, 0)),
            pl.BlockSpec((1, 1, SHIFT_W), lambda s: (head(s) // n_t, 0, 0)),
            pl.BlockSpec((1, N_HEADS, HEAD_DIM, HEAD_DIM), lambda s: (tail(s) // n_t, 0, 0, 0)),
        ],
        out_shape=[
            jax.ShapeDtypeStruct((bsz * t_len, D_MODEL), F32),
            jax.ShapeDtypeStruct((bsz, 2, CONV_CH), F32),
            jax.ShapeDtypeStruct((bsz, 1, SHIFT_W), F32),
            jax.ShapeDtypeStruct((bsz, N_HEADS, HEAD_DIM, HEAD_DIM), F32),
        ],
        scratch_shapes=[
            pltpu.VMEM((tc, IN_W), F32),
            pltpu.VMEM((tc, D_MODEL), F32),
            pltpu.VMEM((CARRY_ROWS + tc, CONV_CH), F32),
            pltpu.VMEM((CARRY_ROWS + tc, SHIFT_W), F32),
            pltpu.VMEM((N_PAIRS, LANES, LANES), F32),
            big(), big(), big(), big(), big(), big(), big(), big(), big(),
            op(BF16), op(F32), op(BF16), op(F32),
            big(), big(),
        ],
        compiler_params=pltpu.CompilerParams(
            dimension_semantics=("arbitrary",), vmem_limit_bytes=VMEM_LIMIT),
        name="prompt_layer",
    )(x2d, x2d, p2d, w["norm_g"].reshape(1, D_MODEL), w["w_in"], w["w_out"], w["w_pg"], w["w_pp"],
      w["final_g"].reshape(1, D_MODEL), *w["prm"], w["headones"], tri, lane_bcast)


def _sample_front_kernel(x_ref, ng_ref, win_ref, cb_ref, sb_ref, *refs):
    prm_refs = refs[:len(_PRM_NAMES)]
    headones_ref = refs[len(_PRM_NAMES)]
    (ya_ref, conv_out_ref, shift_out_ref, gr_ref, bonus_ref,
     nkk_ref, w_ref, b_ref, km_ref, v_ref, r_ref, wbf_ref, xn_s, z_s) = refs[len(_PRM_NAMES) + 1:]
    j = pl.program_id(0)

    @pl.when(j == 0)
    def _():
        x = x_ref[:, 0, :]
        xn_s[...] = (x * lax.rsqrt(jnp.mean(x * x, axis=-1, keepdims=True) + RMS_EPS) * ng_ref[...]).astype(BF16)

    w_blk = win_ref[...].astype(BF16)
    wbf_ref[...] = w_blk
    z_s[j] = _dot(xn_s[...], w_blk)

    @pl.when(j == pl.num_programs(0) - 1)
    def _():
        _sample_front_tokens(z_s, cb_ref, sb_ref, prm_refs, headones_ref, ya_ref, conv_out_ref, shift_out_ref,
                             gr_ref, bonus_ref, nkk_ref, w_ref, b_ref, km_ref, v_ref, r_ref)


def _sample_front_tokens(z_s, cb_ref, sb_ref, prm_refs, headones_ref, ya_ref, conv_out_ref, shift_out_ref,
                         gr_ref, bonus_ref, nkk_ref, w_ref, b_ref, km_ref, v_ref, r_ref):
    prm = _load_prm(prm_refs)
    headones = headones_ref[...]
    z = jnp.concatenate([z_s[i] for i in range(z_s.shape[0])], axis=1)

    u = z[:, CONV_CH:2 * CONV_CH] * z[:, 2 * CONV_CH:3 * CONV_CH]
    cb0 = cb_ref[:, 0, :]
    cb1 = cb_ref[:, 1, :]
    cw = prm["conv_w"]
    conv = cw[0:1, :] * cb0 + cw[1:2, :] * cb1 + cw[2:3, :] * u
    ya_ref[...] = z[:, 0:CONV_CH] * conv * _silu(z[:, 3 * CONV_CH:4 * CONV_CH])
    conv_out_ref[:, 0, :] = cb1
    conv_out_ref[:, 1, :] = u

    zs = z[:, OFF_ZS:OFF_ZS + SHIFT_W]
    shift_out_ref[...] = zs
    gr_ref[...] = z[:, OFF_GR:OFF_GR + RW]
    zm = zs + (sb_ref[...] - zs) * prm["mu"]
    r = zm[:, 0:RW]
    k = zm[:, RW:2 * RW]
    v = zm[:, 2 * RW:3 * RW]
    wa = zm[:, 3 * RW:3 * RW + 2 * LORA]
    lw, a, kk, kmod, bonus = _rwkv_tokens(r, k, v, wa, prm, headones)
    bonus_ref[...] = bonus
    nkk_ref[...] = (-kk).T
    w_ref[...] = jnp.exp(lw).T
    b_ref[...] = (kk * a).T
    km_ref[...] = kmod.T
    v_ref[...] = v.T
    r_ref[...] = r.T


def _sample_step_kernel(s_ref, nkk_ref, w_ref, b_ref, km_ref, v_ref, r_ref, s1_ref, y_ref):
    nkk = nkk_ref[0]
    w = w_ref[0]
    b = b_ref[0]
    km = km_ref[0]
    r = r_ref[0]
    for vi in range(HEAD_DIM):
        s0 = s_ref[0, vi]
        sa = jnp.sum(s0 * nkk, axis=0, keepdims=True)
        s1 = s0 * w + sa * b + v_ref[0, vi:vi + 1, :] * km
        s1_ref[0, vi] = s1
        y_ref[0, vi:vi + 1, :] = jnp.sum(s1 * r, axis=0, keepdims=True)


def _sample_back_kernel(o_ref, bonus_ref, gr_ref, ya_ref, x_ref, p_ref, wo_ref, wg_ref, wp_ref, fg_ref, *refs):
    prm_refs = refs[:len(_PRM_NAMES)]
    headones_ref = refs[len(_PRM_NAMES)]
    y_ref = refs[len(_PRM_NAMES) + 1]
    prm = _load_prm(prm_refs)
    y_r = _rwkv_post(o_ref[...].T, bonus_ref[...], gr_ref[...], prm, headones_ref[...])
    ycat = jnp.concatenate([ya_ref[...], y_r], axis=1).astype(BF16)
    h1 = x_ref[:, 0, :] + _dot(ycat, wo_ref[...])
    gate = _sigmoid(_dot(h1.astype(BF16), wg_ref[...]))
    pe = _dot(p_ref[:, 0, :].astype(BF16), wp_ref[...])
    h2 = h1 + gate * pe
    ms = jnp.mean(h2 * h2, axis=-1, keepdims=True)
    y_ref[:, 0, :] = h2 * lax.rsqrt(ms + RMS_EPS) * fg_ref[...]


def _full(shape):
    nd = len(shape)
    return pl.BlockSpec(shape, lambda *_: (0,) * nd)


def _sample_layer(x3d, p3d, state_conv, state_shift, state_hvkb, w):
    n = x3d.shape[0]
    params = pltpu.CompilerParams(dimension_semantics=("arbitrary",), vmem_limit_bytes=VMEM_LIMIT)
    row = jax.ShapeDtypeStruct((n, RW), F32)
    col = jax.ShapeDtypeStruct((RW, n), F32)
    wblk = IN_W // W_IN_BLOCKS
    front = pl.pallas_call(
        _sample_front_kernel,
        grid=(W_IN_BLOCKS,),
        in_specs=[_full((n, 1, D_MODEL)), _full((1, D_MODEL)), pl.BlockSpec((D_MODEL, wblk), lambda j: (0, j)),
                  _full((n, 2, CONV_CH)), _full((n, SHIFT_W))]
        + _prm_specs(1) + [_full((HEADSUM_W, HEADSUM_W))],
        out_specs=[_full((n, CONV_CH)), _full((n, 2, CONV_CH)), _full((n, SHIFT_W)),
                   _full((n, RW)), _full((n, RW))] + [_full((RW, n))] * 6
        + [pl.BlockSpec((D_MODEL, wblk), lambda j: (0, j))],
        out_shape=[jax.ShapeDtypeStruct((n, CONV_CH), F32), jax.ShapeDtypeStruct((n, 2, CONV_CH), F32),
                   jax.ShapeDtypeStruct((n, SHIFT_W), F32), row, row] + [col] * 6
        + [jax.ShapeDtypeStruct((D_MODEL, IN_W), BF16)],
        scratch_shapes=[pltpu.VMEM((n, D_MODEL), BF16), pltpu.VMEM((W_IN_BLOCKS, n, wblk), F32)],
        compiler_params=params,
        name="sample_front",
    )(x3d, w["norm_g"].reshape(1, D_MODEL), w["w_in_f32"], state_conv, state_shift, *w["prm"], w["headones"])
    ya, conv_new, shift_new, g_r, bonus, nkk, dec, b, km, v, r, w_in_bf16 = front

    hk = lambda a: a.reshape(N_HEADS, HEAD_DIM, n)
    vspec = pl.BlockSpec((1, HEAD_DIM, n), lambda h: (h, 0, 0))
    sspec = pl.BlockSpec((1, HEAD_DIM, HEAD_DIM, n), lambda h: (h, 0, 0, 0))
    s1, y = pl.pallas_call(
        _sample_step_kernel,
        grid=(N_HEADS,),
        in_specs=[sspec] + [vspec] * 6,
        out_specs=[sspec, vspec],
        out_shape=[jax.ShapeDtypeStruct((N_HEADS, HEAD_DIM, HEAD_DIM, n), F32),
                   jax.ShapeDtypeStruct((N_HEADS, HEAD_DIM, n), F32)],
        compiler_params=params,
        name="sample_step",
    )(state_hvkb, hk(nkk), hk(dec), hk(b), hk(km), hk(v), hk(r))

    y_out = pl.pallas_call(
        _sample_back_kernel,
        grid=(1,),
        in_specs=[_full((RW, n)), _full((n, RW)), _full((n, RW)), _full((n, CONV_CH)),
                  _full((n, 1, D_MODEL)), _full((n, 1, D_PLE)), _full((D_MODEL, D_MODEL)),
                  _full((D_MODEL, D_MODEL)), _full((D_PLE, D_MODEL)), _full((1, D_MODEL))]
        + _prm_specs(1) + [_full((HEADSUM_W, HEADSUM_W))],
        out_specs=_full((n, 1, D_MODEL)),
        out_shape=jax.ShapeDtypeStruct((n, 1, D_MODEL), F32),
        compiler_params=params,
        name="sample_back",
    )(y.reshape(RW, n), bonus, g_r, ya, x3d, p3d, w["w_out"], w["w_pg"], w["w_pp"],
      w["final_g"].reshape(1, D_MODEL), *w["prm"], w["headones"])
    return y_out, conv_new, shift_new, s1, w_in_bf16


def _layer_prompt(x, p, w, tc=256):
    bsz, t_len, _ = x.shape
    x2d = x.reshape(bsz * t_len, D_MODEL)
    y, conv_new, shift_new, wkv_new = _prompt_layer(x2d, p.reshape(bsz * t_len, D_PLE), w, bsz, t_len, tc)
    return y.reshape(bsz, t_len, D_MODEL), conv_new, shift_new.reshape(bsz, SHIFT_W), wkv_new


def _layer_sample(x, p, state_conv, state_shift, state_wkv, w):
    n = x.shape[0]
    s_hvkb = jnp.transpose(state_wkv, (1, 2, 3, 0))
    y, conv_new, shift_new, s1, w_in_bf16 = _sample_layer(x, p, state_conv, state_shift, s_hvkb, w)
    return y, conv_new, shift_new, jnp.transpose(s1, (3, 0, 1, 2)), w_in_bf16


def kernel(x_prompt, x_sample, p_prompt, p_sample, state_conv, state_shift, state_wkv, norm_g, w_in, conv_w, mu_shift, w0, w_up, a0, a_up, k_k, k_a, r_k, ln_w, ln_b, w_out, w_pg, w_pp, final_g):
    depth = norm_g.shape[0]
    assert depth == 1
    i = 0
    head_id = np.arange(HEADSUM_W) // HEAD_DIM
    w = {
        "norm_g": norm_g[i],
        "w_in_f32": w_in[i],
        "w_out": w_out[i].astype(BF16),
        "w_pg": w_pg[i].astype(BF16),
        "w_pp": w_pp[i].astype(BF16),
        "final_g": final_g,
        "prm": _prm_arrays(conv_w[i], mu_shift[i], w0[i], w_up[i], a0[i], a_up[i], k_k[i], k_a[i],
                           r_k[i].reshape(RW), ln_w[i], ln_b[i]),
        "headones": jnp.asarray(head_id[:, None] == head_id[None, :], BF16),
    }
    ys, cs, ss, ws, w["w_in"] = _layer_sample(x_sample, p_sample[i], state_conv[i], state_shift[i],
                                              state_wkv[i], w)
    yp, cp, sp, wp = _layer_prompt(x_prompt, p_prompt[i], w)
    return (yp, ys, cp[None], sp[None], wp[None], cs[None], ss[None], ws[None])
```

```python
import functools
import math

import jax
import jax.numpy as jnp
import numpy as np
from jax import lax
from jax.experimental import pallas as pl
from jax.experimental.pallas import tpu as pltpu

F32 = jnp.float32
BF16 = jnp.bfloat16

D_MODEL = 1024
CONV_CH = 512
RW = 512
HEAD_DIM = 64
N_HEADS = 8
N_PAIRS = N_HEADS // 2
LORA = 64
D_PLE = 256
SHIFT_W = 3 * RW + 2 * LORA
IN_W = 4 * CONV_CH + SHIFT_W + RW
OFF_ZS = 4 * CONV_CH
OFF_GR = OFF_ZS + SHIFT_W
RMS_EPS = 1e-6
GN_EPS = 64e-5
DECAY_SCALE = math.exp(-0.5)

LANES = 128
SUBLANES = 8
MXU_DIM = 256
HEADSUM_W = MXU_DIM
SUM_PARTS = 1
CUM_PARTS = 1
LORA_PASSES = 1
CHUNK = 64
W_IN_BLOCKS = 3
CARRY_ROWS = 8
VMEM_LIMIT = 56 * 1024 * 1024


def _dot(a, b):
    return jnp.dot(a, b, preferred_element_type=F32)


def _dot_nt(a, b):
    return lax.dot_general(a, b, (((1,), (1,)), ((), ())), preferred_element_type=F32)


def _dot_tn(a, b):
    return lax.dot_general(a, b, (((0,), (0,)), ((), ())), preferred_element_type=F32)


def _split(x, parts):
    out = []
    rem = x
    for i in range(parts):
        t = rem.astype(BF16)
        out.append(t)
        if i + 1 < parts:
            rem = rem - t.astype(F32)
    return out


def _const_dot(c_bf16, x, parts):
    acc = None
    for t in _split(x, parts):
        d = _dot(c_bf16, t)
        acc = d if acc is None else acc + d
    return acc


def _x_dot_const(x, c_bf16, parts):
    acc = None
    for t in _split(x, parts):
        d = _dot(t, c_bf16)
        acc = d if acc is None else acc + d
    return acc


def _head_sum(x, headones):
    w = headones.shape[0]
    halves = [_x_dot_const(x[:, i * w:(i + 1) * w], headones, SUM_PARTS) for i in range(x.shape[1] // w)]
    return jnp.concatenate(halves, axis=1)


def _mm(a, b, passes):
    if passes == 1:
        return _dot(a.astype(BF16), b.astype(BF16))
    ah, al = _split(a, 2)
    bh, bl = _split(b, 2)
    return _dot(ah, bh) + _dot(al, bh) + _dot(ah, bl)


def _sigmoid(x):
    return 1.0 / (1.0 + jnp.exp(-x))


def _silu(x):
    return x * _sigmoid(x)


def _rwkv_tokens(r, k, v, wa, prm, headones, between_steps=lambda: None):
    n = r.shape[0]
    lane = lax.broadcasted_iota(jnp.int32, (n, LANES), 1)
    th = jnp.where(lane < LORA, jnp.tanh(wa), wa)
    lora = _mm(th, prm["wlora"], LORA_PASSES)
    lw = -DECAY_SCALE * _sigmoid(prm["w0"] + lora[:, :RW])
    a = _sigmoid(prm["a0"] + lora[:, RW:])
    between_steps()
    kk = k * prm["k_k"]
    ss = _head_sum(kk * kk, headones)
    kk = kk / jnp.maximum(jnp.sqrt(ss), 1e-12)
    between_steps()
    kmod = k * (1.0 + (a - 1.0) * prm["k_a"])
    bonus = _head_sum(r * kmod * prm["r_k"], headones) * v
    between_steps()
    return lw, a, kk, kmod, bonus


def _rwkv_post(o, bonus, g_r, prm, headones):
    mean = _head_sum(o, headones) * (1.0 / HEAD_DIM)
    d = o - mean
    var = _head_sum(d * d, headones) * (1.0 / HEAD_DIM)
    on = d * lax.rsqrt(var + GN_EPS)
    on = on * prm["ln_w"] + prm["ln_b"]
    return (on + bonus) * _silu(g_r)


_PRM_NAMES = ("conv_w", "mu", "w0", "a0", "w_up", "a_up", "k_k", "k_a", "r_k", "ln_w", "ln_b")


def _prm_arrays(conv_w, mu_shift, w0, w_up, a0, a_up, k_k, k_a, r_k, ln_w, ln_b):
    row = lambda x: x.reshape(1, -1)
    return (conv_w, row(mu_shift), row(w0), row(a0), w_up, a_up, row(k_k), row(k_a),
            row(r_k), row(ln_w), row(ln_b))


def _prm_specs(grid_rank):
    zero = (lambda *_: (0, 0))
    shapes = ((3, CONV_CH), (1, SHIFT_W), (1, RW), (1, RW), (LORA, RW), (LORA, RW),
              (1, RW), (1, RW), (1, RW), (1, RW), (1, RW))
    return [pl.BlockSpec(s, zero) for s in shapes]


def _load_prm(refs):
    prm = {n: r[...] for n, r in zip(_PRM_NAMES, refs)}
    zero = jnp.zeros((LORA, RW), F32)
    prm["wlora"] = jnp.concatenate([jnp.concatenate([prm["w_up"], zero], axis=1),
                                    jnp.concatenate([zero, prm["a_up"]], axis=1)], axis=0)
    return prm


def _stack_heads(x):
    lane = lax.broadcasted_iota(jnp.int32, x.shape, 1)
    lo = jnp.where(lane < HEAD_DIM, x, 0.0)
    hi = jnp.where(lane >= HEAD_DIM, x, 0.0)
    return jnp.concatenate([lo, hi], axis=0)


def _stack_heads_bf16(x):
    lane = lax.broadcasted_iota(jnp.int32, x.shape, 1)
    lo = jnp.where(lane < HEAD_DIM, 1.0, 0.0).astype(BF16)
    xb = x.astype(BF16)
    return jnp.concatenate([xb * lo, xb * (1.0 - lo)], axis=0)


def _bdot(a, b):
    return _dot(a.astype(BF16), b.astype(BF16))


def _diag_block_inverses_minus_eye(mats, lane_bcast):
    n = mats[0].shape[0]
    nb = n // SUBLANES
    row = lax.broadcasted_iota(jnp.int32, (n, n), 0)
    col = lax.broadcasted_iota(jnp.int32, (n, n), 1)
    in_diag = ((row // SUBLANES) == (col // SUBLANES)) & (row > col)
    packed = []
    for a in mats:
        d = jnp.where(in_diag, a, 0.0)
        acc = d[0:SUBLANES]
        for m in range(1, nb):
            acc = acc + d[m * SUBLANES:(m + 1) * SUBLANES]
        packed.append(acc)
    d_all = jnp.concatenate(packed, axis=0).astype(BF16)
    sub = lax.broadcasted_iota(jnp.int32, (SUBLANES, n), 0)
    lane = lax.broadcasted_iota(jnp.int32, (SUBLANES, n), 1)
    eye8 = jnp.where(sub == lane % SUBLANES, 1.0, 0.0)
    ts = [eye8 for _ in mats]
    for j in range(SUBLANES - 1):
        spread = _dot(d_all, lane_bcast[j])
        ts = [t + spread[i * SUBLANES:(i + 1) * SUBLANES] * jnp.broadcast_to(t[j:j + 1, :], (SUBLANES, n))
              for i, t in enumerate(ts)]
    blk = lax.broadcasted_iota(jnp.int32, (SUBLANES, n), 1) // SUBLANES
    out = []
    for t in ts:
        tm1 = t - eye8
        out.append(jnp.concatenate([jnp.where(blk == m, tm1, 0.0) for m in range(nb)], axis=0))
    return out


def _unit_lower_inverse_minus_eye(mats, lane_bcast, between_levels=lambda: None):
    n = mats[0].shape[0]
    row = lax.broadcasted_iota(jnp.int32, (n, n), 0)
    col = lax.broadcasted_iota(jnp.int32, (n, n), 1)
    xs = _diag_block_inverses_minus_eye(mats, lane_bcast)
    size = 2 * SUBLANES
    while size <= CHUNK:
        half = size // 2
        sel = ((row // size) == (col // size)) & ((row % size) >= half) & ((col % size) < half)
        als = [jnp.where(sel, a, 0.0) for a in mats]
        ps = [al + _bdot(x, al) for x, al in zip(xs, als)]
        xs = [x + p + _bdot(p, x) for x, p in zip(xs, ps)]
        between_levels()
        size *= 2
    return xs


def _chunk_operators(ld, strict, incl, lane_bcast, side_work=()):
    side_work = list(side_work)

    def breathe():
        if side_work:
            side_work.pop(0)()

    cat0 = lambda xs: jnp.concatenate(xs, axis=0)
    la = [_stack_heads(x) for x in ld("la")]
    lr = [_stack_heads(x) for x in ld("lr")]
    n2 = 2 * CHUNK
    amat = [_dot_nt(cat0([a, b]).astype(BF16), cat0([c, d]).astype(BF16))
            for a, b, c, d in zip(la, lr, ld("rb"), ld("rk"))]

    def block_diag_pair(m):
        swapped = pltpu.roll(m, CHUNK, axis=1)
        first = cat0([m[0:CHUNK], swapped[CHUNK:n2]])
        second = cat0([swapped[0:CHUNK], m[CHUNK:n2]])
        return first, second

    top = [block_diag_pair(m[0:n2]) for m in amat]
    a_ab = [jnp.where(strict, t[0], 0.0) for t in top]
    xs = _unit_lower_inverse_minus_eye(a_ab, lane_bcast, breathe)
    vst = [_stack_heads_bf16(x) for x in ld("v")]
    av = [_dot(jnp.where(strict, t[1], 0.0).astype(BF16), v) for t, v in zip(top, vst)]
    breathe()
    uv0 = [y + _bdot(x, y) for x, y in zip(xs, av)]
    tla = [y + _bdot(x, y) for x, y in zip(xs, la)]
    breathe()
    bottom = [block_diag_pair(m[n2:2 * n2]) for m in amat]
    a_rb = [jnp.where(incl, t[0], 0.0) for t in bottom]
    a_rk = [jnp.where(incl, t[1], 0.0) for t in bottom]
    gop = [y + _bdot(a, t) for y, a, t in zip(lr, a_rb, tla)]
    uv = [cat0([u.astype(BF16), v]) for u, v in zip(uv0, vst)]
    y0 = [_dot(jnp.concatenate([a, b], axis=1).astype(BF16), w) for a, b, w in zip(a_rb, a_rk, uv)]
    breathe()
    bt = [_stack_heads_bf16(x) for x in ld("bt")]
    kt = [_stack_heads_bf16(x) for x in ld("kt")]
    pop = [_dot_tn(t.astype(BF16), b) for t, b in zip(tla, bt)]
    qop = [_dot_tn(w, cat0([b, k])) for w, b, k in zip(uv, bt, kt)]
    while side_work:
        breathe()
    return gop, y0, pop, qop


def _prompt_kernel(xh_ref, xt_ref, pt_ref, ng_ref, win_ref, wo_ref, wg_ref, wp_ref, fg_ref, *refs,
                   tc, n_t, n_tiles):
    prm_refs = refs[:len(_PRM_NAMES)]
    headones_ref, tri_ref, bcast_ref = refs[len(_PRM_NAMES):len(_PRM_NAMES) + 3]
    (y_ref, conv_out_ref, shift_out_ref, wkv_out_ref) = refs[len(_PRM_NAMES) + 3:len(_PRM_NAMES) + 7]
    (z_s, ycat_s, ubuf, zsbuf, s_ref, la_s, lr_s, rb_s, rk_s, bt_s, kt_s, v_s, gam_s, o_s,
     g_sc, y0_sc, p_sc, q_sc, bonus_p, gr_p) = refs[len(_PRM_NAMES) + 7:]

    s = pl.program_id(0)
    head_t = lax.rem(s, jnp.int32(n_t))
    tail_t = lax.rem(s + (n_t - 1), jnp.int32(n_t))
    prm = _load_prm(prm_refs)
    headones = headones_ref[...]
    n2 = 2 * CHUNK
    n_chunks = tc // CHUNK

    @pl.when(s == 0)
    def _():
        for ref in (g_sc, y0_sc, p_sc, q_sc, gam_s, bonus_p, gr_p, ycat_s, s_ref):
            ref[...] = jnp.zeros(ref.shape, ref.dtype)

    @pl.when(head_t == 0)
    def _():
        ubuf[0:CARRY_ROWS, :] = jnp.zeros((CARRY_ROWS, CONV_CH), F32)
        zsbuf[0:CARRY_ROWS, :] = jnp.zeros((CARRY_ROWS, SHIFT_W), F32)

    @pl.when(tail_t == 0)
    def _():
        s_ref[...] = jnp.zeros(s_ref.shape, F32)

    tail = {}

    def tail_state_pass(c):
        def run():
            for j in range(N_PAIRS):
                idx = c * N_PAIRS + j
                lanes = slice(j * LANES, (j + 1) * LANES)
                s_old = s_ref[j]
                s_bf = s_old.astype(BF16)
                yst = _dot_nt(g_sc[idx], s_bf) + y0_sc[idx]
                o_s[c * CHUNK:(c + 1) * CHUNK, lanes] = yst[0:CHUNK] + yst[CHUNK:n2]
                gam = gam_s[c * CHUNK:c * CHUNK + 1, lanes]
                s_ref[j] = s_old * gam + _dot(s_bf, p_sc[idx]) + q_sc[idx]
        return run

    def tail_post():
        ycat_s[:, CONV_CH:2 * CONV_CH] = _rwkv_post(o_s[...], bonus_p[...], gr_p[...], prm, headones)

    def tail_out_proj():
        tail["h1"] = xt_ref[...] + _dot(ycat_s[...].astype(BF16), wo_ref[...])

    def tail_gate():
        tail["gate"] = _sigmoid(_dot(tail["h1"].astype(BF16), wg_ref[...]))
        tail["pe"] = _dot(pt_ref[...].astype(BF16), wp_ref[...])

    def tail_norm():
        h2 = tail["h1"] + tail["gate"] * tail["pe"]
        ms = jnp.mean(h2 * h2, axis=-1, keepdims=True)
        y_ref[...] = h2 * lax.rsqrt(ms + RMS_EPS) * fg_ref[...]

    x = xh_ref[...]
    xn = (x * lax.rsqrt(jnp.mean(x * x, axis=-1, keepdims=True) + RMS_EPS) * ng_ref[...]).astype(BF16)

    def project(c0):
        def run():
            z_s[:, c0:c0 + CONV_CH] = _dot(xn, win_ref[:, c0:c0 + CONV_CH])
        return run

    passes = [tail_state_pass(c) for c in range(n_chunks)]
    blocks = [project(c0) for c0 in (OFF_GR, 0, CONV_CH, 2 * CONV_CH, 3 * CONV_CH)]
    side_queue = []
    while passes or len(blocks) > 1:
        side_queue += passes[:1] + (blocks[:1] if len(blocks) > 1 else [])
        passes, blocks = passes[1:], (blocks[1:] if len(blocks) > 1 else blocks)
    side_queue += [tail_post, blocks[0], tail_out_proj, tail_gate, tail_norm]

    def project_next(count=2):
        for _ in range(count):
            if side_queue:
                side_queue.pop(0)()

    z_s[:, OFF_ZS:OFF_ZS + SHIFT_W] = _dot(xn, win_ref[:, OFF_ZS:OFF_ZS + SHIFT_W])

    zs = z_s[:, OFF_ZS:OFF_ZS + SHIFT_W]
    zsbuf[CARRY_ROWS:CARRY_ROWS + tc, :] = zs
    zprev = zsbuf[CARRY_ROWS - 1:CARRY_ROWS - 1 + tc, :]
    zm = zs + (zprev - zs) * prm["mu"]
    zsbuf[CARRY_ROWS - 1:CARRY_ROWS, :] = zsbuf[CARRY_ROWS + tc - 1:CARRY_ROWS + tc, :]

    project_next()
    r = zm[:, 0:RW]
    k = zm[:, RW:2 * RW]
    v = zm[:, 2 * RW:3 * RW]
    wa = zm[:, 3 * RW:3 * RW + 2 * LORA]
    lw, a, kk, kmod, bonus = _rwkv_tokens(r, k, v, wa, prm, headones, project_next)

    g = _const_dot(tri_ref[...], lw, CUM_PARTS)
    gc = jnp.concatenate(
        [jnp.broadcast_to(g[(c + 1) * CHUNK - 1:(c + 1) * CHUNK, :], (CHUNK, RW)) for c in range(tc // CHUNK)],
        axis=0)
    project_next()
    eng = jnp.exp(-g)
    etail = jnp.exp(gc - g)
    b = kk * a
    la_s[...] = -kk * jnp.exp(g - lw)
    lr_s[...] = r * jnp.exp(g)
    project_next()
    rb_s[...] = b * eng
    rk_s[...] = kmod * eng
    bt_s[...] = b * etail
    kt_s[...] = kmod * etail
    v_s[...] = v
    while side_queue:
        project_next()
    gam_s[...] = jnp.exp(gc)

    row = lax.broadcasted_iota(jnp.int32, (n2, n2), 0)
    col = lax.broadcasted_iota(jnp.int32, (n2, n2), 1)
    same = (row // CHUNK) == (col // CHUNK)
    strict = same & (row > col)
    incl = same & (row >= col)

    cw = prm["conv_w"]

    def conv_rows(r0):
        def run():
            rows = slice(r0, r0 + CHUNK)
            u = z_s[rows, CONV_CH:2 * CONV_CH] * z_s[rows, 2 * CONV_CH:3 * CONV_CH]
            ubuf[CARRY_ROWS + r0:CARRY_ROWS + r0 + CHUNK, :] = u
            um1 = ubuf[CARRY_ROWS - 1 + r0:CARRY_ROWS - 1 + r0 + CHUNK, :]
            um2 = ubuf[CARRY_ROWS - 2 + r0:CARRY_ROWS - 2 + r0 + CHUNK, :]
            conv = cw[0:1, :] * um2 + cw[1:2, :] * um1 + cw[2:3, :] * u
            ycat_s[rows, 0:CONV_CH] = z_s[rows, 0:CONV_CH] * conv * _silu(z_s[rows, 3 * CONV_CH:4 * CONV_CH])
        return run

    srcs = {"la": la_s, "lr": lr_s, "rb": rb_s, "rk": rk_s, "bt": bt_s, "kt": kt_s, "v": v_s}
    members = [(c, j) for c in range(n_chunks) for j in range(N_PAIRS)]

    def ld(name):
        ref = srcs[name]
        return [ref[c * CHUNK:(c + 1) * CHUNK, j * LANES:(j + 1) * LANES] for c, j in members]

    lane_bcast = [bcast_ref[j] for j in range(SUBLANES - 1)]
    gop, y0, pop, qop = _chunk_operators(ld, strict, incl, lane_bcast,
                                         [conv_rows(c * CHUNK) for c in range(n_chunks)])
    for (c, j), g_, y_, p_, q_ in zip(members, gop, y0, pop, qop):
        idx = c * N_PAIRS + j
        g_sc[idx] = g_.astype(BF16)
        y0_sc[idx] = y_
        p_sc[idx] = p_.astype(BF16)
        q_sc[idx] = q_

    ubuf[CARRY_ROWS - 2:CARRY_ROWS, :] = ubuf[CARRY_ROWS + tc - 2:CARRY_ROWS + tc, :]

    bonus_p[...] = bonus
    gr_p[...] = z_s[:, OFF_GR:OFF_GR + RW]

    @pl.when((head_t == n_t - 1) & (s < n_tiles))
    def _():
        shift_out_ref[0] = zsbuf[CARRY_ROWS - 1:CARRY_ROWS, :]
        conv_out_ref[0] = ubuf[CARRY_ROWS - 2:CARRY_ROWS, :]

    @pl.when((tail_t == n_t - 1) & (s > 0))
    def _():
        for h in range(N_HEADS):
            j, i = divmod(h, 2)
            blk = s_ref[j]
            wkv_out_ref[0, h] = blk[i * HEAD_DIM:(i + 1) * HEAD_DIM, i * HEAD_DIM:(i + 1) * HEAD_DIM]


def _prompt_layer(x2d, p2d, w, bsz, t_len, tc):
    n_t = t_len // tc
    tok = np.arange(tc)
    blockones = (tok[:, None] // CHUNK) == (tok[None, :] // CHUNK)
    tri = jnp.asarray(blockones & (tok[:, None] >= tok[None, :]), BF16)
    n_tiles = bsz * n_t
    n2 = 2 * CHUNK
    idx = np.arange(n2)
    same_blk = (idx[:, None] // SUBLANES) == (idx[None, :] // SUBLANES)
    lane_bcast = jnp.asarray(
        np.stack([same_blk & (idx[:, None] % SUBLANES == j) for j in range(SUBLANES - 1)]), BF16)
    kern = functools.partial(_prompt_kernel, tc=tc, n_t=n_t, n_tiles=n_tiles)
    big = lambda: pltpu.VMEM((tc, RW), F32)
    n_blk = (tc // CHUNK) * N_PAIRS
    op = lambda dt: pltpu.VMEM((n_blk, LANES, LANES), dt)
    head = lambda s: jnp.minimum(s, n_tiles - 1)
    tail = lambda s: jnp.maximum(s - 1, 0)
    head_tile = lambda width: pl.BlockSpec((tc, width), lambda s: (head(s), 0))
    tail_tile = lambda width: pl.BlockSpec((tc, width), lambda s: (tail(s), 0))
    const = lambda shape: pl.BlockSpec(shape, lambda s: (0, 0), pipeline_mode=pl.Buffered(1))
    return pl.pallas_call(
        kern,
        grid=(n_tiles + 1,),
        in_specs=[head_tile(D_MODEL), tail_tile(D_MODEL), tail_tile(D_PLE), const((1, D_MODEL)),
                  const((D_MODEL, IN_W)), const((D_MODEL, D_MODEL)), const((D_MODEL, D_MODEL)),
                  const((D_PLE, D_MODEL)), const((1, D_MODEL))]
        + _prm_specs(1)
        + [const((HEADSUM_W, HEADSUM_W)), const((tc, tc)),
           pl.BlockSpec((SUBLANES - 1, n2, n2), lambda s: (0, 0, 0), pipeline_mode=pl.Buffered(1))],
        out_specs=[
            tail_tile(D_MODEL),
            pl.BlockSpec((1, 2, CONV_CH), lambda s: (head(s) // n_t, 0, 0)),
            pl.BlockSpec((1, 1, SHIFT_W), lambda s: (head(s) // n_t, 0, 0)),
            pl.BlockSpec((1, N_HEADS, HEAD_DIM, HEAD_DIM), lambda s: (tail(s) // n_t, 0, 0, 0)),
        ],
        out_shape=[
            jax.ShapeDtypeStruct((bsz * t_len, D_MODEL), F32),
            jax.ShapeDtypeStruct((bsz, 2, CONV_CH), F32),
            jax.ShapeDtypeStruct((bsz, 1, SHIFT_W), F32),
            jax.ShapeDtypeStruct((bsz, N_HEADS, HEAD_DIM, HEAD_DIM), F32),
        ],
        scratch_shapes=[
            pltpu.VMEM((tc, IN_W), F32),
            pltpu.VMEM((tc, D_MODEL), F32),
            pltpu.VMEM((CARRY_ROWS + tc, CONV_CH), F32),
            pltpu.VMEM((CARRY_ROWS + tc, SHIFT_W), F32),
            pltpu.VMEM((N_PAIRS, LANES, LANES), F32),
            big(), big(), big(), big(), big(), big(), big(), big(), big(),
            op(BF16), op(F32), op(BF16), op(F32),
            big(), big(),
        ],
        compiler_params=pltpu.CompilerParams(
            dimension_semantics=("arbitrary",), vmem_limit_bytes=VMEM_LIMIT),
        name="prompt_layer",
    )(x2d, x2d, p2d, w["norm_g"].reshape(1, D_MODEL), w["w_in"], w["w_out"], w["w_pg"], w["w_pp"],
      w["final_g"].reshape(1, D_MODEL), *w["prm"], w["headones"], tri, lane_bcast)


def _sample_front_kernel(x_ref, ng_ref, win_ref, cb_ref, sb_ref, *refs):
    prm_refs = refs[:len(_PRM_NAMES)]
    headones_ref = refs[len(_PRM_NAMES)]
    (ya_ref, conv_out_ref, shift_out_ref, gr_ref, bonus_ref,
     nkk_ref, w_ref, b_ref, km_ref, v_ref, r_ref, wbf_ref, xn_s, z_s) = refs[len(_PRM_NAMES) + 1:]
    j = pl.program_id(0)

    @pl.when(j == 0)
    def _():
        x = x_ref[:, 0, :]
        xn_s[...] = (x * lax.rsqrt(jnp.mean(x * x, axis=-1, keepdims=True) + RMS_EPS) * ng_ref[...]).astype(BF16)

    w_blk = win_ref[...].astype(BF16)
    wbf_ref[...] = w_blk
    z_s[j] = _dot(xn_s[...], w_blk)

    @pl.when(j == pl.num_programs(0) - 1)
    def _():
        _sample_front_tokens(z_s, cb_ref, sb_ref, prm_refs, headones_ref, ya_ref, conv_out_ref, shift_out_ref,
                             gr_ref, bonus_ref, nkk_ref, w_ref, b_ref, km_ref, v_ref, r_ref)


def _sample_front_tokens(z_s, cb_ref, sb_ref, prm_refs, headones_ref, ya_ref, conv_out_ref, shift_out_ref,
                         gr_ref, bonus_ref, nkk_ref, w_ref, b_ref, km_ref, v_ref, r_ref):
    prm = _load_prm(prm_refs)
    headones = headones_ref[...]
    z = jnp.concatenate([z_s[i] for i in range(z_s.shape[0])], axis=1)

    u = z[:, CONV_CH:2 * CONV_CH] * z[:, 2 * CONV_CH:3 * CONV_CH]
    cb0 = cb_ref[:, 0, :]
    cb1 = cb_ref[:, 1, :]
    cw = prm["conv_w"]
    conv = cw[0:1, :] * cb0 + cw[1:2, :] * cb1 + cw[2:3, :] * u
    ya_ref[...] = z[:, 0:CONV_CH] * conv * _silu(z[:, 3 * CONV_CH:4 * CONV_CH])
    conv_out_ref[:, 0, :] = cb1
    conv_out_ref[:, 1, :] = u

    zs = z[:, OFF_ZS:OFF_ZS + SHIFT_W]
    shift_out_ref[...] = zs
    gr_ref[...] = z[:, OFF_GR:OFF_GR + RW]
    zm = zs + (sb_ref[...] - zs) * prm["mu"]
    r = zm[:, 0:RW]
    k = zm[:, RW:2 * RW]
    v = zm[:, 2 * RW:3 * RW]
    wa = zm[:, 3 * RW:3 * RW + 2 * LORA]
    lw, a, kk, kmod, bonus = _rwkv_tokens(r, k, v, wa, prm, headones)
    bonus_ref[...] = bonus
    nkk_ref[...] = (-kk).T
    w_ref[...] = jnp.exp(lw).T
    b_ref[...] = (kk * a).T
    km_ref[...] = kmod.T
    v_ref[...] = v.T
    r_ref[...] = r.T


def _sample_step_kernel(s_ref, nkk_ref, w_ref, b_ref, km_ref, v_ref, r_ref, s1_ref, y_ref):
    nkk = nkk_ref[0]
    w = w_ref[0]
    b = b_ref[0]
    km = km_ref[0]
    r = r_ref[0]
    for vi in range(HEAD_DIM):
        s0 = s_ref[0, vi]
        sa = jnp.sum(s0 * nkk, axis=0, keepdims=True)
        s1 = s0 * w + sa * b + v_ref[0, vi:vi + 1, :] * km
        s1_ref[0, vi] = s1
        y_ref[0, vi:vi + 1, :] = jnp.sum(s1 * r, axis=0, keepdims=True)


def _sample_back_kernel(o_ref, bonus_ref, gr_ref, ya_ref, x_ref, p_ref, wo_ref, wg_ref, wp_ref, fg_ref, *refs):
    prm_refs = refs[:len(_PRM_NAMES)]
    headones_ref = refs[len(_PRM_NAMES)]
    y_ref, wo_bf_ref, wg_bf_ref, wp_bf_ref = refs[len(_PRM_NAMES) + 1:]
    prm = _load_prm(prm_refs)
    wo = wo_ref[...].astype(BF16)
    wg = wg_ref[...].astype(BF16)
    wp = wp_ref[...].astype(BF16)
    wo_bf_ref[...] = wo
    wg_bf_ref[...] = wg
    wp_bf_ref[...] = wp
    y_r = _rwkv_post(o_ref[...].T, bonus_ref[...], gr_ref[...], prm, headones_ref[...])
    ycat = jnp.concatenate([ya_ref[...], y_r], axis=1).astype(BF16)
    h1 = x_ref[:, 0, :] + _dot(ycat, wo)
    gate = _sigmoid(_dot(h1.astype(BF16), wg))
    pe = _dot(p_ref[:, 0, :].astype(BF16), wp)
    h2 = h1 + gate * pe
    ms = jnp.mean(h2 * h2, axis=-1, keepdims=True)
    y_ref[:, 0, :] = h2 * lax.rsqrt(ms + RMS_EPS) * fg_ref[...]


def _full(shape):
    nd = len(shape)
    return pl.BlockSpec(shape, lambda *_: (0,) * nd)


def _sample_layer(x3d, p3d, state_conv, state_shift, state_hvkb, w):
    n = x3d.shape[0]
    params = pltpu.CompilerParams(dimension_semantics=("arbitrary",), vmem_limit_bytes=VMEM_LIMIT)
    row = jax.ShapeDtypeStruct((n, RW), F32)
    col = jax.ShapeDtypeStruct((RW, n), F32)
    wblk = IN_W // W_IN_BLOCKS
    front = pl.pallas_call(
        _sample_front_kernel,
        grid=(W_IN_BLOCKS,),
        in_specs=[_full((n, 1, D_MODEL)), _full((1, D_MODEL)), pl.BlockSpec((D_MODEL, wblk), lambda j: (0, j)),
                  _full((n, 2, CONV_CH)), _full((n, SHIFT_W))]
        + _prm_specs(1) + [_full((HEADSUM_W, HEADSUM_W))],
        out_specs=[_full((n, CONV_CH)), _full((n, 2, CONV_CH)), _full((n, SHIFT_W)),
                   _full((n, RW)), _full((n, RW))] + [_full((RW, n))] * 6
        + [pl.BlockSpec((D_MODEL, wblk), lambda j: (0, j))],
        out_shape=[jax.ShapeDtypeStruct((n, CONV_CH), F32), jax.ShapeDtypeStruct((n, 2, CONV_CH), F32),
                   jax.ShapeDtypeStruct((n, SHIFT_W), F32), row, row] + [col] * 6
        + [jax.ShapeDtypeStruct((D_MODEL, IN_W), BF16)],
        scratch_shapes=[pltpu.VMEM((n, D_MODEL), BF16), pltpu.VMEM((W_IN_BLOCKS, n, wblk), F32)],
        compiler_params=params,
        name="sample_front",
    )(x3d, w["norm_g"].reshape(1, D_MODEL), w["w_in_f32"], state_conv, state_shift, *w["prm"], w["headones"])
    ya, conv_new, shift_new, g_r, bonus, nkk, dec, b, km, v, r, w_in_bf16 = front

    hk = lambda a: a.reshape(N_HEADS, HEAD_DIM, n)
    vspec = pl.BlockSpec((1, HEAD_DIM, n), lambda h: (h, 0, 0))
    sspec = pl.BlockSpec((1, HEAD_DIM, HEAD_DIM, n), lambda h: (h, 0, 0, 0))
    s1, y = pl.pallas_call(
        _sample_step_kernel,
        grid=(N_HEADS,),
        in_specs=[sspec] + [vspec] * 6,
        out_specs=[sspec, vspec],
        out_shape=[jax.ShapeDtypeStruct((N_HEADS, HEAD_DIM, HEAD_DIM, n), F32),
                   jax.ShapeDtypeStruct((N_HEADS, HEAD_DIM, n), F32)],
        compiler_params=params,
        name="sample_step",
    )(state_hvkb, hk(nkk), hk(dec), hk(b), hk(km), hk(v), hk(r))

    sq, pp = (D_MODEL, D_MODEL), (D_PLE, D_MODEL)
    y_out, wo_bf, wg_bf, wp_bf = pl.pallas_call(
        _sample_back_kernel,
        grid=(1,),
        in_specs=[_full((RW, n)), _full((n, RW)), _full((n, RW)), _full((n, CONV_CH)),
                  _full((n, 1, D_MODEL)), _full((n, 1, D_PLE)), _full(sq), _full(sq), _full(pp),
                  _full((1, D_MODEL))]
        + _prm_specs(1) + [_full((HEADSUM_W, HEADSUM_W))],
        out_specs=[_full((n, 1, D_MODEL)), _full(sq), _full(sq), _full(pp)],
        out_shape=[jax.ShapeDtypeStruct((n, 1, D_MODEL), F32), jax.ShapeDtypeStruct(sq, BF16),
                   jax.ShapeDtypeStruct(sq, BF16), jax.ShapeDtypeStruct(pp, BF16)],
        compiler_params=params,
        name="sample_back",
    )(y.reshape(RW, n), bonus, g_r, ya, x3d, p3d, w["w_out_f32"], w["w_pg_f32"], w["w_pp_f32"],
      w["final_g"].reshape(1, D_MODEL), *w["prm"], w["headones"])
    rounded = {"w_in": w_in_bf16, "w_out": wo_bf, "w_pg": wg_bf, "w_pp": wp_bf}
    return y_out, conv_new, shift_new, s1, rounded


def _layer_prompt(x, p, w, tc=256):
    bsz, t_len, _ = x.shape
    x2d = x.reshape(bsz * t_len, D_MODEL)
    y, conv_new, shift_new, wkv_new = _prompt_layer(x2d, p.reshape(bsz * t_len, D_PLE), w, bsz, t_len, tc)
    return y.reshape(bsz, t_len, D_MODEL), conv_new, shift_new.reshape(bsz, SHIFT_W), wkv_new


def _layer_sample(x, p, state_conv, state_shift, state_wkv, w):
    n = x.shape[0]
    s_hvkb = jnp.transpose(state_wkv, (1, 2, 3, 0))
    y, conv_new, shift_new, s1, rounded = _sample_layer(x, p, state_conv, state_shift, s_hvkb, w)
    return y, conv_new, shift_new, jnp.transpose(s1, (3, 0, 1, 2)), rounded


def kernel(x_prompt, x_sample, p_prompt, p_sample, state_conv, state_shift, state_wkv, norm_g, w_in, conv_w, mu_shift, w0, w_up, a0, a_up, k_k, k_a, r_k, ln_w, ln_b, w_out, w_pg, w_pp, final_g):
    depth = norm_g.shape[0]
    assert depth == 1
    i = 0
    head_id = np.arange(HEADSUM_W) // HEAD_DIM
    w = {
        "norm_g": norm_g[i],
        "w_in_f32": w_in[i],
        "w_out_f32": w_out[i],
        "w_pg_f32": w_pg[i],
        "w_pp_f32": w_pp[i],
        "final_g": final_g,
        "prm": _prm_arrays(conv_w[i], mu_shift[i], w0[i], w_up[i], a0[i], a_up[i], k_k[i], k_a[i],
                           r_k[i].reshape(RW), ln_w[i], ln_b[i]),
        "headones": jnp.asarray(head_id[:, None] == head_id[None, :], BF16),
    }
    ys, cs, ss, ws, rounded = _layer_sample(x_sample, p_sample[i], state_conv[i], state_shift[i],
                                            state_wkv[i], w)
    yp, cp, sp, wp = _layer_prompt(x_prompt, p_prompt[i], {**w, **rounded})
    return (yp, ys, cp[None], sp[None], wp[None], cs[None], ss[None], ws[None])
```

```python
import functools
import math

import jax
import jax.numpy as jnp
import numpy as np
from jax import lax
from jax.experimental import pallas as pl
from jax.experimental.pallas import tpu as pltpu

F32 = jnp.float32
BF16 = jnp.bfloat16

D_MODEL = 1024
CONV_CH = 512
RW = 512
HEAD_DIM = 64
N_HEADS = 8
N_PAIRS = N_HEADS // 2
LORA = 64
D_PLE = 256
SHIFT_W = 3 * RW + 2 * LORA
IN_W = 4 * CONV_CH + SHIFT_W + RW
OFF_ZS = 4 * CONV_CH
OFF_GR = OFF_ZS + SHIFT_W
RMS_EPS = 1e-6
GN_EPS = 64e-5
DECAY_SCALE = math.exp(-0.5)

LANES = 128
SUBLANES = 8
MXU_DIM = 256
HEADSUM_W = MXU_DIM
SUM_PARTS = 1
CUM_PARTS = 1
LORA_PASSES = 1
CHUNK = 64
W_IN_BLOCKS = 3
CARRY_ROWS = 8
VMEM_LIMIT = 56 * 1024 * 1024


def _dot(a, b):
    return jnp.dot(a, b, preferred_element_type=F32)


def _dot_nt(a, b):
    return lax.dot_general(a, b, (((1,), (1,)), ((), ())), preferred_element_type=F32)


def _dot_tn(a, b):
    return lax.dot_general(a, b, (((0,), (0,)), ((), ())), preferred_element_type=F32)


def _split(x, parts):
    out = []
    rem = x
    for i in range(parts):
        t = rem.astype(BF16)
        out.append(t)
        if i + 1 < parts:
            rem = rem - t.astype(F32)
    return out


def _const_dot(c_bf16, x, parts):
    acc = None
    for t in _split(x, parts):
        d = _dot(c_bf16, t)
        acc = d if acc is None else acc + d
    return acc


def _x_dot_const(x, c_bf16, parts):
    acc = None
    for t in _split(x, parts):
        d = _dot(t, c_bf16)
        acc = d if acc is None else acc + d
    return acc


def _head_sum(x, headones):
    w = headones.shape[0]
    halves = [_x_dot_const(x[:, i * w:(i + 1) * w], headones, SUM_PARTS) for i in range(x.shape[1] // w)]
    return jnp.concatenate(halves, axis=1)


def _mm(a, b, passes):
    if passes == 1:
        return _dot(a.astype(BF16), b.astype(BF16))
    ah, al = _split(a, 2)
    bh, bl = _split(b, 2)
    return _dot(ah, bh) + _dot(al, bh) + _dot(ah, bl)


def _sigmoid(x):
    return 1.0 / (1.0 + jnp.exp(-x))


def _silu(x):
    return x * _sigmoid(x)


def _rwkv_tokens(r, k, v, wa, prm, headones, between_steps=lambda: None):
    n = r.shape[0]
    lane = lax.broadcasted_iota(jnp.int32, (n, LANES), 1)
    th = jnp.where(lane < LORA, jnp.tanh(wa), wa)
    lora = _mm(th, prm["wlora"], LORA_PASSES)
    lw = -DECAY_SCALE * _sigmoid(prm["w0"] + lora[:, :RW])
    a = _sigmoid(prm["a0"] + lora[:, RW:])
    between_steps()
    kk = k * prm["k_k"]
    ss = _head_sum(kk * kk, headones)
    kk = kk / jnp.maximum(jnp.sqrt(ss), 1e-12)
    between_steps()
    kmod = k * (1.0 + (a - 1.0) * prm["k_a"])
    bonus = _head_sum(r * kmod * prm["r_k"], headones) * v
    between_steps()
    return lw, a, kk, kmod, bonus


def _rwkv_post(o, bonus, g_r, prm, headones):
    mean = _head_sum(o, headones) * (1.0 / HEAD_DIM)
    d = o - mean
    var = _head_sum(d * d, headones) * (1.0 / HEAD_DIM)
    on = d * lax.rsqrt(var + GN_EPS)
    on = on * prm["ln_w"] + prm["ln_b"]
    return (on + bonus) * _silu(g_r)


_PRM_NAMES = ("conv_w", "mu", "w0", "a0", "w_up", "a_up", "k_k", "k_a", "r_k", "ln_w", "ln_b")


def _prm_arrays(conv_w, mu_shift, w0, w_up, a0, a_up, k_k, k_a, r_k, ln_w, ln_b):
    row = lambda x: x.reshape(1, -1)
    return (conv_w, row(mu_shift), row(w0), row(a0), w_up, a_up, row(k_k), row(k_a),
            row(r_k), row(ln_w), row(ln_b))


def _prm_specs(grid_rank):
    zero = (lambda *_: (0, 0))
    shapes = ((3, CONV_CH), (1, SHIFT_W), (1, RW), (1, RW), (LORA, RW), (LORA, RW),
              (1, RW), (1, RW), (1, RW), (1, RW), (1, RW))
    return [pl.BlockSpec(s, zero) for s in shapes]


def _load_prm(refs):
    prm = {n: r[...] for n, r in zip(_PRM_NAMES, refs)}
    zero = jnp.zeros((LORA, RW), F32)
    prm["wlora"] = jnp.concatenate([jnp.concatenate([prm["w_up"], zero], axis=1),
                                    jnp.concatenate([zero, prm["a_up"]], axis=1)], axis=0)
    return prm


def _stack_heads(x):
    lane = lax.broadcasted_iota(jnp.int32, x.shape, 1)
    lo = jnp.where(lane < HEAD_DIM, x, 0.0)
    hi = jnp.where(lane >= HEAD_DIM, x, 0.0)
    return jnp.concatenate([lo, hi], axis=0)


def _stack_heads_bf16(x):
    lane = lax.broadcasted_iota(jnp.int32, x.shape, 1)
    lo = jnp.where(lane < HEAD_DIM, 1.0, 0.0).astype(BF16)
    xb = x.astype(BF16)
    return jnp.concatenate([xb * lo, xb * (1.0 - lo)], axis=0)


def _bdot(a, b):
    return _dot(a.astype(BF16), b.astype(BF16))


def _diag_block_inverses_minus_eye(mats, lane_bcast):
    n = mats[0].shape[0]
    nb = n // SUBLANES
    row = lax.broadcasted_iota(jnp.int32, (n, n), 0)
    col = lax.broadcasted_iota(jnp.int32, (n, n), 1)
    in_diag = ((row // SUBLANES) == (col // SUBLANES)) & (row > col)
    packed = []
    for a in mats:
        d = jnp.where(in_diag, a, 0.0)
        acc = d[0:SUBLANES]
        for m in range(1, nb):
            acc = acc + d[m * SUBLANES:(m + 1) * SUBLANES]
        packed.append(acc)
    d_all = jnp.concatenate(packed, axis=0).astype(BF16)
    sub = lax.broadcasted_iota(jnp.int32, (SUBLANES, n), 0)
    lane = lax.broadcasted_iota(jnp.int32, (SUBLANES, n), 1)
    eye8 = jnp.where(sub == lane % SUBLANES, 1.0, 0.0)
    ts = [eye8 for _ in mats]
    for j in range(SUBLANES - 1):
        spread = _dot(d_all, lane_bcast[j])
        ts = [t + spread[i * SUBLANES:(i + 1) * SUBLANES] * jnp.broadcast_to(t[j:j + 1, :], (SUBLANES, n))
              for i, t in enumerate(ts)]
    blk = lax.broadcasted_iota(jnp.int32, (SUBLANES, n), 1) // SUBLANES
    out = []
    for t in ts:
        tm1 = t - eye8
        out.append(jnp.concatenate([jnp.where(blk == m, tm1, 0.0) for m in range(nb)], axis=0))
    return out


def _unit_lower_inverse_minus_eye(mats, lane_bcast, between_levels=lambda: None):
    n = mats[0].shape[0]
    row = lax.broadcasted_iota(jnp.int32, (n, n), 0)
    col = lax.broadcasted_iota(jnp.int32, (n, n), 1)
    xs = _diag_block_inverses_minus_eye(mats, lane_bcast)
    size = 2 * SUBLANES
    while size <= CHUNK:
        half = size // 2
        sel = ((row // size) == (col // size)) & ((row % size) >= half) & ((col % size) < half)
        als = [jnp.where(sel, a, 0.0) for a in mats]
        ps = [al + _bdot(x, al) for x, al in zip(xs, als)]
        xs = [x + p + _bdot(p, x) for x, p in zip(xs, ps)]
        between_levels()
        size *= 2
    return xs


def _chunk_operators(ld, strict, incl, lane_bcast, side_work=()):
    side_work = list(side_work)

    def breathe():
        if side_work:
            side_work.pop(0)()

    cat0 = lambda xs: jnp.concatenate(xs, axis=0)
    la = [_stack_heads(x) for x in ld("la")]
    lr = [_stack_heads(x) for x in ld("lr")]
    n2 = 2 * CHUNK
    amat = [_dot_nt(cat0([a, b]).astype(BF16), cat0([c, d]).astype(BF16))
            for a, b, c, d in zip(la, lr, ld("rb"), ld("rk"))]

    def block_diag_pair(m):
        swapped = pltpu.roll(m, CHUNK, axis=1)
        first = cat0([m[0:CHUNK], swapped[CHUNK:n2]])
        second = cat0([swapped[0:CHUNK], m[CHUNK:n2]])
        return first, second

    breathe()
    top = [block_diag_pair(m[0:n2]) for m in amat]
    a_ab = [jnp.where(strict, t[0], 0.0) for t in top]
    xs = _unit_lower_inverse_minus_eye(a_ab, lane_bcast, breathe)
    vst = [_stack_heads_bf16(x) for x in ld("v")]
    av = [_dot(jnp.where(strict, t[1], 0.0).astype(BF16), v) for t, v in zip(top, vst)]
    breathe()
    uv0 = [y + _bdot(x, y) for x, y in zip(xs, av)]
    tla = [y + _bdot(x, y) for x, y in zip(xs, la)]
    breathe()
    bottom = [block_diag_pair(m[n2:2 * n2]) for m in amat]
    a_rb = [jnp.where(incl, t[0], 0.0) for t in bottom]
    a_rk = [jnp.where(incl, t[1], 0.0) for t in bottom]
    gop = [y + _bdot(a, t) for y, a, t in zip(lr, a_rb, tla)]
    uv = [cat0([u.astype(BF16), v]) for u, v in zip(uv0, vst)]
    y0 = [_dot(jnp.concatenate([a, b], axis=1).astype(BF16), w) for a, b, w in zip(a_rb, a_rk, uv)]
    breathe()
    bt = [_stack_heads_bf16(x) for x in ld("bt")]
    kt = [_stack_heads_bf16(x) for x in ld("kt")]
    pop = [_dot_tn(t.astype(BF16), b) for t, b in zip(tla, bt)]
    qop = [_dot_tn(w, cat0([b, k])) for w, b, k in zip(uv, bt, kt)]
    while side_work:
        breathe()
    return gop, y0, pop, qop


def _prompt_kernel(xh_ref, xt_ref, pt_ref, ng_ref, win_ref, wo_ref, wg_ref, wp_ref, fg_ref, *refs,
                   tc, n_t, n_tiles):
    prm_refs = refs[:len(_PRM_NAMES)]
    headones_ref, tri_ref, bcast_ref = refs[len(_PRM_NAMES):len(_PRM_NAMES) + 3]
    (y_ref, conv_out_ref, shift_out_ref, wkv_out_ref) = refs[len(_PRM_NAMES) + 3:len(_PRM_NAMES) + 7]
    (z_s, ycat_s, ubuf, zsbuf, s_ref, la_s, lr_s, rb_s, rk_s, bt_s, kt_s, v_s, gam_s, o_s,
     g_sc, y0_sc, p_sc, q_sc, bonus_p, gr_p) = refs[len(_PRM_NAMES) + 7:]

    s = pl.program_id(0)
    head_t = lax.rem(s, jnp.int32(n_t))
    tail_t = lax.rem(s + (n_t - 1), jnp.int32(n_t))
    prm = _load_prm(prm_refs)
    headones = headones_ref[...]
    n2 = 2 * CHUNK
    n_chunks = tc // CHUNK

    @pl.when(s == 0)
    def _():
        for ref in (g_sc, y0_sc, p_sc, q_sc, gam_s, bonus_p, gr_p, ycat_s, s_ref):
            ref[...] = jnp.zeros(ref.shape, ref.dtype)

    @pl.when(head_t == 0)
    def _():
        ubuf[0:CARRY_ROWS, :] = jnp.zeros((CARRY_ROWS, CONV_CH), F32)
        zsbuf[0:CARRY_ROWS, :] = jnp.zeros((CARRY_ROWS, SHIFT_W), F32)

    @pl.when(tail_t == 0)
    def _():
        s_ref[...] = jnp.zeros(s_ref.shape, F32)

    tail = {}

    def tail_state_pass(c):
        def run():
            for j in range(N_PAIRS):
                idx = c * N_PAIRS + j
                lanes = slice(j * LANES, (j + 1) * LANES)
                s_old = s_ref[j]
                s_bf = s_old.astype(BF16)
                yst = _dot_nt(g_sc[idx], s_bf) + y0_sc[idx]
                o_s[c * CHUNK:(c + 1) * CHUNK, lanes] = yst[0:CHUNK] + yst[CHUNK:n2]
                gam = gam_s[c * CHUNK:c * CHUNK + 1, lanes]
                s_ref[j] = s_old * gam + _dot(s_bf, p_sc[idx]) + q_sc[idx]
        return run

    def tail_post():
        ycat_s[:, CONV_CH:2 * CONV_CH] = _rwkv_post(o_s[...], bonus_p[...], gr_p[...], prm, headones)

    def tail_out_proj():
        tail["h1"] = xt_ref[...] + _dot(ycat_s[...].astype(BF16), wo_ref[...])

    def tail_gate():
        tail["gate"] = _sigmoid(_dot(tail["h1"].astype(BF16), wg_ref[...]))
        tail["pe"] = _dot(pt_ref[...].astype(BF16), wp_ref[...])

    def tail_norm():
        h2 = tail["h1"] + tail["gate"] * tail["pe"]
        ms = jnp.mean(h2 * h2, axis=-1, keepdims=True)
        y_ref[...] = h2 * lax.rsqrt(ms + RMS_EPS) * fg_ref[...]

    x = xh_ref[...]
    xn = (x * lax.rsqrt(jnp.mean(x * x, axis=-1, keepdims=True) + RMS_EPS) * ng_ref[...]).astype(BF16)

    def project(c0):
        def run():
            z_s[:, c0:c0 + CONV_CH] = _dot(xn, win_ref[:, c0:c0 + CONV_CH])
        return run

    passes = [tail_state_pass(c) for c in range(n_chunks)]
    blocks = [project(c0) for c0 in (OFF_GR, 0, CONV_CH, 2 * CONV_CH, 3 * CONV_CH)]
    side_queue = []
    late_blocks, blocks = blocks[-1:], blocks[:-1]
    while passes or blocks:
        side_queue += passes[:1] + blocks[:1]
        passes, blocks = passes[1:], blocks[1:]
    side_queue += [tail_post, tail_out_proj, tail_gate, tail_norm]

    def project_next(count=2):
        for _ in range(count):
            if side_queue:
                side_queue.pop(0)()

    z_s[:, OFF_ZS:OFF_ZS + SHIFT_W] = _dot(xn, win_ref[:, OFF_ZS:OFF_ZS + SHIFT_W])

    zs = z_s[:, OFF_ZS:OFF_ZS + SHIFT_W]
    zsbuf[CARRY_ROWS:CARRY_ROWS + tc, :] = zs
    zprev = zsbuf[CARRY_ROWS - 1:CARRY_ROWS - 1 + tc, :]
    zm = zs + (zprev - zs) * prm["mu"]
    zsbuf[CARRY_ROWS - 1:CARRY_ROWS, :] = zsbuf[CARRY_ROWS + tc - 1:CARRY_ROWS + tc, :]

    project_next()
    r = zm[:, 0:RW]
    k = zm[:, RW:2 * RW]
    v = zm[:, 2 * RW:3 * RW]
    wa = zm[:, 3 * RW:3 * RW + 2 * LORA]
    lw, a, kk, kmod, bonus = _rwkv_tokens(r, k, v, wa, prm, headones, project_next)

    g = _const_dot(tri_ref[...], lw, CUM_PARTS)
    gc = jnp.concatenate(
        [jnp.broadcast_to(g[(c + 1) * CHUNK - 1:(c + 1) * CHUNK, :], (CHUNK, RW)) for c in range(tc // CHUNK)],
        axis=0)
    project_next()
    eng = jnp.exp(-g)
    etail = jnp.exp(gc - g)
    b = kk * a
    la_s[...] = -kk * jnp.exp(g - lw)
    lr_s[...] = r * jnp.exp(g)
    project_next()
    rb_s[...] = b * eng
    rk_s[...] = kmod * eng
    bt_s[...] = b * etail
    kt_s[...] = kmod * etail
    v_s[...] = v
    while side_queue:
        project_next()
    gam_s[...] = jnp.exp(gc)

    row = lax.broadcasted_iota(jnp.int32, (n2, n2), 0)
    col = lax.broadcasted_iota(jnp.int32, (n2, n2), 1)
    same = (row // CHUNK) == (col // CHUNK)
    strict = same & (row > col)
    incl = same & (row >= col)

    cw = prm["conv_w"]

    def conv_rows(r0):
        def run():
            rows = slice(r0, r0 + CHUNK)
            u = z_s[rows, CONV_CH:2 * CONV_CH] * z_s[rows, 2 * CONV_CH:3 * CONV_CH]
            ubuf[CARRY_ROWS + r0:CARRY_ROWS + r0 + CHUNK, :] = u
            um1 = ubuf[CARRY_ROWS - 1 + r0:CARRY_ROWS - 1 + r0 + CHUNK, :]
            um2 = ubuf[CARRY_ROWS - 2 + r0:CARRY_ROWS - 2 + r0 + CHUNK, :]
            conv = cw[0:1, :] * um2 + cw[1:2, :] * um1 + cw[2:3, :] * u
            ycat_s[rows, 0:CONV_CH] = z_s[rows, 0:CONV_CH] * conv * _silu(z_s[rows, 3 * CONV_CH:4 * CONV_CH])
        return run

    srcs = {"la": la_s, "lr": lr_s, "rb": rb_s, "rk": rk_s, "bt": bt_s, "kt": kt_s, "v": v_s}
    members = [(c, j) for c in range(n_chunks) for j in range(N_PAIRS)]

    def ld(name):
        ref = srcs[name]
        return [ref[c * CHUNK:(c + 1) * CHUNK, j * LANES:(j + 1) * LANES] for c, j in members]

    lane_bcast = [bcast_ref[j] for j in range(SUBLANES - 1)]
    gop, y0, pop, qop = _chunk_operators(ld, strict, incl, lane_bcast,
                                         late_blocks + [conv_rows(c * CHUNK) for c in range(n_chunks)])
    for (c, j), g_, y_, p_, q_ in zip(members, gop, y0, pop, qop):
        idx = c * N_PAIRS + j
        g_sc[idx] = g_.astype(BF16)
        y0_sc[idx] = y_
        p_sc[idx] = p_.astype(BF16)
        q_sc[idx] = q_

    ubuf[CARRY_ROWS - 2:CARRY_ROWS, :] = ubuf[CARRY_ROWS + tc - 2:CARRY_ROWS + tc, :]

    bonus_p[...] = bonus
    gr_p[...] = z_s[:, OFF_GR:OFF_GR + RW]

    @pl.when((head_t == n_t - 1) & (s < n_tiles))
    def _():
        shift_out_ref[0] = zsbuf[CARRY_ROWS - 1:CARRY_ROWS, :]
        conv_out_ref[0] = ubuf[CARRY_ROWS - 2:CARRY_ROWS, :]

    @pl.when((tail_t == n_t - 1) & (s > 0))
    def _():
        for h in range(N_HEADS):
            j, i = divmod(h, 2)
            blk = s_ref[j]
            wkv_out_ref[0, h] = blk[i * HEAD_DIM:(i + 1) * HEAD_DIM, i * HEAD_DIM:(i + 1) * HEAD_DIM]


def _prompt_layer(x2d, p2d, w, bsz, t_len, tc):
    n_t = t_len // tc
    tok = np.arange(tc)
    blockones = (tok[:, None] // CHUNK) == (tok[None, :] // CHUNK)
    tri = jnp.asarray(blockones & (tok[:, None] >= tok[None, :]), BF16)
    n_tiles = bsz * n_t
    n2 = 2 * CHUNK
    idx = np.arange(n2)
    same_blk = (idx[:, None] // SUBLANES) == (idx[None, :] // SUBLANES)
    lane_bcast = jnp.asarray(
        np.stack([same_blk & (idx[:, None] % SUBLANES == j) for j in range(SUBLANES - 1)]), BF16)
    kern = functools.partial(_prompt_kernel, tc=tc, n_t=n_t, n_tiles=n_tiles)
    big = lambda: pltpu.VMEM((tc, RW), F32)
    n_blk = (tc // CHUNK) * N_PAIRS
    op = lambda dt: pltpu.VMEM((n_blk, LANES, LANES), dt)
    head = lambda s: jnp.minimum(s, n_tiles - 1)
    tail = lambda s: jnp.maximum(s - 1, 0)
    head_tile = lambda width: pl.BlockSpec((tc, width), lambda s: (head(s), 0))
    tail_tile = lambda width: pl.BlockSpec((tc, width), lambda s: (tail(s), 0))
    const = lambda shape: pl.BlockSpec(shape, lambda s: (0, 0), pipeline_mode=pl.Buffered(1))
    return pl.pallas_call(
        kern,
        grid=(n_tiles + 1,),
        in_specs=[head_tile(D_MODEL), tail_tile(D_MODEL), tail_tile(D_PLE), const((1, D_MODEL)),
                  const((D_MODEL, IN_W)), const((D_MODEL, D_MODEL)), const((D_MODEL, D_MODEL)),
                  const((D_PLE, D_MODEL)), const((1, D_MODEL))]
        + _prm_specs(1)
        + [const((HEADSUM_W, HEADSUM_W)), const((tc, tc)),
           pl.BlockSpec((SUBLANES - 1, n2, n2), lambda s: (0, 0, 0), pipeline_mode=pl.Buffered(1))],
        out_specs=[
            tail_tile(D_MODEL),
            pl.BlockSpec((1, 2, CONV_CH), lambda s: (head(s) // n_t, 0, 0)),
            pl.BlockSpec((1, 1, SHIFT_W), lambda s: (head(s) // n_t, 0, 0)),
            pl.BlockSpec((1, N_HEADS, HEAD_DIM, HEAD_DIM), lambda s: (tail(s) // n_t, 0, 0, 0)),
        ],
        out_shape=[
            jax.ShapeDtypeStruct((bsz * t_len, D_MODEL), F32),
            jax.ShapeDtypeStruct((bsz, 2, CONV_CH), F32),
            jax.ShapeDtypeStruct((bsz, 1, SHIFT_W), F32),
            jax.ShapeDtypeStruct((bsz, N_HEADS, HEAD_DIM, HEAD_DIM), F32),
        ],
        scratch_shapes=[
            pltpu.VMEM((tc, IN_W), F32),
            pltpu.VMEM((tc, D_MODEL), F32),
            pltpu.VMEM((CARRY_ROWS + tc, CONV_CH), F32),
            pltpu.VMEM((CARRY_ROWS + tc, SHIFT_W), F32),
            pltpu.VMEM((N_PAIRS, LANES, LANES), F32),
            big(), big(), big(), big(), big(), big(), big(), big(), big(),
            op(BF16), op(F32), op(BF16), op(F32),
            big(), big(),
        ],
        compiler_params=pltpu.CompilerParams(
            dimension_semantics=("arbitrary",), vmem_limit_bytes=VMEM_LIMIT),
        name="prompt_layer",
    )(x2d, x2d, p2d, w["norm_g"].reshape(1, D_MODEL), w["w_in"], w["w_out"], w["w_pg"], w["w_pp"],
      w["final_g"].reshape(1, D_MODEL), *w["prm"], w["headones"], tri, lane_bcast)


def _sample_front_kernel(x_ref, ng_ref, win_ref, cb_ref, sb_ref, *refs):
    prm_refs = refs[:len(_PRM_NAMES)]
    headones_ref = refs[len(_PRM_NAMES)]
    (ya_ref, conv_out_ref, shift_out_ref, gr_ref, bonus_ref,
     nkk_ref, w_ref, b_ref, km_ref, v_ref, r_ref, wbf_ref, xn_s, z_s) = refs[len(_PRM_NAMES) + 1:]
    j = pl.program_id(0)

    @pl.when(j == 0)
    def _():
        x = x_ref[:, 0, :]
        xn_s[...] = (x * lax.rsqrt(jnp.mean(x * x, axis=-1, keepdims=True) + RMS_EPS) * ng_ref[...]).astype(BF16)

    w_blk = win_ref[...].astype(BF16)
    wbf_ref[...] = w_blk
    z_s[j] = _dot(xn_s[...], w_blk)

    @pl.when(j == pl.num_programs(0) - 1)
    def _():
        _sample_front_tokens(z_s, cb_ref, sb_ref, prm_refs, headones_ref, ya_ref, conv_out_ref, shift_out_ref,
                             gr_ref, bonus_ref, nkk_ref, w_ref, b_ref, km_ref, v_ref, r_ref)


def _sample_front_tokens(z_s, cb_ref, sb_ref, prm_refs, headones_ref, ya_ref, conv_out_ref, shift_out_ref,
                         gr_ref, bonus_ref, nkk_ref, w_ref, b_ref, km_ref, v_ref, r_ref):
    prm = _load_prm(prm_refs)
    headones = headones_ref[...]
    z = jnp.concatenate([z_s[i] for i in range(z_s.shape[0])], axis=1)

    u = z[:, CONV_CH:2 * CONV_CH] * z[:, 2 * CONV_CH:3 * CONV_CH]
    cb0 = cb_ref[:, 0, :]
    cb1 = cb_ref[:, 1, :]
    cw = prm["conv_w"]
    conv = cw[0:1, :] * cb0 + cw[1:2, :] * cb1 + cw[2:3, :] * u
    ya_ref[...] = z[:, 0:CONV_CH] * conv * _silu(z[:, 3 * CONV_CH:4 * CONV_CH])
    conv_out_ref[:, 0, :] = cb1
    conv_out_ref[:, 1, :] = u

    zs = z[:, OFF_ZS:OFF_ZS + SHIFT_W]
    shift_out_ref[...] = zs
    gr_ref[...] = z[:, OFF_GR:OFF_GR + RW]
    zm = zs + (sb_ref[...] - zs) * prm["mu"]
    r = zm[:, 0:RW]
    k = zm[:, RW:2 * RW]
    v = zm[:, 2 * RW:3 * RW]
    wa = zm[:, 3 * RW:3 * RW + 2 * LORA]
    lw, a, kk, kmod, bonus = _rwkv_tokens(r, k, v, wa, prm, headones)
    bonus_ref[...] = bonus
    nkk_ref[...] = (-kk).T
    w_ref[...] = jnp.exp(lw).T
    b_ref[...] = (kk * a).T
    km_ref[...] = kmod.T
    v_ref[...] = v.T
    r_ref[...] = r.T


def _sample_step_kernel(s_ref, nkk_ref, w_ref, b_ref, km_ref, v_ref, r_ref, s1_ref, y_ref):
    nkk = nkk_ref[0]
    w = w_ref[0]
    b = b_ref[0]
    km = km_ref[0]
    r = r_ref[0]
    for vi in range(HEAD_DIM):
        s0 = s_ref[0, vi]
        sa = jnp.sum(s0 * nkk, axis=0, keepdims=True)
        s1 = s0 * w + sa * b + v_ref[0, vi:vi + 1, :] * km
        s1_ref[0, vi] = s1
        y_ref[0, vi:vi + 1, :] = jnp.sum(s1 * r, axis=0, keepdims=True)


def _sample_back_kernel(o_ref, bonus_ref, gr_ref, ya_ref, x_ref, p_ref, wo_ref, wg_ref, wp_ref, fg_ref, *refs):
    prm_refs = refs[:len(_PRM_NAMES)]
    headones_ref = refs[len(_PRM_NAMES)]
    y_ref, wo_bf_ref, wg_bf_ref, wp_bf_ref = refs[len(_PRM_NAMES) + 1:]
    prm = _load_prm(prm_refs)
    wo = wo_ref[...].astype(BF16)
    wg = wg_ref[...].astype(BF16)
    wp = wp_ref[...].astype(BF16)
    wo_bf_ref[...] = wo
    wg_bf_ref[...] = wg
    wp_bf_ref[...] = wp
    y_r = _rwkv_post(o_ref[...].T, bonus_ref[...], gr_ref[...], prm, headones_ref[...])
    ycat = jnp.concatenate([ya_ref[...], y_r], axis=1).astype(BF16)
    h1 = x_ref[:, 0, :] + _dot(ycat, wo)
    gate = _sigmoid(_dot(h1.astype(BF16), wg))
    pe = _dot(p_ref[:, 0, :].astype(BF16), wp)
    h2 = h1 + gate * pe
    ms = jnp.mean(h2 * h2, axis=-1, keepdims=True)
    y_ref[:, 0, :] = h2 * lax.rsqrt(ms + RMS_EPS) * fg_ref[...]


def _full(shape):
    nd = len(shape)
    return pl.BlockSpec(shape, lambda *_: (0,) * nd)


def _sample_layer(x3d, p3d, state_conv, state_shift, state_hvkb, w):
    n = x3d.shape[0]
    params = pltpu.CompilerParams(dimension_semantics=("arbitrary",), vmem_limit_bytes=VMEM_LIMIT)
    row = jax.ShapeDtypeStruct((n, RW), F32)
    col = jax.ShapeDtypeStruct((RW, n), F32)
    wblk = IN_W // W_IN_BLOCKS
    front = pl.pallas_call(
        _sample_front_kernel,
        grid=(W_IN_BLOCKS,),
        in_specs=[_full((n, 1, D_MODEL)), _full((1, D_MODEL)), pl.BlockSpec((D_MODEL, wblk), lambda j: (0, j)),
                  _full((n, 2, CONV_CH)), _full((n, SHIFT_W))]
        + _prm_specs(1) + [_full((HEADSUM_W, HEADSUM_W))],
        out_specs=[_full((n, CONV_CH)), _full((n, 2, CONV_CH)), _full((n, SHIFT_W)),
                   _full((n, RW)), _full((n, RW))] + [_full((RW, n))] * 6
        + [pl.BlockSpec((D_MODEL, wblk), lambda j: (0, j))],
        out_shape=[jax.ShapeDtypeStruct((n, CONV_CH), F32), jax.ShapeDtypeStruct((n, 2, CONV_CH), F32),
                   jax.ShapeDtypeStruct((n, SHIFT_W), F32), row, row] + [col] * 6
        + [jax.ShapeDtypeStruct((D_MODEL, IN_W), BF16)],
        scratch_shapes=[pltpu.VMEM((n, D_MODEL), BF16), pltpu.VMEM((W_IN_BLOCKS, n, wblk), F32)],
        compiler_params=params,
        name="sample_front",
    )(x3d, w["norm_g"].reshape(1, D_MODEL), w["w_in_f32"], state_conv, state_shift, *w["prm"], w["headones"])
    ya, conv_new, shift_new, g_r, bonus, nkk, dec, b, km, v, r, w_in_bf16 = front

    hk = lambda a: a.reshape(N_HEADS, HEAD_DIM, n)
    vspec = pl.BlockSpec((1, HEAD_DIM, n), lambda h: (h, 0, 0))
    sspec = pl.BlockSpec((1, HEAD_DIM, HEAD_DIM, n), lambda h: (h, 0, 0, 0))
    s1, y = pl.pallas_call(
        _sample_step_kernel,
        grid=(N_HEADS,),
        in_specs=[sspec] + [vspec] * 6,
        out_specs=[sspec, vspec],
        out_shape=[jax.ShapeDtypeStruct((N_HEADS, HEAD_DIM, HEAD_DIM, n), F32),
                   jax.ShapeDtypeStruct((N_HEADS, HEAD_DIM, n), F32)],
        compiler_params=params,
        name="sample_step",
    )(state_hvkb, hk(nkk), hk(dec), hk(b), hk(km), hk(v), hk(r))

    sq, pp = (D_MODEL, D_MODEL), (D_PLE, D_MODEL)
    y_out, wo_bf, wg_bf, wp_bf = pl.pallas_call(
        _sample_back_kernel,
        grid=(1,),
        in_specs=[_full((RW, n)), _full((n, RW)), _full((n, RW)), _full((n, CONV_CH)),
                  _full((n, 1, D_MODEL)), _full((n, 1, D_PLE)), _full(sq), _full(sq), _full(pp),
                  _full((1, D_MODEL))]
        + _prm_specs(1) + [_full((HEADSUM_W, HEADSUM_W))],
        out_specs=[_full((n, 1, D_MODEL)), _full(sq), _full(sq), _full(pp)],
        out_shape=[jax.ShapeDtypeStruct((n, 1, D_MODEL), F32), jax.ShapeDtypeStruct(sq, BF16),
                   jax.ShapeDtypeStruct(sq, BF16), jax.ShapeDtypeStruct(pp, BF16)],
        compiler_params=params,
        name="sample_back",
    )(y.reshape(RW, n), bonus, g_r, ya, x3d, p3d, w["w_out_f32"], w["w_pg_f32"], w["w_pp_f32"],
      w["final_g"].reshape(1, D_MODEL), *w["prm"], w["headones"])
    rounded = {"w_in": w_in_bf16, "w_out": wo_bf, "w_pg": wg_bf, "w_pp": wp_bf}
    return y_out, conv_new, shift_new, s1, rounded


def _layer_prompt(x, p, w, tc=256):
    bsz, t_len, _ = x.shape
    x2d = x.reshape(bsz * t_len, D_MODEL)
    y, conv_new, shift_new, wkv_new = _prompt_layer(x2d, p.reshape(bsz * t_len, D_PLE), w, bsz, t_len, tc)
    return y.reshape(bsz, t_len, D_MODEL), conv_new, shift_new.reshape(bsz, SHIFT_W), wkv_new


def _layer_sample(x, p, state_conv, state_shift, state_wkv, w):
    n = x.shape[0]
    s_hvkb = jnp.transpose(state_wkv, (1, 2, 3, 0))
    y, conv_new, shift_new, s1, rounded = _sample_layer(x, p, state_conv, state_shift, s_hvkb, w)
    return y, conv_new, shift_new, jnp.transpose(s1, (3, 0, 1, 2)), rounded


def kernel(x_prompt, x_sample, p_prompt, p_sample, state_conv, state_shift, state_wkv, norm_g, w_in, conv_w, mu_shift, w0, w_up, a0, a_up, k_k, k_a, r_k, ln_w, ln_b, w_out, w_pg, w_pp, final_g):
    depth = norm_g.shape[0]
    assert depth == 1
    i = 0
    head_id = np.arange(HEADSUM_W) // HEAD_DIM
    w = {
        "norm_g": norm_g[i],
        "w_in_f32": w_in[i],
        "w_out_f32": w_out[i],
        "w_pg_f32": w_pg[i],
        "w_pp_f32": w_pp[i],
        "final_g": final_g,
        "prm": _prm_arrays(conv_w[i], mu_shift[i], w0[i], w_up[i], a0[i], a_up[i], k_k[i], k_a[i],
                           r_k[i].reshape(RW), ln_w[i], ln_b[i]),
        "headones": jnp.asarray(head_id[:, None] == head_id[None, :], BF16),
    }
    ys, cs, ss, ws, rounded = _layer_sample(x_sample, p_sample[i], state_conv[i], state_shift[i],
                                            state_wkv[i], w)
    yp, cp, sp, wp = _layer_prompt(x_prompt, p_prompt[i], {**w, **rounded})
    return (yp, ys, cp[None], sp[None], wp[None], cs[None], ss[None], ws[None])
```

```python
import functools
import math

import jax
import jax.numpy as jnp
import numpy as np
from jax import lax
from jax.experimental import pallas as pl
from jax.experimental.pallas import tpu as pltpu

F32 = jnp.float32
BF16 = jnp.bfloat16

D_MODEL = 1024
CONV_CH = 512
RW = 512
HEAD_DIM = 64
N_HEADS = 8
N_PAIRS = N_HEADS // 2
LORA = 64
D_PLE = 256
SHIFT_W = 3 * RW + 2 * LORA
IN_W = 4 * CONV_CH + SHIFT_W + RW
OFF_ZS = 4 * CONV_CH
OFF_GR = OFF_ZS + SHIFT_W
RMS_EPS = 1e-6
GN_EPS = 64e-5
DECAY_SCALE = math.exp(-0.5)

LANES = 128
SUBLANES = 8
MXU_DIM = 256
HEADSUM_W = MXU_DIM
SUM_PARTS = 1
CUM_PARTS = 1
LORA_PASSES = 1
CHUNK = 64
W_IN_BLOCKS = 3
CARRY_ROWS = 8
VMEM_LIMIT = 56 * 1024 * 1024


def _dot(a, b):
    return jnp.dot(a, b, preferred_element_type=F32)


def _dot_nt(a, b):
    return lax.dot_general(a, b, (((1,), (1,)), ((), ())), preferred_element_type=F32)


def _dot_tn(a, b):
    return lax.dot_general(a, b, (((0,), (0,)), ((), ())), preferred_element_type=F32)


def _split(x, parts):
    out = []
    rem = x
    for i in range(parts):
        t = rem.astype(BF16)
        out.append(t)
        if i + 1 < parts:
            rem = rem - t.astype(F32)
    return out


def _const_dot(c_bf16, x, parts):
    acc = None
    for t in _split(x, parts):
        d = _dot(c_bf16, t)
        acc = d if acc is None else acc + d
    return acc


def _x_dot_const(x, c_bf16, parts):
    acc = None
    for t in _split(x, parts):
        d = _dot(t, c_bf16)
        acc = d if acc is None else acc + d
    return acc


def _head_sum(x, headones):
    w = headones.shape[0]
    halves = [_x_dot_const(x[:, i * w:(i + 1) * w], headones, SUM_PARTS) for i in range(x.shape[1] // w)]
    return jnp.concatenate(halves, axis=1)


def _mm(a, b, passes):
    if passes == 1:
        return _dot(a.astype(BF16), b.astype(BF16))
    ah, al = _split(a, 2)
    bh, bl = _split(b, 2)
    return _dot(ah, bh) + _dot(al, bh) + _dot(ah, bl)


def _sigmoid(x):
    return 1.0 / (1.0 + jnp.exp(-x))


def _silu(x):
    return x * _sigmoid(x)


def _rwkv_tokens(r, k, v, wa, prm, headones, between_steps=lambda: None):
    n = r.shape[0]
    lane = lax.broadcasted_iota(jnp.int32, (n, LANES), 1)
    th = jnp.where(lane < LORA, jnp.tanh(wa), wa)
    lora = _mm(th, prm["wlora"], LORA_PASSES)
    lw = -DECAY_SCALE * _sigmoid(prm["w0"] + lora[:, :RW])
    a = _sigmoid(prm["a0"] + lora[:, RW:])
    between_steps()
    kk = k * prm["k_k"]
    ss = _head_sum(kk * kk, headones)
    kk = kk / jnp.maximum(jnp.sqrt(ss), 1e-12)
    between_steps()
    kmod = k * (1.0 + (a - 1.0) * prm["k_a"])
    bonus = _head_sum(r * kmod * prm["r_k"], headones) * v
    between_steps()
    return lw, a, kk, kmod, bonus


def _rwkv_post(o, bonus, g_r, prm, headones):
    mean = _head_sum(o, headones) * (1.0 / HEAD_DIM)
    d = o - mean
    var = _head_sum(d * d, headones) * (1.0 / HEAD_DIM)
    on = d * lax.rsqrt(var + GN_EPS)
    on = on * prm["ln_w"] + prm["ln_b"]
    return (on + bonus) * _silu(g_r)


_PRM_NAMES = ("conv_w", "mu", "w0", "a0", "w_up", "a_up", "k_k", "k_a", "r_k", "ln_w", "ln_b")


def _prm_arrays(conv_w, mu_shift, w0, w_up, a0, a_up, k_k, k_a, r_k, ln_w, ln_b):
    row = lambda x: x.reshape(1, -1)
    return (conv_w, row(mu_shift), row(w0), row(a0), w_up, a_up, row(k_k), row(k_a),
            row(r_k), row(ln_w), row(ln_b))


def _prm_specs(grid_rank):
    zero = (lambda *_: (0, 0))
    shapes = ((3, CONV_CH), (1, SHIFT_W), (1, RW), (1, RW), (LORA, RW), (LORA, RW),
              (1, RW), (1, RW), (1, RW), (1, RW), (1, RW))
    return [pl.BlockSpec(s, zero) for s in shapes]


def _load_prm(refs):
    prm = {n: r[...] for n, r in zip(_PRM_NAMES, refs)}
    zero = jnp.zeros((LORA, RW), F32)
    prm["wlora"] = jnp.concatenate([jnp.concatenate([prm["w_up"], zero], axis=1),
                                    jnp.concatenate([zero, prm["a_up"]], axis=1)], axis=0)
    return prm


def _stack_heads(x):
    lane = lax.broadcasted_iota(jnp.int32, x.shape, 1)
    lo = jnp.where(lane < HEAD_DIM, x, 0.0)
    hi = jnp.where(lane >= HEAD_DIM, x, 0.0)
    return jnp.concatenate([lo, hi], axis=0)


def _stack_heads_bf16(x):
    lane = lax.broadcasted_iota(jnp.int32, x.shape, 1)
    lo = jnp.where(lane < HEAD_DIM, 1.0, 0.0).astype(BF16)
    xb = x.astype(BF16)
    return jnp.concatenate([xb * lo, xb * (1.0 - lo)], axis=0)


def _bdot(a, b):
    return _dot(a.astype(BF16), b.astype(BF16))


def _diag_block_inverses_minus_eye(mats, lane_bcast):
    n = mats[0].shape[0]
    nb = n // SUBLANES
    row = lax.broadcasted_iota(jnp.int32, (n, n), 0)
    col = lax.broadcasted_iota(jnp.int32, (n, n), 1)
    in_diag = ((row // SUBLANES) == (col // SUBLANES)) & (row > col)
    packed = []
    for a in mats:
        d = jnp.where(in_diag, a, 0.0)
        acc = d[0:SUBLANES]
        for m in range(1, nb):
            acc = acc + d[m * SUBLANES:(m + 1) * SUBLANES]
        packed.append(acc)
    d_all = jnp.concatenate(packed, axis=0).astype(BF16)
    sub = lax.broadcasted_iota(jnp.int32, (SUBLANES, n), 0)
    lane = lax.broadcasted_iota(jnp.int32, (SUBLANES, n), 1)
    eye8 = jnp.where(sub == lane % SUBLANES, 1.0, 0.0)
    ts = [eye8 for _ in mats]
    for j in range(SUBLANES - 1):
        spread = _dot(d_all, lane_bcast[j])
        ts = [t + spread[i * SUBLANES:(i + 1) * SUBLANES] * jnp.broadcast_to(t[j:j + 1, :], (SUBLANES, n))
              for i, t in enumerate(ts)]
    blk = lax.broadcasted_iota(jnp.int32, (SUBLANES, n), 1) // SUBLANES
    out = []
    for t in ts:
        tm1 = t - eye8
        out.append(jnp.concatenate([jnp.where(blk == m, tm1, 0.0) for m in range(nb)], axis=0))
    return out


def _unit_lower_inverse_minus_eye(mats, lane_bcast, between_levels=lambda: None):
    n = mats[0].shape[0]
    row = lax.broadcasted_iota(jnp.int32, (n, n), 0)
    col = lax.broadcasted_iota(jnp.int32, (n, n), 1)
    xs = _diag_block_inverses_minus_eye(mats, lane_bcast)
    size = 2 * SUBLANES
    while size <= CHUNK:
        half = size // 2
        sel = ((row // size) == (col // size)) & ((row % size) >= half) & ((col % size) < half)
        als = [jnp.where(sel, a, 0.0) for a in mats]
        ps = [al + _bdot(x, al) for x, al in zip(xs, als)]
        xs = [x + p + _bdot(p, x) for x, p in zip(xs, ps)]
        between_levels()
        size *= 2
    return xs


def _chunk_operators(ld, strict, incl, lane_bcast, side_work=()):
    side_work = list(side_work)

    def breathe():
        if side_work:
            side_work.pop(0)()

    cat0 = lambda xs: jnp.concatenate(xs, axis=0)
    la = [_stack_heads(x) for x in ld("la")]
    lr = [_stack_heads(x) for x in ld("lr")]
    n2 = 2 * CHUNK
    amat = [_dot_nt(cat0([a, b]).astype(BF16), cat0([c, d]).astype(BF16))
            for a, b, c, d in zip(la, lr, ld("rb"), ld("rk"))]

    def block_diag_pair(m):
        swapped = pltpu.roll(m, CHUNK, axis=1)
        first = cat0([m[0:CHUNK], swapped[CHUNK:n2]])
        second = cat0([swapped[0:CHUNK], m[CHUNK:n2]])
        return first, second

    breathe()
    top = [block_diag_pair(m[0:n2]) for m in amat]
    a_ab = [jnp.where(strict, t[0], 0.0) for t in top]
    xs = _unit_lower_inverse_minus_eye(a_ab, lane_bcast, breathe)
    vst = [_stack_heads_bf16(x) for x in ld("v")]
    av = [_dot(jnp.where(strict, t[1], 0.0).astype(BF16), v) for t, v in zip(top, vst)]
    breathe()
    uv0 = [y + _bdot(x, y) for x, y in zip(xs, av)]
    tla = [y + _bdot(x, y) for x, y in zip(xs, la)]
    breathe()
    bottom = [block_diag_pair(m[n2:2 * n2]) for m in amat]
    a_rb = [jnp.where(incl, t[0], 0.0) for t in bottom]
    a_rk = [jnp.where(incl, t[1], 0.0) for t in bottom]
    gop = [y + _bdot(a, t) for y, a, t in zip(lr, a_rb, tla)]
    uv = [cat0([u.astype(BF16), v]) for u, v in zip(uv0, vst)]
    y0 = [_dot(jnp.concatenate([a, b], axis=1).astype(BF16), w) for a, b, w in zip(a_rb, a_rk, uv)]
    breathe()
    bt = [_stack_heads_bf16(x) for x in ld("bt")]
    kt = [_stack_heads_bf16(x) for x in ld("kt")]
    pop = [_dot_tn(t.astype(BF16), b) for t, b in zip(tla, bt)]
    qop = [_dot_tn(w, cat0([b, k])) for w, b, k in zip(uv, bt, kt)]
    while side_work:
        breathe()
    return gop, y0, pop, qop


def _prompt_kernel(xh_ref, xt_ref, pt_ref, ng_ref, win_ref, wo_ref, wg_ref, wp_ref, fg_ref, *refs,
                   tc, n_t, n_tiles):
    prm_refs = refs[:len(_PRM_NAMES)]
    headones_ref, tri_ref, bcast_ref, st_ref, vecs_ref = refs[len(_PRM_NAMES):len(_PRM_NAMES) + 5]
    (y_ref, conv_out_ref, shift_out_ref, wkv_out_ref, st1_ref, ys_ref) = refs[len(_PRM_NAMES) + 5:len(_PRM_NAMES) + 11]
    (z_s, ycat_s, ubuf, zsbuf, s_ref, la_s, lr_s, rb_s, rk_s, bt_s, kt_s, v_s, gam_s, o_s,
     g_sc, y0_sc, p_sc, q_sc, bonus_p, gr_p) = refs[len(_PRM_NAMES) + 11:]

    s = pl.program_id(0)
    head_t = lax.rem(s, jnp.int32(n_t))
    tail_t = lax.rem(s + (n_t - 1), jnp.int32(n_t))
    prm = _load_prm(prm_refs)
    headones = headones_ref[...]
    n2 = 2 * CHUNK
    n_chunks = tc // CHUNK

    @pl.when(s == 0)
    def _():
        for ref in (g_sc, y0_sc, p_sc, q_sc, gam_s, bonus_p, gr_p, ycat_s, s_ref):
            ref[...] = jnp.zeros(ref.shape, ref.dtype)

    @pl.when(head_t == 0)
    def _():
        ubuf[0:CARRY_ROWS, :] = jnp.zeros((CARRY_ROWS, CONV_CH), F32)
        zsbuf[0:CARRY_ROWS, :] = jnp.zeros((CARRY_ROWS, SHIFT_W), F32)

    @pl.when(tail_t == 0)
    def _():
        s_ref[...] = jnp.zeros(s_ref.shape, F32)

    tail = {}

    def tail_state_pass(c):
        def run():
            for j in range(N_PAIRS):
                idx = c * N_PAIRS + j
                lanes = slice(j * LANES, (j + 1) * LANES)
                s_old = s_ref[j]
                s_bf = s_old.astype(BF16)
                yst = _dot_nt(g_sc[idx], s_bf) + y0_sc[idx]
                o_s[c * CHUNK:(c + 1) * CHUNK, lanes] = yst[0:CHUNK] + yst[CHUNK:n2]
                gam = gam_s[c * CHUNK:c * CHUNK + 1, lanes]
                s_ref[j] = s_old * gam + _dot(s_bf, p_sc[idx]) + q_sc[idx]
        return run

    def tail_post():
        ycat_s[:, CONV_CH:2 * CONV_CH] = _rwkv_post(o_s[...], bonus_p[...], gr_p[...], prm, headones)

    def tail_out_proj():
        tail["h1"] = xt_ref[...] + _dot(ycat_s[...].astype(BF16), wo_ref[...])

    def tail_gate():
        tail["gate"] = _sigmoid(_dot(tail["h1"].astype(BF16), wg_ref[...]))
        tail["pe"] = _dot(pt_ref[...].astype(BF16), wp_ref[...])

    def tail_norm():
        h2 = tail["h1"] + tail["gate"] * tail["pe"]
        ms = jnp.mean(h2 * h2, axis=-1, keepdims=True)
        y_ref[...] = h2 * lax.rsqrt(ms + RMS_EPS) * fg_ref[...]

    x = xh_ref[...]
    xn = (x * lax.rsqrt(jnp.mean(x * x, axis=-1, keepdims=True) + RMS_EPS) * ng_ref[...]).astype(BF16)

    def project(c0):
        def run():
            z_s[:, c0:c0 + CONV_CH] = _dot(xn, win_ref[:, c0:c0 + CONV_CH])
        return run

    passes = [tail_state_pass(c) for c in range(n_chunks)]
    blocks = [project(c0) for c0 in (OFF_GR, 0, CONV_CH, 2 * CONV_CH, 3 * CONV_CH)]
    side_queue = []
    late_blocks, blocks = blocks[-1:], blocks[:-1]
    while passes or blocks:
        side_queue += passes[:1] + blocks[:1]
        passes, blocks = passes[1:], blocks[1:]
    side_queue += [tail_post, tail_out_proj, tail_gate, tail_norm]

    blocks_per_head = HEAD_DIM // STATE_ROWS
    v_row0 = lax.rem(jnp.minimum(s, n_tiles - 1), jnp.int32(blocks_per_head)) * STATE_ROWS
    side_queue.append(lambda: _sample_state_rows(st_ref, vecs_ref, st1_ref, ys_ref, v_row0))

    def project_next(count=2):
        for _ in range(count):
            if side_queue:
                side_queue.pop(0)()

    z_s[:, OFF_ZS:OFF_ZS + SHIFT_W] = _dot(xn, win_ref[:, OFF_ZS:OFF_ZS + SHIFT_W])

    zs = z_s[:, OFF_ZS:OFF_ZS + SHIFT_W]
    zsbuf[CARRY_ROWS:CARRY_ROWS + tc, :] = zs
    zprev = zsbuf[CARRY_ROWS - 1:CARRY_ROWS - 1 + tc, :]
    zm = zs + (zprev - zs) * prm["mu"]
    zsbuf[CARRY_ROWS - 1:CARRY_ROWS, :] = zsbuf[CARRY_ROWS + tc - 1:CARRY_ROWS + tc, :]

    project_next()
    r = zm[:, 0:RW]
    k = zm[:, RW:2 * RW]
    v = zm[:, 2 * RW:3 * RW]
    wa = zm[:, 3 * RW:3 * RW + 2 * LORA]
    lw, a, kk, kmod, bonus = _rwkv_tokens(r, k, v, wa, prm, headones, project_next)

    g = _const_dot(tri_ref[...], lw, CUM_PARTS)
    gc = jnp.concatenate(
        [jnp.broadcast_to(g[(c + 1) * CHUNK - 1:(c + 1) * CHUNK, :], (CHUNK, RW)) for c in range(tc // CHUNK)],
        axis=0)
    project_next()
    eng = jnp.exp(-g)
    etail = jnp.exp(gc - g)
    b = kk * a
    la_s[...] = -kk * jnp.exp(g - lw)
    lr_s[...] = r * jnp.exp(g)
    project_next()
    rb_s[...] = b * eng
    rk_s[...] = kmod * eng
    bt_s[...] = b * etail
    kt_s[...] = kmod * etail
    v_s[...] = v
    while side_queue:
        project_next()
    gam_s[...] = jnp.exp(gc)

    row = lax.broadcasted_iota(jnp.int32, (n2, n2), 0)
    col = lax.broadcasted_iota(jnp.int32, (n2, n2), 1)
    same = (row // CHUNK) == (col // CHUNK)
    strict = same & (row > col)
    incl = same & (row >= col)

    cw = prm["conv_w"]

    def conv_rows(r0):
        def run():
            rows = slice(r0, r0 + CHUNK)
            u = z_s[rows, CONV_CH:2 * CONV_CH] * z_s[rows, 2 * CONV_CH:3 * CONV_CH]
            ubuf[CARRY_ROWS + r0:CARRY_ROWS + r0 + CHUNK, :] = u
            um1 = ubuf[CARRY_ROWS - 1 + r0:CARRY_ROWS - 1 + r0 + CHUNK, :]
            um2 = ubuf[CARRY_ROWS - 2 + r0:CARRY_ROWS - 2 + r0 + CHUNK, :]
            conv = cw[0:1, :] * um2 + cw[1:2, :] * um1 + cw[2:3, :] * u
            ycat_s[rows, 0:CONV_CH] = z_s[rows, 0:CONV_CH] * conv * _silu(z_s[rows, 3 * CONV_CH:4 * CONV_CH])
        return run

    srcs = {"la": la_s, "lr": lr_s, "rb": rb_s, "rk": rk_s, "bt": bt_s, "kt": kt_s, "v": v_s}
    members = [(c, j) for c in range(n_chunks) for j in range(N_PAIRS)]

    def ld(name):
        ref = srcs[name]
        return [ref[c * CHUNK:(c + 1) * CHUNK, j * LANES:(j + 1) * LANES] for c, j in members]

    lane_bcast = [bcast_ref[j] for j in range(SUBLANES - 1)]
    gop, y0, pop, qop = _chunk_operators(ld, strict, incl, lane_bcast,
                                         late_blocks + [conv_rows(c * CHUNK) for c in range(n_chunks)])
    for (c, j), g_, y_, p_, q_ in zip(members, gop, y0, pop, qop):
        idx = c * N_PAIRS + j
        g_sc[idx] = g_.astype(BF16)
        y0_sc[idx] = y_
        p_sc[idx] = p_.astype(BF16)
        q_sc[idx] = q_

    ubuf[CARRY_ROWS - 2:CARRY_ROWS, :] = ubuf[CARRY_ROWS + tc - 2:CARRY_ROWS + tc, :]

    bonus_p[...] = bonus
    gr_p[...] = z_s[:, OFF_GR:OFF_GR + RW]

    @pl.when((head_t == n_t - 1) & (s < n_tiles))
    def _():
        shift_out_ref[0] = zsbuf[CARRY_ROWS - 1:CARRY_ROWS, :]
        conv_out_ref[0] = ubuf[CARRY_ROWS - 2:CARRY_ROWS, :]

    @pl.when((tail_t == n_t - 1) & (s > 0))
    def _():
        for h in range(N_HEADS):
            j, i = divmod(h, 2)
            blk = s_ref[j]
            wkv_out_ref[0, h] = blk[i * HEAD_DIM:(i + 1) * HEAD_DIM, i * HEAD_DIM:(i + 1) * HEAD_DIM]


def _prompt_layer(x2d, p2d, w, bsz, t_len, tc, state_hvkb, step_vecs):
    n_t = t_len // tc
    n_seq = state_hvkb.shape[-1]
    blocks_per_head = HEAD_DIM // STATE_ROWS
    assert bsz * n_t == N_HEADS * blocks_per_head, "one state block per prompt tile"
    state_blk = lambda s: jnp.minimum(s, bsz * n_t - 1)
    tok = np.arange(tc)
    blockones = (tok[:, None] // CHUNK) == (tok[None, :] // CHUNK)
    tri = jnp.asarray(blockones & (tok[:, None] >= tok[None, :]), BF16)
    n_tiles = bsz * n_t
    n2 = 2 * CHUNK
    idx = np.arange(n2)
    same_blk = (idx[:, None] // SUBLANES) == (idx[None, :] // SUBLANES)
    lane_bcast = jnp.asarray(
        np.stack([same_blk & (idx[:, None] % SUBLANES == j) for j in range(SUBLANES - 1)]), BF16)
    kern = functools.partial(_prompt_kernel, tc=tc, n_t=n_t, n_tiles=n_tiles)
    big = lambda: pltpu.VMEM((tc, RW), F32)
    n_blk = (tc // CHUNK) * N_PAIRS
    op = lambda dt: pltpu.VMEM((n_blk, LANES, LANES), dt)
    head = lambda s: jnp.minimum(s, n_tiles - 1)
    tail = lambda s: jnp.maximum(s - 1, 0)
    head_tile = lambda width: pl.BlockSpec((tc, width), lambda s: (head(s), 0))
    tail_tile = lambda width: pl.BlockSpec((tc, width), lambda s: (tail(s), 0))
    const = lambda shape: pl.BlockSpec(shape, lambda s: (0, 0), pipeline_mode=pl.Buffered(1))
    return pl.pallas_call(
        kern,
        grid=(n_tiles + 1,),
        in_specs=[head_tile(D_MODEL), tail_tile(D_MODEL), tail_tile(D_PLE), const((1, D_MODEL)),
                  const((D_MODEL, IN_W)), const((D_MODEL, D_MODEL)), const((D_MODEL, D_MODEL)),
                  const((D_PLE, D_MODEL)), const((1, D_MODEL))]
        + _prm_specs(1)
        + [const((HEADSUM_W, HEADSUM_W)), const((tc, tc)),
           pl.BlockSpec((SUBLANES - 1, n2, n2), lambda s: (0, 0, 0), pipeline_mode=pl.Buffered(1)),
           pl.BlockSpec((1, STATE_ROWS, HEAD_DIM, n_seq),
                        lambda s: (state_blk(s) // blocks_per_head, state_blk(s) % blocks_per_head, 0, 0)),
           pl.BlockSpec((len(_STEP_VECS), 1, HEAD_DIM, n_seq),
                        lambda s: (0, state_blk(s) // blocks_per_head, 0, 0))],
        out_specs=[
            tail_tile(D_MODEL),
            pl.BlockSpec((1, 2, CONV_CH), lambda s: (head(s) // n_t, 0, 0)),
            pl.BlockSpec((1, 1, SHIFT_W), lambda s: (head(s) // n_t, 0, 0)),
            pl.BlockSpec((1, N_HEADS, HEAD_DIM, HEAD_DIM), lambda s: (tail(s) // n_t, 0, 0, 0)),
            pl.BlockSpec((1, STATE_ROWS, HEAD_DIM, n_seq),
                         lambda s: (state_blk(s) // blocks_per_head, state_blk(s) % blocks_per_head, 0, 0)),
            pl.BlockSpec((1, STATE_ROWS, n_seq),
                         lambda s: (state_blk(s) // blocks_per_head, state_blk(s) % blocks_per_head, 0)),
        ],
        out_shape=[
            jax.ShapeDtypeStruct((bsz * t_len, D_MODEL), F32),
            jax.ShapeDtypeStruct((bsz, 2, CONV_CH), F32),
            jax.ShapeDtypeStruct((bsz, 1, SHIFT_W), F32),
            jax.ShapeDtypeStruct((bsz, N_HEADS, HEAD_DIM, HEAD_DIM), F32),
            jax.ShapeDtypeStruct(state_hvkb.shape, F32),
            jax.ShapeDtypeStruct((N_HEADS, HEAD_DIM, n_seq), F32),
        ],
        scratch_shapes=[
            pltpu.VMEM((tc, IN_W), F32),
            pltpu.VMEM((tc, D_MODEL), F32),
            pltpu.VMEM((CARRY_ROWS + tc, CONV_CH), F32),
            pltpu.VMEM((CARRY_ROWS + tc, SHIFT_W), F32),
            pltpu.VMEM((N_PAIRS, LANES, LANES), F32),
            big(), big(), big(), big(), big(), big(), big(), big(), big(),
            op(BF16), op(F32), op(BF16), op(F32),
            big(), big(),
        ],
        compiler_params=pltpu.CompilerParams(
            dimension_semantics=("arbitrary",), vmem_limit_bytes=VMEM_LIMIT),
        name="prompt_layer",
    )(x2d, x2d, p2d, w["norm_g"].reshape(1, D_MODEL), w["w_in"], w["w_out"], w["w_pg"], w["w_pp"],
      w["final_g"].reshape(1, D_MODEL), *w["prm"], w["headones"], tri, lane_bcast, state_hvkb, step_vecs)


def _sample_front_kernel(x_ref, ng_ref, win_ref, wo_ref, wg_ref, wp_ref, cb_ref, sb_ref, *refs):
    prm_refs = refs[:len(_PRM_NAMES)]
    headones_ref = refs[len(_PRM_NAMES)]
    (ya_ref, conv_out_ref, shift_out_ref, gr_ref, bonus_ref, vecs_ref,
     wbf_ref, wo_bf_ref, wg_bf_ref, wp_bf_ref, xn_s, z_s) = refs[len(_PRM_NAMES) + 1:]
    j = pl.program_id(0)

    @pl.when(j == 0)
    def _():
        x = x_ref[:, 0, :]
        xn_s[...] = (x * lax.rsqrt(jnp.mean(x * x, axis=-1, keepdims=True) + RMS_EPS) * ng_ref[...]).astype(BF16)
        wo_bf_ref[...] = wo_ref[...].astype(BF16)
        wg_bf_ref[...] = wg_ref[...].astype(BF16)
        wp_bf_ref[...] = wp_ref[...].astype(BF16)

    w_blk = win_ref[...].astype(BF16)
    wbf_ref[...] = w_blk
    z_s[j] = _dot(xn_s[...], w_blk)

    @pl.when(j == pl.num_programs(0) - 1)
    def _():
        _sample_front_tokens(z_s, cb_ref, sb_ref, prm_refs, headones_ref, ya_ref, conv_out_ref, shift_out_ref,
                             gr_ref, bonus_ref, vecs_ref)


def _sample_front_tokens(z_s, cb_ref, sb_ref, prm_refs, headones_ref, ya_ref, conv_out_ref, shift_out_ref,
                         gr_ref, bonus_ref, vecs_ref):
    prm = _load_prm(prm_refs)
    headones = headones_ref[...]
    z = jnp.concatenate([z_s[i] for i in range(z_s.shape[0])], axis=1)

    u = z[:, CONV_CH:2 * CONV_CH] * z[:, 2 * CONV_CH:3 * CONV_CH]
    cb0 = cb_ref[:, 0, :]
    cb1 = cb_ref[:, 1, :]
    cw = prm["conv_w"]
    conv = cw[0:1, :] * cb0 + cw[1:2, :] * cb1 + cw[2:3, :] * u
    ya_ref[...] = z[:, 0:CONV_CH] * conv * _silu(z[:, 3 * CONV_CH:4 * CONV_CH])
    conv_out_ref[:, 0, :] = cb1
    conv_out_ref[:, 1, :] = u

    zs = z[:, OFF_ZS:OFF_ZS + SHIFT_W]
    shift_out_ref[...] = zs
    gr_ref[...] = z[:, OFF_GR:OFF_GR + RW]
    zm = zs + (sb_ref[...] - zs) * prm["mu"]
    r = zm[:, 0:RW]
    k = zm[:, RW:2 * RW]
    v = zm[:, 2 * RW:3 * RW]
    wa = zm[:, 3 * RW:3 * RW + 2 * LORA]
    lw, a, kk, kmod, bonus = _rwkv_tokens(r, k, v, wa, prm, headones)
    bonus_ref[...] = bonus
    for i, vec in enumerate((-kk, jnp.exp(lw), kk * a, kmod, v, r)):
        vecs_ref[i] = vec.T


_STEP_VECS = ("nkk", "decay", "b", "kmod", "v", "r")
STATE_ROWS = 8


def _sample_state_rows(st_ref, vecs_ref, s1_ref, y_ref, v_row0):
    nkk, w, b, km = (vecs_ref[i, 0] for i in range(4))
    r = vecs_ref[5, 0]
    for vi in range(STATE_ROWS):
        s0 = st_ref[0, vi]
        sa = jnp.sum(s0 * nkk, axis=0, keepdims=True)
        v_row = vecs_ref[4, 0, pl.ds(v_row0 + vi, 1), :]
        s1 = s0 * w + sa * b + v_row * km
        s1_ref[0, vi] = s1
        y_ref[0, vi:vi + 1, :] = jnp.sum(s1 * r, axis=0, keepdims=True)


def _sample_back_kernel(o_ref, bonus_ref, gr_ref, ya_ref, x_ref, p_ref, wo_ref, wg_ref, wp_ref, fg_ref, *refs):
    prm_refs = refs[:len(_PRM_NAMES)]
    headones_ref = refs[len(_PRM_NAMES)]
    y_ref = refs[len(_PRM_NAMES) + 1]
    prm = _load_prm(prm_refs)
    wo = wo_ref[...]
    wg = wg_ref[...]
    wp = wp_ref[...]
    y_r = _rwkv_post(o_ref[...].T, bonus_ref[...], gr_ref[...], prm, headones_ref[...])
    ycat = jnp.concatenate([ya_ref[...], y_r], axis=1).astype(BF16)
    h1 = x_ref[:, 0, :] + _dot(ycat, wo)
    gate = _sigmoid(_dot(h1.astype(BF16), wg))
    pe = _dot(p_ref[:, 0, :].astype(BF16), wp)
    h2 = h1 + gate * pe
    ms = jnp.mean(h2 * h2, axis=-1, keepdims=True)
    y_ref[:, 0, :] = h2 * lax.rsqrt(ms + RMS_EPS) * fg_ref[...]


def _full(shape):
    nd = len(shape)
    return pl.BlockSpec(shape, lambda *_: (0,) * nd)


_SAMPLE_PARAMS = pltpu.CompilerParams(dimension_semantics=("arbitrary",), vmem_limit_bytes=VMEM_LIMIT)
_SQ, _PP = (D_MODEL, D_MODEL), (D_PLE, D_MODEL)


def _sample_front(x3d, state_conv, state_shift, w):
    n = x3d.shape[0]
    row = jax.ShapeDtypeStruct((n, RW), F32)
    wblk = IN_W // W_IN_BLOCKS
    once = lambda shape: pl.BlockSpec(shape, lambda j: (0,) * len(shape), pipeline_mode=pl.Buffered(1))
    outs = pl.pallas_call(
        _sample_front_kernel,
        grid=(W_IN_BLOCKS,),
        in_specs=[_full((n, 1, D_MODEL)), _full((1, D_MODEL)), pl.BlockSpec((D_MODEL, wblk), lambda j: (0, j)),
                  once(_SQ), once(_SQ), once(_PP), _full((n, 2, CONV_CH)), _full((n, SHIFT_W))]
        + _prm_specs(1) + [_full((HEADSUM_W, HEADSUM_W))],
        out_specs=[_full((n, CONV_CH)), _full((n, 2, CONV_CH)), _full((n, SHIFT_W)),
                   _full((n, RW)), _full((n, RW)), _full((len(_STEP_VECS), RW, n)),
                   pl.BlockSpec((D_MODEL, wblk), lambda j: (0, j)), once(_SQ), once(_SQ), once(_PP)],
        out_shape=[jax.ShapeDtypeStruct((n, CONV_CH), F32), jax.ShapeDtypeStruct((n, 2, CONV_CH), F32),
                   jax.ShapeDtypeStruct((n, SHIFT_W), F32), row, row,
                   jax.ShapeDtypeStruct((len(_STEP_VECS), RW, n), F32),
                   jax.ShapeDtypeStruct((D_MODEL, IN_W), BF16), jax.ShapeDtypeStruct(_SQ, BF16),
                   jax.ShapeDtypeStruct(_SQ, BF16), jax.ShapeDtypeStruct(_PP, BF16)],
        scratch_shapes=[pltpu.VMEM((n, D_MODEL), BF16), pltpu.VMEM((W_IN_BLOCKS, n, wblk), F32)],
        compiler_params=_SAMPLE_PARAMS,
        name="sample_front",
    )(x3d, w["norm_g"].reshape(1, D_MODEL), w["w_in_f32"], w["w_out_f32"], w["w_pg_f32"], w["w_pp_f32"],
      state_conv, state_shift, *w["prm"], w["headones"])
    ya, conv_new, shift_new, g_r, bonus, vecs, w_in, w_out, w_pg, w_pp = outs
    rounded = {"w_in": w_in, "w_out": w_out, "w_pg": w_pg, "w_pp": w_pp}
    return (ya, g_r, bonus), conv_new, shift_new, vecs.reshape(len(_STEP_VECS), N_HEADS, HEAD_DIM, n), rounded


def _sample_back(y_t, token_parts, x3d, p3d, w):
    n = x3d.shape[0]
    ya, g_r, bonus = token_parts
    return pl.pallas_call(
        _sample_back_kernel,
        grid=(1,),
        in_specs=[_full((RW, n)), _full((n, RW)), _full((n, RW)), _full((n, CONV_CH)),
                  _full((n, 1, D_MODEL)), _full((n, 1, D_PLE)), _full(_SQ), _full(_SQ), _full(_PP),
                  _full((1, D_MODEL))]
        + _prm_specs(1) + [_full((HEADSUM_W, HEADSUM_W))],
        out_specs=_full((n, 1, D_MODEL)),
        out_shape=jax.ShapeDtypeStruct((n, 1, D_MODEL), F32),
        compiler_params=_SAMPLE_PARAMS,
        name="sample_back",
    )(y_t.reshape(RW, n), bonus, g_r, ya, x3d, p3d, w["w_out"], w["w_pg"], w["w_pp"],
      w["final_g"].reshape(1, D_MODEL), *w["prm"], w["headones"])


def _layer(x_prompt, p_prompt, x_sample, p_sample, state_conv, state_shift, state_wkv, w, tc=256):
    bsz, t_len, _ = x_prompt.shape
    token_parts, conv_s, shift_s, vecs, rounded = _sample_front(x_sample, state_conv, state_shift, w)
    w = {**w, **rounded}
    s_hvkb = jnp.transpose(state_wkv, (1, 2, 3, 0))
    y_p, conv_p, shift_p, wkv_p, s1, y_t = _prompt_layer(
        x_prompt.reshape(bsz * t_len, D_MODEL), p_prompt.reshape(bsz * t_len, D_PLE), w, bsz, t_len, tc,
        s_hvkb, vecs)
    y_s = _sample_back(y_t, token_parts, x_sample, p_sample, w)
    prompt_out = (y_p.reshape(bsz, t_len, D_MODEL), conv_p, shift_p.reshape(bsz, SHIFT_W), wkv_p)
    sample_out = (y_s, conv_s, shift_s, jnp.transpose(s1, (3, 0, 1, 2)))
    return prompt_out, sample_out


def kernel(x_prompt, x_sample, p_prompt, p_sample, state_conv, state_shift, state_wkv, norm_g, w_in, conv_w, mu_shift, w0, w_up, a0, a_up, k_k, k_a, r_k, ln_w, ln_b, w_out, w_pg, w_pp, final_g):
    depth = norm_g.shape[0]
    assert depth == 1
    i = 0
    head_id = np.arange(HEADSUM_W) // HEAD_DIM
    w = {
        "norm_g": norm_g[i],
        "w_in_f32": w_in[i],
        "w_out_f32": w_out[i],
        "w_pg_f32": w_pg[i],
        "w_pp_f32": w_pp[i],
        "final_g": final_g,
        "prm": _prm_arrays(conv_w[i], mu_shift[i], w0[i], w_up[i], a0[i], a_up[i], k_k[i], k_a[i],
                           r_k[i].reshape(RW), ln_w[i], ln_b[i]),
        "headones": jnp.asarray(head_id[:, None] == head_id[None, :], BF16),
    }
    (yp, cp, sp, wp), (ys, cs, ss, ws) = _layer(x_prompt, p_prompt[i], x_sample, p_sample[i], state_conv[i],
                                                state_shift[i], state_wkv[i], w)
    return (yp, ys, cp[None], sp[None], wp[None], cs[None], ss[None], ws[None])
```

```python
import functools
import math

import jax
import jax.numpy as jnp
import numpy as np
from jax import lax
from jax.experimental import pallas as pl
from jax.experimental.pallas import tpu as pltpu

F32 = jnp.float32
BF16 = jnp.bfloat16

D_MODEL = 1024
CONV_CH = 512
RW = 512
HEAD_DIM = 64
N_HEADS = 8
N_PAIRS = N_HEADS // 2
LORA = 64
D_PLE = 256
SHIFT_W = 3 * RW + 2 * LORA
IN_W = 4 * CONV_CH + SHIFT_W + RW
OFF_ZS = 4 * CONV_CH
OFF_GR = OFF_ZS + SHIFT_W
RMS_EPS = 1e-6
GN_EPS = 64e-5
DECAY_SCALE = math.exp(-0.5)

LANES = 128
SUBLANES = 8
MXU_DIM = 256
HEADSUM_W = MXU_DIM
SUM_PARTS = 1
CUM_PARTS = 1
LORA_PASSES = 1
CHUNK = 64
W_IN_BLOCKS = 11
CARRY_ROWS = 8
VMEM_LIMIT = 56 * 1024 * 1024


def _dot(a, b):
    return jnp.dot(a, b, preferred_element_type=F32)


def _dot_nt(a, b):
    return lax.dot_general(a, b, (((1,), (1,)), ((), ())), preferred_element_type=F32)


def _dot_tn(a, b):
    return lax.dot_general(a, b, (((0,), (0,)), ((), ())), preferred_element_type=F32)


def _split(x, parts):
    out = []
    rem = x
    for i in range(parts):
        t = rem.astype(BF16)
        out.append(t)
        if i + 1 < parts:
            rem = rem - t.astype(F32)
    return out


def _const_dot(c_bf16, x, parts):
    acc = None
    for t in _split(x, parts):
        d = _dot(c_bf16, t)
        acc = d if acc is None else acc + d
    return acc


def _x_dot_const(x, c_bf16, parts):
    acc = None
    for t in _split(x, parts):
        d = _dot(t, c_bf16)
        acc = d if acc is None else acc + d
    return acc


def _head_sum(x, headones):
    w = headones.shape[0]
    halves = [_x_dot_const(x[:, i * w:(i + 1) * w], headones, SUM_PARTS) for i in range(x.shape[1] // w)]
    return jnp.concatenate(halves, axis=1)


def _mm(a, b, passes):
    if passes == 1:
        return _dot(a.astype(BF16), b.astype(BF16))
    ah, al = _split(a, 2)
    bh, bl = _split(b, 2)
    return _dot(ah, bh) + _dot(al, bh) + _dot(ah, bl)


def _sigmoid(x):
    return 1.0 / (1.0 + jnp.exp(-x))


def _silu(x):
    return x * _sigmoid(x)


def _rwkv_tokens(r, k, v, wa, prm, headones, between_steps=lambda: None):
    n = r.shape[0]
    lane = lax.broadcasted_iota(jnp.int32, (n, LANES), 1)
    th = jnp.where(lane < LORA, jnp.tanh(wa), wa)
    lora = _mm(th, prm["wlora"], LORA_PASSES)
    lw = -DECAY_SCALE * _sigmoid(prm["w0"] + lora[:, :RW])
    a = _sigmoid(prm["a0"] + lora[:, RW:])
    between_steps()
    kk = k * prm["k_k"]
    ss = _head_sum(kk * kk, headones)
    kk = kk / jnp.maximum(jnp.sqrt(ss), 1e-12)
    between_steps()
    kmod = k * (1.0 + (a - 1.0) * prm["k_a"])
    bonus = _head_sum(r * kmod * prm["r_k"], headones) * v
    between_steps()
    return lw, a, kk, kmod, bonus


def _rwkv_post(o, bonus, g_r, prm, headones):
    mean = _head_sum(o, headones) * (1.0 / HEAD_DIM)
    d = o - mean
    var = _head_sum(d * d, headones) * (1.0 / HEAD_DIM)
    on = d * lax.rsqrt(var + GN_EPS)
    on = on * prm["ln_w"] + prm["ln_b"]
    return (on + bonus) * _silu(g_r)


_PRM_NAMES = ("conv_w", "mu", "w0", "a0", "w_up", "a_up", "k_k", "k_a", "r_k", "ln_w", "ln_b")


def _prm_arrays(conv_w, mu_shift, w0, w_up, a0, a_up, k_k, k_a, r_k, ln_w, ln_b):
    row = lambda x: x.reshape(1, -1)
    return (conv_w, row(mu_shift), row(w0), row(a0), w_up, a_up, row(k_k), row(k_a),
            row(r_k), row(ln_w), row(ln_b))


def _prm_specs(grid_rank):
    zero = (lambda *_: (0, 0))
    shapes = ((3, CONV_CH), (1, SHIFT_W), (1, RW), (1, RW), (LORA, RW), (LORA, RW),
              (1, RW), (1, RW), (1, RW), (1, RW), (1, RW))
    return [pl.BlockSpec(s, zero) for s in shapes]


def _load_prm(refs):
    prm = {n: r[...] for n, r in zip(_PRM_NAMES, refs)}
    zero = jnp.zeros((LORA, RW), F32)
    prm["wlora"] = jnp.concatenate([jnp.concatenate([prm["w_up"], zero], axis=1),
                                    jnp.concatenate([zero, prm["a_up"]], axis=1)], axis=0)
    return prm


def _stack_heads(x):
    lane = lax.broadcasted_iota(jnp.int32, x.shape, 1)
    lo = jnp.where(lane < HEAD_DIM, x, 0.0)
    hi = jnp.where(lane >= HEAD_DIM, x, 0.0)
    return jnp.concatenate([lo, hi], axis=0)


def _stack_heads_bf16(x):
    lane = lax.broadcasted_iota(jnp.int32, x.shape, 1)
    lo = jnp.where(lane < HEAD_DIM, 1.0, 0.0).astype(BF16)
    xb = x.astype(BF16)
    return jnp.concatenate([xb * lo, xb * (1.0 - lo)], axis=0)


def _bdot(a, b):
    return _dot(a.astype(BF16), b.astype(BF16))


def _diag_block_inverses_minus_eye(mats, lane_bcast):
    n = mats[0].shape[0]
    nb = n // SUBLANES
    row = lax.broadcasted_iota(jnp.int32, (n, n), 0)
    col = lax.broadcasted_iota(jnp.int32, (n, n), 1)
    in_diag = ((row // SUBLANES) == (col // SUBLANES)) & (row > col)
    packed = []
    for a in mats:
        d = jnp.where(in_diag, a, 0.0)
        acc = d[0:SUBLANES]
        for m in range(1, nb):
            acc = acc + d[m * SUBLANES:(m + 1) * SUBLANES]
        packed.append(acc)
    d_all = jnp.concatenate(packed, axis=0).astype(BF16)
    sub = lax.broadcasted_iota(jnp.int32, (SUBLANES, n), 0)
    lane = lax.broadcasted_iota(jnp.int32, (SUBLANES, n), 1)
    eye8 = jnp.where(sub == lane % SUBLANES, 1.0, 0.0)
    ts = [eye8 for _ in mats]
    for j in range(SUBLANES - 1):
        spread = _dot(d_all, lane_bcast[j])
        ts = [t + spread[i * SUBLANES:(i + 1) * SUBLANES] * jnp.broadcast_to(t[j:j + 1, :], (SUBLANES, n))
              for i, t in enumerate(ts)]
    blk = lax.broadcasted_iota(jnp.int32, (SUBLANES, n), 1) // SUBLANES
    out = []
    for t in ts:
        tm1 = t - eye8
        out.append(jnp.concatenate([jnp.where(blk == m, tm1, 0.0) for m in range(nb)], axis=0))
    return out


def _unit_lower_inverse_minus_eye(mats, lane_bcast, between_levels=lambda: None):
    n = mats[0].shape[0]
    row = lax.broadcasted_iota(jnp.int32, (n, n), 0)
    col = lax.broadcasted_iota(jnp.int32, (n, n), 1)
    xs = _diag_block_inverses_minus_eye(mats, lane_bcast)
    size = 2 * SUBLANES
    while size <= CHUNK:
        half = size // 2
        sel = ((row // size) == (col // size)) & ((row % size) >= half) & ((col % size) < half)
        als = [jnp.where(sel, a, 0.0) for a in mats]
        ps = [al + _bdot(x, al) for x, al in zip(xs, als)]
        xs = [x + p + _bdot(p, x) for x, p in zip(xs, ps)]
        between_levels()
        size *= 2
    return xs


def _chunk_operators(ld, strict, incl, lane_bcast, side_work=()):
    side_work = list(side_work)

    def breathe():
        if side_work:
            side_work.pop(0)()

    cat0 = lambda xs: jnp.concatenate(xs, axis=0)
    la = [_stack_heads(x) for x in ld("la")]
    lr = [_stack_heads(x) for x in ld("lr")]
    n2 = 2 * CHUNK
    amat = [_dot_nt(cat0([a, b]).astype(BF16), cat0([c, d]).astype(BF16))
            for a, b, c, d in zip(la, lr, ld("rb"), ld("rk"))]

    def block_diag_pair(m):
        swapped = pltpu.roll(m, CHUNK, axis=1)
        first = cat0([m[0:CHUNK], swapped[CHUNK:n2]])
        second = cat0([swapped[0:CHUNK], m[CHUNK:n2]])
        return first, second

    breathe()
    top = [block_diag_pair(m[0:n2]) for m in amat]
    a_ab = [jnp.where(strict, t[0], 0.0) for t in top]
    xs = _unit_lower_inverse_minus_eye(a_ab, lane_bcast, breathe)
    vst = [_stack_heads_bf16(x) for x in ld("v")]
    av = [_dot(jnp.where(strict, t[1], 0.0).astype(BF16), v) for t, v in zip(top, vst)]
    breathe()
    uv0 = [y + _bdot(x, y) for x, y in zip(xs, av)]
    tla = [y + _bdot(x, y) for x, y in zip(xs, la)]
    breathe()
    bottom = [block_diag_pair(m[n2:2 * n2]) for m in amat]
    a_rb = [jnp.where(incl, t[0], 0.0) for t in bottom]
    a_rk = [jnp.where(incl, t[1], 0.0) for t in bottom]
    gop = [y + _bdot(a, t) for y, a, t in zip(lr, a_rb, tla)]
    uv = [cat0([u.astype(BF16), v]) for u, v in zip(uv0, vst)]
    y0 = [_dot(jnp.concatenate([a, b], axis=1).astype(BF16), w) for a, b, w in zip(a_rb, a_rk, uv)]
    breathe()
    bt = [_stack_heads_bf16(x) for x in ld("bt")]
    kt = [_stack_heads_bf16(x) for x in ld("kt")]
    pop = [_dot_tn(t.astype(BF16), b) for t, b in zip(tla, bt)]
    qop = [_dot_tn(w, cat0([b, k])) for w, b, k in zip(uv, bt, kt)]
    while side_work:
        breathe()
    return gop, y0, pop, qop


def _prompt_kernel(xh_ref, xt_ref, pt_ref, ng_ref, win_ref, wo_ref, wg_ref, wp_ref, fg_ref, *refs,
                   tc, n_t, n_tiles):
    prm_refs = refs[:len(_PRM_NAMES)]
    headones_ref, tri_ref, bcast_ref, st_ref, vecs_ref = refs[len(_PRM_NAMES):len(_PRM_NAMES) + 5]
    (y_ref, conv_out_ref, shift_out_ref, wkv_out_ref, st1_ref, ys_ref) = refs[len(_PRM_NAMES) + 5:len(_PRM_NAMES) + 11]
    (z_s, ycat_s, ubuf, zsbuf, s_ref, la_s, lr_s, rb_s, rk_s, bt_s, kt_s, v_s, gam_s, o_s,
     g_sc, y0_sc, p_sc, q_sc, bonus_p, gr_p) = refs[len(_PRM_NAMES) + 11:]

    s = pl.program_id(0)
    head_t = lax.rem(s, jnp.int32(n_t))
    tail_t = lax.rem(s + (n_t - 1), jnp.int32(n_t))
    prm = _load_prm(prm_refs)
    headones = headones_ref[...]
    n2 = 2 * CHUNK
    n_chunks = tc // CHUNK

    @pl.when(s == 0)
    def _():
        for ref in (g_sc, y0_sc, p_sc, q_sc, gam_s, bonus_p, gr_p, ycat_s, s_ref):
            ref[...] = jnp.zeros(ref.shape, ref.dtype)

    @pl.when(head_t == 0)
    def _():
        ubuf[0:CARRY_ROWS, :] = jnp.zeros((CARRY_ROWS, CONV_CH), F32)
        zsbuf[0:CARRY_ROWS, :] = jnp.zeros((CARRY_ROWS, SHIFT_W), F32)

    @pl.when(tail_t == 0)
    def _():
        s_ref[...] = jnp.zeros(s_ref.shape, F32)

    tail = {}

    def tail_state_pass(c):
        def run():
            for j in range(N_PAIRS):
                idx = c * N_PAIRS + j
                lanes = slice(j * LANES, (j + 1) * LANES)
                s_old = s_ref[j]
                s_bf = s_old.astype(BF16)
                yst = _dot_nt(g_sc[idx], s_bf) + y0_sc[idx]
                o_s[c * CHUNK:(c + 1) * CHUNK, lanes] = yst[0:CHUNK] + yst[CHUNK:n2]
                gam = gam_s[c * CHUNK:c * CHUNK + 1, lanes]
                s_ref[j] = s_old * gam + _dot(s_bf, p_sc[idx]) + q_sc[idx]
        return run

    def tail_post():
        ycat_s[:, CONV_CH:2 * CONV_CH] = _rwkv_post(o_s[...], bonus_p[...], gr_p[...], prm, headones)

    def tail_out_proj():
        tail["h1"] = xt_ref[...] + _dot(ycat_s[...].astype(BF16), wo_ref[...])

    def tail_gate():
        tail["gate"] = _sigmoid(_dot(tail["h1"].astype(BF16), wg_ref[...]))
        tail["pe"] = _dot(pt_ref[...].astype(BF16), wp_ref[...])

    def tail_norm():
        h2 = tail["h1"] + tail["gate"] * tail["pe"]
        ms = jnp.mean(h2 * h2, axis=-1, keepdims=True)
        y_ref[...] = h2 * lax.rsqrt(ms + RMS_EPS) * fg_ref[...]

    x = xh_ref[...]
    xn = (x * lax.rsqrt(jnp.mean(x * x, axis=-1, keepdims=True) + RMS_EPS) * ng_ref[...]).astype(BF16)

    def project(c0):
        def run():
            z_s[:, c0:c0 + CONV_CH] = _dot(xn, win_ref[:, c0:c0 + CONV_CH])
        return run

    passes = [tail_state_pass(c) for c in range(n_chunks)]
    blocks = [project(c0) for c0 in (OFF_GR, 0, CONV_CH, 2 * CONV_CH, 3 * CONV_CH)]
    side_queue = []
    late_blocks, blocks = blocks[-1:], blocks[:-1]
    while passes or blocks:
        side_queue += passes[:1] + blocks[:1]
        passes, blocks = passes[1:], blocks[1:]
    side_queue += [tail_post, tail_out_proj, tail_gate, tail_norm]

    blocks_per_head = HEAD_DIM // STATE_ROWS
    v_row0 = lax.rem(jnp.minimum(s, n_tiles - 1), jnp.int32(blocks_per_head)) * STATE_ROWS
    side_queue.append(lambda: _sample_state_rows(st_ref, vecs_ref, st1_ref, ys_ref, v_row0))

    def project_next(count=2):
        for _ in range(count):
            if side_queue:
                side_queue.pop(0)()

    z_s[:, OFF_ZS:OFF_ZS + SHIFT_W] = _dot(xn, win_ref[:, OFF_ZS:OFF_ZS + SHIFT_W])

    zs = z_s[:, OFF_ZS:OFF_ZS + SHIFT_W]
    zsbuf[CARRY_ROWS:CARRY_ROWS + tc, :] = zs
    zprev = zsbuf[CARRY_ROWS - 1:CARRY_ROWS - 1 + tc, :]
    zm = zs + (zprev - zs) * prm["mu"]
    zsbuf[CARRY_ROWS - 1:CARRY_ROWS, :] = zsbuf[CARRY_ROWS + tc - 1:CARRY_ROWS + tc, :]

    project_next()
    r = zm[:, 0:RW]
    k = zm[:, RW:2 * RW]
    v = zm[:, 2 * RW:3 * RW]
    wa = zm[:, 3 * RW:3 * RW + 2 * LORA]
    lw, a, kk, kmod, bonus = _rwkv_tokens(r, k, v, wa, prm, headones, project_next)

    g = _const_dot(tri_ref[...], lw, CUM_PARTS)
    gc = jnp.concatenate(
        [jnp.broadcast_to(g[(c + 1) * CHUNK - 1:(c + 1) * CHUNK, :], (CHUNK, RW)) for c in range(tc // CHUNK)],
        axis=0)
    project_next()
    eng = jnp.exp(-g)
    etail = jnp.exp(gc - g)
    b = kk * a
    la_s[...] = -kk * jnp.exp(g - lw)
    lr_s[...] = r * jnp.exp(g)
    project_next()
    rb_s[...] = b * eng
    rk_s[...] = kmod * eng
    bt_s[...] = b * etail
    kt_s[...] = kmod * etail
    v_s[...] = v
    while side_queue:
        project_next()
    gam_s[...] = jnp.exp(gc)

    row = lax.broadcasted_iota(jnp.int32, (n2, n2), 0)
    col = lax.broadcasted_iota(jnp.int32, (n2, n2), 1)
    same = (row // CHUNK) == (col // CHUNK)
    strict = same & (row > col)
    incl = same & (row >= col)

    cw = prm["conv_w"]

    def conv_rows(r0):
        def run():
            rows = slice(r0, r0 + CHUNK)
            u = z_s[rows, CONV_CH:2 * CONV_CH] * z_s[rows, 2 * CONV_CH:3 * CONV_CH]
            ubuf[CARRY_ROWS + r0:CARRY_ROWS + r0 + CHUNK, :] = u
            um1 = ubuf[CARRY_ROWS - 1 + r0:CARRY_ROWS - 1 + r0 + CHUNK, :]
            um2 = ubuf[CARRY_ROWS - 2 + r0:CARRY_ROWS - 2 + r0 + CHUNK, :]
            conv = cw[0:1, :] * um2 + cw[1:2, :] * um1 + cw[2:3, :] * u
            ycat_s[rows, 0:CONV_CH] = z_s[rows, 0:CONV_CH] * conv * _silu(z_s[rows, 3 * CONV_CH:4 * CONV_CH])
        return run

    srcs = {"la": la_s, "lr": lr_s, "rb": rb_s, "rk": rk_s, "bt": bt_s, "kt": kt_s, "v": v_s}
    members = [(c, j) for c in range(n_chunks) for j in range(N_PAIRS)]

    def ld(name):
        ref = srcs[name]
        return [ref[c * CHUNK:(c + 1) * CHUNK, j * LANES:(j + 1) * LANES] for c, j in members]

    lane_bcast = [bcast_ref[j] for j in range(SUBLANES - 1)]
    gop, y0, pop, qop = _chunk_operators(ld, strict, incl, lane_bcast,
                                         late_blocks + [conv_rows(c * CHUNK) for c in range(n_chunks)])
    for (c, j), g_, y_, p_, q_ in zip(members, gop, y0, pop, qop):
        idx = c * N_PAIRS + j
        g_sc[idx] = g_.astype(BF16)
        y0_sc[idx] = y_
        p_sc[idx] = p_.astype(BF16)
        q_sc[idx] = q_

    ubuf[CARRY_ROWS - 2:CARRY_ROWS, :] = ubuf[CARRY_ROWS + tc - 2:CARRY_ROWS + tc, :]

    bonus_p[...] = bonus
    gr_p[...] = z_s[:, OFF_GR:OFF_GR + RW]

    @pl.when((head_t == n_t - 1) & (s < n_tiles))
    def _():
        shift_out_ref[0] = zsbuf[CARRY_ROWS - 1:CARRY_ROWS, :]
        conv_out_ref[0] = ubuf[CARRY_ROWS - 2:CARRY_ROWS, :]

    @pl.when((tail_t == n_t - 1) & (s > 0))
    def _():
        for h in range(N_HEADS):
            j, i = divmod(h, 2)
            blk = s_ref[j]
            wkv_out_ref[0, h] = blk[i * HEAD_DIM:(i + 1) * HEAD_DIM, i * HEAD_DIM:(i + 1) * HEAD_DIM]


def _prompt_layer(x2d, p2d, w, bsz, t_len, tc, state_hvkb, step_vecs):
    n_t = t_len // tc
    n_seq = state_hvkb.shape[-1]
    blocks_per_head = HEAD_DIM // STATE_ROWS
    assert bsz * n_t == N_HEADS * blocks_per_head, "one state block per prompt tile"
    state_blk = lambda s: jnp.minimum(s, bsz * n_t - 1)
    tok = np.arange(tc)
    blockones = (tok[:, None] // CHUNK) == (tok[None, :] // CHUNK)
    tri = jnp.asarray(blockones & (tok[:, None] >= tok[None, :]), BF16)
    n_tiles = bsz * n_t
    n2 = 2 * CHUNK
    idx = np.arange(n2)
    same_blk = (idx[:, None] // SUBLANES) == (idx[None, :] // SUBLANES)
    lane_bcast = jnp.asarray(
        np.stack([same_blk & (idx[:, None] % SUBLANES == j) for j in range(SUBLANES - 1)]), BF16)
    kern = functools.partial(_prompt_kernel, tc=tc, n_t=n_t, n_tiles=n_tiles)
    big = lambda: pltpu.VMEM((tc, RW), F32)
    n_blk = (tc // CHUNK) * N_PAIRS
    op = lambda dt: pltpu.VMEM((n_blk, LANES, LANES), dt)
    head = lambda s: jnp.minimum(s, n_tiles - 1)
    tail = lambda s: jnp.maximum(s - 1, 0)
    head_tile = lambda width: pl.BlockSpec((tc, width), lambda s: (head(s), 0))
    tail_tile = lambda width: pl.BlockSpec((tc, width), lambda s: (tail(s), 0))
    const = lambda shape: pl.BlockSpec(shape, lambda s: (0, 0), pipeline_mode=pl.Buffered(1))
    return pl.pallas_call(
        kern,
        grid=(n_tiles + 1,),
        in_specs=[head_tile(D_MODEL), tail_tile(D_MODEL), tail_tile(D_PLE), const((1, D_MODEL)),
                  const((D_MODEL, IN_W)), const((D_MODEL, D_MODEL)), const((D_MODEL, D_MODEL)),
                  const((D_PLE, D_MODEL)), const((1, D_MODEL))]
        + _prm_specs(1)
        + [const((HEADSUM_W, HEADSUM_W)), const((tc, tc)),
           pl.BlockSpec((SUBLANES - 1, n2, n2), lambda s: (0, 0, 0), pipeline_mode=pl.Buffered(1)),
           pl.BlockSpec((1, STATE_ROWS, HEAD_DIM, n_seq),
                        lambda s: (state_blk(s) // blocks_per_head, state_blk(s) % blocks_per_head, 0, 0)),
           pl.BlockSpec((len(_STEP_VECS), 1, HEAD_DIM, n_seq),
                        lambda s: (0, state_blk(s) // blocks_per_head, 0, 0))],
        out_specs=[
            tail_tile(D_MODEL),
            pl.BlockSpec((1, 2, CONV_CH), lambda s: (head(s) // n_t, 0, 0)),
            pl.BlockSpec((1, 1, SHIFT_W), lambda s: (head(s) // n_t, 0, 0)),
            pl.BlockSpec((1, N_HEADS, HEAD_DIM, HEAD_DIM), lambda s: (tail(s) // n_t, 0, 0, 0)),
            pl.BlockSpec((1, STATE_ROWS, HEAD_DIM, n_seq),
                         lambda s: (state_blk(s) // blocks_per_head, state_blk(s) % blocks_per_head, 0, 0)),
            pl.BlockSpec((1, STATE_ROWS, n_seq),
                         lambda s: (state_blk(s) // blocks_per_head, state_blk(s) % blocks_per_head, 0)),
        ],
        out_shape=[
            jax.ShapeDtypeStruct((bsz * t_len, D_MODEL), F32),
            jax.ShapeDtypeStruct((bsz, 2, CONV_CH), F32),
            jax.ShapeDtypeStruct((bsz, 1, SHIFT_W), F32),
            jax.ShapeDtypeStruct((bsz, N_HEADS, HEAD_DIM, HEAD_DIM), F32),
            jax.ShapeDtypeStruct(state_hvkb.shape, F32),
            jax.ShapeDtypeStruct((N_HEADS, HEAD_DIM, n_seq), F32),
        ],
        scratch_shapes=[
            pltpu.VMEM((tc, IN_W), F32),
            pltpu.VMEM((tc, D_MODEL), F32),
            pltpu.VMEM((CARRY_ROWS + tc, CONV_CH), F32),
            pltpu.VMEM((CARRY_ROWS + tc, SHIFT_W), F32),
            pltpu.VMEM((N_PAIRS, LANES, LANES), F32),
            big(), big(), big(), big(), big(), big(), big(), big(), big(),
            op(BF16), op(F32), op(BF16), op(F32),
            big(), big(),
        ],
        compiler_params=pltpu.CompilerParams(
            dimension_semantics=("arbitrary",), vmem_limit_bytes=VMEM_LIMIT),
        name="prompt_layer",
    )(x2d, x2d, p2d, w["norm_g"].reshape(1, D_MODEL), w["w_in"], w["w_out"], w["w_pg"], w["w_pp"],
      w["final_g"].reshape(1, D_MODEL), *w["prm"], w["headones"], tri, lane_bcast, state_hvkb, step_vecs)


def _sample_front_kernel(x_ref, ng_ref, win_ref, wo_ref, wg_ref, wp_ref, cb_ref, sb_ref, *refs):
    prm_refs = refs[:len(_PRM_NAMES)]
    headones_ref = refs[len(_PRM_NAMES)]
    (ya_ref, conv_out_ref, shift_out_ref, gr_ref, bonus_ref, vecs_ref,
     wbf_ref, wo_bf_ref, wg_bf_ref, wp_bf_ref, xn_s, z_s) = refs[len(_PRM_NAMES) + 1:]
    j = pl.program_id(0)

    @pl.when(j == 0)
    def _():
        x = x_ref[:, 0, :]
        xn_s[...] = (x * lax.rsqrt(jnp.mean(x * x, axis=-1, keepdims=True) + RMS_EPS) * ng_ref[...]).astype(BF16)
        wo_bf_ref[...] = wo_ref[...].astype(BF16)
        wg_bf_ref[...] = wg_ref[...].astype(BF16)
        wp_bf_ref[...] = wp_ref[...].astype(BF16)

    w_blk = win_ref[...].astype(BF16)
    wbf_ref[...] = w_blk
    z_s[j] = _dot(xn_s[...], w_blk)

    @pl.when(j == pl.num_programs(0) - 1)
    def _():
        _sample_front_tokens(z_s, cb_ref, sb_ref, prm_refs, headones_ref, ya_ref, conv_out_ref, shift_out_ref,
                             gr_ref, bonus_ref, vecs_ref)


def _sample_front_tokens(z_s, cb_ref, sb_ref, prm_refs, headones_ref, ya_ref, conv_out_ref, shift_out_ref,
                         gr_ref, bonus_ref, vecs_ref):
    prm = _load_prm(prm_refs)
    headones = headones_ref[...]
    z = jnp.concatenate([z_s[i] for i in range(z_s.shape[0])], axis=1)

    u = z[:, CONV_CH:2 * CONV_CH] * z[:, 2 * CONV_CH:3 * CONV_CH]
    cb0 = cb_ref[:, 0, :]
    cb1 = cb_ref[:, 1, :]
    cw = prm["conv_w"]
    conv = cw[0:1, :] * cb0 + cw[1:2, :] * cb1 + cw[2:3, :] * u
    ya_ref[...] = z[:, 0:CONV_CH] * conv * _silu(z[:, 3 * CONV_CH:4 * CONV_CH])
    conv_out_ref[:, 0, :] = cb1
    conv_out_ref[:, 1, :] = u

    zs = z[:, OFF_ZS:OFF_ZS + SHIFT_W]
    shift_out_ref[...] = zs
    gr_ref[...] = z[:, OFF_GR:OFF_GR + RW]
    zm = zs + (sb_ref[...] - zs) * prm["mu"]
    r = zm[:, 0:RW]
    k = zm[:, RW:2 * RW]
    v = zm[:, 2 * RW:3 * RW]
    wa = zm[:, 3 * RW:3 * RW + 2 * LORA]
    lw, a, kk, kmod, bonus = _rwkv_tokens(r, k, v, wa, prm, headones)
    bonus_ref[...] = bonus
    for i, vec in enumerate((-kk, jnp.exp(lw), kk * a, kmod, v, r)):
        vecs_ref[i] = vec.T


_STEP_VECS = ("nkk", "decay", "b", "kmod", "v", "r")
STATE_ROWS = 8


def _sample_state_rows(st_ref, vecs_ref, s1_ref, y_ref, v_row0):
    nkk, w, b, km = (vecs_ref[i, 0] for i in range(4))
    r = vecs_ref[5, 0]
    for vi in range(STATE_ROWS):
        s0 = st_ref[0, vi]
        sa = jnp.sum(s0 * nkk, axis=0, keepdims=True)
        v_row = vecs_ref[4, 0, pl.ds(v_row0 + vi, 1), :]
        s1 = s0 * w + sa * b + v_row * km
        s1_ref[0, vi] = s1
        y_ref[0, vi:vi + 1, :] = jnp.sum(s1 * r, axis=0, keepdims=True)


def _sample_back_kernel(o_ref, bonus_ref, gr_ref, ya_ref, x_ref, p_ref, wo_ref, wg_ref, wp_ref, fg_ref, *refs):
    prm_refs = refs[:len(_PRM_NAMES)]
    headones_ref = refs[len(_PRM_NAMES)]
    y_ref = refs[len(_PRM_NAMES) + 1]
    prm = _load_prm(prm_refs)
    wo = wo_ref[...]
    wg = wg_ref[...]
    wp = wp_ref[...]
    y_r = _rwkv_post(o_ref[...].T, bonus_ref[...], gr_ref[...], prm, headones_ref[...])
    ycat = jnp.concatenate([ya_ref[...], y_r], axis=1).astype(BF16)
    h1 = x_ref[:, 0, :] + _dot(ycat, wo)
    gate = _sigmoid(_dot(h1.astype(BF16), wg))
    pe = _dot(p_ref[:, 0, :].astype(BF16), wp)
    h2 = h1 + gate * pe
    ms = jnp.mean(h2 * h2, axis=-1, keepdims=True)
    y_ref[:, 0, :] = h2 * lax.rsqrt(ms + RMS_EPS) * fg_ref[...]


def _full(shape):
    nd = len(shape)
    return pl.BlockSpec(shape, lambda *_: (0,) * nd)


_SAMPLE_PARAMS = pltpu.CompilerParams(dimension_semantics=("arbitrary",), vmem_limit_bytes=VMEM_LIMIT)
_SQ, _PP = (D_MODEL, D_MODEL), (D_PLE, D_MODEL)


def _sample_front(x3d, state_conv, state_shift, w):
    n = x3d.shape[0]
    row = jax.ShapeDtypeStruct((n, RW), F32)
    wblk = IN_W // W_IN_BLOCKS
    once = lambda shape: pl.BlockSpec(shape, lambda j: (0,) * len(shape), pipeline_mode=pl.Buffered(1))
    outs = pl.pallas_call(
        _sample_front_kernel,
        grid=(W_IN_BLOCKS,),
        in_specs=[_full((n, 1, D_MODEL)), _full((1, D_MODEL)), pl.BlockSpec((D_MODEL, wblk), lambda j: (0, j)),
                  once(_SQ), once(_SQ), once(_PP), _full((n, 2, CONV_CH)), _full((n, SHIFT_W))]
        + _prm_specs(1) + [_full((HEADSUM_W, HEADSUM_W))],
        out_specs=[_full((n, CONV_CH)), _full((n, 2, CONV_CH)), _full((n, SHIFT_W)),
                   _full((n, RW)), _full((n, RW)), _full((len(_STEP_VECS), RW, n)),
                   pl.BlockSpec((D_MODEL, wblk), lambda j: (0, j)), once(_SQ), once(_SQ), once(_PP)],
        out_shape=[jax.ShapeDtypeStruct((n, CONV_CH), F32), jax.ShapeDtypeStruct((n, 2, CONV_CH), F32),
                   jax.ShapeDtypeStruct((n, SHIFT_W), F32), row, row,
                   jax.ShapeDtypeStruct((len(_STEP_VECS), RW, n), F32),
                   jax.ShapeDtypeStruct((D_MODEL, IN_W), BF16), jax.ShapeDtypeStruct(_SQ, BF16),
                   jax.ShapeDtypeStruct(_SQ, BF16), jax.ShapeDtypeStruct(_PP, BF16)],
        scratch_shapes=[pltpu.VMEM((n, D_MODEL), BF16), pltpu.VMEM((W_IN_BLOCKS, n, wblk), F32)],
        compiler_params=_SAMPLE_PARAMS,
        name="sample_front",
    )(x3d, w["norm_g"].reshape(1, D_MODEL), w["w_in_f32"], w["w_out_f32"], w["w_pg_f32"], w["w_pp_f32"],
      state_conv, state_shift, *w["prm"], w["headones"])
    ya, conv_new, shift_new, g_r, bonus, vecs, w_in, w_out, w_pg, w_pp = outs
    rounded = {"w_in": w_in, "w_out": w_out, "w_pg": w_pg, "w_pp": w_pp}
    return (ya, g_r, bonus), conv_new, shift_new, vecs.reshape(len(_STEP_VECS), N_HEADS, HEAD_DIM, n), rounded


def _sample_back(y_t, token_parts, x3d, p3d, w):
    n = x3d.shape[0]
    ya, g_r, bonus = token_parts
    return pl.pallas_call(
        _sample_back_kernel,
        grid=(1,),
        in_specs=[_full((RW, n)), _full((n, RW)), _full((n, RW)), _full((n, CONV_CH)),
                  _full((n, 1, D_MODEL)), _full((n, 1, D_PLE)), _full(_SQ), _full(_SQ), _full(_PP),
                  _full((1, D_MODEL))]
        + _prm_specs(1) + [_full((HEADSUM_W, HEADSUM_W))],
        out_specs=_full((n, 1, D_MODEL)),
        out_shape=jax.ShapeDtypeStruct((n, 1, D_MODEL), F32),
        compiler_params=_SAMPLE_PARAMS,
        name="sample_back",
    )(y_t.reshape(RW, n), bonus, g_r, ya, x3d, p3d, w["w_out"], w["w_pg"], w["w_pp"],
      w["final_g"].reshape(1, D_MODEL), *w["prm"], w["headones"])


def _layer(x_prompt, p_prompt, x_sample, p_sample, state_conv, state_shift, state_wkv, w, tc=256):
    bsz, t_len, _ = x_prompt.shape
    token_parts, conv_s, shift_s, vecs, rounded = _sample_front(x_sample, state_conv, state_shift, w)
    w = {**w, **rounded}
    s_hvkb = jnp.transpose(state_wkv, (1, 2, 3, 0))
    y_p, conv_p, shift_p, wkv_p, s1, y_t = _prompt_layer(
        x_prompt.reshape(bsz * t_len, D_MODEL), p_prompt.reshape(bsz * t_len, D_PLE), w, bsz, t_len, tc,
        s_hvkb, vecs)
    y_s = _sample_back(y_t, token_parts, x_sample, p_sample, w)
    prompt_out = (y_p.reshape(bsz, t_len, D_MODEL), conv_p, shift_p.reshape(bsz, SHIFT_W), wkv_p)
    sample_out = (y_s, conv_s, shift_s, jnp.transpose(s1, (3, 0, 1, 2)))
    return prompt_out, sample_out


def kernel(x_prompt, x_sample, p_prompt, p_sample, state_conv, state_shift, state_wkv, norm_g, w_in, conv_w, mu_shift, w0, w_up, a0, a_up, k_k, k_a, r_k, ln_w, ln_b, w_out, w_pg, w_pp, final_g):
    depth = norm_g.shape[0]
    assert depth == 1
    i = 0
    head_id = np.arange(HEADSUM_W) // HEAD_DIM
    w = {
        "norm_g": norm_g[i],
        "w_in_f32": w_in[i],
        "w_out_f32": w_out[i],
        "w_pg_f32": w_pg[i],
        "w_pp_f32": w_pp[i],
        "final_g": final_g,
        "prm": _prm_arrays(conv_w[i], mu_shift[i], w0[i], w_up[i], a0[i], a_up[i], k_k[i], k_a[i],
                           r_k[i].reshape(RW), ln_w[i], ln_b[i]),
        "headones": jnp.asarray(head_id[:, None] == head_id[None, :], BF16),
    }
    (yp, cp, sp, wp), (ys, cs, ss, ws) = _layer(x_prompt, p_prompt[i], x_sample, p_sample[i], state_conv[i],
                                                state_shift[i], state_wkv[i], w)
    return (yp, ys, cp[None], sp[None], wp[None], cs[None], ss[None], ws[None])
```

```python
import functools
import math

import jax
import jax.numpy as jnp
import numpy as np
from jax import lax
from jax.experimental import pallas as pl
from jax.experimental.pallas import tpu as pltpu

F32 = jnp.float32
BF16 = jnp.bfloat16

D_MODEL = 1024
CONV_CH = 512
RW = 512
HEAD_DIM = 64
N_HEADS = 8
N_PAIRS = N_HEADS // 2
LORA = 64
D_PLE = 256
SHIFT_W = 3 * RW + 2 * LORA
IN_W = 4 * CONV_CH + SHIFT_W + RW
OFF_ZS = 4 * CONV_CH
OFF_GR = OFF_ZS + SHIFT_W
RMS_EPS = 1e-6
GN_EPS = 64e-5
DECAY_SCALE = math.exp(-0.5)

LANES = 128
SUBLANES = 8
MXU_DIM = 256
HEADSUM_W = MXU_DIM
SUM_PARTS = 1
CUM_PARTS = 1
LORA_PASSES = 1
CHUNK = 64
W_IN_BLOCKS = 3
CARRY_ROWS = 8
VMEM_LIMIT = 56 * 1024 * 1024


def _dot(a, b):
    return jnp.dot(a, b, preferred_element_type=F32)


def _dot_nt(a, b):
    return lax.dot_general(a, b, (((1,), (1,)), ((), ())), preferred_element_type=F32)


def _dot_tn(a, b):
    return lax.dot_general(a, b, (((0,), (0,)), ((), ())), preferred_element_type=F32)


def _split(x, parts):
    out = []
    rem = x
    for i in range(parts):
        t = rem.astype(BF16)
        out.append(t)
        if i + 1 < parts:
            rem = rem - t.astype(F32)
    return out


def _const_dot(c_bf16, x, parts):
    acc = None
    for t in _split(x, parts):
        d = _dot(c_bf16, t)
        acc = d if acc is None else acc + d
    return acc


def _x_dot_const(x, c_bf16, parts):
    acc = None
    for t in _split(x, parts):
        d = _dot(t, c_bf16)
        acc = d if acc is None else acc + d
    return acc


def _head_sum(x, headones):
    w = headones.shape[0]
    halves = [_x_dot_const(x[:, i * w:(i + 1) * w], headones, SUM_PARTS) for i in range(x.shape[1] // w)]
    return jnp.concatenate(halves, axis=1)


def _mm(a, b, passes):
    if passes == 1:
        return _dot(a.astype(BF16), b.astype(BF16))
    ah, al = _split(a, 2)
    bh, bl = _split(b, 2)
    return _dot(ah, bh) + _dot(al, bh) + _dot(ah, bl)


def _sigmoid(x):
    return 1.0 / (1.0 + jnp.exp(-x))


def _silu(x):
    return x * _sigmoid(x)


def _rwkv_tokens(r, k, v, wa, prm, headones, between_steps=lambda: None):
    n = r.shape[0]
    lane = lax.broadcasted_iota(jnp.int32, (n, LANES), 1)
    th = jnp.where(lane < LORA, jnp.tanh(wa), wa)
    lora = _mm(th, prm["wlora"], LORA_PASSES)
    lw = -DECAY_SCALE * _sigmoid(prm["w0"] + lora[:, :RW])
    a = _sigmoid(prm["a0"] + lora[:, RW:])
    between_steps()
    kk = k * prm["k_k"]
    ss = _head_sum(kk * kk, headones)
    kk = kk / jnp.maximum(jnp.sqrt(ss), 1e-12)
    between_steps()
    kmod = k * (1.0 + (a - 1.0) * prm["k_a"])
    bonus = _head_sum(r * kmod * prm["r_k"], headones) * v
    between_steps()
    return lw, a, kk, kmod, bonus


def _rwkv_post(o, bonus, g_r, prm, headones):
    mean = _head_sum(o, headones) * (1.0 / HEAD_DIM)
    d = o - mean
    var = _head_sum(d * d, headones) * (1.0 / HEAD_DIM)
    on = d * lax.rsqrt(var + GN_EPS)
    on = on * prm["ln_w"] + prm["ln_b"]
    return (on + bonus) * _silu(g_r)


_PRM_NAMES = ("conv_w", "mu", "w0", "a0", "w_up", "a_up", "k_k", "k_a", "r_k", "ln_w", "ln_b")


def _prm_arrays(conv_w, mu_shift, w0, w_up, a0, a_up, k_k, k_a, r_k, ln_w, ln_b):
    row = lambda x: x.reshape(1, -1)
    return (conv_w, row(mu_shift), row(w0), row(a0), w_up, a_up, row(k_k), row(k_a),
            row(r_k), row(ln_w), row(ln_b))


def _prm_specs(grid_rank):
    zero = (lambda *_: (0, 0))
    shapes = ((3, CONV_CH), (1, SHIFT_W), (1, RW), (1, RW), (LORA, RW), (LORA, RW),
              (1, RW), (1, RW), (1, RW), (1, RW), (1, RW))
    return [pl.BlockSpec(s, zero) for s in shapes]


def _load_prm(refs):
    prm = {n: r[...] for n, r in zip(_PRM_NAMES, refs)}
    zero = jnp.zeros((LORA, RW), F32)
    prm["wlora"] = jnp.concatenate([jnp.concatenate([prm["w_up"], zero], axis=1),
                                    jnp.concatenate([zero, prm["a_up"]], axis=1)], axis=0)
    return prm


def _stack_heads(x):
    lane = lax.broadcasted_iota(jnp.int32, x.shape, 1)
    lo = jnp.where(lane < HEAD_DIM, x, 0.0)
    hi = jnp.where(lane >= HEAD_DIM, x, 0.0)
    return jnp.concatenate([lo, hi], axis=0)


def _stack_heads_bf16(x):
    lane = lax.broadcasted_iota(jnp.int32, x.shape, 1)
    lo = jnp.where(lane < HEAD_DIM, 1.0, 0.0).astype(BF16)
    xb = x.astype(BF16)
    return jnp.concatenate([xb * lo, xb * (1.0 - lo)], axis=0)


def _bdot(a, b):
    return _dot(a.astype(BF16), b.astype(BF16))


def _diag_block_inverses_minus_eye(mats, lane_bcast):
    n = mats[0].shape[0]
    nb = n // SUBLANES
    row = lax.broadcasted_iota(jnp.int32, (n, n), 0)
    col = lax.broadcasted_iota(jnp.int32, (n, n), 1)
    in_diag = ((row // SUBLANES) == (col // SUBLANES)) & (row > col)
    packed = []
    for a in mats:
        d = jnp.where(in_diag, a, 0.0)
        acc = d[0:SUBLANES]
        for m in range(1, nb):
            acc = acc + d[m * SUBLANES:(m + 1) * SUBLANES]
        packed.append(acc)
    d_all = jnp.concatenate(packed, axis=0).astype(BF16)
    sub = lax.broadcasted_iota(jnp.int32, (SUBLANES, n), 0)
    lane = lax.broadcasted_iota(jnp.int32, (SUBLANES, n), 1)
    eye8 = jnp.where(sub == lane % SUBLANES, 1.0, 0.0)
    ts = [eye8 for _ in mats]
    for j in range(SUBLANES - 1):
        spread = _dot(d_all, lane_bcast[j])
        ts = [t + spread[i * SUBLANES:(i + 1) * SUBLANES] * jnp.broadcast_to(t[j:j + 1, :], (SUBLANES, n))
              for i, t in enumerate(ts)]
    blk = lax.broadcasted_iota(jnp.int32, (SUBLANES, n), 1) // SUBLANES
    out = []
    for t in ts:
        tm1 = t - eye8
        out.append(jnp.concatenate([jnp.where(blk == m, tm1, 0.0) for m in range(nb)], axis=0))
    return out


def _unit_lower_inverse_minus_eye(mats, lane_bcast, between_levels=lambda: None):
    n = mats[0].shape[0]
    row = lax.broadcasted_iota(jnp.int32, (n, n), 0)
    col = lax.broadcasted_iota(jnp.int32, (n, n), 1)
    xs = _diag_block_inverses_minus_eye(mats, lane_bcast)
    size = 2 * SUBLANES
    while size <= CHUNK:
        half = size // 2
        sel = ((row // size) == (col // size)) & ((row % size) >= half) & ((col % size) < half)
        als = [jnp.where(sel, a, 0.0) for a in mats]
        ps = [al + _bdot(x, al) for x, al in zip(xs, als)]
        xs = [x + p + _bdot(p, x) for x, p in zip(xs, ps)]
        between_levels()
        size *= 2
    return xs


def _chunk_operators(ld, strict, incl, lane_bcast, side_work=()):
    side_work = list(side_work)

    def breathe():
        if side_work:
            side_work.pop(0)()

    cat0 = lambda xs: jnp.concatenate(xs, axis=0)
    la = [_stack_heads(x) for x in ld("la")]
    lr = [_stack_heads(x) for x in ld("lr")]
    n2 = 2 * CHUNK
    amat = [_dot_nt(cat0([a, b]).astype(BF16), cat0([c, d]).astype(BF16))
            for a, b, c, d in zip(la, lr, ld("rb"), ld("rk"))]

    def block_diag_pair(m):
        swapped = pltpu.roll(m, CHUNK, axis=1)
        first = cat0([m[0:CHUNK], swapped[CHUNK:n2]])
        second = cat0([swapped[0:CHUNK], m[CHUNK:n2]])
        return first, second

    breathe()
    top = [block_diag_pair(m[0:n2]) for m in amat]
    a_ab = [jnp.where(strict, t[0], 0.0) for t in top]
    xs = _unit_lower_inverse_minus_eye(a_ab, lane_bcast, breathe)
    vst = [_stack_heads_bf16(x) for x in ld("v")]
    av = [_dot(jnp.where(strict, t[1], 0.0).astype(BF16), v) for t, v in zip(top, vst)]
    breathe()
    uv0 = [y + _bdot(x, y) for x, y in zip(xs, av)]
    tla = [y + _bdot(x, y) for x, y in zip(xs, la)]
    breathe()
    bottom = [block_diag_pair(m[n2:2 * n2]) for m in amat]
    a_rb = [jnp.where(incl, t[0], 0.0) for t in bottom]
    a_rk = [jnp.where(incl, t[1], 0.0) for t in bottom]
    gop = [y + _bdot(a, t) for y, a, t in zip(lr, a_rb, tla)]
    uv = [cat0([u.astype(BF16), v]) for u, v in zip(uv0, vst)]
    y0 = [_dot(jnp.concatenate([a, b], axis=1).astype(BF16), w) for a, b, w in zip(a_rb, a_rk, uv)]
    breathe()
    bt = [_stack_heads_bf16(x) for x in ld("bt")]
    kt = [_stack_heads_bf16(x) for x in ld("kt")]
    pop = [_dot_tn(t.astype(BF16), b) for t, b in zip(tla, bt)]
    qop = [_dot_tn(w, cat0([b, k])) for w, b, k in zip(uv, bt, kt)]
    while side_work:
        breathe()
    return gop, y0, pop, qop


def _prompt_kernel(xh_ref, xt_ref, pt_ref, ng_ref, win_ref, wo_ref, wg_ref, wp_ref, fg_ref, *refs,
                   tc, n_t, n_tiles):
    prm_refs = refs[:len(_PRM_NAMES)]
    headones_ref, tri_ref, bcast_ref, st_ref, vecs_ref = refs[len(_PRM_NAMES):len(_PRM_NAMES) + 5]
    (y_ref, conv_out_ref, shift_out_ref, wkv_out_ref, st1_ref, ys_ref) = refs[len(_PRM_NAMES) + 5:len(_PRM_NAMES) + 11]
    (z_s, ycat_s, ubuf, zsbuf, s_ref, la_s, lr_s, rb_s, rk_s, bt_s, kt_s, v_s, gam_s, o_s,
     g_sc, y0_sc, p_sc, q_sc, bonus_p, gr_p) = refs[len(_PRM_NAMES) + 11:]

    s = pl.program_id(0)
    head_t = lax.rem(s, jnp.int32(n_t))
    tail_t = lax.rem(s + (n_t - 1), jnp.int32(n_t))
    prm = _load_prm(prm_refs)
    headones = headones_ref[...]
    n2 = 2 * CHUNK
    n_chunks = tc // CHUNK

    @pl.when(s == 0)
    def _():
        for ref in (g_sc, y0_sc, p_sc, q_sc, gam_s, bonus_p, gr_p, ycat_s, s_ref):
            ref[...] = jnp.zeros(ref.shape, ref.dtype)

    @pl.when(head_t == 0)
    def _():
        ubuf[0:CARRY_ROWS, :] = jnp.zeros((CARRY_ROWS, CONV_CH), F32)
        zsbuf[0:CARRY_ROWS, :] = jnp.zeros((CARRY_ROWS, SHIFT_W), F32)

    @pl.when(tail_t == 0)
    def _():
        s_ref[...] = jnp.zeros(s_ref.shape, F32)

    tail = {}

    def tail_state_pass(c):
        def run():
            for j in range(N_PAIRS):
                idx = c * N_PAIRS + j
                lanes = slice(j * LANES, (j + 1) * LANES)
                s_old = s_ref[j]
                s_bf = s_old.astype(BF16)
                yst = _dot_nt(g_sc[idx], s_bf) + y0_sc[idx]
                o_s[c * CHUNK:(c + 1) * CHUNK, lanes] = yst[0:CHUNK] + yst[CHUNK:n2]
                gam = gam_s[c * CHUNK:c * CHUNK + 1, lanes]
                s_ref[j] = s_old * gam + _dot(s_bf, p_sc[idx]) + q_sc[idx]
        return run

    def tail_post():
        ycat_s[:, CONV_CH:2 * CONV_CH] = _rwkv_post(o_s[...], bonus_p[...], gr_p[...], prm, headones)

    def tail_out_proj():
        tail["h1"] = xt_ref[...] + _dot(ycat_s[...].astype(BF16), wo_ref[...])

    def tail_gate():
        tail["gate"] = _sigmoid(_dot(tail["h1"].astype(BF16), wg_ref[...]))
        tail["pe"] = _dot(pt_ref[...].astype(BF16), wp_ref[...])

    def tail_norm():
        h2 = tail["h1"] + tail["gate"] * tail["pe"]
        ms = jnp.mean(h2 * h2, axis=-1, keepdims=True)
        y_ref[...] = h2 * lax.rsqrt(ms + RMS_EPS) * fg_ref[...]

    x = xh_ref[...]
    xn = (x * lax.rsqrt(jnp.mean(x * x, axis=-1, keepdims=True) + RMS_EPS) * ng_ref[...]).astype(BF16)

    def project(c0):
        def run():
            z_s[:, c0:c0 + CONV_CH] = _dot(xn, win_ref[:, c0:c0 + CONV_CH])
        return run

    passes = [tail_state_pass(c) for c in range(n_chunks)]
    blocks = [project(c0) for c0 in (OFF_GR, 0, CONV_CH, 2 * CONV_CH, 3 * CONV_CH)]
    side_queue = []
    late_blocks, blocks = blocks[-1:], blocks[:-1]
    while passes or blocks:
        side_queue += passes[:1] + blocks[:1]
        passes, blocks = passes[1:], blocks[1:]
    side_queue += [tail_post, tail_out_proj, tail_gate, tail_norm]

    blocks_per_head = HEAD_DIM // STATE_ROWS
    v_row0 = lax.rem(jnp.minimum(s, n_tiles - 1), jnp.int32(blocks_per_head)) * STATE_ROWS
    state_rows = lambda: _sample_state_rows(st_ref, vecs_ref, st1_ref, ys_ref, v_row0)

    def project_next(count=2):
        for _ in range(count):
            if side_queue:
                side_queue.pop(0)()

    z_s[:, OFF_ZS:OFF_ZS + SHIFT_W] = _dot(xn, win_ref[:, OFF_ZS:OFF_ZS + SHIFT_W])
    state_rows()

    zs = z_s[:, OFF_ZS:OFF_ZS + SHIFT_W]
    zsbuf[CARRY_ROWS:CARRY_ROWS + tc, :] = zs
    zprev = zsbuf[CARRY_ROWS - 1:CARRY_ROWS - 1 + tc, :]
    zm = zs + (zprev - zs) * prm["mu"]
    zsbuf[CARRY_ROWS - 1:CARRY_ROWS, :] = zsbuf[CARRY_ROWS + tc - 1:CARRY_ROWS + tc, :]

    project_next()
    r = zm[:, 0:RW]
    k = zm[:, RW:2 * RW]
    v = zm[:, 2 * RW:3 * RW]
    wa = zm[:, 3 * RW:3 * RW + 2 * LORA]
    lw, a, kk, kmod, bonus = _rwkv_tokens(r, k, v, wa, prm, headones, project_next)

    g = _const_dot(tri_ref[...], lw, CUM_PARTS)
    gc = jnp.concatenate(
        [jnp.broadcast_to(g[(c + 1) * CHUNK - 1:(c + 1) * CHUNK, :], (CHUNK, RW)) for c in range(tc // CHUNK)],
        axis=0)
    project_next()
    eng = jnp.exp(-g)
    etail = jnp.exp(gc - g)
    b = kk * a
    la_s[...] = -kk * jnp.exp(g - lw)
    lr_s[...] = r * jnp.exp(g)
    project_next()
    rb_s[...] = b * eng
    rk_s[...] = kmod * eng
    bt_s[...] = b * etail
    kt_s[...] = kmod * etail
    v_s[...] = v
    while side_queue:
        project_next()
    gam_s[...] = jnp.exp(gc)

    row = lax.broadcasted_iota(jnp.int32, (n2, n2), 0)
    col = lax.broadcasted_iota(jnp.int32, (n2, n2), 1)
    same = (row // CHUNK) == (col // CHUNK)
    strict = same & (row > col)
    incl = same & (row >= col)

    cw = prm["conv_w"]

    def conv_rows(r0):
        def run():
            rows = slice(r0, r0 + CHUNK)
            u = z_s[rows, CONV_CH:2 * CONV_CH] * z_s[rows, 2 * CONV_CH:3 * CONV_CH]
            ubuf[CARRY_ROWS + r0:CARRY_ROWS + r0 + CHUNK, :] = u
            um1 = ubuf[CARRY_ROWS - 1 + r0:CARRY_ROWS - 1 + r0 + CHUNK, :]
            um2 = ubuf[CARRY_ROWS - 2 + r0:CARRY_ROWS - 2 + r0 + CHUNK, :]
            conv = cw[0:1, :] * um2 + cw[1:2, :] * um1 + cw[2:3, :] * u
            ycat_s[rows, 0:CONV_CH] = z_s[rows, 0:CONV_CH] * conv * _silu(z_s[rows, 3 * CONV_CH:4 * CONV_CH])
        return run

    srcs = {"la": la_s, "lr": lr_s, "rb": rb_s, "rk": rk_s, "bt": bt_s, "kt": kt_s, "v": v_s}
    members = [(c, j) for c in range(n_chunks) for j in range(N_PAIRS)]

    def ld(name):
        ref = srcs[name]
        return [ref[c * CHUNK:(c + 1) * CHUNK, j * LANES:(j + 1) * LANES] for c, j in members]

    lane_bcast = [bcast_ref[j] for j in range(SUBLANES - 1)]
    gop, y0, pop, qop = _chunk_operators(ld, strict, incl, lane_bcast,
                                         late_blocks + [conv_rows(c * CHUNK) for c in range(n_chunks)])
    for (c, j), g_, y_, p_, q_ in zip(members, gop, y0, pop, qop):
        idx = c * N_PAIRS + j
        g_sc[idx] = g_.astype(BF16)
        y0_sc[idx] = y_
        p_sc[idx] = p_.astype(BF16)
        q_sc[idx] = q_

    ubuf[CARRY_ROWS - 2:CARRY_ROWS, :] = ubuf[CARRY_ROWS + tc - 2:CARRY_ROWS + tc, :]

    bonus_p[...] = bonus
    gr_p[...] = z_s[:, OFF_GR:OFF_GR + RW]

    @pl.when((head_t == n_t - 1) & (s < n_tiles))
    def _():
        shift_out_ref[0] = zsbuf[CARRY_ROWS - 1:CARRY_ROWS, :]
        conv_out_ref[0] = ubuf[CARRY_ROWS - 2:CARRY_ROWS, :]

    @pl.when((tail_t == n_t - 1) & (s > 0))
    def _():
        for h in range(N_HEADS):
            j, i = divmod(h, 2)
            blk = s_ref[j]
            wkv_out_ref[0, h] = blk[i * HEAD_DIM:(i + 1) * HEAD_DIM, i * HEAD_DIM:(i + 1) * HEAD_DIM]


def _prompt_layer(x2d, p2d, w, bsz, t_len, tc, state_hvkb, step_vecs):
    n_t = t_len // tc
    n_seq = state_hvkb.shape[-1]
    blocks_per_head = HEAD_DIM // STATE_ROWS
    assert bsz * n_t == N_HEADS * blocks_per_head, "one state block per prompt tile"
    state_blk = lambda s: jnp.minimum(s, bsz * n_t - 1)
    tok = np.arange(tc)
    blockones = (tok[:, None] // CHUNK) == (tok[None, :] // CHUNK)
    tri = jnp.asarray(blockones & (tok[:, None] >= tok[None, :]), BF16)
    n_tiles = bsz * n_t
    n2 = 2 * CHUNK
    idx = np.arange(n2)
    same_blk = (idx[:, None] // SUBLANES) == (idx[None, :] // SUBLANES)
    lane_bcast = jnp.asarray(
        np.stack([same_blk & (idx[:, None] % SUBLANES == j) for j in range(SUBLANES - 1)]), BF16)
    kern = functools.partial(_prompt_kernel, tc=tc, n_t=n_t, n_tiles=n_tiles)
    big = lambda: pltpu.VMEM((tc, RW), F32)
    n_blk = (tc // CHUNK) * N_PAIRS
    op = lambda dt: pltpu.VMEM((n_blk, LANES, LANES), dt)
    head = lambda s: jnp.minimum(s, n_tiles - 1)
    tail = lambda s: jnp.maximum(s - 1, 0)
    head_tile = lambda width: pl.BlockSpec((tc, width), lambda s: (head(s), 0))
    tail_tile = lambda width: pl.BlockSpec((tc, width), lambda s: (tail(s), 0))
    const = lambda shape: pl.BlockSpec(shape, lambda s: (0, 0), pipeline_mode=pl.Buffered(1))
    return pl.pallas_call(
        kern,
        grid=(n_tiles + 1,),
        in_specs=[head_tile(D_MODEL), tail_tile(D_MODEL), tail_tile(D_PLE), const((1, D_MODEL)),
                  const((D_MODEL, IN_W)), const((D_MODEL, D_MODEL)), const((D_MODEL, D_MODEL)),
                  const((D_PLE, D_MODEL)), const((1, D_MODEL))]
        + _prm_specs(1)
        + [const((HEADSUM_W, HEADSUM_W)), const((tc, tc)),
           pl.BlockSpec((SUBLANES - 1, n2, n2), lambda s: (0, 0, 0), pipeline_mode=pl.Buffered(1)),
           pl.BlockSpec((1, STATE_ROWS, HEAD_DIM, n_seq),
                        lambda s: (state_blk(s) // blocks_per_head, state_blk(s) % blocks_per_head, 0, 0)),
           pl.BlockSpec((len(_STEP_VECS), 1, HEAD_DIM, n_seq),
                        lambda s: (0, state_blk(s) // blocks_per_head, 0, 0))],
        out_specs=[
            tail_tile(D_MODEL),
            pl.BlockSpec((1, 2, CONV_CH), lambda s: (head(s) // n_t, 0, 0)),
            pl.BlockSpec((1, 1, SHIFT_W), lambda s: (head(s) // n_t, 0, 0)),
            pl.BlockSpec((1, N_HEADS, HEAD_DIM, HEAD_DIM), lambda s: (tail(s) // n_t, 0, 0, 0)),
            pl.BlockSpec((1, STATE_ROWS, HEAD_DIM, n_seq),
                         lambda s: (state_blk(s) // blocks_per_head, state_blk(s) % blocks_per_head, 0, 0)),
            pl.BlockSpec((1, STATE_ROWS, n_seq),
                         lambda s: (state_blk(s) // blocks_per_head, state_blk(s) % blocks_per_head, 0)),
        ],
        out_shape=[
            jax.ShapeDtypeStruct((bsz * t_len, D_MODEL), F32),
            jax.ShapeDtypeStruct((bsz, 2, CONV_CH), F32),
            jax.ShapeDtypeStruct((bsz, 1, SHIFT_W), F32),
            jax.ShapeDtypeStruct((bsz, N_HEADS, HEAD_DIM, HEAD_DIM), F32),
            jax.ShapeDtypeStruct(state_hvkb.shape, F32),
            jax.ShapeDtypeStruct((N_HEADS, HEAD_DIM, n_seq), F32),
        ],
        scratch_shapes=[
            pltpu.VMEM((tc, IN_W), F32),
            pltpu.VMEM((tc, D_MODEL), F32),
            pltpu.VMEM((CARRY_ROWS + tc, CONV_CH), F32),
            pltpu.VMEM((CARRY_ROWS + tc, SHIFT_W), F32),
            pltpu.VMEM((N_PAIRS, LANES, LANES), F32),
            big(), big(), big(), big(), big(), big(), big(), big(), big(),
            op(BF16), op(F32), op(BF16), op(F32),
            big(), big(),
        ],
        compiler_params=pltpu.CompilerParams(
            dimension_semantics=("arbitrary",), vmem_limit_bytes=VMEM_LIMIT),
        name="prompt_layer",
    )(x2d, x2d, p2d, w["norm_g"].reshape(1, D_MODEL), w["w_in"], w["w_out"], w["w_pg"], w["w_pp"],
      w["final_g"].reshape(1, D_MODEL), *w["prm"], w["headones"], tri, lane_bcast, state_hvkb, step_vecs)


def _sample_front_kernel(x_ref, ng_ref, win_ref, wo_ref, wg_ref, wp_ref, cb_ref, sb_ref, *refs):
    prm_refs = refs[:len(_PRM_NAMES)]
    headones_ref = refs[len(_PRM_NAMES)]
    (ya_ref, conv_out_ref, shift_out_ref, gr_ref, bonus_ref, vecs_ref,
     wbf_ref, wo_bf_ref, wg_bf_ref, wp_bf_ref, xn_s, z_s) = refs[len(_PRM_NAMES) + 1:]
    j = pl.program_id(0)

    @pl.when(j == 0)
    def _():
        x = x_ref[:, 0, :]
        xn_s[...] = (x * lax.rsqrt(jnp.mean(x * x, axis=-1, keepdims=True) + RMS_EPS) * ng_ref[...]).astype(BF16)
        wo_bf_ref[...] = wo_ref[...].astype(BF16)
        wg_bf_ref[...] = wg_ref[...].astype(BF16)
        wp_bf_ref[...] = wp_ref[...].astype(BF16)

    w_blk = win_ref[...].astype(BF16)
    wbf_ref[...] = w_blk
    z_s[j] = _dot(xn_s[...], w_blk)

    @pl.when(j == pl.num_programs(0) - 1)
    def _():
        _sample_front_tokens(z_s, cb_ref, sb_ref, prm_refs, headones_ref, ya_ref, conv_out_ref, shift_out_ref,
                             gr_ref, bonus_ref, vecs_ref)


def _sample_front_tokens(z_s, cb_ref, sb_ref, prm_refs, headones_ref, ya_ref, conv_out_ref, shift_out_ref,
                         gr_ref, bonus_ref, vecs_ref):
    prm = _load_prm(prm_refs)
    headones = headones_ref[...]
    z = jnp.concatenate([z_s[i] for i in range(z_s.shape[0])], axis=1)

    u = z[:, CONV_CH:2 * CONV_CH] * z[:, 2 * CONV_CH:3 * CONV_CH]
    cb0 = cb_ref[:, 0, :]
    cb1 = cb_ref[:, 1, :]
    cw = prm["conv_w"]
    conv = cw[0:1, :] * cb0 + cw[1:2, :] * cb1 + cw[2:3, :] * u
    ya_ref[...] = z[:, 0:CONV_CH] * conv * _silu(z[:, 3 * CONV_CH:4 * CONV_CH])
    conv_out_ref[:, 0, :] = cb1
    conv_out_ref[:, 1, :] = u

    zs = z[:, OFF_ZS:OFF_ZS + SHIFT_W]
    shift_out_ref[...] = zs
    gr_ref[...] = z[:, OFF_GR:OFF_GR + RW]
    zm = zs + (sb_ref[...] - zs) * prm["mu"]
    r = zm[:, 0:RW]
    k = zm[:, RW:2 * RW]
    v = zm[:, 2 * RW:3 * RW]
    wa = zm[:, 3 * RW:3 * RW + 2 * LORA]
    lw, a, kk, kmod, bonus = _rwkv_tokens(r, k, v, wa, prm, headones)
    bonus_ref[...] = bonus
    for i, vec in enumerate((-kk, jnp.exp(lw), kk * a, kmod, v, r)):
        vecs_ref[i] = vec.T


_STEP_VECS = ("nkk", "decay", "b", "kmod", "v", "r")
STATE_ROWS = 8


def _sample_state_rows(st_ref, vecs_ref, s1_ref, y_ref, v_row0):
    nkk, w, b, km = (vecs_ref[i, 0] for i in range(4))
    r = vecs_ref[5, 0]
    for vi in range(STATE_ROWS):
        s0 = st_ref[0, vi]
        sa = jnp.sum(s0 * nkk, axis=0, keepdims=True)
        v_row = vecs_ref[4, 0, pl.ds(v_row0 + vi, 1), :]
        s1 = s0 * w + sa * b + v_row * km
        s1_ref[0, vi] = s1
        y_ref[0, vi:vi + 1, :] = jnp.sum(s1 * r, axis=0, keepdims=True)


def _sample_back_kernel(o_ref, bonus_ref, gr_ref, ya_ref, x_ref, p_ref, wo_ref, wg_ref, wp_ref, fg_ref, *refs):
    prm_refs = refs[:len(_PRM_NAMES)]
    headones_ref = refs[len(_PRM_NAMES)]
    y_ref = refs[len(_PRM_NAMES) + 1]
    prm = _load_prm(prm_refs)
    wo = wo_ref[...]
    wg = wg_ref[...]
    wp = wp_ref[...]
    y_r = _rwkv_post(o_ref[...].T, bonus_ref[...], gr_ref[...], prm, headones_ref[...])
    ycat = jnp.concatenate([ya_ref[...], y_r], axis=1).astype(BF16)
    h1 = x_ref[:, 0, :] + _dot(ycat, wo)
    gate = _sigmoid(_dot(h1.astype(BF16), wg))
    pe = _dot(p_ref[:, 0, :].astype(BF16), wp)
    h2 = h1 + gate * pe
    ms = jnp.mean(h2 * h2, axis=-1, keepdims=True)
    y_ref[:, 0, :] = h2 * lax.rsqrt(ms + RMS_EPS) * fg_ref[...]


def _full(shape):
    nd = len(shape)
    return pl.BlockSpec(shape, lambda *_: (0,) * nd)


_SAMPLE_PARAMS = pltpu.CompilerParams(dimension_semantics=("arbitrary",), vmem_limit_bytes=VMEM_LIMIT)
_SQ, _PP = (D_MODEL, D_MODEL), (D_PLE, D_MODEL)


def _sample_front(x3d, state_conv, state_shift, w):
    n = x3d.shape[0]
    row = jax.ShapeDtypeStruct((n, RW), F32)
    wblk = IN_W // W_IN_BLOCKS
    once = lambda shape: pl.BlockSpec(shape, lambda j: (0,) * len(shape), pipeline_mode=pl.Buffered(1))
    outs = pl.pallas_call(
        _sample_front_kernel,
        grid=(W_IN_BLOCKS,),
        in_specs=[_full((n, 1, D_MODEL)), _full((1, D_MODEL)), pl.BlockSpec((D_MODEL, wblk), lambda j: (0, j)),
                  once(_SQ), once(_SQ), once(_PP), _full((n, 2, CONV_CH)), _full((n, SHIFT_W))]
        + _prm_specs(1) + [_full((HEADSUM_W, HEADSUM_W))],
        out_specs=[_full((n, CONV_CH)), _full((n, 2, CONV_CH)), _full((n, SHIFT_W)),
                   _full((n, RW)), _full((n, RW)), _full((len(_STEP_VECS), RW, n)),
                   pl.BlockSpec((D_MODEL, wblk), lambda j: (0, j)), once(_SQ), once(_SQ), once(_PP)],
        out_shape=[jax.ShapeDtypeStruct((n, CONV_CH), F32), jax.ShapeDtypeStruct((n, 2, CONV_CH), F32),
                   jax.ShapeDtypeStruct((n, SHIFT_W), F32), row, row,
                   jax.ShapeDtypeStruct((len(_STEP_VECS), RW, n), F32),
                   jax.ShapeDtypeStruct((D_MODEL, IN_W), BF16), jax.ShapeDtypeStruct(_SQ, BF16),
                   jax.ShapeDtypeStruct(_SQ, BF16), jax.ShapeDtypeStruct(_PP, BF16)],
        scratch_shapes=[pltpu.VMEM((n, D_MODEL), BF16), pltpu.VMEM((W_IN_BLOCKS, n, wblk), F32)],
        compiler_params=_SAMPLE_PARAMS,
        name="sample_front",
    )(x3d, w["norm_g"].reshape(1, D_MODEL), w["w_in_f32"], w["w_out_f32"], w["w_pg_f32"], w["w_pp_f32"],
      state_conv, state_shift, *w["prm"], w["headones"])
    ya, conv_new, shift_new, g_r, bonus, vecs, w_in, w_out, w_pg, w_pp = outs
    rounded = {"w_in": w_in, "w_out": w_out, "w_pg": w_pg, "w_pp": w_pp}
    return (ya, g_r, bonus), conv_new, shift_new, vecs.reshape(len(_STEP_VECS), N_HEADS, HEAD_DIM, n), rounded


def _sample_back(y_t, token_parts, x3d, p3d, w):
    n = x3d.shape[0]
    ya, g_r, bonus = token_parts
    return pl.pallas_call(
        _sample_back_kernel,
        grid=(1,),
        in_specs=[_full((RW, n)), _full((n, RW)), _full((n, RW)), _full((n, CONV_CH)),
                  _full((n, 1, D_MODEL)), _full((n, 1, D_PLE)), _full(_SQ), _full(_SQ), _full(_PP),
                  _full((1, D_MODEL))]
        + _prm_specs(1) + [_full((HEADSUM_W, HEADSUM_W))],
        out_specs=_full((n, 1, D_MODEL)),
        out_shape=jax.ShapeDtypeStruct((n, 1, D_MODEL), F32),
        compiler_params=_SAMPLE_PARAMS,
        name="sample_back",
    )(y_t.reshape(RW, n), bonus, g_r, ya, x3d, p3d, w["w_out"], w["w_pg"], w["w_pp"],
      w["final_g"].reshape(1, D_MODEL), *w["prm"], w["headones"])


def _layer(x_prompt, p_prompt, x_sample, p_sample, state_conv, state_shift, state_wkv, w, tc=256):
    bsz, t_len, _ = x_prompt.shape
    token_parts, conv_s, shift_s, vecs, rounded = _sample_front(x_sample, state_conv, state_shift, w)
    w = {**w, **rounded}
    s_hvkb = jnp.transpose(state_wkv, (1, 2, 3, 0))
    y_p, conv_p, shift_p, wkv_p, s1, y_t = _prompt_layer(
        x_prompt.reshape(bsz * t_len, D_MODEL), p_prompt.reshape(bsz * t_len, D_PLE), w, bsz, t_len, tc,
        s_hvkb, vecs)
    y_s = _sample_back(y_t, token_parts, x_sample, p_sample, w)
    prompt_out = (y_p.reshape(bsz, t_len, D_MODEL), conv_p, shift_p.reshape(bsz, SHIFT_W), wkv_p)
    sample_out = (y_s, conv_s, shift_s, jnp.transpose(s1, (3, 0, 1, 2)))
    return prompt_out, sample_out


def kernel(x_prompt, x_sample, p_prompt, p_sample, state_conv, state_shift, state_wkv, norm_g, w_in, conv_w, mu_shift, w0, w_up, a0, a_up, k_k, k_a, r_k, ln_w, ln_b, w_out, w_pg, w_pp, final_g):
    depth = norm_g.shape[0]
    assert depth == 1
    i = 0
    head_id = np.arange(HEADSUM_W) // HEAD_DIM
    w = {
        "norm_g": norm_g[i],
        "w_in_f32": w_in[i],
        "w_out_f32": w_out[i],
        "w_pg_f32": w_pg[i],
        "w_pp_f32": w_pp[i],
        "final_g": final_g,
        "prm": _prm_arrays(conv_w[i], mu_shift[i], w0[i], w_up[i], a0[i], a_up[i], k_k[i], k_a[i],
                           r_k[i].reshape(RW), ln_w[i], ln_b[i]),
        "headones": jnp.asarray(head_id[:, None] == head_id[None, :], BF16),
    }
    (yp, cp, sp, wp), (ys, cs, ss, ws) = _layer(x_prompt, p_prompt[i], x_sample, p_sample[i], state_conv[i],
                                                state_shift[i], state_wkv[i], w)
    return (yp, ys, cp[None], sp[None], wp[None], cs[None], ss[None], ws[None])
```

```python
import functools
import math

import jax
import jax.numpy as jnp
import numpy as np
from jax import lax
from jax.experimental import pallas as pl
from jax.experimental.pallas import tpu as pltpu

F32 = jnp.float32
BF16 = jnp.bfloat16

D_MODEL = 1024
CONV_CH = 512
RW = 512
HEAD_DIM = 64
N_HEADS = 8
N_PAIRS = N_HEADS // 2
LORA = 64
D_PLE = 256
SHIFT_W = 3 * RW + 2 * LORA
IN_W = 4 * CONV_CH + SHIFT_W + RW
OFF_ZS = 4 * CONV_CH
OFF_GR = OFF_ZS + SHIFT_W
RMS_EPS = 1e-6
GN_EPS = 64e-5
DECAY_SCALE = math.exp(-0.5)

LANES = 128
SUBLANES = 8
MXU_DIM = 256
HEADSUM_W = MXU_DIM
SUM_PARTS = 1
CUM_PARTS = 1
LORA_PASSES = 1
CHUNK = 64
W_IN_BLOCKS = 3
CARRY_ROWS = 8
VMEM_LIMIT = 56 * 1024 * 1024


def _dot(a, b):
    return jnp.dot(a, b, preferred_element_type=F32)


def _dot_nt(a, b):
    return lax.dot_general(a, b, (((1,), (1,)), ((), ())), preferred_element_type=F32)


def _dot_tn(a, b):
    return lax.dot_general(a, b, (((0,), (0,)), ((), ())), preferred_element_type=F32)


def _split(x, parts):
    out = []
    rem = x
    for i in range(parts):
        t = rem.astype(BF16)
        out.append(t)
        if i + 1 < parts:
            rem = rem - t.astype(F32)
    return out


def _const_dot(c_bf16, x, parts):
    acc = None
    for t in _split(x, parts):
        d = _dot(c_bf16, t)
        acc = d if acc is None else acc + d
    return acc


def _x_dot_const(x, c_bf16, parts):
    acc = None
    for t in _split(x, parts):
        d = _dot(t, c_bf16)
        acc = d if acc is None else acc + d
    return acc


def _head_sum(x, headones):
    w = headones.shape[0]
    halves = [_x_dot_const(x[:, i * w:(i + 1) * w], headones, SUM_PARTS) for i in range(x.shape[1] // w)]
    return jnp.concatenate(halves, axis=1)


def _mm(a, b, passes):
    if passes == 1:
        return _dot(a.astype(BF16), b.astype(BF16))
    ah, al = _split(a, 2)
    bh, bl = _split(b, 2)
    return _dot(ah, bh) + _dot(al, bh) + _dot(ah, bl)


def _sigmoid(x):
    return 1.0 / (1.0 + jnp.exp(-x))


def _silu(x):
    return x * _sigmoid(x)


def _rwkv_tokens(r, k, v, wa, prm, headones, between_steps=lambda: None):
    n = r.shape[0]
    lane = lax.broadcasted_iota(jnp.int32, (n, LANES), 1)
    th = jnp.where(lane < LORA, jnp.tanh(wa), wa)
    lora = _mm(th, prm["wlora"], LORA_PASSES)
    lw = -DECAY_SCALE * _sigmoid(prm["w0"] + lora[:, :RW])
    a = _sigmoid(prm["a0"] + lora[:, RW:])
    between_steps()
    kk = k * prm["k_k"]
    ss = _head_sum(kk * kk, headones)
    kk = kk / jnp.maximum(jnp.sqrt(ss), 1e-12)
    between_steps()
    kmod = k * (1.0 + (a - 1.0) * prm["k_a"])
    bonus = _head_sum(r * kmod * prm["r_k"], headones) * v
    between_steps()
    return lw, a, kk, kmod, bonus


def _rwkv_post(o, bonus, g_r, prm, headones):
    mean = _head_sum(o, headones) * (1.0 / HEAD_DIM)
    d = o - mean
    var = _head_sum(d * d, headones) * (1.0 / HEAD_DIM)
    on = d * lax.rsqrt(var + GN_EPS)
    on = on * prm["ln_w"] + prm["ln_b"]
    return (on + bonus) * _silu(g_r)


_PRM_NAMES = ("conv_w", "mu", "w0", "a0", "w_up", "a_up", "k_k", "k_a", "r_k", "ln_w", "ln_b")


def _prm_arrays(conv_w, mu_shift, w0, w_up, a0, a_up, k_k, k_a, r_k, ln_w, ln_b):
    row = lambda x: x.reshape(1, -1)
    return (conv_w, row(mu_shift), row(w0), row(a0), w_up, a_up, row(k_k), row(k_a),
            row(r_k), row(ln_w), row(ln_b))


def _prm_specs():
    zero = (lambda *_: (0, 0))
    shapes = ((3, CONV_CH), (1, SHIFT_W), (1, RW), (1, RW), (LORA, RW), (LORA, RW),
              (1, RW), (1, RW), (1, RW), (1, RW), (1, RW))
    return [pl.BlockSpec(s, zero) for s in shapes]


def _load_prm(refs):
    prm = {n: r[...] for n, r in zip(_PRM_NAMES, refs)}
    zero = jnp.zeros((LORA, RW), F32)
    prm["wlora"] = jnp.concatenate([jnp.concatenate([prm["w_up"], zero], axis=1),
                                    jnp.concatenate([zero, prm["a_up"]], axis=1)], axis=0)
    return prm


def _stack_heads(x):
    lane = lax.broadcasted_iota(jnp.int32, x.shape, 1)
    lo = jnp.where(lane < HEAD_DIM, x, 0.0)
    hi = jnp.where(lane >= HEAD_DIM, x, 0.0)
    return jnp.concatenate([lo, hi], axis=0)


def _stack_heads_bf16(x):
    lane = lax.broadcasted_iota(jnp.int32, x.shape, 1)
    lo = jnp.where(lane < HEAD_DIM, 1.0, 0.0).astype(BF16)
    xb = x.astype(BF16)
    return jnp.concatenate([xb * lo, xb * (1.0 - lo)], axis=0)


def _bdot(a, b):
    return _dot(a.astype(BF16), b.astype(BF16))


def _diag_block_inverses_minus_eye(mats, lane_bcast):
    n = mats[0].shape[0]
    nb = n // SUBLANES
    row = lax.broadcasted_iota(jnp.int32, (n, n), 0)
    col = lax.broadcasted_iota(jnp.int32, (n, n), 1)
    in_diag = ((row // SUBLANES) == (col // SUBLANES)) & (row > col)
    packed = []
    for a in mats:
        d = jnp.where(in_diag, a, 0.0)
        acc = d[0:SUBLANES]
        for m in range(1, nb):
            acc = acc + d[m * SUBLANES:(m + 1) * SUBLANES]
        packed.append(acc)
    d_all = jnp.concatenate(packed, axis=0).astype(BF16)
    sub = lax.broadcasted_iota(jnp.int32, (SUBLANES, n), 0)
    lane = lax.broadcasted_iota(jnp.int32, (SUBLANES, n), 1)
    eye8 = jnp.where(sub == lane % SUBLANES, 1.0, 0.0)
    ts = [eye8 for _ in mats]
    for j in range(SUBLANES - 1):
        spread = _dot(d_all, lane_bcast[j])
        ts = [t + spread[i * SUBLANES:(i + 1) * SUBLANES] * jnp.broadcast_to(t[j:j + 1, :], (SUBLANES, n))
              for i, t in enumerate(ts)]
    blk = lax.broadcasted_iota(jnp.int32, (SUBLANES, n), 1) // SUBLANES
    out = []
    for t in ts:
        tm1 = t - eye8
        out.append(jnp.concatenate([jnp.where(blk == m, tm1, 0.0) for m in range(nb)], axis=0))
    return out


def _unit_lower_inverse_minus_eye(mats, lane_bcast, between_levels=lambda: None):
    n = mats[0].shape[0]
    row = lax.broadcasted_iota(jnp.int32, (n, n), 0)
    col = lax.broadcasted_iota(jnp.int32, (n, n), 1)
    xs = _diag_block_inverses_minus_eye(mats, lane_bcast)
    size = 2 * SUBLANES
    while size <= CHUNK:
        half = size // 2
        sel = ((row // size) == (col // size)) & ((row % size) >= half) & ((col % size) < half)
        als = [jnp.where(sel, a, 0.0) for a in mats]
        ps = [al + _bdot(x, al) for x, al in zip(xs, als)]
        xs = [x + p + _bdot(p, x) for x, p in zip(xs, ps)]
        between_levels()
        size *= 2
    return xs


def _chunk_operators(ld, strict, incl, lane_bcast, side_work=()):
    side_work = list(side_work)

    def breathe():
        if side_work:
            side_work.pop(0)()

    cat0 = lambda xs: jnp.concatenate(xs, axis=0)
    la = [_stack_heads(x) for x in ld("la")]
    lr = [_stack_heads(x) for x in ld("lr")]
    n2 = 2 * CHUNK
    amat = [_dot_nt(cat0([a, b]).astype(BF16), cat0([c, d]).astype(BF16))
            for a, b, c, d in zip(la, lr, ld("rb"), ld("rk"))]

    def block_diag_pair(m):
        swapped = pltpu.roll(m, CHUNK, axis=1)
        first = cat0([m[0:CHUNK], swapped[CHUNK:n2]])
        second = cat0([swapped[0:CHUNK], m[CHUNK:n2]])
        return first, second

    breathe()
    top = [block_diag_pair(m[0:n2]) for m in amat]
    a_ab = [jnp.where(strict, t[0], 0.0) for t in top]
    xs = _unit_lower_inverse_minus_eye(a_ab, lane_bcast, breathe)
    vst = [_stack_heads_bf16(x) for x in ld("v")]
    av = [_dot(jnp.where(strict, t[1], 0.0).astype(BF16), v) for t, v in zip(top, vst)]
    breathe()
    uv0 = [y + _bdot(x, y) for x, y in zip(xs, av)]
    tla = [y + _bdot(x, y) for x, y in zip(xs, la)]
    breathe()
    bottom = [block_diag_pair(m[n2:2 * n2]) for m in amat]
    a_rb = [jnp.where(incl, t[0], 0.0) for t in bottom]
    a_rk = [jnp.where(incl, t[1], 0.0) for t in bottom]
    gop = [y + _bdot(a, t) for y, a, t in zip(lr, a_rb, tla)]
    uv = [cat0([u.astype(BF16), v]) for u, v in zip(uv0, vst)]
    y0 = [_dot(jnp.concatenate([a, b], axis=1).astype(BF16), w) for a, b, w in zip(a_rb, a_rk, uv)]
    breathe()
    bt = [_stack_heads_bf16(x) for x in ld("bt")]
    kt = [_stack_heads_bf16(x) for x in ld("kt")]
    pop = [_dot_tn(t.astype(BF16), b) for t, b in zip(tla, bt)]
    qop = [_dot_tn(w, cat0([b, k])) for w, b, k in zip(uv, bt, kt)]
    while side_work:
        breathe()
    return gop, y0, pop, qop


def _prompt_kernel(xh_ref, xt_ref, pt_ref, ng_ref, win_ref, wo_ref, wg_ref, wp_ref, fg_ref, *refs,
                   tc, n_t, n_tiles):
    prm_refs = refs[:len(_PRM_NAMES)]
    headones_ref, tri_ref, bcast_ref, st_ref, vecs_ref = refs[len(_PRM_NAMES):len(_PRM_NAMES) + 5]
    (y_ref, conv_out_ref, shift_out_ref, wkv_out_ref, st1_ref, ys_ref) = refs[len(_PRM_NAMES) + 5:len(_PRM_NAMES) + 11]
    (z_s, ycat_s, ubuf, zsbuf, s_ref, la_s, lr_s, rb_s, rk_s, bt_s, kt_s, v_s, gam_s, o_s,
     g_sc, y0_sc, p_sc, q_sc, bonus_p, gr_p) = refs[len(_PRM_NAMES) + 11:]

    s = pl.program_id(0)
    head_t = lax.rem(s, jnp.int32(n_t))
    tail_t = lax.rem(s + (n_t - 1), jnp.int32(n_t))
    prm = _load_prm(prm_refs)
    headones = headones_ref[...]
    n2 = 2 * CHUNK
    n_chunks = tc // CHUNK

    @pl.when(s == 0)
    def _():
        for ref in (g_sc, y0_sc, p_sc, q_sc, gam_s, bonus_p, gr_p, ycat_s, s_ref):
            ref[...] = jnp.zeros(ref.shape, ref.dtype)

    @pl.when(head_t == 0)
    def _():
        ubuf[0:CARRY_ROWS, :] = jnp.zeros((CARRY_ROWS, CONV_CH), F32)
        zsbuf[0:CARRY_ROWS, :] = jnp.zeros((CARRY_ROWS, SHIFT_W), F32)

    @pl.when(tail_t == 0)
    def _():
        s_ref[...] = jnp.zeros(s_ref.shape, F32)

    tail = {}

    def tail_state_pass(c):
        def run():
            for j in range(N_PAIRS):
                idx = c * N_PAIRS + j
                lanes = slice(j * LANES, (j + 1) * LANES)
                s_old = s_ref[j]
                s_bf = s_old.astype(BF16)
                yst = _dot_nt(g_sc[idx], s_bf) + y0_sc[idx]
                o_s[c * CHUNK:(c + 1) * CHUNK, lanes] = yst[0:CHUNK] + yst[CHUNK:n2]
                gam = gam_s[c * CHUNK:c * CHUNK + 1, lanes]
                s_ref[j] = s_old * gam + _dot(s_bf, p_sc[idx]) + q_sc[idx]
        return run

    def tail_post():
        ycat_s[:, CONV_CH:2 * CONV_CH] = _rwkv_post(o_s[...], bonus_p[...], gr_p[...], prm, headones)

    def tail_out_proj():
        tail["h1"] = xt_ref[...] + _dot(ycat_s[...].astype(BF16), wo_ref[...])

    def tail_gate():
        tail["gate"] = _sigmoid(_dot(tail["h1"].astype(BF16), wg_ref[...]))
        tail["pe"] = _dot(pt_ref[...].astype(BF16), wp_ref[...])

    def tail_norm():
        h2 = tail["h1"] + tail["gate"] * tail["pe"]
        ms = jnp.mean(h2 * h2, axis=-1, keepdims=True)
        y_ref[...] = h2 * lax.rsqrt(ms + RMS_EPS) * fg_ref[...]

    x = xh_ref[...]
    xn = (x * lax.rsqrt(jnp.mean(x * x, axis=-1, keepdims=True) + RMS_EPS) * ng_ref[...]).astype(BF16)

    def project(c0):
        def run():
            z_s[:, c0:c0 + CONV_CH] = _dot(xn, win_ref[:, c0:c0 + CONV_CH])
        return run

    passes = [tail_state_pass(c) for c in range(n_chunks)]
    blocks = [project(c0) for c0 in (OFF_GR, 0, CONV_CH, 2 * CONV_CH, 3 * CONV_CH)]
    side_queue = []
    late_blocks, blocks = blocks[-1:], blocks[:-1]
    while passes or blocks:
        side_queue += passes[:1] + blocks[:1]
        passes, blocks = passes[1:], blocks[1:]
    side_queue += [tail_post, tail_out_proj, tail_gate, tail_norm]

    def project_next(count=2):
        for _ in range(count):
            if side_queue:
                side_queue.pop(0)()

    z_s[:, OFF_ZS:OFF_ZS + SHIFT_W] = _dot(xn, win_ref[:, OFF_ZS:OFF_ZS + SHIFT_W])

    blocks_per_head = HEAD_DIM // STATE_ROWS
    v_row0 = lax.rem(jnp.minimum(s, n_tiles - 1), jnp.int32(blocks_per_head)) * STATE_ROWS
    _sample_state_rows(st_ref, vecs_ref, st1_ref, ys_ref, v_row0)

    zs = z_s[:, OFF_ZS:OFF_ZS + SHIFT_W]
    zsbuf[CARRY_ROWS:CARRY_ROWS + tc, :] = zs
    zprev = zsbuf[CARRY_ROWS - 1:CARRY_ROWS - 1 + tc, :]
    zm = zs + (zprev - zs) * prm["mu"]
    zsbuf[CARRY_ROWS - 1:CARRY_ROWS, :] = zsbuf[CARRY_ROWS + tc - 1:CARRY_ROWS + tc, :]

    project_next()
    r = zm[:, 0:RW]
    k = zm[:, RW:2 * RW]
    v = zm[:, 2 * RW:3 * RW]
    wa = zm[:, 3 * RW:3 * RW + 2 * LORA]
    lw, a, kk, kmod, bonus = _rwkv_tokens(r, k, v, wa, prm, headones, project_next)

    g = _const_dot(tri_ref[...], lw, CUM_PARTS)
    gc = jnp.concatenate(
        [jnp.broadcast_to(g[(c + 1) * CHUNK - 1:(c + 1) * CHUNK, :], (CHUNK, RW)) for c in range(tc // CHUNK)],
        axis=0)
    project_next()
    eng = jnp.exp(-g)
    etail = jnp.exp(gc - g)
    b = kk * a
    la_s[...] = -kk * jnp.exp(g - lw)
    lr_s[...] = r * jnp.exp(g)
    project_next()
    rb_s[...] = b * eng
    rk_s[...] = kmod * eng
    bt_s[...] = b * etail
    kt_s[...] = kmod * etail
    v_s[...] = v
    while side_queue:
        project_next()
    gam_s[...] = jnp.exp(gc)

    row = lax.broadcasted_iota(jnp.int32, (n2, n2), 0)
    col = lax.broadcasted_iota(jnp.int32, (n2, n2), 1)
    same = (row // CHUNK) == (col // CHUNK)
    strict = same & (row > col)
    incl = same & (row >= col)

    cw = prm["conv_w"]

    def conv_rows(r0):
        def run():
            rows = slice(r0, r0 + CHUNK)
            u = z_s[rows, CONV_CH:2 * CONV_CH] * z_s[rows, 2 * CONV_CH:3 * CONV_CH]
            ubuf[CARRY_ROWS + r0:CARRY_ROWS + r0 + CHUNK, :] = u
            um1 = ubuf[CARRY_ROWS - 1 + r0:CARRY_ROWS - 1 + r0 + CHUNK, :]
            um2 = ubuf[CARRY_ROWS - 2 + r0:CARRY_ROWS - 2 + r0 + CHUNK, :]
            conv = cw[0:1, :] * um2 + cw[1:2, :] * um1 + cw[2:3, :] * u
            ycat_s[rows, 0:CONV_CH] = z_s[rows, 0:CONV_CH] * conv * _silu(z_s[rows, 3 * CONV_CH:4 * CONV_CH])
        return run

    srcs = {"la": la_s, "lr": lr_s, "rb": rb_s, "rk": rk_s, "bt": bt_s, "kt": kt_s, "v": v_s}
    members = [(c, j) for c in range(n_chunks) for j in range(N_PAIRS)]

    def ld(name):
        ref = srcs[name]
        return [ref[c * CHUNK:(c + 1) * CHUNK, j * LANES:(j + 1) * LANES] for c, j in members]

    lane_bcast = [bcast_ref[j] for j in range(SUBLANES - 1)]
    gop, y0, pop, qop = _chunk_operators(ld, strict, incl, lane_bcast,
                                         late_blocks + [conv_rows(c * CHUNK) for c in range(n_chunks)])
    for (c, j), g_, y_, p_, q_ in zip(members, gop, y0, pop, qop):
        idx = c * N_PAIRS + j
        g_sc[idx] = g_.astype(BF16)
        y0_sc[idx] = y_
        p_sc[idx] = p_.astype(BF16)
        q_sc[idx] = q_

    ubuf[CARRY_ROWS - 2:CARRY_ROWS, :] = ubuf[CARRY_ROWS + tc - 2:CARRY_ROWS + tc, :]

    bonus_p[...] = bonus
    gr_p[...] = z_s[:, OFF_GR:OFF_GR + RW]

    @pl.when((head_t == n_t - 1) & (s < n_tiles))
    def _():
        shift_out_ref[0] = zsbuf[CARRY_ROWS - 1:CARRY_ROWS, :]
        conv_out_ref[0] = ubuf[CARRY_ROWS - 2:CARRY_ROWS, :]

    @pl.when((tail_t == n_t - 1) & (s > 0))
    def _():
        for h in range(N_HEADS):
            j, i = divmod(h, 2)
            blk = s_ref[j]
            wkv_out_ref[0, h] = blk[i * HEAD_DIM:(i + 1) * HEAD_DIM, i * HEAD_DIM:(i + 1) * HEAD_DIM]


def _prompt_layer(x2d, p2d, w, bsz, t_len, tc, state_hvkb, step_vecs):
    n_t = t_len // tc
    n_seq = state_hvkb.shape[-1]
    blocks_per_head = HEAD_DIM // STATE_ROWS
    assert bsz * n_t == N_HEADS * blocks_per_head, "one state block per prompt tile"
    state_blk = lambda s: jnp.minimum(s, bsz * n_t - 1)
    tok = np.arange(tc)
    blockones = (tok[:, None] // CHUNK) == (tok[None, :] // CHUNK)
    tri = jnp.asarray(blockones & (tok[:, None] >= tok[None, :]), BF16)
    n_tiles = bsz * n_t
    n2 = 2 * CHUNK
    idx = np.arange(n2)
    same_blk = (idx[:, None] // SUBLANES) == (idx[None, :] // SUBLANES)
    lane_bcast = jnp.asarray(
        np.stack([same_blk & (idx[:, None] % SUBLANES == j) for j in range(SUBLANES - 1)]), BF16)
    kern = functools.partial(_prompt_kernel, tc=tc, n_t=n_t, n_tiles=n_tiles)
    big = lambda: pltpu.VMEM((tc, RW), F32)
    n_blk = (tc // CHUNK) * N_PAIRS
    op = lambda dt: pltpu.VMEM((n_blk, LANES, LANES), dt)
    head = lambda s: jnp.minimum(s, n_tiles - 1)
    tail = lambda s: jnp.maximum(s - 1, 0)
    head_tile = lambda width: pl.BlockSpec((tc, width), lambda s: (head(s), 0))
    tail_tile = lambda width: pl.BlockSpec((tc, width), lambda s: (tail(s), 0))
    const = lambda shape: pl.BlockSpec(shape, lambda s: (0, 0), pipeline_mode=pl.Buffered(1))
    return pl.pallas_call(
        kern,
        grid=(n_tiles + 1,),
        in_specs=[head_tile(D_MODEL), tail_tile(D_MODEL), tail_tile(D_PLE), const((1, D_MODEL)),
                  const((D_MODEL, IN_W)), const((D_MODEL, D_MODEL)), const((D_MODEL, D_MODEL)),
                  const((D_PLE, D_MODEL)), const((1, D_MODEL))]
        + _prm_specs()
        + [const((HEADSUM_W, HEADSUM_W)), const((tc, tc)),
           pl.BlockSpec((SUBLANES - 1, n2, n2), lambda s: (0, 0, 0), pipeline_mode=pl.Buffered(1)),
           pl.BlockSpec((1, STATE_ROWS, HEAD_DIM, n_seq),
                        lambda s: (state_blk(s) // blocks_per_head, state_blk(s) % blocks_per_head, 0, 0)),
           pl.BlockSpec((len(_STEP_VECS), 1, HEAD_DIM, n_seq),
                        lambda s: (0, state_blk(s) // blocks_per_head, 0, 0))],
        out_specs=[
            tail_tile(D_MODEL),
            pl.BlockSpec((1, 2, CONV_CH), lambda s: (head(s) // n_t, 0, 0)),
            pl.BlockSpec((1, 1, SHIFT_W), lambda s: (head(s) // n_t, 0, 0)),
            pl.BlockSpec((1, N_HEADS, HEAD_DIM, HEAD_DIM), lambda s: (tail(s) // n_t, 0, 0, 0)),
            pl.BlockSpec((1, STATE_ROWS, HEAD_DIM, n_seq),
                         lambda s: (state_blk(s) // blocks_per_head, state_blk(s) % blocks_per_head, 0, 0)),
            pl.BlockSpec((1, STATE_ROWS, n_seq),
                         lambda s: (state_blk(s) // blocks_per_head, state_blk(s) % blocks_per_head, 0)),
        ],
        out_shape=[
            jax.ShapeDtypeStruct((bsz * t_len, D_MODEL), F32),
            jax.ShapeDtypeStruct((bsz, 2, CONV_CH), F32),
            jax.ShapeDtypeStruct((bsz, 1, SHIFT_W), F32),
            jax.ShapeDtypeStruct((bsz, N_HEADS, HEAD_DIM, HEAD_DIM), F32),
            jax.ShapeDtypeStruct(state_hvkb.shape, F32),
            jax.ShapeDtypeStruct((N_HEADS, HEAD_DIM, n_seq), F32),
        ],
        scratch_shapes=[
            pltpu.VMEM((tc, IN_W), F32),
            pltpu.VMEM((tc, D_MODEL), F32),
            pltpu.VMEM((CARRY_ROWS + tc, CONV_CH), F32),
            pltpu.VMEM((CARRY_ROWS + tc, SHIFT_W), F32),
            pltpu.VMEM((N_PAIRS, LANES, LANES), F32),
            big(), big(), big(), big(), big(), big(), big(), big(), big(),
            op(BF16), op(F32), op(BF16), op(F32),
            big(), big(),
        ],
        compiler_params=pltpu.CompilerParams(
            dimension_semantics=("arbitrary",), vmem_limit_bytes=VMEM_LIMIT),
        name="prompt_layer",
    )(x2d, x2d, p2d, w["norm_g"].reshape(1, D_MODEL), w["w_in"], w["w_out"], w["w_pg"], w["w_pp"],
      w["final_g"].reshape(1, D_MODEL), *w["prm"], w["headones"], tri, lane_bcast, state_hvkb, step_vecs)


def _sample_front_kernel(x_ref, ng_ref, win_ref, wo_ref, wg_ref, wp_ref, cb_ref, sb_ref, *refs):
    prm_refs = refs[:len(_PRM_NAMES)]
    headones_ref = refs[len(_PRM_NAMES)]
    (ya_ref, conv_out_ref, shift_out_ref, gr_ref, bonus_ref, vecs_ref,
     wbf_ref, wo_bf_ref, wg_bf_ref, wp_bf_ref, xn_s, z_s) = refs[len(_PRM_NAMES) + 1:]
    j = pl.program_id(0)

    @pl.when(j == 0)
    def _():
        x = x_ref[:, 0, :]
        xn_s[...] = (x * lax.rsqrt(jnp.mean(x * x, axis=-1, keepdims=True) + RMS_EPS) * ng_ref[...]).astype(BF16)
        wo_bf_ref[...] = wo_ref[...].astype(BF16)
        wg_bf_ref[...] = wg_ref[...].astype(BF16)
        wp_bf_ref[...] = wp_ref[...].astype(BF16)

    w_blk = win_ref[...].astype(BF16)
    wbf_ref[...] = w_blk
    z_s[j] = _dot(xn_s[...], w_blk)

    @pl.when(j == pl.num_programs(0) - 1)
    def _():
        _sample_front_tokens(z_s, cb_ref, sb_ref, prm_refs, headones_ref, ya_ref, conv_out_ref, shift_out_ref,
                             gr_ref, bonus_ref, vecs_ref)


def _sample_front_tokens(z_s, cb_ref, sb_ref, prm_refs, headones_ref, ya_ref, conv_out_ref, shift_out_ref,
                         gr_ref, bonus_ref, vecs_ref):
    prm = _load_prm(prm_refs)
    headones = headones_ref[...]
    z = jnp.concatenate([z_s[i] for i in range(z_s.shape[0])], axis=1)

    u = z[:, CONV_CH:2 * CONV_CH] * z[:, 2 * CONV_CH:3 * CONV_CH]
    cb0 = cb_ref[:, 0, :]
    cb1 = cb_ref[:, 1, :]
    cw = prm["conv_w"]
    conv = cw[0:1, :] * cb0 + cw[1:2, :] * cb1 + cw[2:3, :] * u
    ya_ref[...] = z[:, 0:CONV_CH] * conv * _silu(z[:, 3 * CONV_CH:4 * CONV_CH])
    conv_out_ref[:, 0, :] = cb1
    conv_out_ref[:, 1, :] = u

    zs = z[:, OFF_ZS:OFF_ZS + SHIFT_W]
    shift_out_ref[...] = zs
    gr_ref[...] = z[:, OFF_GR:OFF_GR + RW]
    zm = zs + (sb_ref[...] - zs) * prm["mu"]
    r = zm[:, 0:RW]
    k = zm[:, RW:2 * RW]
    v = zm[:, 2 * RW:3 * RW]
    wa = zm[:, 3 * RW:3 * RW + 2 * LORA]
    lw, a, kk, kmod, bonus = _rwkv_tokens(r, k, v, wa, prm, headones)
    bonus_ref[...] = bonus
    for i, vec in enumerate((-kk, jnp.exp(lw), kk * a, kmod, v, r)):
        vecs_ref[i] = vec.T


_STEP_VECS = ("nkk", "decay", "b", "kmod", "v", "r")
STATE_ROWS = 8


def _sample_state_rows(st_ref, vecs_ref, s1_ref, y_ref, v_row0):
    nkk, w, b, km = (vecs_ref[i, 0] for i in range(4))
    r = vecs_ref[5, 0]
    for vi in range(STATE_ROWS):
        s0 = st_ref[0, vi]
        sa = jnp.sum(s0 * nkk, axis=0, keepdims=True)
        v_row = vecs_ref[4, 0, pl.ds(v_row0 + vi, 1), :]
        s1 = s0 * w + sa * b + v_row * km
        s1_ref[0, vi] = s1
        y_ref[0, vi:vi + 1, :] = jnp.sum(s1 * r, axis=0, keepdims=True)


def _sample_back_kernel(o_ref, bonus_ref, gr_ref, ya_ref, x_ref, p_ref, wo_ref, wg_ref, wp_ref, fg_ref, *refs):
    prm_refs = refs[:len(_PRM_NAMES)]
    headones_ref = refs[len(_PRM_NAMES)]
    y_ref = refs[len(_PRM_NAMES) + 1]
    prm = _load_prm(prm_refs)
    wo = wo_ref[...]
    wg = wg_ref[...]
    wp = wp_ref[...]
    y_r = _rwkv_post(o_ref[...].T, bonus_ref[...], gr_ref[...], prm, headones_ref[...])
    ycat = jnp.concatenate([ya_ref[...], y_r], axis=1).astype(BF16)
    h1 = x_ref[:, 0, :] + _dot(ycat, wo)
    gate = _sigmoid(_dot(h1.astype(BF16), wg))
    pe = _dot(p_ref[:, 0, :].astype(BF16), wp)
    h2 = h1 + gate * pe
    ms = jnp.mean(h2 * h2, axis=-1, keepdims=True)
    y_ref[:, 0, :] = h2 * lax.rsqrt(ms + RMS_EPS) * fg_ref[...]


def _full(shape):
    nd = len(shape)
    return pl.BlockSpec(shape, lambda *_: (0,) * nd)


_SAMPLE_PARAMS = pltpu.CompilerParams(dimension_semantics=("arbitrary",), vmem_limit_bytes=VMEM_LIMIT)
_SQ, _PP = (D_MODEL, D_MODEL), (D_PLE, D_MODEL)


def _sample_front(x3d, state_conv, state_shift, w):
    n = x3d.shape[0]
    row = jax.ShapeDtypeStruct((n, RW), F32)
    wblk = IN_W // W_IN_BLOCKS
    once = lambda shape: pl.BlockSpec(shape, lambda j: (0,) * len(shape), pipeline_mode=pl.Buffered(1))
    outs = pl.pallas_call(
        _sample_front_kernel,
        grid=(W_IN_BLOCKS,),
        in_specs=[_full((n, 1, D_MODEL)), _full((1, D_MODEL)), pl.BlockSpec((D_MODEL, wblk), lambda j: (0, j)),
                  once(_SQ), once(_SQ), once(_PP), _full((n, 2, CONV_CH)), _full((n, SHIFT_W))]
        + _prm_specs() + [_full((HEADSUM_W, HEADSUM_W))],
        out_specs=[_full((n, CONV_CH)), _full((n, 2, CONV_CH)), _full((n, SHIFT_W)),
                   _full((n, RW)), _full((n, RW)), _full((len(_STEP_VECS), RW, n)),
                   pl.BlockSpec((D_MODEL, wblk), lambda j: (0, j)), once(_SQ), once(_SQ), once(_PP)],
        out_shape=[jax.ShapeDtypeStruct((n, CONV_CH), F32), jax.ShapeDtypeStruct((n, 2, CONV_CH), F32),
                   jax.ShapeDtypeStruct((n, SHIFT_W), F32), row, row,
                   jax.ShapeDtypeStruct((len(_STEP_VECS), RW, n), F32),
                   jax.ShapeDtypeStruct((D_MODEL, IN_W), BF16), jax.ShapeDtypeStruct(_SQ, BF16),
                   jax.ShapeDtypeStruct(_SQ, BF16), jax.ShapeDtypeStruct(_PP, BF16)],
        scratch_shapes=[pltpu.VMEM((n, D_MODEL), BF16), pltpu.VMEM((W_IN_BLOCKS, n, wblk), F32)],
        compiler_params=_SAMPLE_PARAMS,
        name="sample_front",
    )(x3d, w["norm_g"].reshape(1, D_MODEL), w["w_in_f32"], w["w_out_f32"], w["w_pg_f32"], w["w_pp_f32"],
      state_conv, state_shift, *w["prm"], w["headones"])
    ya, conv_new, shift_new, g_r, bonus, vecs, w_in, w_out, w_pg, w_pp = outs
    rounded = {"w_in": w_in, "w_out": w_out, "w_pg": w_pg, "w_pp": w_pp}
    return (ya, g_r, bonus), conv_new, shift_new, vecs.reshape(len(_STEP_VECS), N_HEADS, HEAD_DIM, n), rounded


def _sample_back(y_t, token_parts, x3d, p3d, w):
    n = x3d.shape[0]
    ya, g_r, bonus = token_parts
    return pl.pallas_call(
        _sample_back_kernel,
        grid=(1,),
        in_specs=[_full((RW, n)), _full((n, RW)), _full((n, RW)), _full((n, CONV_CH)),
                  _full((n, 1, D_MODEL)), _full((n, 1, D_PLE)), _full(_SQ), _full(_SQ), _full(_PP),
                  _full((1, D_MODEL))]
        + _prm_specs() + [_full((HEADSUM_W, HEADSUM_W))],
        out_specs=_full((n, 1, D_MODEL)),
        out_shape=jax.ShapeDtypeStruct((n, 1, D_MODEL), F32),
        compiler_params=_SAMPLE_PARAMS,
        name="sample_back",
    )(y_t.reshape(RW, n), bonus, g_r, ya, x3d, p3d, w["w_out"], w["w_pg"], w["w_pp"],
      w["final_g"].reshape(1, D_MODEL), *w["prm"], w["headones"])


def _layer(x_prompt, p_prompt, x_sample, p_sample, state_conv, state_shift, state_wkv, w, tc=256):
    bsz, t_len, _ = x_prompt.shape
    token_parts, conv_s, shift_s, vecs, rounded = _sample_front(x_sample, state_conv, state_shift, w)
    w = {**w, **rounded}
    s_hvkb = jnp.transpose(state_wkv, (1, 2, 3, 0))
    y_p, conv_p, shift_p, wkv_p, s1, y_t = _prompt_layer(
        x_prompt.reshape(bsz * t_len, D_MODEL), p_prompt.reshape(bsz * t_len, D_PLE), w, bsz, t_len, tc,
        s_hvkb, vecs)
    y_s = _sample_back(y_t, token_parts, x_sample, p_sample, w)
    prompt_out = (y_p.reshape(bsz, t_len, D_MODEL), conv_p, shift_p.reshape(bsz, SHIFT_W), wkv_p)
    sample_out = (y_s, conv_s, shift_s, jnp.transpose(s1, (3, 0, 1, 2)))
    return prompt_out, sample_out


def kernel(x_prompt, x_sample, p_prompt, p_sample, state_conv, state_shift, state_wkv, norm_g, w_in, conv_w, mu_shift, w0, w_up, a0, a_up, k_k, k_a, r_k, ln_w, ln_b, w_out, w_pg, w_pp, final_g):
    depth = norm_g.shape[0]
    assert depth == 1
    i = 0
    head_id = np.arange(HEADSUM_W) // HEAD_DIM
    w = {
        "norm_g": norm_g[i],
        "w_in_f32": w_in[i],
        "w_out_f32": w_out[i],
        "w_pg_f32": w_pg[i],
        "w_pp_f32": w_pp[i],
        "final_g": final_g,
        "prm": _prm_arrays(conv_w[i], mu_shift[i], w0[i], w_up[i], a0[i], a_up[i], k_k[i], k_a[i],
                           r_k[i].reshape(RW), ln_w[i], ln_b[i]),
        "headones": jnp.asarray(head_id[:, None] == head_id[None, :], BF16),
    }
    (yp, cp, sp, wp), (ys, cs, ss, ws) = _layer(x_prompt, p_prompt[i], x_sample, p_sample[i], state_conv[i],
                                                state_shift[i], state_wkv[i], w)
    return (yp, ys, cp[None], sp[None], wp[None], cs[None], ss[None], ws[None])
```

```python
import functools
import math

import jax
import jax.numpy as jnp
import numpy as np
from jax import lax
from jax.experimental import pallas as pl
from jax.experimental.pallas import tpu as pltpu

F32 = jnp.float32
BF16 = jnp.bfloat16

D_MODEL = 1024
CONV_CH = 512
RW = 512
HEAD_DIM = 64
N_HEADS = 8
N_PAIRS = N_HEADS // 2
LORA = 64
D_PLE = 256
SHIFT_W = 3 * RW + 2 * LORA
IN_W = 4 * CONV_CH + SHIFT_W + RW
OFF_ZS = 4 * CONV_CH
OFF_GR = OFF_ZS + SHIFT_W
RMS_EPS = 1e-6
GN_EPS = 64e-5
DECAY_SCALE = math.exp(-0.5)

LANES = 128
SUBLANES = 8
MXU_DIM = 256
HEADSUM_W = MXU_DIM
SUM_PARTS = 1
CUM_PARTS = 1
LORA_PASSES = 1
CHUNK = 64
W_IN_BLOCKS = 3
CARRY_ROWS = 8
VMEM_LIMIT = 56 * 1024 * 1024


def _dot(a, b):
    return jnp.dot(a, b, preferred_element_type=F32)


def _dot_nt(a, b):
    return lax.dot_general(a, b, (((1,), (1,)), ((), ())), preferred_element_type=F32)


def _dot_tn(a, b):
    return lax.dot_general(a, b, (((0,), (0,)), ((), ())), preferred_element_type=F32)


def _split(x, parts):
    out = []
    rem = x
    for i in range(parts):
        t = rem.astype(BF16)
        out.append(t)
        if i + 1 < parts:
            rem = rem - t.astype(F32)
    return out


def _const_dot(c_bf16, x, parts):
    acc = None
    for t in _split(x, parts):
        d = _dot(c_bf16, t)
        acc = d if acc is None else acc + d
    return acc


def _x_dot_const(x, c_bf16, parts):
    acc = None
    for t in _split(x, parts):
        d = _dot(t, c_bf16)
        acc = d if acc is None else acc + d
    return acc


def _head_sum(x, headones):
    w = headones.shape[0]
    halves = [_x_dot_const(x[:, i * w:(i + 1) * w], headones, SUM_PARTS) for i in range(x.shape[1] // w)]
    return jnp.concatenate(halves, axis=1)


def _mm(a, b, passes):
    if passes == 1:
        return _dot(a.astype(BF16), b.astype(BF16))
    ah, al = _split(a, 2)
    bh, bl = _split(b, 2)
    return _dot(ah, bh) + _dot(al, bh) + _dot(ah, bl)


def _sigmoid(x):
    return 1.0 / (1.0 + jnp.exp(-x))


def _silu(x):
    return x * _sigmoid(x)


def _rwkv_tokens(r, k, v, wa, prm, headones, between_steps=lambda: None):
    n = r.shape[0]
    lane = lax.broadcasted_iota(jnp.int32, (n, LANES), 1)
    th = jnp.where(lane < LORA, jnp.tanh(wa), wa)
    lora = _mm(th, prm["wlora"], LORA_PASSES)
    lw = -DECAY_SCALE * _sigmoid(prm["w0"] + lora[:, :RW])
    a = _sigmoid(prm["a0"] + lora[:, RW:])
    between_steps()
    kk = k * prm["k_k"]
    ss = _head_sum(kk * kk, headones)
    kk = kk / jnp.maximum(jnp.sqrt(ss), 1e-12)
    between_steps()
    kmod = k * (1.0 + (a - 1.0) * prm["k_a"])
    bonus = _head_sum(r * kmod * prm["r_k"], headones) * v
    between_steps()
    return lw, a, kk, kmod, bonus


def _rwkv_post(o, bonus, g_r, prm, headones):
    mean = _head_sum(o, headones) * (1.0 / HEAD_DIM)
    d = o - mean
    var = _head_sum(d * d, headones) * (1.0 / HEAD_DIM)
    on = d * lax.rsqrt(var + GN_EPS)
    on = on * prm["ln_w"] + prm["ln_b"]
    return (on + bonus) * _silu(g_r)


_PRM_NAMES = ("conv_w", "mu", "w0", "a0", "w_up", "a_up", "k_k", "k_a", "r_k", "ln_w", "ln_b")


def _prm_arrays(conv_w, mu_shift, w0, w_up, a0, a_up, k_k, k_a, r_k, ln_w, ln_b):
    row = lambda x: x.reshape(1, -1)
    return (conv_w, row(mu_shift), row(w0), row(a0), w_up, a_up, row(k_k), row(k_a),
            row(r_k), row(ln_w), row(ln_b))


def _prm_specs():
    zero = (lambda *_: (0, 0))
    shapes = ((3, CONV_CH), (1, SHIFT_W), (1, RW), (1, RW), (LORA, RW), (LORA, RW),
              (1, RW), (1, RW), (1, RW), (1, RW), (1, RW))
    return [pl.BlockSpec(s, zero) for s in shapes]


def _load_prm(refs):
    prm = {n: r[...] for n, r in zip(_PRM_NAMES, refs)}
    zero = jnp.zeros((LORA, RW), F32)
    prm["wlora"] = jnp.concatenate([jnp.concatenate([prm["w_up"], zero], axis=1),
                                    jnp.concatenate([zero, prm["a_up"]], axis=1)], axis=0)
    return prm


def _stack_heads(x):
    lane = lax.broadcasted_iota(jnp.int32, x.shape, 1)
    lo = jnp.where(lane < HEAD_DIM, x, 0.0)
    hi = jnp.where(lane >= HEAD_DIM, x, 0.0)
    return jnp.concatenate([lo, hi], axis=0)


def _stack_heads_bf16(x):
    lane = lax.broadcasted_iota(jnp.int32, x.shape, 1)
    lo = jnp.where(lane < HEAD_DIM, 1.0, 0.0).astype(BF16)
    xb = x.astype(BF16)
    return jnp.concatenate([xb * lo, xb * (1.0 - lo)], axis=0)


def _bdot(a, b):
    return _dot(a.astype(BF16), b.astype(BF16))


def _diag_block_inverses_minus_eye(mats, lane_bcast):
    n = mats[0].shape[0]
    nb = n // SUBLANES
    row = lax.broadcasted_iota(jnp.int32, (n, n), 0)
    col = lax.broadcasted_iota(jnp.int32, (n, n), 1)
    in_diag = ((row // SUBLANES) == (col // SUBLANES)) & (row > col)
    packed = []
    for a in mats:
        d = jnp.where(in_diag, a, 0.0)
        acc = d[0:SUBLANES]
        for m in range(1, nb):
            acc = acc + d[m * SUBLANES:(m + 1) * SUBLANES]
        packed.append(acc)
    d_all = jnp.concatenate(packed, axis=0).astype(BF16)
    sub = lax.broadcasted_iota(jnp.int32, (SUBLANES, n), 0)
    lane = lax.broadcasted_iota(jnp.int32, (SUBLANES, n), 1)
    eye8 = jnp.where(sub == lane % SUBLANES, 1.0, 0.0)
    ts = [eye8 for _ in mats]
    for j in range(SUBLANES - 1):
        spread = _dot(d_all, lane_bcast[j])
        ts = [t + spread[i * SUBLANES:(i + 1) * SUBLANES] * jnp.broadcast_to(t[j:j + 1, :], (SUBLANES, n))
              for i, t in enumerate(ts)]
    blk = lax.broadcasted_iota(jnp.int32, (SUBLANES, n), 1) // SUBLANES
    out = []
    for t in ts:
        tm1 = t - eye8
        out.append(jnp.concatenate([jnp.where(blk == m, tm1, 0.0) for m in range(nb)], axis=0))
    return out


def _unit_lower_inverse_minus_eye(mats, lane_bcast, between_levels=lambda: None):
    n = mats[0].shape[0]
    row = lax.broadcasted_iota(jnp.int32, (n, n), 0)
    col = lax.broadcasted_iota(jnp.int32, (n, n), 1)
    xs = _diag_block_inverses_minus_eye(mats, lane_bcast)
    size = 2 * SUBLANES
    while size <= CHUNK:
        half = size // 2
        sel = ((row // size) == (col // size)) & ((row % size) >= half) & ((col % size) < half)
        als = [jnp.where(sel, a, 0.0) for a in mats]
        ps = [al + _bdot(x, al) for x, al in zip(xs, als)]
        xs = [x + p + _bdot(p, x) for x, p in zip(xs, ps)]
        between_levels()
        size *= 2
    return xs


def _chunk_operators(ld, strict, incl, lane_bcast, side_work=()):
    side_work = list(side_work)

    def breathe():
        if side_work:
            side_work.pop(0)()

    cat0 = lambda xs: jnp.concatenate(xs, axis=0)
    la = [_stack_heads(x) for x in ld("la")]
    lr = [_stack_heads(x) for x in ld("lr")]
    n2 = 2 * CHUNK
    amat = [_dot_nt(cat0([a, b]).astype(BF16), cat0([c, d]).astype(BF16))
            for a, b, c, d in zip(la, lr, ld("rb"), ld("rk"))]

    def block_diag_pair(m):
        swapped = pltpu.roll(m, CHUNK, axis=1)
        first = cat0([m[0:CHUNK], swapped[CHUNK:n2]])
        second = cat0([swapped[0:CHUNK], m[CHUNK:n2]])
        return first, second

    breathe()
    top = [block_diag_pair(m[0:n2]) for m in amat]
    a_ab = [jnp.where(strict, t[0], 0.0) for t in top]
    xs = _unit_lower_inverse_minus_eye(a_ab, lane_bcast, breathe)
    vst = [_stack_heads_bf16(x) for x in ld("v")]
    av = [_dot(jnp.where(strict, t[1], 0.0).astype(BF16), v) for t, v in zip(top, vst)]
    breathe()
    uv0 = [y + _bdot(x, y) for x, y in zip(xs, av)]
    tla = [y + _bdot(x, y) for x, y in zip(xs, la)]
    breathe()
    bottom = [block_diag_pair(m[n2:2 * n2]) for m in amat]
    a_rb = [jnp.where(incl, t[0], 0.0) for t in bottom]
    a_rk = [jnp.where(incl, t[1], 0.0) for t in bottom]
    gop = [y + _bdot(a, t) for y, a, t in zip(lr, a_rb, tla)]
    uv = [cat0([u.astype(BF16), v]) for u, v in zip(uv0, vst)]
    y0 = [_dot(jnp.concatenate([a, b], axis=1).astype(BF16), w) for a, b, w in zip(a_rb, a_rk, uv)]
    breathe()
    bt = [_stack_heads_bf16(x) for x in ld("bt")]
    kt = [_stack_heads_bf16(x) for x in ld("kt")]
    pop = [_dot_tn(t.astype(BF16), b) for t, b in zip(tla, bt)]
    qop = [_dot_tn(w, cat0([b, k])) for w, b, k in zip(uv, bt, kt)]
    while side_work:
        breathe()
    return gop, y0, pop, qop


def _prompt_kernel(xh_ref, xt_ref, pt_ref, ng_ref, win_ref, wo_ref, wg_ref, wp_ref, fg_ref, *refs,
                   tc, n_t, n_tiles):
    prm_refs = refs[:len(_PRM_NAMES)]
    headones_ref, tri_ref, bcast_ref, st_ref, vecs_ref = refs[len(_PRM_NAMES):len(_PRM_NAMES) + 5]
    (y_ref, conv_out_ref, shift_out_ref, wkv_out_ref, st1_ref, ys_ref) = refs[len(_PRM_NAMES) + 5:len(_PRM_NAMES) + 11]
    (z_s, ycat_s, ubuf, zsbuf, s_ref, la_s, lr_s, rb_s, rk_s, bt_s, kt_s, v_s, gam_s, o_s,
     g_sc, y0_sc, p_sc, q_sc, bonus_p, gr_p) = refs[len(_PRM_NAMES) + 11:]

    s = pl.program_id(0)
    head_t = lax.rem(s, jnp.int32(n_t))
    tail_t = lax.rem(s + (n_t - 1), jnp.int32(n_t))
    prm = _load_prm(prm_refs)
    headones = headones_ref[...]
    n2 = 2 * CHUNK
    n_chunks = tc // CHUNK

    @pl.when(s == 0)
    def _():
        for ref in (g_sc, y0_sc, p_sc, q_sc, gam_s, bonus_p, gr_p, ycat_s, s_ref):
            ref[...] = jnp.zeros(ref.shape, ref.dtype)

    @pl.when(head_t == 0)
    def _():
        ubuf[0:CARRY_ROWS, :] = jnp.zeros((CARRY_ROWS, CONV_CH), F32)
        zsbuf[0:CARRY_ROWS, :] = jnp.zeros((CARRY_ROWS, SHIFT_W), F32)

    @pl.when(tail_t == 0)
    def _():
        s_ref[...] = jnp.zeros(s_ref.shape, F32)

    tail = {}

    def tail_state_pass(c):
        def run():
            for j in range(N_PAIRS):
                idx = c * N_PAIRS + j
                lanes = slice(j * LANES, (j + 1) * LANES)
                s_old = s_ref[j]
                s_bf = s_old.astype(BF16)
                yst = _dot_nt(g_sc[idx], s_bf) + y0_sc[idx]
                o_s[c * CHUNK:(c + 1) * CHUNK, lanes] = yst[0:CHUNK] + yst[CHUNK:n2]
                gam = gam_s[c * CHUNK:c * CHUNK + 1, lanes]
                s_ref[j] = s_old * gam + _dot(s_bf, p_sc[idx]) + q_sc[idx]
        return run

    def tail_post():
        ycat_s[:, CONV_CH:2 * CONV_CH] = _rwkv_post(o_s[...], bonus_p[...], gr_p[...], prm, headones)

    def tail_out_proj():
        tail["h1"] = xt_ref[...] + _dot(ycat_s[...].astype(BF16), wo_ref[...])

    def tail_gate():
        tail["gate"] = _sigmoid(_dot(tail["h1"].astype(BF16), wg_ref[...]))
        tail["pe"] = _dot(pt_ref[...].astype(BF16), wp_ref[...])

    def tail_norm():
        h2 = tail["h1"] + tail["gate"] * tail["pe"]
        ms = jnp.mean(h2 * h2, axis=-1, keepdims=True)
        y_ref[...] = h2 * lax.rsqrt(ms + RMS_EPS) * fg_ref[...]

    x = xh_ref[...]
    xn = (x * lax.rsqrt(jnp.mean(x * x, axis=-1, keepdims=True) + RMS_EPS) * ng_ref[...]).astype(BF16)

    def project(c0):
        def run():
            z_s[:, c0:c0 + CONV_CH] = _dot(xn, win_ref[:, c0:c0 + CONV_CH])
        return run

    passes = [tail_state_pass(c) for c in range(n_chunks)]
    blocks = [project(c0) for c0 in (OFF_GR, 0, CONV_CH, 2 * CONV_CH, 3 * CONV_CH)]
    side_queue = []
    late_blocks, blocks = blocks[-1:], blocks[:-1]
    while passes or blocks:
        side_queue += passes[:1] + blocks[:1]
        passes, blocks = passes[1:], blocks[1:]
    side_queue += [tail_post, tail_out_proj, tail_gate, tail_norm]

    def project_next(count=2):
        for _ in range(count):
            if side_queue:
                side_queue.pop(0)()

    z_s[:, OFF_ZS:OFF_ZS + SHIFT_W] = _dot(xn, win_ref[:, OFF_ZS:OFF_ZS + SHIFT_W])

    blocks_per_head = HEAD_DIM // STATE_ROWS
    v_row0 = lax.rem(jnp.minimum(s, n_tiles - 1), jnp.int32(blocks_per_head)) * STATE_ROWS
    _sample_state_rows(st_ref, vecs_ref, st1_ref, ys_ref, v_row0)

    zs = z_s[:, OFF_ZS:OFF_ZS + SHIFT_W]
    zsbuf[CARRY_ROWS:CARRY_ROWS + tc, :] = zs
    zprev = zsbuf[CARRY_ROWS - 1:CARRY_ROWS - 1 + tc, :]
    zm = zs + (zprev - zs) * prm["mu"]
    zsbuf[CARRY_ROWS - 1:CARRY_ROWS, :] = zsbuf[CARRY_ROWS + tc - 1:CARRY_ROWS + tc, :]

    project_next()
    r = zm[:, 0:RW]
    k = zm[:, RW:2 * RW]
    v = zm[:, 2 * RW:3 * RW]
    wa = zm[:, 3 * RW:3 * RW + 2 * LORA]
    lw, a, kk, kmod, bonus = _rwkv_tokens(r, k, v, wa, prm, headones, project_next)

    g = _const_dot(tri_ref[...], lw, CUM_PARTS)
    gc = jnp.concatenate(
        [jnp.broadcast_to(g[(c + 1) * CHUNK - 1:(c + 1) * CHUNK, :], (CHUNK, RW)) for c in range(tc // CHUNK)],
        axis=0)
    project_next()
    eng = jnp.exp(-g)
    etail = jnp.exp(gc - g)
    b = kk * a
    la_s[...] = -kk * jnp.exp(g - lw)
    lr_s[...] = r * jnp.exp(g)
    project_next()
    rb_s[...] = b * eng
    rk_s[...] = kmod * eng
    bt_s[...] = b * etail
    kt_s[...] = kmod * etail
    v_s[...] = v
    while side_queue:
        project_next()
    gam_s[...] = jnp.exp(gc)

    row = lax.broadcasted_iota(jnp.int32, (n2, n2), 0)
    col = lax.broadcasted_iota(jnp.int32, (n2, n2), 1)
    same = (row // CHUNK) == (col // CHUNK)
    strict = same & (row > col)
    incl = same & (row >= col)

    cw = prm["conv_w"]

    def conv_rows(r0):
        def run():
            rows = slice(r0, r0 + CHUNK)
            u = z_s[rows, CONV_CH:2 * CONV_CH] * z_s[rows, 2 * CONV_CH:3 * CONV_CH]
            ubuf[CARRY_ROWS + r0:CARRY_ROWS + r0 + CHUNK, :] = u
            um1 = ubuf[CARRY_ROWS - 1 + r0:CARRY_ROWS - 1 + r0 + CHUNK, :]
            um2 = ubuf[CARRY_ROWS - 2 + r0:CARRY_ROWS - 2 + r0 + CHUNK, :]
            conv = cw[0:1, :] * um2 + cw[1:2, :] * um1 + cw[2:3, :] * u
            ycat_s[rows, 0:CONV_CH] = z_s[rows, 0:CONV_CH] * conv * _silu(z_s[rows, 3 * CONV_CH:4 * CONV_CH])
        return run

    srcs = {"la": la_s, "lr": lr_s, "rb": rb_s, "rk": rk_s, "bt": bt_s, "kt": kt_s, "v": v_s}
    members = [(c, j) for c in range(n_chunks) for j in range(N_PAIRS)]

    def ld(name):
        ref = srcs[name]
        return [ref[c * CHUNK:(c + 1) * CHUNK, j * LANES:(j + 1) * LANES] for c, j in members]

    lane_bcast = [bcast_ref[j] for j in range(SUBLANES - 1)]
    gop, y0, pop, qop = _chunk_operators(ld, strict, incl, lane_bcast,
                                         late_blocks + [conv_rows(c * CHUNK) for c in range(n_chunks)])
    for (c, j), g_, y_, p_, q_ in zip(members, gop, y0, pop, qop):
        idx = c * N_PAIRS + j
        g_sc[idx] = g_.astype(BF16)
        y0_sc[idx] = y_
        p_sc[idx] = p_.astype(BF16)
        q_sc[idx] = q_

    ubuf[CARRY_ROWS - 2:CARRY_ROWS, :] = ubuf[CARRY_ROWS + tc - 2:CARRY_ROWS + tc, :]

    bonus_p[...] = bonus
    gr_p[...] = z_s[:, OFF_GR:OFF_GR + RW]

    @pl.when((head_t == n_t - 1) & (s < n_tiles))
    def _():
        seq = lax.div(s, jnp.int32(n_t))
        shift_out_ref[pl.ds(seq, 1), :] = zsbuf[CARRY_ROWS - 1:CARRY_ROWS, :]
        conv_out_ref[0] = ubuf[CARRY_ROWS - 2:CARRY_ROWS, :]

    @pl.when((tail_t == n_t - 1) & (s > 0))
    def _():
        for h in range(N_HEADS):
            j, i = divmod(h, 2)
            blk = s_ref[j]
            wkv_out_ref[0, h] = blk[i * HEAD_DIM:(i + 1) * HEAD_DIM, i * HEAD_DIM:(i + 1) * HEAD_DIM]


def _prompt_layer(x2d, p2d, w, bsz, t_len, tc, state_hvkb, step_vecs):
    n_t = t_len // tc
    n_seq = state_hvkb.shape[-1]
    blocks_per_head = HEAD_DIM // STATE_ROWS
    assert bsz * n_t == N_HEADS * blocks_per_head, "one state block per prompt tile"
    state_blk = lambda s: jnp.minimum(s, bsz * n_t - 1)
    tok = np.arange(tc)
    blockones = (tok[:, None] // CHUNK) == (tok[None, :] // CHUNK)
    tri = jnp.asarray(blockones & (tok[:, None] >= tok[None, :]), BF16)
    n_tiles = bsz * n_t
    n2 = 2 * CHUNK
    idx = np.arange(n2)
    same_blk = (idx[:, None] // SUBLANES) == (idx[None, :] // SUBLANES)
    lane_bcast = jnp.asarray(
        np.stack([same_blk & (idx[:, None] % SUBLANES == j) for j in range(SUBLANES - 1)]), BF16)
    kern = functools.partial(_prompt_kernel, tc=tc, n_t=n_t, n_tiles=n_tiles)
    big = lambda: pltpu.VMEM((tc, RW), F32)
    n_blk = (tc // CHUNK) * N_PAIRS
    op = lambda dt: pltpu.VMEM((n_blk, LANES, LANES), dt)
    head = lambda s: jnp.minimum(s, n_tiles - 1)
    tail = lambda s: jnp.maximum(s - 1, 0)
    head_tile = lambda width: pl.BlockSpec((tc, width), lambda s: (head(s), 0))
    tail_tile = lambda width: pl.BlockSpec((tc, width), lambda s: (tail(s), 0))
    const = lambda shape: pl.BlockSpec(shape, lambda s: (0, 0), pipeline_mode=pl.Buffered(1))
    return pl.pallas_call(
        kern,
        grid=(n_tiles + 1,),
        in_specs=[head_tile(D_MODEL), tail_tile(D_MODEL), tail_tile(D_PLE), const((1, D_MODEL)),
                  const((D_MODEL, IN_W)), const((D_MODEL, D_MODEL)), const((D_MODEL, D_MODEL)),
                  const((D_PLE, D_MODEL)), const((1, D_MODEL))]
        + _prm_specs()
        + [const((HEADSUM_W, HEADSUM_W)), const((tc, tc)),
           pl.BlockSpec((SUBLANES - 1, n2, n2), lambda s: (0, 0, 0), pipeline_mode=pl.Buffered(1)),
           pl.BlockSpec((1, STATE_ROWS, HEAD_DIM, n_seq),
                        lambda s: (state_blk(s) // blocks_per_head, state_blk(s) % blocks_per_head, 0, 0)),
           pl.BlockSpec((len(_STEP_VECS), 1, HEAD_DIM, n_seq),
                        lambda s: (0, state_blk(s) // blocks_per_head, 0, 0))],
        out_specs=[
            tail_tile(D_MODEL),
            pl.BlockSpec((1, 2, CONV_CH), lambda s: (head(s) // n_t, 0, 0)),
            pl.BlockSpec((bsz, SHIFT_W), lambda s: (0, 0)),
            pl.BlockSpec((1, N_HEADS, HEAD_DIM, HEAD_DIM), lambda s: (tail(s) // n_t, 0, 0, 0)),
            pl.BlockSpec((1, STATE_ROWS, HEAD_DIM, n_seq),
                         lambda s: (state_blk(s) // blocks_per_head, state_blk(s) % blocks_per_head, 0, 0)),
            pl.BlockSpec((1, STATE_ROWS, n_seq),
                         lambda s: (state_blk(s) // blocks_per_head, state_blk(s) % blocks_per_head, 0)),
        ],
        out_shape=[
            jax.ShapeDtypeStruct((bsz * t_len, D_MODEL), F32),
            jax.ShapeDtypeStruct((bsz, 2, CONV_CH), F32),
            jax.ShapeDtypeStruct((bsz, SHIFT_W), F32),
            jax.ShapeDtypeStruct((bsz, N_HEADS, HEAD_DIM, HEAD_DIM), F32),
            jax.ShapeDtypeStruct(state_hvkb.shape, F32),
            jax.ShapeDtypeStruct((N_HEADS, HEAD_DIM, n_seq), F32),
        ],
        scratch_shapes=[
            pltpu.VMEM((tc, IN_W), F32),
            pltpu.VMEM((tc, D_MODEL), F32),
            pltpu.VMEM((CARRY_ROWS + tc, CONV_CH), F32),
            pltpu.VMEM((CARRY_ROWS + tc, SHIFT_W), F32),
            pltpu.VMEM((N_PAIRS, LANES, LANES), F32),
            big(), big(), big(), big(), big(), big(), big(), big(), big(),
            op(BF16), op(F32), op(BF16), op(F32),
            big(), big(),
        ],
        compiler_params=pltpu.CompilerParams(
            dimension_semantics=("arbitrary",), vmem_limit_bytes=VMEM_LIMIT),
        name="prompt_layer",
    )(x2d, x2d, p2d, w["norm_g"].reshape(1, D_MODEL), w["w_in"], w["w_out"], w["w_pg"], w["w_pp"],
      w["final_g"].reshape(1, D_MODEL), *w["prm"], w["headones"], tri, lane_bcast, state_hvkb, step_vecs)


def _sample_front_kernel(x_ref, ng_ref, win_ref, wo_ref, wg_ref, wp_ref, cb_ref, sb_ref, *refs):
    prm_refs = refs[:len(_PRM_NAMES)]
    headones_ref = refs[len(_PRM_NAMES)]
    (ya_ref, conv_out_ref, shift_out_ref, gr_ref, bonus_ref, vecs_ref,
     wbf_ref, wo_bf_ref, wg_bf_ref, wp_bf_ref, xn_s, z_s) = refs[len(_PRM_NAMES) + 1:]
    j = pl.program_id(0)

    @pl.when(j == 0)
    def _():
        x = x_ref[:, 0, :]
        xn_s[...] = (x * lax.rsqrt(jnp.mean(x * x, axis=-1, keepdims=True) + RMS_EPS) * ng_ref[...]).astype(BF16)
        wo_bf_ref[...] = wo_ref[...].astype(BF16)
        wg_bf_ref[...] = wg_ref[...].astype(BF16)
        wp_bf_ref[...] = wp_ref[...].astype(BF16)

    w_blk = win_ref[...].astype(BF16)
    wbf_ref[...] = w_blk
    z_s[j] = _dot(xn_s[...], w_blk)

    @pl.when(j == pl.num_programs(0) - 1)
    def _():
        _sample_front_tokens(z_s, cb_ref, sb_ref, prm_refs, headones_ref, ya_ref, conv_out_ref, shift_out_ref,
                             gr_ref, bonus_ref, vecs_ref)


def _sample_front_tokens(z_s, cb_ref, sb_ref, prm_refs, headones_ref, ya_ref, conv_out_ref, shift_out_ref,
                         gr_ref, bonus_ref, vecs_ref):
    prm = _load_prm(prm_refs)
    headones = headones_ref[...]
    z = jnp.concatenate([z_s[i] for i in range(z_s.shape[0])], axis=1)

    u = z[:, CONV_CH:2 * CONV_CH] * z[:, 2 * CONV_CH:3 * CONV_CH]
    cb0 = cb_ref[:, 0, :]
    cb1 = cb_ref[:, 1, :]
    cw = prm["conv_w"]
    conv = cw[0:1, :] * cb0 + cw[1:2, :] * cb1 + cw[2:3, :] * u
    ya_ref[...] = z[:, 0:CONV_CH] * conv * _silu(z[:, 3 * CONV_CH:4 * CONV_CH])
    conv_out_ref[:, 0, :] = cb1
    conv_out_ref[:, 1, :] = u

    zs = z[:, OFF_ZS:OFF_ZS + SHIFT_W]
    shift_out_ref[...] = zs
    gr_ref[...] = z[:, OFF_GR:OFF_GR + RW]
    zm = zs + (sb_ref[...] - zs) * prm["mu"]
    r = zm[:, 0:RW]
    k = zm[:, RW:2 * RW]
    v = zm[:, 2 * RW:3 * RW]
    wa = zm[:, 3 * RW:3 * RW + 2 * LORA]
    lw, a, kk, kmod, bonus = _rwkv_tokens(r, k, v, wa, prm, headones)
    bonus_ref[...] = bonus
    for i, vec in enumerate((-kk, jnp.exp(lw), kk * a, kmod, v, r)):
        vecs_ref[i] = vec.T


_STEP_VECS = ("nkk", "decay", "b", "kmod", "v", "r")
STATE_ROWS = 8


def _sample_state_rows(st_ref, vecs_ref, s1_ref, y_ref, v_row0):
    nkk, w, b, km = (vecs_ref[i, 0] for i in range(4))
    r = vecs_ref[5, 0]
    for vi in range(STATE_ROWS):
        s0 = st_ref[0, vi]
        sa = jnp.sum(s0 * nkk, axis=0, keepdims=True)
        v_row = vecs_ref[4, 0, pl.ds(v_row0 + vi, 1), :]
        s1 = s0 * w + sa * b + v_row * km
        s1_ref[0, vi] = s1
        y_ref[0, vi:vi + 1, :] = jnp.sum(s1 * r, axis=0, keepdims=True)


def _sample_back_kernel(o_ref, bonus_ref, gr_ref, ya_ref, x_ref, p_ref, wo_ref, wg_ref, wp_ref, fg_ref, *refs):
    prm_refs = refs[:len(_PRM_NAMES)]
    headones_ref = refs[len(_PRM_NAMES)]
    y_ref = refs[len(_PRM_NAMES) + 1]
    prm = _load_prm(prm_refs)
    wo = wo_ref[...]
    wg = wg_ref[...]
    wp = wp_ref[...]
    y_r = _rwkv_post(o_ref[...].T, bonus_ref[...], gr_ref[...], prm, headones_ref[...])
    ycat = jnp.concatenate([ya_ref[...], y_r], axis=1).astype(BF16)
    h1 = x_ref[:, 0, :] + _dot(ycat, wo)
    gate = _sigmoid(_dot(h1.astype(BF16), wg))
    pe = _dot(p_ref[:, 0, :].astype(BF16), wp)
    h2 = h1 + gate * pe
    ms = jnp.mean(h2 * h2, axis=-1, keepdims=True)
    y_ref[:, 0, :] = h2 * lax.rsqrt(ms + RMS_EPS) * fg_ref[...]


def _full(shape):
    nd = len(shape)
    return pl.BlockSpec(shape, lambda *_: (0,) * nd)


_SAMPLE_PARAMS = pltpu.CompilerParams(dimension_semantics=("arbitrary",), vmem_limit_bytes=VMEM_LIMIT)
_SQ, _PP = (D_MODEL, D_MODEL), (D_PLE, D_MODEL)


def _sample_front(x3d, state_conv, state_shift, w):
    n = x3d.shape[0]
    row = jax.ShapeDtypeStruct((n, RW), F32)
    wblk = IN_W // W_IN_BLOCKS
    once = lambda shape: pl.BlockSpec(shape, lambda j: (0,) * len(shape), pipeline_mode=pl.Buffered(1))
    outs = pl.pallas_call(
        _sample_front_kernel,
        grid=(W_IN_BLOCKS,),
        in_specs=[_full((n, 1, D_MODEL)), _full((1, D_MODEL)), pl.BlockSpec((D_MODEL, wblk), lambda j: (0, j)),
                  once(_SQ), once(_SQ), once(_PP), _full((n, 2, CONV_CH)), _full((n, SHIFT_W))]
        + _prm_specs() + [_full((HEADSUM_W, HEADSUM_W))],
        out_specs=[_full((n, CONV_CH)), _full((n, 2, CONV_CH)), _full((n, SHIFT_W)),
                   _full((n, RW)), _full((n, RW)), _full((len(_STEP_VECS), RW, n)),
                   pl.BlockSpec((D_MODEL, wblk), lambda j: (0, j)), once(_SQ), once(_SQ), once(_PP)],
        out_shape=[jax.ShapeDtypeStruct((n, CONV_CH), F32), jax.ShapeDtypeStruct((n, 2, CONV_CH), F32),
                   jax.ShapeDtypeStruct((n, SHIFT_W), F32), row, row,
                   jax.ShapeDtypeStruct((len(_STEP_VECS), RW, n), F32),
                   jax.ShapeDtypeStruct((D_MODEL, IN_W), BF16), jax.ShapeDtypeStruct(_SQ, BF16),
                   jax.ShapeDtypeStruct(_SQ, BF16), jax.ShapeDtypeStruct(_PP, BF16)],
        scratch_shapes=[pltpu.VMEM((n, D_MODEL), BF16), pltpu.VMEM((W_IN_BLOCKS, n, wblk), F32)],
        compiler_params=_SAMPLE_PARAMS,
        name="sample_front",
    )(x3d, w["norm_g"].reshape(1, D_MODEL), w["w_in_f32"], w["w_out_f32"], w["w_pg_f32"], w["w_pp_f32"],
      state_conv, state_shift, *w["prm"], w["headones"])
    ya, conv_new, shift_new, g_r, bonus, vecs, w_in, w_out, w_pg, w_pp = outs
    rounded = {"w_in": w_in, "w_out": w_out, "w_pg": w_pg, "w_pp": w_pp}
    return (ya, g_r, bonus), conv_new, shift_new, vecs.reshape(len(_STEP_VECS), N_HEADS, HEAD_DIM, n), rounded


def _sample_back(y_t, token_parts, x3d, p3d, w):
    n = x3d.shape[0]
    ya, g_r, bonus = token_parts
    return pl.pallas_call(
        _sample_back_kernel,
        grid=(1,),
        in_specs=[_full((RW, n)), _full((n, RW)), _full((n, RW)), _full((n, CONV_CH)),
                  _full((n, 1, D_MODEL)), _full((n, 1, D_PLE)), _full(_SQ), _full(_SQ), _full(_PP),
                  _full((1, D_MODEL))]
        + _prm_specs() + [_full((HEADSUM_W, HEADSUM_W))],
        out_specs=_full((n, 1, D_MODEL)),
        out_shape=jax.ShapeDtypeStruct((n, 1, D_MODEL), F32),
        compiler_params=_SAMPLE_PARAMS,
        name="sample_back",
    )(y_t.reshape(RW, n), bonus, g_r, ya, x3d, p3d, w["w_out"], w["w_pg"], w["w_pp"],
      w["final_g"].reshape(1, D_MODEL), *w["prm"], w["headones"])


def _layer(x_prompt, p_prompt, x_sample, p_sample, state_conv, state_shift, state_wkv, w, tc=256):
    bsz, t_len, _ = x_prompt.shape
    token_parts, conv_s, shift_s, vecs, rounded = _sample_front(x_sample, state_conv, state_shift, w)
    w = {**w, **rounded}
    s_hvkb = jnp.transpose(state_wkv, (1, 2, 3, 0))
    y_p, conv_p, shift_p, wkv_p, s1, y_t = _prompt_layer(
        x_prompt.reshape(bsz * t_len, D_MODEL), p_prompt.reshape(bsz * t_len, D_PLE), w, bsz, t_len, tc,
        s_hvkb, vecs)
    y_s = _sample_back(y_t, token_parts, x_sample, p_sample, w)
    prompt_out = (y_p.reshape(bsz, t_len, D_MODEL), conv_p, shift_p, wkv_p)
    sample_out = (y_s, conv_s, shift_s, jnp.transpose(s1, (3, 0, 1, 2)))
    return prompt_out, sample_out


def kernel(x_prompt, x_sample, p_prompt, p_sample, state_conv, state_shift, state_wkv, norm_g, w_in, conv_w, mu_shift, w0, w_up, a0, a_up, k_k, k_a, r_k, ln_w, ln_b, w_out, w_pg, w_pp, final_g):
    depth = norm_g.shape[0]
    assert depth == 1
    i = 0
    head_id = np.arange(HEADSUM_W) // HEAD_DIM
    w = {
        "norm_g": norm_g[i],
        "w_in_f32": w_in[i],
        "w_out_f32": w_out[i],
        "w_pg_f32": w_pg[i],
        "w_pp_f32": w_pp[i],
        "final_g": final_g,
        "prm": _prm_arrays(conv_w[i], mu_shift[i], w0[i], w_up[i], a0[i], a_up[i], k_k[i], k_a[i],
                           r_k[i].reshape(RW), ln_w[i], ln_b[i]),
        "headones": jnp.asarray(head_id[:, None] == head_id[None, :], BF16),
    }
    (yp, cp, sp, wp), (ys, cs, ss, ws) = _layer(x_prompt, p_prompt[i], x_sample, p_sample[i], state_conv[i],
                                                state_shift[i], state_wkv[i], w)
    return (yp, ys, cp[None], sp[None], wp[None], cs[None], ss[None], ws[None])
```

```python
import functools
import math

import jax
import jax.numpy as jnp
import numpy as np
from jax import lax
from jax.experimental import pallas as pl
from jax.experimental.pallas import tpu as pltpu

F32 = jnp.float32
BF16 = jnp.bfloat16

D_MODEL = 1024
CONV_CH = 512
RW = 512
HEAD_DIM = 64
N_HEADS = 8
N_PAIRS = N_HEADS // 2
LORA = 64
D_PLE = 256
SHIFT_W = 3 * RW + 2 * LORA
IN_W = 4 * CONV_CH + SHIFT_W + RW
OFF_ZS = 4 * CONV_CH
OFF_GR = OFF_ZS + SHIFT_W
RMS_EPS = 1e-6
GN_EPS = 64e-5
DECAY_SCALE = math.exp(-0.5)

LANES = 128
SUBLANES = 8
MXU_DIM = 256
HEADSUM_W = MXU_DIM
SUM_PARTS = 1
CUM_PARTS = 1
LORA_PASSES = 1
CHUNK = 64
W_IN_BLOCKS = 3
CARRY_ROWS = 8
VMEM_LIMIT = 56 * 1024 * 1024


def _dot(a, b):
    return jnp.dot(a, b, preferred_element_type=F32)


def _dot_nt(a, b):
    return lax.dot_general(a, b, (((1,), (1,)), ((), ())), preferred_element_type=F32)


def _dot_tn(a, b):
    return lax.dot_general(a, b, (((0,), (0,)), ((), ())), preferred_element_type=F32)


def _split(x, parts):
    out = []
    rem = x
    for i in range(parts):
        t = rem.astype(BF16)
        out.append(t)
        if i + 1 < parts:
            rem = rem - t.astype(F32)
    return out


def _const_dot(c_bf16, x, parts):
    acc = None
    for t in _split(x, parts):
        d = _dot(c_bf16, t)
        acc = d if acc is None else acc + d
    return acc


def _x_dot_const(x, c_bf16, parts):
    acc = None
    for t in _split(x, parts):
        d = _dot(t, c_bf16)
        acc = d if acc is None else acc + d
    return acc


def _head_sum(x, headones):
    w = headones.shape[0]
    halves = [_x_dot_const(x[:, i * w:(i + 1) * w], headones, SUM_PARTS) for i in range(x.shape[1] // w)]
    return jnp.concatenate(halves, axis=1)


def _mm(a, b, passes):
    if passes == 1:
        return _dot(a.astype(BF16), b.astype(BF16))
    ah, al = _split(a, 2)
    bh, bl = _split(b, 2)
    return _dot(ah, bh) + _dot(al, bh) + _dot(ah, bl)


def _sigmoid(x):
    return 1.0 / (1.0 + jnp.exp(-x))


def _silu(x):
    return x * _sigmoid(x)


def _rwkv_tokens(r, k, v, wa, prm, headones, between_steps=lambda: None):
    n = r.shape[0]
    lane = lax.broadcasted_iota(jnp.int32, (n, LANES), 1)
    th = jnp.where(lane < LORA, jnp.tanh(wa), wa)
    lora = _mm(th, prm["wlora"], LORA_PASSES)
    lw = -DECAY_SCALE * _sigmoid(prm["w0"] + lora[:, :RW])
    a = _sigmoid(prm["a0"] + lora[:, RW:])
    between_steps()
    kk = k * prm["k_k"]
    ss = _head_sum(kk * kk, headones)
    kk = kk / jnp.maximum(jnp.sqrt(ss), 1e-12)
    between_steps()
    kmod = k * (1.0 + (a - 1.0) * prm["k_a"])
    bonus = _head_sum(r * kmod * prm["r_k"], headones) * v
    between_steps()
    return lw, a, kk, kmod, bonus


def _rwkv_post(o, bonus, g_r, prm, headones):
    mean = _head_sum(o, headones) * (1.0 / HEAD_DIM)
    d = o - mean
    var = _head_sum(d * d, headones) * (1.0 / HEAD_DIM)
    on = d * lax.rsqrt(var + GN_EPS)
    on = on * prm["ln_w"] + prm["ln_b"]
    return (on + bonus) * _silu(g_r)


_PRM_NAMES = ("conv_w", "mu", "w0", "a0", "w_up", "a_up", "k_k", "k_a", "r_k", "ln_w", "ln_b")


def _prm_arrays(conv_w, mu_shift, w0, w_up, a0, a_up, k_k, k_a, r_k, ln_w, ln_b):
    row = lambda x: x.reshape(1, -1)
    return (conv_w, row(mu_shift), row(w0), row(a0), w_up, a_up, row(k_k), row(k_a),
            r_k, row(ln_w), row(ln_b))


def _prm_specs():
    zero = (lambda *_: (0, 0))
    shapes = ((3, CONV_CH), (1, SHIFT_W), (1, RW), (1, RW), (LORA, RW), (LORA, RW),
              (1, RW), (1, RW), (1, RW), (1, RW), (1, RW))
    return [pl.BlockSpec(s, zero) for s in shapes]


def _load_prm(refs):
    prm = {n: r[...] for n, r in zip(_PRM_NAMES, refs)}
    zero = jnp.zeros((LORA, RW), F32)
    prm["wlora"] = jnp.concatenate([jnp.concatenate([prm["w_up"], zero], axis=1),
                                    jnp.concatenate([zero, prm["a_up"]], axis=1)], axis=0)
    return prm


def _stack_heads(x):
    lane = lax.broadcasted_iota(jnp.int32, x.shape, 1)
    lo = jnp.where(lane < HEAD_DIM, x, 0.0)
    hi = jnp.where(lane >= HEAD_DIM, x, 0.0)
    return jnp.concatenate([lo, hi], axis=0)


def _stack_heads_bf16(x):
    lane = lax.broadcasted_iota(jnp.int32, x.shape, 1)
    lo = jnp.where(lane < HEAD_DIM, 1.0, 0.0).astype(BF16)
    xb = x.astype(BF16)
    return jnp.concatenate([xb * lo, xb * (1.0 - lo)], axis=0)


def _bdot(a, b):
    return _dot(a.astype(BF16), b.astype(BF16))


def _diag_block_inverses_minus_eye(mats, lane_bcast):
    n = mats[0].shape[0]
    nb = n // SUBLANES
    row = lax.broadcasted_iota(jnp.int32, (n, n), 0)
    col = lax.broadcasted_iota(jnp.int32, (n, n), 1)
    in_diag = ((row // SUBLANES) == (col // SUBLANES)) & (row > col)
    packed = []
    for a in mats:
        d = jnp.where(in_diag, a, 0.0)
        acc = d[0:SUBLANES]
        for m in range(1, nb):
            acc = acc + d[m * SUBLANES:(m + 1) * SUBLANES]
        packed.append(acc)
    d_all = jnp.concatenate(packed, axis=0).astype(BF16)
    sub = lax.broadcasted_iota(jnp.int32, (SUBLANES, n), 0)
    lane = lax.broadcasted_iota(jnp.int32, (SUBLANES, n), 1)
    eye8 = jnp.where(sub == lane % SUBLANES, 1.0, 0.0)
    ts = [eye8 for _ in mats]
    for j in range(SUBLANES - 1):
        spread = _dot(d_all, lane_bcast[j])
        ts = [t + spread[i * SUBLANES:(i + 1) * SUBLANES] * jnp.broadcast_to(t[j:j + 1, :], (SUBLANES, n))
              for i, t in enumerate(ts)]
    blk = lax.broadcasted_iota(jnp.int32, (SUBLANES, n), 1) // SUBLANES
    out = []
    for t in ts:
        tm1 = t - eye8
        out.append(jnp.concatenate([jnp.where(blk == m, tm1, 0.0) for m in range(nb)], axis=0))
    return out


def _unit_lower_inverse_minus_eye(mats, lane_bcast, between_levels=lambda: None):
    n = mats[0].shape[0]
    row = lax.broadcasted_iota(jnp.int32, (n, n), 0)
    col = lax.broadcasted_iota(jnp.int32, (n, n), 1)
    xs = _diag_block_inverses_minus_eye(mats, lane_bcast)
    size = 2 * SUBLANES
    while size <= CHUNK:
        half = size // 2
        sel = ((row // size) == (col // size)) & ((row % size) >= half) & ((col % size) < half)
        als = [jnp.where(sel, a, 0.0) for a in mats]
        ps = [al + _bdot(x, al) for x, al in zip(xs, als)]
        xs = [x + p + _bdot(p, x) for x, p in zip(xs, ps)]
        between_levels()
        size *= 2
    return xs


def _chunk_operators(ld, strict, incl, lane_bcast, side_work=()):
    side_work = list(side_work)

    def breathe():
        if side_work:
            side_work.pop(0)()

    cat0 = lambda xs: jnp.concatenate(xs, axis=0)
    la = [_stack_heads(x) for x in ld("la")]
    lr = [_stack_heads(x) for x in ld("lr")]
    n2 = 2 * CHUNK
    amat = [_dot_nt(cat0([a, b]).astype(BF16), cat0([c, d]).astype(BF16))
            for a, b, c, d in zip(la, lr, ld("rb"), ld("rk"))]

    def block_diag_pair(m):
        swapped = pltpu.roll(m, CHUNK, axis=1)
        first = cat0([m[0:CHUNK], swapped[CHUNK:n2]])
        second = cat0([swapped[0:CHUNK], m[CHUNK:n2]])
        return first, second

    breathe()
    top = [block_diag_pair(m[0:n2]) for m in amat]
    a_ab = [jnp.where(strict, t[0], 0.0) for t in top]
    xs = _unit_lower_inverse_minus_eye(a_ab, lane_bcast, breathe)
    vst = [_stack_heads_bf16(x) for x in ld("v")]
    av = [_dot(jnp.where(strict, t[1], 0.0).astype(BF16), v) for t, v in zip(top, vst)]
    breathe()
    uv0 = [y + _bdot(x, y) for x, y in zip(xs, av)]
    tla = [y + _bdot(x, y) for x, y in zip(xs, la)]
    breathe()
    bottom = [block_diag_pair(m[n2:2 * n2]) for m in amat]
    a_rb = [jnp.where(incl, t[0], 0.0) for t in bottom]
    a_rk = [jnp.where(incl, t[1], 0.0) for t in bottom]
    gop = [y + _bdot(a, t) for y, a, t in zip(lr, a_rb, tla)]
    uv = [cat0([u.astype(BF16), v]) for u, v in zip(uv0, vst)]
    y0 = [_dot(jnp.concatenate([a, b], axis=1).astype(BF16), w) for a, b, w in zip(a_rb, a_rk, uv)]
    breathe()
    bt = [_stack_heads_bf16(x) for x in ld("bt")]
    kt = [_stack_heads_bf16(x) for x in ld("kt")]
    pop = [_dot_tn(t.astype(BF16), b) for t, b in zip(tla, bt)]
    qop = [_dot_tn(w, cat0([b, k])) for w, b, k in zip(uv, bt, kt)]
    while side_work:
        breathe()
    return gop, y0, pop, qop


def _prompt_kernel(xh_ref, xt_ref, pt_ref, ng_ref, win_ref, wo_ref, wg_ref, wp_ref, fg_ref, *refs,
                   tc, n_t, n_tiles):
    prm_refs = refs[:len(_PRM_NAMES)]
    headones_ref, tri_ref, bcast_ref, st_ref, vecs_ref = refs[len(_PRM_NAMES):len(_PRM_NAMES) + 5]
    (y_ref, conv_out_ref, shift_out_ref, wkv_out_ref, st1_ref, ys_ref) = refs[len(_PRM_NAMES) + 5:len(_PRM_NAMES) + 11]
    (z_s, ycat_s, ubuf, zsbuf, s_ref, la_s, lr_s, rb_s, rk_s, bt_s, kt_s, v_s, gam_s, o_s,
     g_sc, y0_sc, p_sc, q_sc, bonus_p, gr_p) = refs[len(_PRM_NAMES) + 11:]

    s = pl.program_id(0)
    head_t = lax.rem(s, jnp.int32(n_t))
    tail_t = lax.rem(s + (n_t - 1), jnp.int32(n_t))
    prm = _load_prm(prm_refs)
    headones = headones_ref[...]
    n2 = 2 * CHUNK
    n_chunks = tc // CHUNK

    @pl.when(s == 0)
    def _():
        for ref in (g_sc, y0_sc, p_sc, q_sc, gam_s, bonus_p, gr_p, ycat_s, s_ref):
            ref[...] = jnp.zeros(ref.shape, ref.dtype)

    @pl.when(head_t == 0)
    def _():
        ubuf[0:CARRY_ROWS, :] = jnp.zeros((CARRY_ROWS, CONV_CH), F32)
        zsbuf[0:CARRY_ROWS, :] = jnp.zeros((CARRY_ROWS, SHIFT_W), F32)

    @pl.when(tail_t == 0)
    def _():
        s_ref[...] = jnp.zeros(s_ref.shape, F32)

    tail = {}

    def tail_state_pass(c):
        def run():
            for j in range(N_PAIRS):
                idx = c * N_PAIRS + j
                lanes = slice(j * LANES, (j + 1) * LANES)
                s_old = s_ref[j]
                s_bf = s_old.astype(BF16)
                yst = _dot_nt(g_sc[idx], s_bf) + y0_sc[idx]
                o_s[c * CHUNK:(c + 1) * CHUNK, lanes] = yst[0:CHUNK] + yst[CHUNK:n2]
                gam = gam_s[c * CHUNK:c * CHUNK + 1, lanes]
                s_ref[j] = s_old * gam + _dot(s_bf, p_sc[idx]) + q_sc[idx]
        return run

    def tail_post():
        ycat_s[:, CONV_CH:2 * CONV_CH] = _rwkv_post(o_s[...], bonus_p[...], gr_p[...], prm, headones)

    def tail_out_proj():
        tail["h1"] = xt_ref[...] + _dot(ycat_s[...].astype(BF16), wo_ref[...])

    def tail_gate():
        tail["gate"] = _sigmoid(_dot(tail["h1"].astype(BF16), wg_ref[...]))
        tail["pe"] = _dot(pt_ref[...].astype(BF16), wp_ref[...])

    def tail_norm():
        h2 = tail["h1"] + tail["gate"] * tail["pe"]
        ms = jnp.mean(h2 * h2, axis=-1, keepdims=True)
        y_ref[...] = h2 * lax.rsqrt(ms + RMS_EPS) * fg_ref[...]

    x = xh_ref[...]
    xn = (x * lax.rsqrt(jnp.mean(x * x, axis=-1, keepdims=True) + RMS_EPS) * ng_ref[...]).astype(BF16)

    def project(c0):
        def run():
            z_s[:, c0:c0 + CONV_CH] = _dot(xn, win_ref[:, c0:c0 + CONV_CH])
        return run

    passes = [tail_state_pass(c) for c in range(n_chunks)]
    blocks = [project(c0) for c0 in (OFF_GR, 0, CONV_CH, 2 * CONV_CH, 3 * CONV_CH)]
    side_queue = []
    late_blocks, blocks = blocks[-1:], blocks[:-1]
    while passes or blocks:
        side_queue += passes[:1] + blocks[:1]
        passes, blocks = passes[1:], blocks[1:]
    side_queue += [tail_post, tail_out_proj, tail_gate, tail_norm]

    def project_next(count=2):
        for _ in range(count):
            if side_queue:
                side_queue.pop(0)()

    z_s[:, OFF_ZS:OFF_ZS + SHIFT_W] = _dot(xn, win_ref[:, OFF_ZS:OFF_ZS + SHIFT_W])

    blocks_per_head = HEAD_DIM // STATE_ROWS
    v_row0 = lax.rem(jnp.minimum(s, n_tiles - 1), jnp.int32(blocks_per_head)) * STATE_ROWS
    _sample_state_rows(st_ref, vecs_ref, st1_ref, ys_ref, v_row0)

    zs = z_s[:, OFF_ZS:OFF_ZS + SHIFT_W]
    zsbuf[CARRY_ROWS:CARRY_ROWS + tc, :] = zs
    zprev = zsbuf[CARRY_ROWS - 1:CARRY_ROWS - 1 + tc, :]
    zm = zs + (zprev - zs) * prm["mu"]
    zsbuf[CARRY_ROWS - 1:CARRY_ROWS, :] = zsbuf[CARRY_ROWS + tc - 1:CARRY_ROWS + tc, :]

    project_next()
    r = zm[:, 0:RW]
    k = zm[:, RW:2 * RW]
    v = zm[:, 2 * RW:3 * RW]
    wa = zm[:, 3 * RW:3 * RW + 2 * LORA]
    lw, a, kk, kmod, bonus = _rwkv_tokens(r, k, v, wa, prm, headones, project_next)

    g = _const_dot(tri_ref[...], lw, CUM_PARTS)
    gc = jnp.concatenate(
        [jnp.broadcast_to(g[(c + 1) * CHUNK - 1:(c + 1) * CHUNK, :], (CHUNK, RW)) for c in range(tc // CHUNK)],
        axis=0)
    project_next()
    eng = jnp.exp(-g)
    etail = jnp.exp(gc - g)
    b = kk * a
    la_s[...] = -kk * jnp.exp(g - lw)
    lr_s[...] = r * jnp.exp(g)
    project_next()
    rb_s[...] = b * eng
    rk_s[...] = kmod * eng
    bt_s[...] = b * etail
    kt_s[...] = kmod * etail
    v_s[...] = v
    while side_queue:
        project_next()
    gam_s[...] = jnp.exp(gc)

    row = lax.broadcasted_iota(jnp.int32, (n2, n2), 0)
    col = lax.broadcasted_iota(jnp.int32, (n2, n2), 1)
    same = (row // CHUNK) == (col // CHUNK)
    strict = same & (row > col)
    incl = same & (row >= col)

    cw = prm["conv_w"]

    def conv_rows(r0):
        def run():
            rows = slice(r0, r0 + CHUNK)
            u = z_s[rows, CONV_CH:2 * CONV_CH] * z_s[rows, 2 * CONV_CH:3 * CONV_CH]
            ubuf[CARRY_ROWS + r0:CARRY_ROWS + r0 + CHUNK, :] = u
            um1 = ubuf[CARRY_ROWS - 1 + r0:CARRY_ROWS - 1 + r0 + CHUNK, :]
            um2 = ubuf[CARRY_ROWS - 2 + r0:CARRY_ROWS - 2 + r0 + CHUNK, :]
            conv = cw[0:1, :] * um2 + cw[1:2, :] * um1 + cw[2:3, :] * u
            ycat_s[rows, 0:CONV_CH] = z_s[rows, 0:CONV_CH] * conv * _silu(z_s[rows, 3 * CONV_CH:4 * CONV_CH])
        return run

    srcs = {"la": la_s, "lr": lr_s, "rb": rb_s, "rk": rk_s, "bt": bt_s, "kt": kt_s, "v": v_s}
    members = [(c, j) for c in range(n_chunks) for j in range(N_PAIRS)]

    def ld(name):
        ref = srcs[name]
        return [ref[c * CHUNK:(c + 1) * CHUNK, j * LANES:(j + 1) * LANES] for c, j in members]

    lane_bcast = [bcast_ref[j] for j in range(SUBLANES - 1)]
    gop, y0, pop, qop = _chunk_operators(ld, strict, incl, lane_bcast,
                                         late_blocks + [conv_rows(c * CHUNK) for c in range(n_chunks)])
    for (c, j), g_, y_, p_, q_ in zip(members, gop, y0, pop, qop):
        idx = c * N_PAIRS + j
        g_sc[idx] = g_.astype(BF16)
        y0_sc[idx] = y_
        p_sc[idx] = p_.astype(BF16)
        q_sc[idx] = q_

    ubuf[CARRY_ROWS - 2:CARRY_ROWS, :] = ubuf[CARRY_ROWS + tc - 2:CARRY_ROWS + tc, :]

    bonus_p[...] = bonus
    gr_p[...] = z_s[:, OFF_GR:OFF_GR + RW]

    @pl.when((head_t == n_t - 1) & (s < n_tiles))
    def _():
        seq = lax.div(s, jnp.int32(n_t))
        shift_out_ref[pl.ds(seq, 1), :] = zsbuf[CARRY_ROWS - 1:CARRY_ROWS, :]
        conv_out_ref[0] = ubuf[CARRY_ROWS - 2:CARRY_ROWS, :]

    @pl.when((tail_t == n_t - 1) & (s > 0))
    def _():
        for h in range(N_HEADS):
            j, i = divmod(h, 2)
            blk = s_ref[j]
            wkv_out_ref[0, h] = blk[i * HEAD_DIM:(i + 1) * HEAD_DIM, i * HEAD_DIM:(i + 1) * HEAD_DIM]


def _prompt_layer(x2d, p2d, w, bsz, t_len, tc, state_hvkb, step_vecs):
    n_t = t_len // tc
    n_seq = state_hvkb.shape[-1]
    blocks_per_head = HEAD_DIM // STATE_ROWS
    assert bsz * n_t == N_HEADS * blocks_per_head, "one state block per prompt tile"
    state_blk = lambda s: jnp.minimum(s, bsz * n_t - 1)
    tok = np.arange(tc)
    blockones = (tok[:, None] // CHUNK) == (tok[None, :] // CHUNK)
    tri = jnp.asarray(blockones & (tok[:, None] >= tok[None, :]), BF16)
    n_tiles = bsz * n_t
    n2 = 2 * CHUNK
    idx = np.arange(n2)
    same_blk = (idx[:, None] // SUBLANES) == (idx[None, :] // SUBLANES)
    lane_bcast = jnp.asarray(
        np.stack([same_blk & (idx[:, None] % SUBLANES == j) for j in range(SUBLANES - 1)]), BF16)
    kern = functools.partial(_prompt_kernel, tc=tc, n_t=n_t, n_tiles=n_tiles)
    big = lambda: pltpu.VMEM((tc, RW), F32)
    n_blk = (tc // CHUNK) * N_PAIRS
    op = lambda dt: pltpu.VMEM((n_blk, LANES, LANES), dt)
    head = lambda s: jnp.minimum(s, n_tiles - 1)
    tail = lambda s: jnp.maximum(s - 1, 0)
    head_tile = lambda width: pl.BlockSpec((tc, width), lambda s: (head(s), 0))
    tail_tile = lambda width: pl.BlockSpec((tc, width), lambda s: (tail(s), 0))
    const = lambda shape: pl.BlockSpec(shape, lambda s: (0, 0), pipeline_mode=pl.Buffered(1))
    return pl.pallas_call(
        kern,
        grid=(n_tiles + 1,),
        in_specs=[head_tile(D_MODEL), tail_tile(D_MODEL), tail_tile(D_PLE), const((1, D_MODEL)),
                  const((D_MODEL, IN_W)), const((D_MODEL, D_MODEL)), const((D_MODEL, D_MODEL)),
                  const((D_PLE, D_MODEL)), const((1, D_MODEL))]
        + _prm_specs()
        + [const((HEADSUM_W, HEADSUM_W)), const((tc, tc)),
           pl.BlockSpec((SUBLANES - 1, n2, n2), lambda s: (0, 0, 0), pipeline_mode=pl.Buffered(1)),
           pl.BlockSpec((1, STATE_ROWS, HEAD_DIM, n_seq),
                        lambda s: (state_blk(s) // blocks_per_head, state_blk(s) % blocks_per_head, 0, 0)),
           pl.BlockSpec((len(_STEP_VECS), 1, HEAD_DIM, n_seq),
                        lambda s: (0, state_blk(s) // blocks_per_head, 0, 0))],
        out_specs=[
            tail_tile(D_MODEL),
            pl.BlockSpec((1, 2, CONV_CH), lambda s: (head(s) // n_t, 0, 0)),
            pl.BlockSpec((bsz, SHIFT_W), lambda s: (0, 0)),
            pl.BlockSpec((1, N_HEADS, HEAD_DIM, HEAD_DIM), lambda s: (tail(s) // n_t, 0, 0, 0)),
            pl.BlockSpec((1, STATE_ROWS, HEAD_DIM, n_seq),
                         lambda s: (state_blk(s) // blocks_per_head, state_blk(s) % blocks_per_head, 0, 0)),
            pl.BlockSpec((1, STATE_ROWS, n_seq),
                         lambda s: (state_blk(s) // blocks_per_head, state_blk(s) % blocks_per_head, 0)),
        ],
        out_shape=[
            jax.ShapeDtypeStruct((bsz * t_len, D_MODEL), F32),
            jax.ShapeDtypeStruct((bsz, 2, CONV_CH), F32),
            jax.ShapeDtypeStruct((bsz, SHIFT_W), F32),
            jax.ShapeDtypeStruct((bsz, N_HEADS, HEAD_DIM, HEAD_DIM), F32),
            jax.ShapeDtypeStruct(state_hvkb.shape, F32),
            jax.ShapeDtypeStruct((N_HEADS, HEAD_DIM, n_seq), F32),
        ],
        scratch_shapes=[
            pltpu.VMEM((tc, IN_W), F32),
            pltpu.VMEM((tc, D_MODEL), F32),
            pltpu.VMEM((CARRY_ROWS + tc, CONV_CH), F32),
            pltpu.VMEM((CARRY_ROWS + tc, SHIFT_W), F32),
            pltpu.VMEM((N_PAIRS, LANES, LANES), F32),
            big(), big(), big(), big(), big(), big(), big(), big(), big(),
            op(BF16), op(F32), op(BF16), op(F32),
            big(), big(),
        ],
        compiler_params=pltpu.CompilerParams(
            dimension_semantics=("arbitrary",), vmem_limit_bytes=VMEM_LIMIT),
        name="prompt_layer",
    )(x2d, x2d, p2d, w["norm_g"].reshape(1, D_MODEL), w["w_in"], w["w_out"], w["w_pg"], w["w_pp"],
      w["final_g"].reshape(1, D_MODEL), *w["prm"], w["headones"], tri, lane_bcast, state_hvkb, step_vecs)


def _sample_front_kernel(x_ref, ng_ref, win_ref, wo_ref, wg_ref, wp_ref, cb_ref, sb_ref, *refs):
    prm_refs = refs[:len(_PRM_NAMES)]
    headones_ref = refs[len(_PRM_NAMES)]
    (ya_ref, conv_out_ref, shift_out_ref, gr_ref, bonus_ref, vecs_ref,
     wbf_ref, wo_bf_ref, wg_bf_ref, wp_bf_ref, cw_rows_ref, rk_row_ref, xn_s, z_s) = refs[len(_PRM_NAMES) + 1:]
    j = pl.program_id(0)

    @pl.when(j == 0)
    def _():
        x = x_ref[:, 0, :]
        xn_s[...] = (x * lax.rsqrt(jnp.mean(x * x, axis=-1, keepdims=True) + RMS_EPS) * ng_ref[...]).astype(BF16)
        wo_bf_ref[...] = wo_ref[...].astype(BF16)
        wg_bf_ref[...] = wg_ref[...].astype(BF16)
        wp_bf_ref[...] = wp_ref[...].astype(BF16)

    w_blk = win_ref[...].astype(BF16)
    wbf_ref[...] = w_blk
    z_s[j] = _dot(xn_s[...], w_blk)

    @pl.when(j == pl.num_programs(0) - 1)
    def _():
        _sample_front_tokens(z_s, cb_ref, sb_ref, prm_refs, headones_ref, ya_ref, conv_out_ref, shift_out_ref,
                             gr_ref, bonus_ref, vecs_ref, cw_rows_ref, rk_row_ref)


def _sample_front_tokens(z_s, cb_ref, sb_ref, prm_refs, headones_ref, ya_ref, conv_out_ref, shift_out_ref,
                         gr_ref, bonus_ref, vecs_ref, cw_rows_ref, rk_row_ref):
    prm = _load_prm(prm_refs)
    prm["conv_w"] = jnp.concatenate([prm["conv_w"][t] for t in range(3)], axis=0)
    prm["r_k"] = jnp.concatenate([prm["r_k"][h:h + 1, :] for h in range(N_HEADS)], axis=1)
    cw_rows_ref[...] = prm["conv_w"]
    rk_row_ref[...] = prm["r_k"]
    headones = headones_ref[...]
    z = jnp.concatenate([z_s[i] for i in range(z_s.shape[0])], axis=1)

    u = z[:, CONV_CH:2 * CONV_CH] * z[:, 2 * CONV_CH:3 * CONV_CH]
    cb0 = cb_ref[:, 0, :]
    cb1 = cb_ref[:, 1, :]
    cw = prm["conv_w"]
    conv = cw[0:1, :] * cb0 + cw[1:2, :] * cb1 + cw[2:3, :] * u
    ya_ref[...] = z[:, 0:CONV_CH] * conv * _silu(z[:, 3 * CONV_CH:4 * CONV_CH])
    conv_out_ref[:, 0, :] = cb1
    conv_out_ref[:, 1, :] = u

    zs = z[:, OFF_ZS:OFF_ZS + SHIFT_W]
    shift_out_ref[...] = zs
    gr_ref[...] = z[:, OFF_GR:OFF_GR + RW]
    zm = zs + (sb_ref[...] - zs) * prm["mu"]
    r = zm[:, 0:RW]
    k = zm[:, RW:2 * RW]
    v = zm[:, 2 * RW:3 * RW]
    wa = zm[:, 3 * RW:3 * RW + 2 * LORA]
    lw, a, kk, kmod, bonus = _rwkv_tokens(r, k, v, wa, prm, headones)
    bonus_ref[...] = bonus
    for i, vec in enumerate((-kk, jnp.exp(lw), kk * a, kmod, v, r)):
        vecs_ref[i] = vec.T


_STEP_VECS = ("nkk", "decay", "b", "kmod", "v", "r")
STATE_ROWS = 8


def _sample_state_rows(st_ref, vecs_ref, s1_ref, y_ref, v_row0):
    nkk, w, b, km = (vecs_ref[i, 0] for i in range(4))
    r = vecs_ref[5, 0]
    for vi in range(STATE_ROWS):
        s0 = st_ref[0, vi]
        sa = jnp.sum(s0 * nkk, axis=0, keepdims=True)
        v_row = vecs_ref[4, 0, pl.ds(v_row0 + vi, 1), :]
        s1 = s0 * w + sa * b + v_row * km
        s1_ref[0, vi] = s1
        y_ref[0, vi:vi + 1, :] = jnp.sum(s1 * r, axis=0, keepdims=True)


def _sample_back_kernel(o_ref, bonus_ref, gr_ref, ya_ref, x_ref, p_ref, wo_ref, wg_ref, wp_ref, fg_ref, *refs):
    prm_refs = refs[:len(_PRM_NAMES)]
    headones_ref = refs[len(_PRM_NAMES)]
    y_ref = refs[len(_PRM_NAMES) + 1]
    prm = _load_prm(prm_refs)
    wo = wo_ref[...]
    wg = wg_ref[...]
    wp = wp_ref[...]
    y_r = _rwkv_post(o_ref[...].T, bonus_ref[...], gr_ref[...], prm, headones_ref[...])
    ycat = jnp.concatenate([ya_ref[...], y_r], axis=1).astype(BF16)
    h1 = x_ref[:, 0, :] + _dot(ycat, wo)
    gate = _sigmoid(_dot(h1.astype(BF16), wg))
    pe = _dot(p_ref[:, 0, :].astype(BF16), wp)
    h2 = h1 + gate * pe
    ms = jnp.mean(h2 * h2, axis=-1, keepdims=True)
    y_ref[:, 0, :] = h2 * lax.rsqrt(ms + RMS_EPS) * fg_ref[...]


def _full(shape):
    nd = len(shape)
    return pl.BlockSpec(shape, lambda *_: (0,) * nd)


_SAMPLE_PARAMS = pltpu.CompilerParams(dimension_semantics=("arbitrary",), vmem_limit_bytes=VMEM_LIMIT)
_SQ, _PP = (D_MODEL, D_MODEL), (D_PLE, D_MODEL)


def _sample_front(x3d, state_conv, state_shift, w):
    n = x3d.shape[0]
    row = jax.ShapeDtypeStruct((n, RW), F32)
    wblk = IN_W // W_IN_BLOCKS
    once = lambda shape: pl.BlockSpec(shape, lambda j: (0,) * len(shape), pipeline_mode=pl.Buffered(1))
    raw_shapes = {"conv_w": (3, 1, CONV_CH), "r_k": (N_HEADS, HEAD_DIM)}
    prm_specs = [_full(raw_shapes[name]) if name in raw_shapes else spec
                 for name, spec in zip(_PRM_NAMES, _prm_specs())]
    outs = pl.pallas_call(
        _sample_front_kernel,
        grid=(W_IN_BLOCKS,),
        in_specs=[_full((n, 1, D_MODEL)), _full((1, D_MODEL)), pl.BlockSpec((D_MODEL, wblk), lambda j: (0, j)),
                  once(_SQ), once(_SQ), once(_PP), _full((n, 2, CONV_CH)), _full((n, SHIFT_W))]
        + prm_specs + [_full((HEADSUM_W, HEADSUM_W))],
        out_specs=[_full((n, CONV_CH)), _full((n, 2, CONV_CH)), _full((n, SHIFT_W)),
                   _full((n, RW)), _full((n, RW)), _full((len(_STEP_VECS), RW, n)),
                   pl.BlockSpec((D_MODEL, wblk), lambda j: (0, j)), once(_SQ), once(_SQ), once(_PP),
                   _full((3, CONV_CH)), _full((1, RW))],
        out_shape=[jax.ShapeDtypeStruct((n, CONV_CH), F32), jax.ShapeDtypeStruct((n, 2, CONV_CH), F32),
                   jax.ShapeDtypeStruct((n, SHIFT_W), F32), row, row,
                   jax.ShapeDtypeStruct((len(_STEP_VECS), RW, n), F32),
                   jax.ShapeDtypeStruct((D_MODEL, IN_W), BF16), jax.ShapeDtypeStruct(_SQ, BF16),
                   jax.ShapeDtypeStruct(_SQ, BF16), jax.ShapeDtypeStruct(_PP, BF16),
                   jax.ShapeDtypeStruct((3, CONV_CH), F32), jax.ShapeDtypeStruct((1, RW), F32)],
        scratch_shapes=[pltpu.VMEM((n, D_MODEL), BF16), pltpu.VMEM((W_IN_BLOCKS, n, wblk), F32)],
        compiler_params=_SAMPLE_PARAMS,
        name="sample_front",
    )(x3d, w["norm_g"].reshape(1, D_MODEL), w["w_in_f32"], w["w_out_f32"], w["w_pg_f32"], w["w_pp_f32"],
      state_conv, state_shift, *w["prm_raw"], w["headones"])
    ya, conv_new, shift_new, g_r, bonus, vecs, w_in, w_out, w_pg, w_pp, cw_rows, rk_row = outs
    prm = tuple({"conv_w": cw_rows, "r_k": rk_row}.get(name, arr) for name, arr in zip(_PRM_NAMES, w["prm_raw"]))
    derived = {"w_in": w_in, "w_out": w_out, "w_pg": w_pg, "w_pp": w_pp, "prm": prm}
    return (ya, g_r, bonus), conv_new, shift_new, vecs.reshape(len(_STEP_VECS), N_HEADS, HEAD_DIM, n), derived


def _sample_back(y_t, token_parts, x3d, p3d, w):
    n = x3d.shape[0]
    ya, g_r, bonus = token_parts
    return pl.pallas_call(
        _sample_back_kernel,
        grid=(1,),
        in_specs=[_full((RW, n)), _full((n, RW)), _full((n, RW)), _full((n, CONV_CH)),
                  _full((n, 1, D_MODEL)), _full((n, 1, D_PLE)), _full(_SQ), _full(_SQ), _full(_PP),
                  _full((1, D_MODEL))]
        + _prm_specs() + [_full((HEADSUM_W, HEADSUM_W))],
        out_specs=_full((n, 1, D_MODEL)),
        out_shape=jax.ShapeDtypeStruct((n, 1, D_MODEL), F32),
        compiler_params=_SAMPLE_PARAMS,
        name="sample_back",
    )(y_t.reshape(RW, n), bonus, g_r, ya, x3d, p3d, w["w_out"], w["w_pg"], w["w_pp"],
      w["final_g"].reshape(1, D_MODEL), *w["prm"], w["headones"])


def _layer(x_prompt, p_prompt, x_sample, p_sample, state_conv, state_shift, state_wkv, w, tc=256):
    bsz, t_len, _ = x_prompt.shape
    token_parts, conv_s, shift_s, vecs, derived = _sample_front(x_sample, state_conv, state_shift, w)
    w = {**w, **derived}
    s_hvkb = jnp.transpose(state_wkv, (1, 2, 3, 0))
    y_p, conv_p, shift_p, wkv_p, s1, y_t = _prompt_layer(
        x_prompt.reshape(bsz * t_len, D_MODEL), p_prompt.reshape(bsz * t_len, D_PLE), w, bsz, t_len, tc,
        s_hvkb, vecs)
    y_s = _sample_back(y_t, token_parts, x_sample, p_sample, w)
    prompt_out = (y_p.reshape(bsz, t_len, D_MODEL), conv_p, shift_p, wkv_p)
    sample_out = (y_s, conv_s, shift_s, jnp.transpose(s1, (3, 0, 1, 2)))
    return prompt_out, sample_out


def kernel(x_prompt, x_sample, p_prompt, p_sample, state_conv, state_shift, state_wkv, norm_g, w_in, conv_w, mu_shift, w0, w_up, a0, a_up, k_k, k_a, r_k, ln_w, ln_b, w_out, w_pg, w_pp, final_g):
    depth = norm_g.shape[0]
    assert depth == 1
    i = 0
    head_id = np.arange(HEADSUM_W) // HEAD_DIM
    w = {
        "norm_g": norm_g[i],
        "w_in_f32": w_in[i],
        "w_out_f32": w_out[i],
        "w_pg_f32": w_pg[i],
        "w_pp_f32": w_pp[i],
        "final_g": final_g,
        "prm_raw": _prm_arrays(jnp.transpose(conv_w[i:i + 1], (1, 0, 2)), mu_shift[i], w0[i], w_up[i], a0[i],
                               a_up[i], k_k[i], k_a[i], r_k[i], ln_w[i], ln_b[i]),
        "headones": jnp.asarray(head_id[:, None] == head_id[None, :], BF16),
    }
    (yp, cp, sp, wp), (ys, cs, ss, ws) = _layer(x_prompt, p_prompt[i], x_sample, p_sample[i], state_conv[i],
                                                state_shift[i], state_wkv[i], w)
    return (yp, ys, cp[None], sp[None], wp[None], cs[None], ss[None], ws[None])
```

```python
import functools
import math

import jax
import jax.numpy as jnp
import numpy as np
from jax import lax
from jax.experimental import pallas as pl
from jax.experimental.pallas import tpu as pltpu

F32 = jnp.float32
BF16 = jnp.bfloat16

D_MODEL = 1024
CONV_CH = 512
RW = 512
HEAD_DIM = 64
N_HEADS = 8
N_PAIRS = N_HEADS // 2
LORA = 64
D_PLE = 256
SHIFT_W = 3 * RW + 2 * LORA
IN_W = 4 * CONV_CH + SHIFT_W + RW
OFF_ZS = 4 * CONV_CH
OFF_GR = OFF_ZS + SHIFT_W
RMS_EPS = 1e-6
GN_EPS = 64e-5
DECAY_SCALE = math.exp(-0.5)

LANES = 128
SUBLANES = 8
MXU_DIM = 256
HEADSUM_W = MXU_DIM
SUM_PARTS = 1
CUM_PARTS = 1
LORA_PASSES = 1
CHUNK = 64
W_IN_BLOCKS = 3
CARRY_ROWS = 8
VMEM_LIMIT = 56 * 1024 * 1024


def _dot(a, b):
    return jnp.dot(a, b, preferred_element_type=F32)


def _dot_nt(a, b):
    return lax.dot_general(a, b, (((1,), (1,)), ((), ())), preferred_element_type=F32)


def _dot_tn(a, b):
    return lax.dot_general(a, b, (((0,), (0,)), ((), ())), preferred_element_type=F32)


def _split(x, parts):
    out = []
    rem = x
    for i in range(parts):
        t = rem.astype(BF16)
        out.append(t)
        if i + 1 < parts:
            rem = rem - t.astype(F32)
    return out


def _const_dot(c_bf16, x, parts):
    acc = None
    for t in _split(x, parts):
        d = _dot(c_bf16, t)
        acc = d if acc is None else acc + d
    return acc


def _x_dot_const(x, c_bf16, parts):
    acc = None
    for t in _split(x, parts):
        d = _dot(t, c_bf16)
        acc = d if acc is None else acc + d
    return acc


def _head_sum(x, headones):
    w = headones.shape[0]
    halves = [_x_dot_const(x[:, i * w:(i + 1) * w], headones, SUM_PARTS) for i in range(x.shape[1] // w)]
    return jnp.concatenate(halves, axis=1)


def _mm(a, b, passes):
    if passes == 1:
        return _dot(a.astype(BF16), b.astype(BF16))
    ah, al = _split(a, 2)
    bh, bl = _split(b, 2)
    return _dot(ah, bh) + _dot(al, bh) + _dot(ah, bl)


def _sigmoid(x):
    return 1.0 / (1.0 + jnp.exp(-x))


def _silu(x):
    return x * _sigmoid(x)


def _rwkv_tokens(r, k, v, wa, prm, headones, between_steps=lambda: None):
    n = r.shape[0]
    lane = lax.broadcasted_iota(jnp.int32, (n, LANES), 1)
    th = jnp.where(lane < LORA, jnp.tanh(wa), wa)
    lora = _mm(th, prm["wlora"], LORA_PASSES)
    lw = -DECAY_SCALE * _sigmoid(prm["w0"] + lora[:, :RW])
    a = _sigmoid(prm["a0"] + lora[:, RW:])
    between_steps()
    kk = k * prm["k_k"]
    ss = _head_sum(kk * kk, headones)
    kk = kk / jnp.maximum(jnp.sqrt(ss), 1e-12)
    between_steps()
    kmod = k * (1.0 + (a - 1.0) * prm["k_a"])
    bonus = _head_sum(r * kmod * prm["r_k"], headones) * v
    between_steps()
    return lw, a, kk, kmod, bonus


def _rwkv_post(o, bonus, g_r, prm, headones):
    mean = _head_sum(o, headones) * (1.0 / HEAD_DIM)
    d = o - mean
    var = _head_sum(d * d, headones) * (1.0 / HEAD_DIM)
    on = d * lax.rsqrt(var + GN_EPS)
    on = on * prm["ln_w"] + prm["ln_b"]
    return (on + bonus) * _silu(g_r)


_PRM_NAMES = ("conv_w", "mu", "w0", "a0", "w_up", "a_up", "k_k", "k_a", "r_k", "ln_w", "ln_b")


def _prm_arrays(conv_w, mu_shift, w0, w_up, a0, a_up, k_k, k_a, r_k, ln_w, ln_b):
    row = lambda x: x.reshape(1, -1)
    return (conv_w, row(mu_shift), row(w0), row(a0), w_up, a_up, row(k_k), row(k_a),
            r_k, row(ln_w), row(ln_b))


def _prm_specs():
    zero = (lambda *_: (0, 0))
    shapes = ((3, CONV_CH), (1, SHIFT_W), (1, RW), (1, RW), (LORA, RW), (LORA, RW),
              (1, RW), (1, RW), (1, RW), (1, RW), (1, RW))
    return [pl.BlockSpec(s, zero) for s in shapes]


def _load_prm(refs):
    prm = {n: r[...] for n, r in zip(_PRM_NAMES, refs)}
    zero = jnp.zeros((LORA, RW), F32)
    prm["wlora"] = jnp.concatenate([jnp.concatenate([prm["w_up"], zero], axis=1),
                                    jnp.concatenate([zero, prm["a_up"]], axis=1)], axis=0)
    return prm


def _stack_heads(x):
    lane = lax.broadcasted_iota(jnp.int32, x.shape, 1)
    lo = jnp.where(lane < HEAD_DIM, x, 0.0)
    hi = jnp.where(lane >= HEAD_DIM, x, 0.0)
    return jnp.concatenate([lo, hi], axis=0)


def _stack_heads_bf16(x):
    lane = lax.broadcasted_iota(jnp.int32, x.shape, 1)
    lo = jnp.where(lane < HEAD_DIM, 1.0, 0.0).astype(BF16)
    xb = x.astype(BF16)
    return jnp.concatenate([xb * lo, xb * (1.0 - lo)], axis=0)


def _bdot(a, b):
    return _dot(a.astype(BF16), b.astype(BF16))


def _diag_block_inverses_minus_eye(mats, lane_bcast):
    n = mats[0].shape[0]
    nb = n // SUBLANES
    row = lax.broadcasted_iota(jnp.int32, (n, n), 0)
    col = lax.broadcasted_iota(jnp.int32, (n, n), 1)
    in_diag = ((row // SUBLANES) == (col // SUBLANES)) & (row > col)
    packed = []
    for a in mats:
        d = jnp.where(in_diag, a, 0.0)
        acc = d[0:SUBLANES]
        for m in range(1, nb):
            acc = acc + d[m * SUBLANES:(m + 1) * SUBLANES]
        packed.append(acc)
    d_all = jnp.concatenate(packed, axis=0).astype(BF16)
    sub = lax.broadcasted_iota(jnp.int32, (SUBLANES, n), 0)
    lane = lax.broadcasted_iota(jnp.int32, (SUBLANES, n), 1)
    eye8 = jnp.where(sub == lane % SUBLANES, 1.0, 0.0)
    ts = [eye8 for _ in mats]
    for j in range(SUBLANES - 1):
        spread = _dot(d_all, lane_bcast[j])
        ts = [t + spread[i * SUBLANES:(i + 1) * SUBLANES] * jnp.broadcast_to(t[j:j + 1, :], (SUBLANES, n))
              for i, t in enumerate(ts)]
    blk = lax.broadcasted_iota(jnp.int32, (SUBLANES, n), 1) // SUBLANES
    out = []
    for t in ts:
        tm1 = t - eye8
        out.append(jnp.concatenate([jnp.where(blk == m, tm1, 0.0) for m in range(nb)], axis=0))
    return out


def _unit_lower_inverse_minus_eye(mats, lane_bcast, between_levels=lambda: None):
    n = mats[0].shape[0]
    row = lax.broadcasted_iota(jnp.int32, (n, n), 0)
    col = lax.broadcasted_iota(jnp.int32, (n, n), 1)
    xs = _diag_block_inverses_minus_eye(mats, lane_bcast)
    size = 2 * SUBLANES
    while size <= CHUNK:
        half = size // 2
        sel = ((row // size) == (col // size)) & ((row % size) >= half) & ((col % size) < half)
        als = [jnp.where(sel, a, 0.0) for a in mats]
        ps = [al + _bdot(x, al) for x, al in zip(xs, als)]
        xs = [x + p + _bdot(p, x) for x, p in zip(xs, ps)]
        between_levels()
        size *= 2
    return xs


def _chunk_operators(ld, strict, incl, lane_bcast, side_work=()):
    side_work = list(side_work)

    def breathe():
        if side_work:
            side_work.pop(0)()

    cat0 = lambda xs: jnp.concatenate(xs, axis=0)
    la = [_stack_heads(x) for x in ld("la")]
    lr = [_stack_heads(x) for x in ld("lr")]
    n2 = 2 * CHUNK
    amat = [_dot_nt(cat0([a, b]).astype(BF16), cat0([c, d]).astype(BF16))
            for a, b, c, d in zip(la, lr, ld("rb"), ld("rk"))]

    def block_diag_pair(m):
        swapped = pltpu.roll(m, CHUNK, axis=1)
        first = cat0([m[0:CHUNK], swapped[CHUNK:n2]])
        second = cat0([swapped[0:CHUNK], m[CHUNK:n2]])
        return first, second

    breathe()
    top = [block_diag_pair(m[0:n2]) for m in amat]
    a_ab = [jnp.where(strict, t[0], 0.0) for t in top]
    xs = _unit_lower_inverse_minus_eye(a_ab, lane_bcast, breathe)
    vst = [_stack_heads_bf16(x) for x in ld("v")]
    av = [_dot(jnp.where(strict, t[1], 0.0).astype(BF16), v) for t, v in zip(top, vst)]
    breathe()
    uv0 = [y + _bdot(x, y) for x, y in zip(xs, av)]
    tla = [y + _bdot(x, y) for x, y in zip(xs, la)]
    breathe()
    bottom = [block_diag_pair(m[n2:2 * n2]) for m in amat]
    a_rb = [jnp.where(incl, t[0], 0.0) for t in bottom]
    a_rk = [jnp.where(incl, t[1], 0.0) for t in bottom]
    gop = [y + _bdot(a, t) for y, a, t in zip(lr, a_rb, tla)]
    uv = [cat0([u.astype(BF16), v]) for u, v in zip(uv0, vst)]
    y0 = [_dot(jnp.concatenate([a, b], axis=1).astype(BF16), w) for a, b, w in zip(a_rb, a_rk, uv)]
    breathe()
    bt = [_stack_heads_bf16(x) for x in ld("bt")]
    kt = [_stack_heads_bf16(x) for x in ld("kt")]
    pop = [_dot_tn(t.astype(BF16), b) for t, b in zip(tla, bt)]
    qop = [_dot_tn(w, cat0([b, k])) for w, b, k in zip(uv, bt, kt)]
    while side_work:
        breathe()
    return gop, y0, pop, qop


def _prompt_kernel(xh_ref, xt_ref, pt_ref, ng_ref, win_ref, wo_ref, wg_ref, wp_ref, fg_ref, *refs,
                   tc, n_t, n_tiles):
    prm_refs = refs[:len(_PRM_NAMES)]
    headones_ref, tri_ref, bcast_ref, st_ref, vecs_ref = refs[len(_PRM_NAMES):len(_PRM_NAMES) + 5]
    smp_refs = refs[len(_PRM_NAMES) + 5:len(_PRM_NAMES) + 10]
    (y_ref, conv_out_ref, shift_out_ref, wkv_out_ref, st1_ref, y_smp_ref) = refs[len(_PRM_NAMES) + 10:len(_PRM_NAMES) + 16]
    (z_s, ycat_s, ubuf, zsbuf, s_ref, la_s, lr_s, rb_s, rk_s, bt_s, kt_s, v_s, gam_s, o_s,
     g_sc, y0_sc, p_sc, q_sc, bonus_p, gr_p, ot_s) = refs[len(_PRM_NAMES) + 16:]

    s = pl.program_id(0)
    head_t = lax.rem(s, jnp.int32(n_t))
    tail_t = lax.rem(s + (n_t - 1), jnp.int32(n_t))
    prm = _load_prm(prm_refs)
    headones = headones_ref[...]
    n2 = 2 * CHUNK
    n_chunks = tc // CHUNK

    @pl.when(s == 0)
    def _():
        for ref in (g_sc, y0_sc, p_sc, q_sc, gam_s, bonus_p, gr_p, ycat_s, s_ref):
            ref[...] = jnp.zeros(ref.shape, ref.dtype)

    @pl.when(head_t == 0)
    def _():
        ubuf[0:CARRY_ROWS, :] = jnp.zeros((CARRY_ROWS, CONV_CH), F32)
        zsbuf[0:CARRY_ROWS, :] = jnp.zeros((CARRY_ROWS, SHIFT_W), F32)

    @pl.when(tail_t == 0)
    def _():
        s_ref[...] = jnp.zeros(s_ref.shape, F32)

    tail = {}

    def tail_state_pass(c):
        def run():
            for j in range(N_PAIRS):
                idx = c * N_PAIRS + j
                lanes = slice(j * LANES, (j + 1) * LANES)
                s_old = s_ref[j]
                s_bf = s_old.astype(BF16)
                yst = _dot_nt(g_sc[idx], s_bf) + y0_sc[idx]
                o_s[c * CHUNK:(c + 1) * CHUNK, lanes] = yst[0:CHUNK] + yst[CHUNK:n2]
                gam = gam_s[c * CHUNK:c * CHUNK + 1, lanes]
                s_ref[j] = s_old * gam + _dot(s_bf, p_sc[idx]) + q_sc[idx]
        return run

    def tail_post():
        ycat_s[:, CONV_CH:2 * CONV_CH] = _rwkv_post(o_s[...], bonus_p[...], gr_p[...], prm, headones)

    def tail_out_proj():
        tail["h1"] = xt_ref[...] + _dot(ycat_s[...].astype(BF16), wo_ref[...])

    def tail_gate():
        tail["gate"] = _sigmoid(_dot(tail["h1"].astype(BF16), wg_ref[...]))
        tail["pe"] = _dot(pt_ref[...].astype(BF16), wp_ref[...])

    def tail_norm():
        h2 = tail["h1"] + tail["gate"] * tail["pe"]
        ms = jnp.mean(h2 * h2, axis=-1, keepdims=True)
        y_ref[...] = h2 * lax.rsqrt(ms + RMS_EPS) * fg_ref[...]

    x = xh_ref[...]
    xn = (x * lax.rsqrt(jnp.mean(x * x, axis=-1, keepdims=True) + RMS_EPS) * ng_ref[...]).astype(BF16)

    def project(c0):
        def run():
            z_s[:, c0:c0 + CONV_CH] = _dot(xn, win_ref[:, c0:c0 + CONV_CH])
        return run

    passes = [tail_state_pass(c) for c in range(n_chunks)]
    blocks = [project(c0) for c0 in (OFF_GR, 0, CONV_CH, 2 * CONV_CH, 3 * CONV_CH)]
    side_queue = []
    late_blocks, blocks = blocks[-1:], blocks[:-1]
    while passes or blocks:
        side_queue += passes[:1] + blocks[:1]
        passes, blocks = passes[1:], blocks[1:]
    side_queue += [tail_post, tail_out_proj, tail_gate, tail_norm]

    def project_next(count=2):
        for _ in range(count):
            if side_queue:
                side_queue.pop(0)()

    z_s[:, OFF_ZS:OFF_ZS + SHIFT_W] = _dot(xn, win_ref[:, OFF_ZS:OFF_ZS + SHIFT_W])

    blocks_per_head = HEAD_DIM // STATE_ROWS
    state_blk = jnp.minimum(s, n_tiles - 1)
    v_row0 = lax.rem(state_blk, jnp.int32(blocks_per_head)) * STATE_ROWS
    _sample_state_rows(st_ref, vecs_ref, st1_ref, ot_s, v_row0, state_blk * STATE_ROWS)

    zs = z_s[:, OFF_ZS:OFF_ZS + SHIFT_W]
    zsbuf[CARRY_ROWS:CARRY_ROWS + tc, :] = zs
    zprev = zsbuf[CARRY_ROWS - 1:CARRY_ROWS - 1 + tc, :]
    zm = zs + (zprev - zs) * prm["mu"]
    zsbuf[CARRY_ROWS - 1:CARRY_ROWS, :] = zsbuf[CARRY_ROWS + tc - 1:CARRY_ROWS + tc, :]

    project_next()
    r = zm[:, 0:RW]
    k = zm[:, RW:2 * RW]
    v = zm[:, 2 * RW:3 * RW]
    wa = zm[:, 3 * RW:3 * RW + 2 * LORA]
    lw, a, kk, kmod, bonus = _rwkv_tokens(r, k, v, wa, prm, headones, project_next)

    g = _const_dot(tri_ref[...], lw, CUM_PARTS)
    gc = jnp.concatenate(
        [jnp.broadcast_to(g[(c + 1) * CHUNK - 1:(c + 1) * CHUNK, :], (CHUNK, RW)) for c in range(tc // CHUNK)],
        axis=0)
    project_next()
    eng = jnp.exp(-g)
    etail = jnp.exp(gc - g)
    b = kk * a
    la_s[...] = -kk * jnp.exp(g - lw)
    lr_s[...] = r * jnp.exp(g)
    project_next()
    rb_s[...] = b * eng
    rk_s[...] = kmod * eng
    bt_s[...] = b * etail
    kt_s[...] = kmod * etail
    v_s[...] = v
    while side_queue:
        project_next()
    gam_s[...] = jnp.exp(gc)

    row = lax.broadcasted_iota(jnp.int32, (n2, n2), 0)
    col = lax.broadcasted_iota(jnp.int32, (n2, n2), 1)
    same = (row // CHUNK) == (col // CHUNK)
    strict = same & (row > col)
    incl = same & (row >= col)

    cw = prm["conv_w"]

    def conv_rows(r0):
        def run():
            rows = slice(r0, r0 + CHUNK)
            u = z_s[rows, CONV_CH:2 * CONV_CH] * z_s[rows, 2 * CONV_CH:3 * CONV_CH]
            ubuf[CARRY_ROWS + r0:CARRY_ROWS + r0 + CHUNK, :] = u
            um1 = ubuf[CARRY_ROWS - 1 + r0:CARRY_ROWS - 1 + r0 + CHUNK, :]
            um2 = ubuf[CARRY_ROWS - 2 + r0:CARRY_ROWS - 2 + r0 + CHUNK, :]
            conv = cw[0:1, :] * um2 + cw[1:2, :] * um1 + cw[2:3, :] * u
            ycat_s[rows, 0:CONV_CH] = z_s[rows, 0:CONV_CH] * conv * _silu(z_s[rows, 3 * CONV_CH:4 * CONV_CH])
        return run

    srcs = {"la": la_s, "lr": lr_s, "rb": rb_s, "rk": rk_s, "bt": bt_s, "kt": kt_s, "v": v_s}
    members = [(c, j) for c in range(n_chunks) for j in range(N_PAIRS)]

    def ld(name):
        ref = srcs[name]
        return [ref[c * CHUNK:(c + 1) * CHUNK, j * LANES:(j + 1) * LANES] for c, j in members]

    lane_bcast = [bcast_ref[j] for j in range(SUBLANES - 1)]
    gop, y0, pop, qop = _chunk_operators(ld, strict, incl, lane_bcast,
                                         late_blocks + [conv_rows(c * CHUNK) for c in range(n_chunks)])
    for (c, j), g_, y_, p_, q_ in zip(members, gop, y0, pop, qop):
        idx = c * N_PAIRS + j
        g_sc[idx] = g_.astype(BF16)
        y0_sc[idx] = y_
        p_sc[idx] = p_.astype(BF16)
        q_sc[idx] = q_

    ubuf[CARRY_ROWS - 2:CARRY_ROWS, :] = ubuf[CARRY_ROWS + tc - 2:CARRY_ROWS + tc, :]

    bonus_p[...] = bonus
    gr_p[...] = z_s[:, OFF_GR:OFF_GR + RW]

    @pl.when((head_t == n_t - 1) & (s < n_tiles))
    def _():
        seq = lax.div(s, jnp.int32(n_t))
        shift_out_ref[pl.ds(seq, 1), :] = zsbuf[CARRY_ROWS - 1:CARRY_ROWS, :]
        conv_out_ref[0] = ubuf[CARRY_ROWS - 2:CARRY_ROWS, :]

    @pl.when((tail_t == n_t - 1) & (s > 0))
    def _():
        for h in range(N_HEADS):
            j, i = divmod(h, 2)
            blk = s_ref[j]
            wkv_out_ref[0, h] = blk[i * HEAD_DIM:(i + 1) * HEAD_DIM, i * HEAD_DIM:(i + 1) * HEAD_DIM]

    @pl.when(s == n_tiles)
    def _():
        bonus_ref, gr_ref, ya_ref, xs_ref, ps_ref = smp_refs
        y_r = _rwkv_post(ot_s[...].T, bonus_ref[...], gr_ref[...], _load_prm(prm_refs), headones_ref[...])
        ycat = jnp.concatenate([ya_ref[...], y_r], axis=1).astype(BF16)
        h1 = xs_ref[:, 0, :] + _dot(ycat, wo_ref[...])
        gate = _sigmoid(_dot(h1.astype(BF16), wg_ref[...]))
        pe = _dot(ps_ref[:, 0, :].astype(BF16), wp_ref[...])
        h2 = h1 + gate * pe
        ms = jnp.mean(h2 * h2, axis=-1, keepdims=True)
        y_smp_ref[:, 0, :] = h2 * lax.rsqrt(ms + RMS_EPS) * fg_ref[...]


def _prompt_layer(x2d, p2d, w, bsz, t_len, tc, state_hvkb, step_vecs, sample_parts):
    n_t = t_len // tc
    n_seq = state_hvkb.shape[-1]
    once3 = lambda width: pl.BlockSpec((n_seq, 1, width), lambda s: (0, 0, 0), pipeline_mode=pl.Buffered(1))
    blocks_per_head = HEAD_DIM // STATE_ROWS
    assert bsz * n_t == N_HEADS * blocks_per_head, "one state block per prompt tile"
    state_blk = lambda s: jnp.minimum(s, bsz * n_t - 1)
    tok = np.arange(tc)
    blockones = (tok[:, None] // CHUNK) == (tok[None, :] // CHUNK)
    tri = jnp.asarray(blockones & (tok[:, None] >= tok[None, :]), BF16)
    n_tiles = bsz * n_t
    n2 = 2 * CHUNK
    idx = np.arange(n2)
    same_blk = (idx[:, None] // SUBLANES) == (idx[None, :] // SUBLANES)
    lane_bcast = jnp.asarray(
        np.stack([same_blk & (idx[:, None] % SUBLANES == j) for j in range(SUBLANES - 1)]), BF16)
    kern = functools.partial(_prompt_kernel, tc=tc, n_t=n_t, n_tiles=n_tiles)
    big = lambda: pltpu.VMEM((tc, RW), F32)
    n_blk = (tc // CHUNK) * N_PAIRS
    op = lambda dt: pltpu.VMEM((n_blk, LANES, LANES), dt)
    head = lambda s: jnp.minimum(s, n_tiles - 1)
    tail = lambda s: jnp.maximum(s - 1, 0)
    head_tile = lambda width: pl.BlockSpec((tc, width), lambda s: (head(s), 0))
    tail_tile = lambda width: pl.BlockSpec((tc, width), lambda s: (tail(s), 0))
    const = lambda shape: pl.BlockSpec(shape, lambda s: (0, 0), pipeline_mode=pl.Buffered(1))
    return pl.pallas_call(
        kern,
        grid=(n_tiles + 1,),
        in_specs=[head_tile(D_MODEL), tail_tile(D_MODEL), tail_tile(D_PLE), const((1, D_MODEL)),
                  const((D_MODEL, IN_W)), const((D_MODEL, D_MODEL)), const((D_MODEL, D_MODEL)),
                  const((D_PLE, D_MODEL)), const((1, D_MODEL))]
        + _prm_specs()
        + [const((HEADSUM_W, HEADSUM_W)), const((tc, tc)),
           pl.BlockSpec((SUBLANES - 1, n2, n2), lambda s: (0, 0, 0), pipeline_mode=pl.Buffered(1)),
           pl.BlockSpec((1, STATE_ROWS, HEAD_DIM, n_seq),
                        lambda s: (state_blk(s) // blocks_per_head, state_blk(s) % blocks_per_head, 0, 0)),
           pl.BlockSpec((len(_STEP_VECS), 1, HEAD_DIM, n_seq),
                        lambda s: (0, state_blk(s) // blocks_per_head, 0, 0)),
           const((n_seq, RW)), const((n_seq, RW)), const((n_seq, CONV_CH)), once3(D_MODEL), once3(D_PLE)],
        out_specs=[
            tail_tile(D_MODEL),
            pl.BlockSpec((1, 2, CONV_CH), lambda s: (head(s) // n_t, 0, 0)),
            pl.BlockSpec((bsz, SHIFT_W), lambda s: (0, 0)),
            pl.BlockSpec((1, N_HEADS, HEAD_DIM, HEAD_DIM), lambda s: (tail(s) // n_t, 0, 0, 0)),
            pl.BlockSpec((1, STATE_ROWS, HEAD_DIM, n_seq),
                         lambda s: (state_blk(s) // blocks_per_head, state_blk(s) % blocks_per_head, 0, 0)),
            pl.BlockSpec((n_seq, 1, D_MODEL), lambda s: (0, 0, 0)),
        ],
        out_shape=[
            jax.ShapeDtypeStruct((bsz * t_len, D_MODEL), F32),
            jax.ShapeDtypeStruct((bsz, 2, CONV_CH), F32),
            jax.ShapeDtypeStruct((bsz, SHIFT_W), F32),
            jax.ShapeDtypeStruct((bsz, N_HEADS, HEAD_DIM, HEAD_DIM), F32),
            jax.ShapeDtypeStruct(state_hvkb.shape, F32),
            jax.ShapeDtypeStruct((n_seq, 1, D_MODEL), F32),
        ],
        scratch_shapes=[
            pltpu.VMEM((tc, IN_W), F32),
            pltpu.VMEM((tc, D_MODEL), F32),
            pltpu.VMEM((CARRY_ROWS + tc, CONV_CH), F32),
            pltpu.VMEM((CARRY_ROWS + tc, SHIFT_W), F32),
            pltpu.VMEM((N_PAIRS, LANES, LANES), F32),
            big(), big(), big(), big(), big(), big(), big(), big(), big(),
            op(BF16), op(F32), op(BF16), op(F32),
            big(), big(),
            pltpu.VMEM((RW, n_seq), F32),
        ],
        compiler_params=pltpu.CompilerParams(
            dimension_semantics=("arbitrary",), vmem_limit_bytes=VMEM_LIMIT),
        name="prompt_layer",
    )(x2d, x2d, p2d, w["norm_g"].reshape(1, D_MODEL), w["w_in"], w["w_out"], w["w_pg"], w["w_pp"],
      w["final_g"].reshape(1, D_MODEL), *w["prm"], w["headones"], tri, lane_bcast, state_hvkb, step_vecs,
      *sample_parts)


def _sample_front_kernel(x_ref, ng_ref, win_ref, wo_ref, wg_ref, wp_ref, cb_ref, sb_ref, *refs):
    prm_refs = refs[:len(_PRM_NAMES)]
    headones_ref = refs[len(_PRM_NAMES)]
    (ya_ref, conv_out_ref, shift_out_ref, gr_ref, bonus_ref, vecs_ref,
     wbf_ref, wo_bf_ref, wg_bf_ref, wp_bf_ref, cw_rows_ref, rk_row_ref, xn_s, z_s) = refs[len(_PRM_NAMES) + 1:]
    j = pl.program_id(0)

    @pl.when(j == 0)
    def _():
        x = x_ref[:, 0, :]
        xn_s[...] = (x * lax.rsqrt(jnp.mean(x * x, axis=-1, keepdims=True) + RMS_EPS) * ng_ref[...]).astype(BF16)
        wo_bf_ref[...] = wo_ref[...].astype(BF16)
        wg_bf_ref[...] = wg_ref[...].astype(BF16)
        wp_bf_ref[...] = wp_ref[...].astype(BF16)

    w_blk = win_ref[...].astype(BF16)
    wbf_ref[...] = w_blk
    z_s[j] = _dot(xn_s[...], w_blk)

    @pl.when(j == pl.num_programs(0) - 1)
    def _():
        _sample_front_tokens(z_s, cb_ref, sb_ref, prm_refs, headones_ref, ya_ref, conv_out_ref, shift_out_ref,
                             gr_ref, bonus_ref, vecs_ref, cw_rows_ref, rk_row_ref)


def _sample_front_tokens(z_s, cb_ref, sb_ref, prm_refs, headones_ref, ya_ref, conv_out_ref, shift_out_ref,
                         gr_ref, bonus_ref, vecs_ref, cw_rows_ref, rk_row_ref):
    prm = _load_prm(prm_refs)
    prm["conv_w"] = jnp.concatenate([prm["conv_w"][t] for t in range(3)], axis=0)
    prm["r_k"] = jnp.concatenate([prm["r_k"][h:h + 1, :] for h in range(N_HEADS)], axis=1)
    cw_rows_ref[...] = prm["conv_w"]
    rk_row_ref[...] = prm["r_k"]
    headones = headones_ref[...]
    z = jnp.concatenate([z_s[i] for i in range(z_s.shape[0])], axis=1)

    u = z[:, CONV_CH:2 * CONV_CH] * z[:, 2 * CONV_CH:3 * CONV_CH]
    cb0 = cb_ref[:, 0, :]
    cb1 = cb_ref[:, 1, :]
    cw = prm["conv_w"]
    conv = cw[0:1, :] * cb0 + cw[1:2, :] * cb1 + cw[2:3, :] * u
    ya_ref[...] = z[:, 0:CONV_CH] * conv * _silu(z[:, 3 * CONV_CH:4 * CONV_CH])
    conv_out_ref[:, 0, :] = cb1
    conv_out_ref[:, 1, :] = u

    zs = z[:, OFF_ZS:OFF_ZS + SHIFT_W]
    shift_out_ref[...] = zs
    gr_ref[...] = z[:, OFF_GR:OFF_GR + RW]
    zm = zs + (sb_ref[...] - zs) * prm["mu"]
    r = zm[:, 0:RW]
    k = zm[:, RW:2 * RW]
    v = zm[:, 2 * RW:3 * RW]
    wa = zm[:, 3 * RW:3 * RW + 2 * LORA]
    lw, a, kk, kmod, bonus = _rwkv_tokens(r, k, v, wa, prm, headones)
    bonus_ref[...] = bonus
    for i, vec in enumerate((-kk, jnp.exp(lw), kk * a, kmod, v, r)):
        vecs_ref[i] = vec.T


_STEP_VECS = ("nkk", "decay", "b", "kmod", "v", "r")
STATE_ROWS = 8


def _sample_state_rows(st_ref, vecs_ref, s1_ref, y_ref, v_row0, y_row0):
    nkk, w, b, km = (vecs_ref[i, 0] for i in range(4))
    r = vecs_ref[5, 0]
    for vi in range(STATE_ROWS):
        s0 = st_ref[0, vi]
        sa = jnp.sum(s0 * nkk, axis=0, keepdims=True)
        v_row = vecs_ref[4, 0, pl.ds(v_row0 + vi, 1), :]
        s1 = s0 * w + sa * b + v_row * km
        s1_ref[0, vi] = s1
        y_ref[pl.ds(y_row0 + vi, 1), :] = jnp.sum(s1 * r, axis=0, keepdims=True)


def _full(shape):
    nd = len(shape)
    return pl.BlockSpec(shape, lambda *_: (0,) * nd)


_SAMPLE_PARAMS = pltpu.CompilerParams(dimension_semantics=("arbitrary",), vmem_limit_bytes=VMEM_LIMIT)
_SQ, _PP = (D_MODEL, D_MODEL), (D_PLE, D_MODEL)


def _sample_front(x3d, state_conv, state_shift, w):
    n = x3d.shape[0]
    row = jax.ShapeDtypeStruct((n, RW), F32)
    wblk = IN_W // W_IN_BLOCKS
    once = lambda shape: pl.BlockSpec(shape, lambda j: (0,) * len(shape), pipeline_mode=pl.Buffered(1))
    raw_shapes = {"conv_w": (3, 1, CONV_CH), "r_k": (N_HEADS, HEAD_DIM)}
    prm_specs = [_full(raw_shapes[name]) if name in raw_shapes else spec
                 for name, spec in zip(_PRM_NAMES, _prm_specs())]
    outs = pl.pallas_call(
        _sample_front_kernel,
        grid=(W_IN_BLOCKS,),
        in_specs=[_full((n, 1, D_MODEL)), _full((1, D_MODEL)), pl.BlockSpec((D_MODEL, wblk), lambda j: (0, j)),
                  once(_SQ), once(_SQ), once(_PP), _full((n, 2, CONV_CH)), _full((n, SHIFT_W))]
        + prm_specs + [_full((HEADSUM_W, HEADSUM_W))],
        out_specs=[_full((n, CONV_CH)), _full((n, 2, CONV_CH)), _full((n, SHIFT_W)),
                   _full((n, RW)), _full((n, RW)), _full((len(_STEP_VECS), RW, n)),
                   pl.BlockSpec((D_MODEL, wblk), lambda j: (0, j)), once(_SQ), once(_SQ), once(_PP),
                   _full((3, CONV_CH)), _full((1, RW))],
        out_shape=[jax.ShapeDtypeStruct((n, CONV_CH), F32), jax.ShapeDtypeStruct((n, 2, CONV_CH), F32),
                   jax.ShapeDtypeStruct((n, SHIFT_W), F32), row, row,
                   jax.ShapeDtypeStruct((len(_STEP_VECS), RW, n), F32),
                   jax.ShapeDtypeStruct((D_MODEL, IN_W), BF16), jax.ShapeDtypeStruct(_SQ, BF16),
                   jax.ShapeDtypeStruct(_SQ, BF16), jax.ShapeDtypeStruct(_PP, BF16),
                   jax.ShapeDtypeStruct((3, CONV_CH), F32), jax.ShapeDtypeStruct((1, RW), F32)],
        scratch_shapes=[pltpu.VMEM((n, D_MODEL), BF16), pltpu.VMEM((W_IN_BLOCKS, n, wblk), F32)],
        compiler_params=_SAMPLE_PARAMS,
        name="sample_front",
    )(x3d, w["norm_g"].reshape(1, D_MODEL), w["w_in_f32"], w["w_out_f32"], w["w_pg_f32"], w["w_pp_f32"],
      state_conv, state_shift, *w["prm_raw"], w["headones"])
    ya, conv_new, shift_new, g_r, bonus, vecs, w_in, w_out, w_pg, w_pp, cw_rows, rk_row = outs
    prm = tuple({"conv_w": cw_rows, "r_k": rk_row}.get(name, arr) for name, arr in zip(_PRM_NAMES, w["prm_raw"]))
    derived = {"w_in": w_in, "w_out": w_out, "w_pg": w_pg, "w_pp": w_pp, "prm": prm}
    return (ya, g_r, bonus), conv_new, shift_new, vecs.reshape(len(_STEP_VECS), N_HEADS, HEAD_DIM, n), derived


def _layer(x_prompt, p_prompt, x_sample, p_sample, state_conv, state_shift, state_wkv, w, tc=256):
    bsz, t_len, _ = x_prompt.shape
    (ya, g_r, bonus), conv_s, shift_s, vecs, derived = _sample_front(x_sample, state_conv, state_shift, w)
    w = {**w, **derived}
    s_hvkb = jnp.transpose(state_wkv, (1, 2, 3, 0))
    y_p, conv_p, shift_p, wkv_p, s1, y_s = _prompt_layer(
        x_prompt.reshape(bsz * t_len, D_MODEL), p_prompt.reshape(bsz * t_len, D_PLE), w, bsz, t_len, tc,
        s_hvkb, vecs, (bonus, g_r, ya, x_sample, p_sample))
    prompt_out = (y_p.reshape(bsz, t_len, D_MODEL), conv_p, shift_p, wkv_p)
    sample_out = (y_s, conv_s, shift_s, jnp.transpose(s1, (3, 0, 1, 2)))
    return prompt_out, sample_out


def kernel(x_prompt, x_sample, p_prompt, p_sample, state_conv, state_shift, state_wkv, norm_g, w_in, conv_w, mu_shift, w0, w_up, a0, a_up, k_k, k_a, r_k, ln_w, ln_b, w_out, w_pg, w_pp, final_g):
    depth = norm_g.shape[0]
    assert depth == 1
    i = 0
    head_id = np.arange(HEADSUM_W) // HEAD_DIM
    w = {
        "norm_g": norm_g[i],
        "w_in_f32": w_in[i],
        "w_out_f32": w_out[i],
        "w_pg_f32": w_pg[i],
        "w_pp_f32": w_pp[i],
        "final_g": final_g,
        "prm_raw": _prm_arrays(jnp.transpose(conv_w[i:i + 1], (1, 0, 2)), mu_shift[i], w0[i], w_up[i], a0[i],
                               a_up[i], k_k[i], k_a[i], r_k[i], ln_w[i], ln_b[i]),
        "headones": jnp.asarray(head_id[:, None] == head_id[None, :], BF16),
    }
    (yp, cp, sp, wp), (ys, cs, ss, ws) = _layer(x_prompt, p_prompt[i], x_sample, p_sample[i], state_conv[i],
                                                state_shift[i], state_wkv[i], w)
    return (yp, ys, cp[None], sp[None], wp[None], cs[None], ss[None], ws[None])
```

```python
import functools
import math

import jax
import jax.numpy as jnp
import numpy as np
from jax import lax
from jax.experimental import pallas as pl
from jax.experimental.pallas import tpu as pltpu

F32 = jnp.float32
BF16 = jnp.bfloat16

D_MODEL = 1024
CONV_CH = 512
RW = 512
HEAD_DIM = 64
N_HEADS = 8
N_PAIRS = N_HEADS // 2
LORA = 64
D_PLE = 256
SHIFT_W = 3 * RW + 2 * LORA
IN_W = 4 * CONV_CH + SHIFT_W + RW
OFF_ZS = 4 * CONV_CH
OFF_GR = OFF_ZS + SHIFT_W
RMS_EPS = 1e-6
GN_EPS = 64e-5
DECAY_SCALE = math.exp(-0.5)

LANES = 128
SUBLANES = 8
MXU_DIM = 256
HEADSUM_W = MXU_DIM
SUM_PARTS = 1
CUM_PARTS = 1
LORA_PASSES = 1
CHUNK = 64
W_IN_BLOCKS = 3
CARRY_ROWS = 8
VMEM_LIMIT = 56 * 1024 * 1024


def _dot(a, b):
    return jnp.dot(a, b, preferred_element_type=F32)


def _dot_nt(a, b):
    return lax.dot_general(a, b, (((1,), (1,)), ((), ())), preferred_element_type=F32)


def _dot_tn(a, b):
    return lax.dot_general(a, b, (((0,), (0,)), ((), ())), preferred_element_type=F32)


def _split(x, parts):
    out = []
    rem = x
    for i in range(parts):
        t = rem.astype(BF16)
        out.append(t)
        if i + 1 < parts:
            rem = rem - t.astype(F32)
    return out


def _const_dot(c_bf16, x, parts):
    acc = None
    for t in _split(x, parts):
        d = _dot(c_bf16, t)
        acc = d if acc is None else acc + d
    return acc


def _x_dot_const(x, c_bf16, parts):
    acc = None
    for t in _split(x, parts):
        d = _dot(t, c_bf16)
        acc = d if acc is None else acc + d
    return acc


def _head_sum(x, headones):
    w = headones.shape[0]
    halves = [_x_dot_const(x[:, i * w:(i + 1) * w], headones, SUM_PARTS) for i in range(x.shape[1] // w)]
    return jnp.concatenate(halves, axis=1)


def _mm(a, b, passes):
    if passes == 1:
        return _dot(a.astype(BF16), b.astype(BF16))
    ah, al = _split(a, 2)
    bh, bl = _split(b, 2)
    return _dot(ah, bh) + _dot(al, bh) + _dot(ah, bl)


def _sigmoid(x):
    return 1.0 / (1.0 + jnp.exp(-x))


def _silu(x):
    return x * _sigmoid(x)


def _rwkv_tokens(r, k, v, wa, prm, headones, between_steps=lambda: None):
    n = r.shape[0]
    lane = lax.broadcasted_iota(jnp.int32, (n, LANES), 1)
    th = jnp.where(lane < LORA, jnp.tanh(wa), wa)
    lora = _mm(th, prm["wlora"], LORA_PASSES)
    lw = -DECAY_SCALE * _sigmoid(prm["w0"] + lora[:, :RW])
    a = _sigmoid(prm["a0"] + lora[:, RW:])
    between_steps()
    kk = k * prm["k_k"]
    ss = _head_sum(kk * kk, headones)
    kk = kk / jnp.maximum(jnp.sqrt(ss), 1e-12)
    between_steps()
    kmod = k * (1.0 + (a - 1.0) * prm["k_a"])
    bonus = _head_sum(r * kmod * prm["r_k"], headones) * v
    between_steps()
    return lw, a, kk, kmod, bonus


def _rwkv_post(o, bonus, g_r, prm, headones):
    mean = _head_sum(o, headones) * (1.0 / HEAD_DIM)
    d = o - mean
    var = _head_sum(d * d, headones) * (1.0 / HEAD_DIM)
    on = d * lax.rsqrt(var + GN_EPS)
    on = on * prm["ln_w"] + prm["ln_b"]
    return (on + bonus) * _silu(g_r)


_PRM_NAMES = ("conv_w", "mu", "w0", "a0", "w_up", "a_up", "k_k", "k_a", "r_k", "ln_w", "ln_b")


def _prm_arrays(conv_w, mu_shift, w0, w_up, a0, a_up, k_k, k_a, r_k, ln_w, ln_b):
    row = lambda x: x.reshape(1, -1)
    return (conv_w, row(mu_shift), row(w0), row(a0), w_up, a_up, row(k_k), row(k_a),
            r_k, row(ln_w), row(ln_b))


def _prm_specs():
    zero = (lambda *_: (0, 0))
    shapes = ((3, CONV_CH), (1, SHIFT_W), (1, RW), (1, RW), (LORA, RW), (LORA, RW),
              (1, RW), (1, RW), (1, RW), (1, RW), (1, RW))
    return [pl.BlockSpec(s, zero) for s in shapes]


def _load_prm(refs):
    prm = {n: r[...] for n, r in zip(_PRM_NAMES, refs)}
    zero = jnp.zeros((LORA, RW), F32)
    prm["wlora"] = jnp.concatenate([jnp.concatenate([prm["w_up"], zero], axis=1),
                                    jnp.concatenate([zero, prm["a_up"]], axis=1)], axis=0)
    return prm


def _stack_heads(x):
    lane = lax.broadcasted_iota(jnp.int32, x.shape, 1)
    lo = jnp.where(lane < HEAD_DIM, x, 0.0)
    hi = jnp.where(lane >= HEAD_DIM, x, 0.0)
    return jnp.concatenate([lo, hi], axis=0)


def _stack_heads_bf16(x):
    lane = lax.broadcasted_iota(jnp.int32, x.shape, 1)
    lo = jnp.where(lane < HEAD_DIM, 1.0, 0.0).astype(BF16)
    xb = x.astype(BF16)
    return jnp.concatenate([xb * lo, xb * (1.0 - lo)], axis=0)


def _bdot(a, b):
    return _dot(a.astype(BF16), b.astype(BF16))


def _diag_block_inverses_minus_eye(mats, lane_bcast):
    n = mats[0].shape[0]
    nb = n // SUBLANES
    row = lax.broadcasted_iota(jnp.int32, (n, n), 0)
    col = lax.broadcasted_iota(jnp.int32, (n, n), 1)
    in_diag = ((row // SUBLANES) == (col // SUBLANES)) & (row > col)
    packed = []
    for a in mats:
        d = jnp.where(in_diag, a, 0.0)
        acc = d[0:SUBLANES]
        for m in range(1, nb):
            acc = acc + d[m * SUBLANES:(m + 1) * SUBLANES]
        packed.append(acc)
    d_all = jnp.concatenate(packed, axis=0).astype(BF16)
    sub = lax.broadcasted_iota(jnp.int32, (SUBLANES, n), 0)
    lane = lax.broadcasted_iota(jnp.int32, (SUBLANES, n), 1)
    eye8 = jnp.where(sub == lane % SUBLANES, 1.0, 0.0)
    ts = [eye8 for _ in mats]
    for j in range(SUBLANES - 1):
        spread = _dot(d_all, lane_bcast[j])
        ts = [t + spread[i * SUBLANES:(i + 1) * SUBLANES] * jnp.broadcast_to(t[j:j + 1, :], (SUBLANES, n))
              for i, t in enumerate(ts)]
    blk = lax.broadcasted_iota(jnp.int32, (SUBLANES, n), 1) // SUBLANES
    out = []
    for t in ts:
        tm1 = t - eye8
        out.append(jnp.concatenate([jnp.where(blk == m, tm1, 0.0) for m in range(nb)], axis=0))
    return out


def _unit_lower_inverse_minus_eye(mats, lane_bcast, between_levels=lambda: None):
    n = mats[0].shape[0]
    row = lax.broadcasted_iota(jnp.int32, (n, n), 0)
    col = lax.broadcasted_iota(jnp.int32, (n, n), 1)
    xs = _diag_block_inverses_minus_eye(mats, lane_bcast)
    size = 2 * SUBLANES
    while size <= CHUNK:
        half = size // 2
        sel = ((row // size) == (col // size)) & ((row % size) >= half) & ((col % size) < half)
        als = [jnp.where(sel, a, 0.0) for a in mats]
        ps = [al + _bdot(x, al) for x, al in zip(xs, als)]
        xs = [x + p + _bdot(p, x) for x, p in zip(xs, ps)]
        between_levels()
        size *= 2
    return xs


def _chunk_operators(ld, strict, incl, lane_bcast, side_work=()):
    side_work = list(side_work)

    def breathe():
        if side_work:
            side_work.pop(0)()

    cat0 = lambda xs: jnp.concatenate(xs, axis=0)
    la = [_stack_heads(x) for x in ld("la")]
    lr = [_stack_heads(x) for x in ld("lr")]
    n2 = 2 * CHUNK
    amat = [_dot_nt(cat0([a, b]).astype(BF16), cat0([c, d]).astype(BF16))
            for a, b, c, d in zip(la, lr, ld("rb"), ld("rk"))]

    def block_diag_pair(m):
        swapped = pltpu.roll(m, CHUNK, axis=1)
        first = cat0([m[0:CHUNK], swapped[CHUNK:n2]])
        second = cat0([swapped[0:CHUNK], m[CHUNK:n2]])
        return first, second

    breathe()
    top = [block_diag_pair(m[0:n2]) for m in amat]
    a_ab = [jnp.where(strict, t[0], 0.0) for t in top]
    xs = _unit_lower_inverse_minus_eye(a_ab, lane_bcast, breathe)
    vst = [_stack_heads_bf16(x) for x in ld("v")]
    av = [_dot(jnp.where(strict, t[1], 0.0).astype(BF16), v) for t, v in zip(top, vst)]
    breathe()
    uv0 = [y + _bdot(x, y) for x, y in zip(xs, av)]
    tla = [y + _bdot(x, y) for x, y in zip(xs, la)]
    breathe()
    bottom = [block_diag_pair(m[n2:2 * n2]) for m in amat]
    a_rb = [jnp.where(incl, t[0], 0.0) for t in bottom]
    a_rk = [jnp.where(incl, t[1], 0.0) for t in bottom]
    gop = [y + _bdot(a, t) for y, a, t in zip(lr, a_rb, tla)]
    uv = [cat0([u.astype(BF16), v]) for u, v in zip(uv0, vst)]
    y0 = [_dot(jnp.concatenate([a, b], axis=1).astype(BF16), w) for a, b, w in zip(a_rb, a_rk, uv)]
    breathe()
    bt = [_stack_heads_bf16(x) for x in ld("bt")]
    kt = [_stack_heads_bf16(x) for x in ld("kt")]
    pop = [_dot_tn(t.astype(BF16), b) for t, b in zip(tla, bt)]
    qop = [_dot_tn(w, cat0([b, k])) for w, b, k in zip(uv, bt, kt)]
    while side_work:
        breathe()
    return gop, y0, pop, qop


def _prompt_kernel(xh_ref, xt_ref, pt_ref, ng_ref, win_ref, wo_f32_ref, wg_f32_ref, wp_f32_ref, fg_ref, *refs,
                   tc, n_t, n_tiles):
    prm_refs = refs[:len(_PRM_NAMES)]
    headones_ref, tri_ref, bcast_ref, st_ref, vecs_ref = refs[len(_PRM_NAMES):len(_PRM_NAMES) + 5]
    smp_refs = refs[len(_PRM_NAMES) + 5:len(_PRM_NAMES) + 10]
    (y_ref, conv_out_ref, shift_out_ref, wkv_out_ref, st1_ref, y_smp_ref) = refs[len(_PRM_NAMES) + 10:len(_PRM_NAMES) + 16]
    (z_s, ycat_s, ubuf, zsbuf, s_ref, la_s, lr_s, rb_s, rk_s, bt_s, kt_s, v_s, gam_s, o_s,
     g_sc, y0_sc, p_sc, q_sc, bonus_p, gr_p, ot_s, wo_ref, wg_ref, wp_ref) = refs[len(_PRM_NAMES) + 16:]

    s = pl.program_id(0)
    head_t = lax.rem(s, jnp.int32(n_t))
    tail_t = lax.rem(s + (n_t - 1), jnp.int32(n_t))
    prm = _load_prm(prm_refs)
    headones = headones_ref[...]
    n2 = 2 * CHUNK
    n_chunks = tc // CHUNK

    @pl.when(s == 0)
    def _():
        for ref in (g_sc, y0_sc, p_sc, q_sc, gam_s, bonus_p, gr_p, ycat_s, s_ref):
            ref[...] = jnp.zeros(ref.shape, ref.dtype)
        wo_ref[...] = wo_f32_ref[...].astype(BF16)
        wg_ref[...] = wg_f32_ref[...].astype(BF16)
        wp_ref[...] = wp_f32_ref[...].astype(BF16)

    @pl.when(head_t == 0)
    def _():
        ubuf[0:CARRY_ROWS, :] = jnp.zeros((CARRY_ROWS, CONV_CH), F32)
        zsbuf[0:CARRY_ROWS, :] = jnp.zeros((CARRY_ROWS, SHIFT_W), F32)

    @pl.when(tail_t == 0)
    def _():
        s_ref[...] = jnp.zeros(s_ref.shape, F32)

    tail = {}

    def tail_state_pass(c):
        def run():
            for j in range(N_PAIRS):
                idx = c * N_PAIRS + j
                lanes = slice(j * LANES, (j + 1) * LANES)
                s_old = s_ref[j]
                s_bf = s_old.astype(BF16)
                yst = _dot_nt(g_sc[idx], s_bf) + y0_sc[idx]
                o_s[c * CHUNK:(c + 1) * CHUNK, lanes] = yst[0:CHUNK] + yst[CHUNK:n2]
                gam = gam_s[c * CHUNK:c * CHUNK + 1, lanes]
                s_ref[j] = s_old * gam + _dot(s_bf, p_sc[idx]) + q_sc[idx]
        return run

    def tail_post():
        ycat_s[:, CONV_CH:2 * CONV_CH] = _rwkv_post(o_s[...], bonus_p[...], gr_p[...], prm, headones)

    def tail_out_proj():
        tail["h1"] = xt_ref[...] + _dot(ycat_s[...].astype(BF16), wo_ref[...])

    def tail_gate():
        tail["gate"] = _sigmoid(_dot(tail["h1"].astype(BF16), wg_ref[...]))
        tail["pe"] = _dot(pt_ref[...].astype(BF16), wp_ref[...])

    def tail_norm():
        h2 = tail["h1"] + tail["gate"] * tail["pe"]
        ms = jnp.mean(h2 * h2, axis=-1, keepdims=True)
        y_ref[...] = h2 * lax.rsqrt(ms + RMS_EPS) * fg_ref[...]

    x = xh_ref[...]
    xn = (x * lax.rsqrt(jnp.mean(x * x, axis=-1, keepdims=True) + RMS_EPS) * ng_ref[...]).astype(BF16)

    def project(c0):
        def run():
            z_s[:, c0:c0 + CONV_CH] = _dot(xn, win_ref[:, c0:c0 + CONV_CH])
        return run

    passes = [tail_state_pass(c) for c in range(n_chunks)]
    blocks = [project(c0) for c0 in (OFF_GR, 0, CONV_CH, 2 * CONV_CH, 3 * CONV_CH)]
    side_queue = []
    late_blocks, blocks = blocks[-1:], blocks[:-1]
    while passes or blocks:
        side_queue += passes[:1] + blocks[:1]
        passes, blocks = passes[1:], blocks[1:]
    side_queue += [tail_post, tail_out_proj, tail_gate, tail_norm]

    def project_next(count=2):
        for _ in range(count):
            if side_queue:
                side_queue.pop(0)()

    z_s[:, OFF_ZS:OFF_ZS + SHIFT_W] = _dot(xn, win_ref[:, OFF_ZS:OFF_ZS + SHIFT_W])

    blocks_per_head = HEAD_DIM // STATE_ROWS
    state_blk = jnp.minimum(s, n_tiles - 1)
    v_row0 = lax.rem(state_blk, jnp.int32(blocks_per_head)) * STATE_ROWS
    _sample_state_rows(st_ref, vecs_ref, st1_ref, ot_s, v_row0, state_blk * STATE_ROWS)

    zs = z_s[:, OFF_ZS:OFF_ZS + SHIFT_W]
    zsbuf[CARRY_ROWS:CARRY_ROWS + tc, :] = zs
    zprev = zsbuf[CARRY_ROWS - 1:CARRY_ROWS - 1 + tc, :]
    zm = zs + (zprev - zs) * prm["mu"]
    zsbuf[CARRY_ROWS - 1:CARRY_ROWS, :] = zsbuf[CARRY_ROWS + tc - 1:CARRY_ROWS + tc, :]

    project_next()
    r = zm[:, 0:RW]
    k = zm[:, RW:2 * RW]
    v = zm[:, 2 * RW:3 * RW]
    wa = zm[:, 3 * RW:3 * RW + 2 * LORA]
    lw, a, kk, kmod, bonus = _rwkv_tokens(r, k, v, wa, prm, headones, project_next)

    g = _const_dot(tri_ref[...], lw, CUM_PARTS)
    gc = jnp.concatenate(
        [jnp.broadcast_to(g[(c + 1) * CHUNK - 1:(c + 1) * CHUNK, :], (CHUNK, RW)) for c in range(tc // CHUNK)],
        axis=0)
    project_next()
    eng = jnp.exp(-g)
    etail = jnp.exp(gc - g)
    b = kk * a
    la_s[...] = -kk * jnp.exp(g - lw)
    lr_s[...] = r * jnp.exp(g)
    project_next()
    rb_s[...] = b * eng
    rk_s[...] = kmod * eng
    bt_s[...] = b * etail
    kt_s[...] = kmod * etail
    v_s[...] = v
    while side_queue:
        project_next()
    gam_s[...] = jnp.exp(gc)

    row = lax.broadcasted_iota(jnp.int32, (n2, n2), 0)
    col = lax.broadcasted_iota(jnp.int32, (n2, n2), 1)
    same = (row // CHUNK) == (col // CHUNK)
    strict = same & (row > col)
    incl = same & (row >= col)

    cw = prm["conv_w"]

    def conv_rows(r0):
        def run():
            rows = slice(r0, r0 + CHUNK)
            u = z_s[rows, CONV_CH:2 * CONV_CH] * z_s[rows, 2 * CONV_CH:3 * CONV_CH]
            ubuf[CARRY_ROWS + r0:CARRY_ROWS + r0 + CHUNK, :] = u
            um1 = ubuf[CARRY_ROWS - 1 + r0:CARRY_ROWS - 1 + r0 + CHUNK, :]
            um2 = ubuf[CARRY_ROWS - 2 + r0:CARRY_ROWS - 2 + r0 + CHUNK, :]
            conv = cw[0:1, :] * um2 + cw[1:2, :] * um1 + cw[2:3, :] * u
            ycat_s[rows, 0:CONV_CH] = z_s[rows, 0:CONV_CH] * conv * _silu(z_s[rows, 3 * CONV_CH:4 * CONV_CH])
        return run

    srcs = {"la": la_s, "lr": lr_s, "rb": rb_s, "rk": rk_s, "bt": bt_s, "kt": kt_s, "v": v_s}
    members = [(c, j) for c in range(n_chunks) for j in range(N_PAIRS)]

    def ld(name):
        ref = srcs[name]
        return [ref[c * CHUNK:(c + 1) * CHUNK, j * LANES:(j + 1) * LANES] for c, j in members]

    lane_bcast = [bcast_ref[j] for j in range(SUBLANES - 1)]
    gop, y0, pop, qop = _chunk_operators(ld, strict, incl, lane_bcast,
                                         late_blocks + [conv_rows(c * CHUNK) for c in range(n_chunks)])
    for (c, j), g_, y_, p_, q_ in zip(members, gop, y0, pop, qop):
        idx = c * N_PAIRS + j
        g_sc[idx] = g_.astype(BF16)
        y0_sc[idx] = y_
        p_sc[idx] = p_.astype(BF16)
        q_sc[idx] = q_

    ubuf[CARRY_ROWS - 2:CARRY_ROWS, :] = ubuf[CARRY_ROWS + tc - 2:CARRY_ROWS + tc, :]

    bonus_p[...] = bonus
    gr_p[...] = z_s[:, OFF_GR:OFF_GR + RW]

    @pl.when((head_t == n_t - 1) & (s < n_tiles))
    def _():
        seq = lax.div(s, jnp.int32(n_t))
        shift_out_ref[pl.ds(seq, 1), :] = zsbuf[CARRY_ROWS - 1:CARRY_ROWS, :]
        conv_out_ref[0] = ubuf[CARRY_ROWS - 2:CARRY_ROWS, :]

    @pl.when((tail_t == n_t - 1) & (s > 0))
    def _():
        for h in range(N_HEADS):
            j, i = divmod(h, 2)
            blk = s_ref[j]
            wkv_out_ref[0, h] = blk[i * HEAD_DIM:(i + 1) * HEAD_DIM, i * HEAD_DIM:(i + 1) * HEAD_DIM]

    @pl.when(s == n_tiles)
    def _():
        bonus_ref, gr_ref, ya_ref, xs_ref, ps_ref = smp_refs
        y_r = _rwkv_post(ot_s[...].T, bonus_ref[...], gr_ref[...], _load_prm(prm_refs), headones_ref[...])
        ycat = jnp.concatenate([ya_ref[...], y_r], axis=1).astype(BF16)
        h1 = xs_ref[:, 0, :] + _dot(ycat, wo_ref[...])
        gate = _sigmoid(_dot(h1.astype(BF16), wg_ref[...]))
        pe = _dot(ps_ref[:, 0, :].astype(BF16), wp_ref[...])
        h2 = h1 + gate * pe
        ms = jnp.mean(h2 * h2, axis=-1, keepdims=True)
        y_smp_ref[:, 0, :] = h2 * lax.rsqrt(ms + RMS_EPS) * fg_ref[...]


def _prompt_layer(x2d, p2d, w, bsz, t_len, tc, state_hvkb, step_vecs, sample_parts):
    n_t = t_len // tc
    n_seq = state_hvkb.shape[-1]
    once3 = lambda width: pl.BlockSpec((n_seq, 1, width), lambda s: (0, 0, 0), pipeline_mode=pl.Buffered(1))
    blocks_per_head = HEAD_DIM // STATE_ROWS
    assert bsz * n_t == N_HEADS * blocks_per_head, "one state block per prompt tile"
    state_blk = lambda s: jnp.minimum(s, bsz * n_t - 1)
    tok = np.arange(tc)
    blockones = (tok[:, None] // CHUNK) == (tok[None, :] // CHUNK)
    tri = jnp.asarray(blockones & (tok[:, None] >= tok[None, :]), BF16)
    n_tiles = bsz * n_t
    n2 = 2 * CHUNK
    idx = np.arange(n2)
    same_blk = (idx[:, None] // SUBLANES) == (idx[None, :] // SUBLANES)
    lane_bcast = jnp.asarray(
        np.stack([same_blk & (idx[:, None] % SUBLANES == j) for j in range(SUBLANES - 1)]), BF16)
    kern = functools.partial(_prompt_kernel, tc=tc, n_t=n_t, n_tiles=n_tiles)
    big = lambda: pltpu.VMEM((tc, RW), F32)
    n_blk = (tc // CHUNK) * N_PAIRS
    op = lambda dt: pltpu.VMEM((n_blk, LANES, LANES), dt)
    head = lambda s: jnp.minimum(s, n_tiles - 1)
    tail = lambda s: jnp.maximum(s - 1, 0)
    head_tile = lambda width: pl.BlockSpec((tc, width), lambda s: (head(s), 0))
    tail_tile = lambda width: pl.BlockSpec((tc, width), lambda s: (tail(s), 0))
    const = lambda shape: pl.BlockSpec(shape, lambda s: (0, 0), pipeline_mode=pl.Buffered(1))
    return pl.pallas_call(
        kern,
        grid=(n_tiles + 1,),
        in_specs=[head_tile(D_MODEL), tail_tile(D_MODEL), tail_tile(D_PLE), const((1, D_MODEL)),
                  const((D_MODEL, IN_W)), const((D_MODEL, D_MODEL)), const((D_MODEL, D_MODEL)),
                  const((D_PLE, D_MODEL)), const((1, D_MODEL))]
        + _prm_specs()
        + [const((HEADSUM_W, HEADSUM_W)), const((tc, tc)),
           pl.BlockSpec((SUBLANES - 1, n2, n2), lambda s: (0, 0, 0), pipeline_mode=pl.Buffered(1)),
           pl.BlockSpec((1, STATE_ROWS, HEAD_DIM, n_seq),
                        lambda s: (state_blk(s) // blocks_per_head, state_blk(s) % blocks_per_head, 0, 0)),
           pl.BlockSpec((len(_STEP_VECS), 1, HEAD_DIM, n_seq),
                        lambda s: (0, state_blk(s) // blocks_per_head, 0, 0)),
           const((n_seq, RW)), const((n_seq, RW)), const((n_seq, CONV_CH)), once3(D_MODEL), once3(D_PLE)],
        out_specs=[
            tail_tile(D_MODEL),
            pl.BlockSpec((1, 2, CONV_CH), lambda s: (head(s) // n_t, 0, 0)),
            pl.BlockSpec((bsz, SHIFT_W), lambda s: (0, 0)),
            pl.BlockSpec((1, N_HEADS, HEAD_DIM, HEAD_DIM), lambda s: (tail(s) // n_t, 0, 0, 0)),
            pl.BlockSpec((1, STATE_ROWS, HEAD_DIM, n_seq),
                         lambda s: (state_blk(s) // blocks_per_head, state_blk(s) % blocks_per_head, 0, 0)),
            pl.BlockSpec((n_seq, 1, D_MODEL), lambda s: (0, 0, 0)),
        ],
        out_shape=[
            jax.ShapeDtypeStruct((bsz * t_len, D_MODEL), F32),
            jax.ShapeDtypeStruct((bsz, 2, CONV_CH), F32),
            jax.ShapeDtypeStruct((bsz, SHIFT_W), F32),
            jax.ShapeDtypeStruct((bsz, N_HEADS, HEAD_DIM, HEAD_DIM), F32),
            jax.ShapeDtypeStruct(state_hvkb.shape, F32),
            jax.ShapeDtypeStruct((n_seq, 1, D_MODEL), F32),
        ],
        scratch_shapes=[
            pltpu.VMEM((tc, IN_W), F32),
            pltpu.VMEM((tc, D_MODEL), F32),
            pltpu.VMEM((CARRY_ROWS + tc, CONV_CH), F32),
            pltpu.VMEM((CARRY_ROWS + tc, SHIFT_W), F32),
            pltpu.VMEM((N_PAIRS, LANES, LANES), F32),
            big(), big(), big(), big(), big(), big(), big(), big(), big(),
            op(BF16), op(F32), op(BF16), op(F32),
            big(), big(),
            pltpu.VMEM((RW, n_seq), F32),
            pltpu.VMEM((D_MODEL, D_MODEL), BF16), pltpu.VMEM((D_MODEL, D_MODEL), BF16),
            pltpu.VMEM((D_PLE, D_MODEL), BF16),
        ],
        compiler_params=pltpu.CompilerParams(
            dimension_semantics=("arbitrary",), vmem_limit_bytes=VMEM_LIMIT),
        name="prompt_layer",
    )(x2d, x2d, p2d, w["norm_g"].reshape(1, D_MODEL), w["w_in"], w["w_out_f32"], w["w_pg_f32"], w["w_pp_f32"],
      w["final_g"].reshape(1, D_MODEL), *w["prm"], w["headones"], tri, lane_bcast, state_hvkb, step_vecs,
      *sample_parts)


def _sample_front_kernel(x_ref, ng_ref, win_ref, cb_ref, sb_ref, *refs):
    prm_refs = refs[:len(_PRM_NAMES)]
    headones_ref = refs[len(_PRM_NAMES)]
    (ya_ref, conv_out_ref, shift_out_ref, gr_ref, bonus_ref, vecs_ref,
     wbf_ref, cw_rows_ref, rk_row_ref, xn_s, z_s) = refs[len(_PRM_NAMES) + 1:]
    j = pl.program_id(0)

    @pl.when(j == 0)
    def _():
        x = x_ref[:, 0, :]
        xn_s[...] = (x * lax.rsqrt(jnp.mean(x * x, axis=-1, keepdims=True) + RMS_EPS) * ng_ref[...]).astype(BF16)

    w_blk = win_ref[...].astype(BF16)
    wbf_ref[...] = w_blk
    z_s[j] = _dot(xn_s[...], w_blk)

    @pl.when(j == pl.num_programs(0) - 1)
    def _():
        _sample_front_tokens(z_s, cb_ref, sb_ref, prm_refs, headones_ref, ya_ref, conv_out_ref, shift_out_ref,
                             gr_ref, bonus_ref, vecs_ref, cw_rows_ref, rk_row_ref)


def _sample_front_tokens(z_s, cb_ref, sb_ref, prm_refs, headones_ref, ya_ref, conv_out_ref, shift_out_ref,
                         gr_ref, bonus_ref, vecs_ref, cw_rows_ref, rk_row_ref):
    prm = _load_prm(prm_refs)
    prm["conv_w"] = jnp.concatenate([prm["conv_w"][t] for t in range(3)], axis=0)
    prm["r_k"] = jnp.concatenate([prm["r_k"][h:h + 1, :] for h in range(N_HEADS)], axis=1)
    cw_rows_ref[...] = prm["conv_w"]
    rk_row_ref[...] = prm["r_k"]
    headones = headones_ref[...]
    z = jnp.concatenate([z_s[i] for i in range(z_s.shape[0])], axis=1)

    u = z[:, CONV_CH:2 * CONV_CH] * z[:, 2 * CONV_CH:3 * CONV_CH]
    cb0 = cb_ref[:, 0, :]
    cb1 = cb_ref[:, 1, :]
    cw = prm["conv_w"]
    conv = cw[0:1, :] * cb0 + cw[1:2, :] * cb1 + cw[2:3, :] * u
    ya_ref[...] = z[:, 0:CONV_CH] * conv * _silu(z[:, 3 * CONV_CH:4 * CONV_CH])
    conv_out_ref[:, 0, :] = cb1
    conv_out_ref[:, 1, :] = u

    zs = z[:, OFF_ZS:OFF_ZS + SHIFT_W]
    shift_out_ref[...] = zs
    gr_ref[...] = z[:, OFF_GR:OFF_GR + RW]
    zm = zs + (sb_ref[...] - zs) * prm["mu"]
    r = zm[:, 0:RW]
    k = zm[:, RW:2 * RW]
    v = zm[:, 2 * RW:3 * RW]
    wa = zm[:, 3 * RW:3 * RW + 2 * LORA]
    lw, a, kk, kmod, bonus = _rwkv_tokens(r, k, v, wa, prm, headones)
    bonus_ref[...] = bonus
    for i, vec in enumerate((-kk, jnp.exp(lw), kk * a, kmod, v, r)):
        vecs_ref[i] = vec.T


_STEP_VECS = ("nkk", "decay", "b", "kmod", "v", "r")
STATE_ROWS = 8


def _sample_state_rows(st_ref, vecs_ref, s1_ref, y_ref, v_row0, y_row0):
    nkk, w, b, km = (vecs_ref[i, 0] for i in range(4))
    r = vecs_ref[5, 0]
    for vi in range(STATE_ROWS):
        s0 = st_ref[0, vi]
        sa = jnp.sum(s0 * nkk, axis=0, keepdims=True)
        v_row = vecs_ref[4, 0, pl.ds(v_row0 + vi, 1), :]
        s1 = s0 * w + sa * b + v_row * km
        s1_ref[0, vi] = s1
        y_ref[pl.ds(y_row0 + vi, 1), :] = jnp.sum(s1 * r, axis=0, keepdims=True)


def _full(shape):
    nd = len(shape)
    return pl.BlockSpec(shape, lambda *_: (0,) * nd)


def _sample_front(x3d, state_conv, state_shift, w):
    n = x3d.shape[0]
    row = jax.ShapeDtypeStruct((n, RW), F32)
    wblk = IN_W // W_IN_BLOCKS
    raw_shapes = {"conv_w": (3, 1, CONV_CH), "r_k": (N_HEADS, HEAD_DIM)}
    prm_specs = [_full(raw_shapes[name]) if name in raw_shapes else spec
                 for name, spec in zip(_PRM_NAMES, _prm_specs())]
    outs = pl.pallas_call(
        _sample_front_kernel,
        grid=(W_IN_BLOCKS,),
        in_specs=[_full((n, 1, D_MODEL)), _full((1, D_MODEL)), pl.BlockSpec((D_MODEL, wblk), lambda j: (0, j)),
                  _full((n, 2, CONV_CH)), _full((n, SHIFT_W))]
        + prm_specs + [_full((HEADSUM_W, HEADSUM_W))],
        out_specs=[_full((n, CONV_CH)), _full((n, 2, CONV_CH)), _full((n, SHIFT_W)),
                   _full((n, RW)), _full((n, RW)), _full((len(_STEP_VECS), RW, n)),
                   pl.BlockSpec((D_MODEL, wblk), lambda j: (0, j)),
                   _full((3, CONV_CH)), _full((1, RW))],
        out_shape=[jax.ShapeDtypeStruct((n, CONV_CH), F32), jax.ShapeDtypeStruct((n, 2, CONV_CH), F32),
                   jax.ShapeDtypeStruct((n, SHIFT_W), F32), row, row,
                   jax.ShapeDtypeStruct((len(_STEP_VECS), RW, n), F32),
                   jax.ShapeDtypeStruct((D_MODEL, IN_W), BF16),
                   jax.ShapeDtypeStruct((3, CONV_CH), F32), jax.ShapeDtypeStruct((1, RW), F32)],
        scratch_shapes=[pltpu.VMEM((n, D_MODEL), BF16), pltpu.VMEM((W_IN_BLOCKS, n, wblk), F32)],
        compiler_params=pltpu.CompilerParams(dimension_semantics=("arbitrary",), vmem_limit_bytes=VMEM_LIMIT),
        name="sample_front",
    )(x3d, w["norm_g"].reshape(1, D_MODEL), w["w_in_f32"], state_conv, state_shift, *w["prm_raw"], w["headones"])
    ya, conv_new, shift_new, g_r, bonus, vecs, w_in, cw_rows, rk_row = outs
    prm = tuple({"conv_w": cw_rows, "r_k": rk_row}.get(name, arr) for name, arr in zip(_PRM_NAMES, w["prm_raw"]))
    derived = {"w_in": w_in, "prm": prm}
    return (ya, g_r, bonus), conv_new, shift_new, vecs.reshape(len(_STEP_VECS), N_HEADS, HEAD_DIM, n), derived


def _layer(x_prompt, p_prompt, x_sample, p_sample, state_conv, state_shift, state_wkv, w, tc=256):
    bsz, t_len, _ = x_prompt.shape
    (ya, g_r, bonus), conv_s, shift_s, vecs, derived = _sample_front(x_sample, state_conv, state_shift, w)
    w = {**w, **derived}
    s_hvkb = jnp.transpose(state_wkv, (1, 2, 3, 0))
    y_p, conv_p, shift_p, wkv_p, s1, y_s = _prompt_layer(
        x_prompt.reshape(bsz * t_len, D_MODEL), p_prompt.reshape(bsz * t_len, D_PLE), w, bsz, t_len, tc,
        s_hvkb, vecs, (bonus, g_r, ya, x_sample, p_sample))
    prompt_out = (y_p.reshape(bsz, t_len, D_MODEL), conv_p, shift_p, wkv_p)
    sample_out = (y_s, conv_s, shift_s, jnp.transpose(s1, (3, 0, 1, 2)))
    return prompt_out, sample_out


def kernel(x_prompt, x_sample, p_prompt, p_sample, state_conv, state_shift, state_wkv, norm_g, w_in, conv_w, mu_shift, w0, w_up, a0, a_up, k_k, k_a, r_k, ln_w, ln_b, w_out, w_pg, w_pp, final_g):
    depth = norm_g.shape[0]
    assert depth == 1
    i = 0
    head_id = np.arange(HEADSUM_W) // HEAD_DIM
    w = {
        "norm_g": norm_g[i],
        "w_in_f32": w_in[i],
        "w_out_f32": w_out[i],
        "w_pg_f32": w_pg[i],
        "w_pp_f32": w_pp[i],
        "final_g": final_g,
        "prm_raw": _prm_arrays(jnp.transpose(conv_w[i:i + 1], (1, 0, 2)), mu_shift[i], w0[i], w_up[i], a0[i],
                               a_up[i], k_k[i], k_a[i], r_k[i], ln_w[i], ln_b[i]),
        "headones": jnp.asarray(head_id[:, None] == head_id[None, :], BF16),
    }
    (yp, cp, sp, wp), (ys, cs, ss, ws) = _layer(x_prompt, p_prompt[i], x_sample, p_sample[i], state_conv[i],
                                                state_shift[i], state_wkv[i], w)
    return (yp, ys, cp[None], sp[None], wp[None], cs[None], ss[None], ws[None])
```

```python
import functools
import math

import jax
import jax.numpy as jnp
import numpy as np
from jax import lax
from jax.experimental import pallas as pl
from jax.experimental.pallas import tpu as pltpu

F32 = jnp.float32
BF16 = jnp.bfloat16

D_MODEL = 1024
CONV_CH = 512
RW = 512
HEAD_DIM = 64
N_HEADS = 8
N_PAIRS = N_HEADS // 2
LORA = 64
D_PLE = 256
SHIFT_W = 3 * RW + 2 * LORA
IN_W = 4 * CONV_CH + SHIFT_W + RW
OFF_ZS = 4 * CONV_CH
OFF_GR = OFF_ZS + SHIFT_W
RMS_EPS = 1e-6
GN_EPS = 64e-5
DECAY_SCALE = math.exp(-0.5)

LANES = 128
SUBLANES = 8
MXU_DIM = 256
HEADSUM_W = MXU_DIM
SUM_PARTS = 1
CUM_PARTS = 1
LORA_PASSES = 1
CHUNK = 64
W_IN_BLOCKS = 3
CARRY_ROWS = 8
VMEM_LIMIT = 60 * 1024 * 1024


def _dot(a, b):
    return jnp.dot(a, b, preferred_element_type=F32)


def _dot_nt(a, b):
    return lax.dot_general(a, b, (((1,), (1,)), ((), ())), preferred_element_type=F32)


def _dot_tn(a, b):
    return lax.dot_general(a, b, (((0,), (0,)), ((), ())), preferred_element_type=F32)


def _split(x, parts):
    out = []
    rem = x
    for i in range(parts):
        t = rem.astype(BF16)
        out.append(t)
        if i + 1 < parts:
            rem = rem - t.astype(F32)
    return out


def _const_dot(c_bf16, x, parts):
    acc = None
    for t in _split(x, parts):
        d = _dot(c_bf16, t)
        acc = d if acc is None else acc + d
    return acc


def _x_dot_const(x, c_bf16, parts):
    acc = None
    for t in _split(x, parts):
        d = _dot(t, c_bf16)
        acc = d if acc is None else acc + d
    return acc


def _head_sum(x, headones):
    w = headones.shape[0]
    halves = [_x_dot_const(x[:, i * w:(i + 1) * w], headones, SUM_PARTS) for i in range(x.shape[1] // w)]
    return jnp.concatenate(halves, axis=1)


def _mm(a, b, passes):
    if passes == 1:
        return _dot(a.astype(BF16), b.astype(BF16))
    ah, al = _split(a, 2)
    bh, bl = _split(b, 2)
    return _dot(ah, bh) + _dot(al, bh) + _dot(ah, bl)


def _sigmoid(x):
    return 1.0 / (1.0 + jnp.exp(-x))


def _silu(x):
    return x * _sigmoid(x)


def _rwkv_tokens(r, k, v, wa, prm, headones, between_steps=lambda: None):
    n = r.shape[0]
    lane = lax.broadcasted_iota(jnp.int32, (n, LANES), 1)
    th = jnp.where(lane < LORA, jnp.tanh(wa), wa)
    lora = _mm(th, prm["wlora"], LORA_PASSES)
    lw = -DECAY_SCALE * _sigmoid(prm["w0"] + lora[:, :RW])
    a = _sigmoid(prm["a0"] + lora[:, RW:])
    between_steps()
    kk = k * prm["k_k"]
    ss = _head_sum(kk * kk, headones)
    kk = kk / jnp.maximum(jnp.sqrt(ss), 1e-12)
    between_steps()
    kmod = k * (1.0 + (a - 1.0) * prm["k_a"])
    bonus = _head_sum(r * kmod * prm["r_k"], headones) * v
    between_steps()
    return lw, a, kk, kmod, bonus


def _rwkv_post(o, bonus, g_r, prm, headones):
    mean = _head_sum(o, headones) * (1.0 / HEAD_DIM)
    d = o - mean
    var = _head_sum(d * d, headones) * (1.0 / HEAD_DIM)
    on = d * lax.rsqrt(var + GN_EPS)
    on = on * prm["ln_w"] + prm["ln_b"]
    return (on + bonus) * _silu(g_r)


_PRM_NAMES = ("conv_w", "mu", "w0", "a0", "w_up", "a_up", "k_k", "k_a", "r_k", "ln_w", "ln_b")


def _prm_arrays(conv_w, mu_shift, w0, w_up, a0, a_up, k_k, k_a, r_k, ln_w, ln_b):
    row = lambda x: x.reshape(1, -1)
    return (conv_w, row(mu_shift), row(w0), row(a0), w_up, a_up, row(k_k), row(k_a),
            r_k, row(ln_w), row(ln_b))


def _prm_specs():
    zero = (lambda *_: (0, 0))
    shapes = ((3, CONV_CH), (1, SHIFT_W), (1, RW), (1, RW), (LORA, RW), (LORA, RW),
              (1, RW), (1, RW), (1, RW), (1, RW), (1, RW))
    return [pl.BlockSpec(s, zero) for s in shapes]


def _load_prm(refs):
    prm = {n: r[...] for n, r in zip(_PRM_NAMES, refs)}
    zero = jnp.zeros((LORA, RW), F32)
    prm["wlora"] = jnp.concatenate([jnp.concatenate([prm["w_up"], zero], axis=1),
                                    jnp.concatenate([zero, prm["a_up"]], axis=1)], axis=0)
    return prm


def _stack_heads(x):
    lane = lax.broadcasted_iota(jnp.int32, x.shape, 1)
    lo = jnp.where(lane < HEAD_DIM, x, 0.0)
    hi = jnp.where(lane >= HEAD_DIM, x, 0.0)
    return jnp.concatenate([lo, hi], axis=0)


def _stack_heads_bf16(x):
    lane = lax.broadcasted_iota(jnp.int32, x.shape, 1)
    lo = jnp.where(lane < HEAD_DIM, 1.0, 0.0).astype(BF16)
    xb = x.astype(BF16)
    return jnp.concatenate([xb * lo, xb * (1.0 - lo)], axis=0)


def _bdot(a, b):
    return _dot(a.astype(BF16), b.astype(BF16))


def _diag_block_inverses_minus_eye(mats, lane_bcast):
    n = mats[0].shape[0]
    nb = n // SUBLANES
    row = lax.broadcasted_iota(jnp.int32, (n, n), 0)
    col = lax.broadcasted_iota(jnp.int32, (n, n), 1)
    in_diag = ((row // SUBLANES) == (col // SUBLANES)) & (row > col)
    packed = []
    for a in mats:
        d = jnp.where(in_diag, a, 0.0)
        acc = d[0:SUBLANES]
        for m in range(1, nb):
            acc = acc + d[m * SUBLANES:(m + 1) * SUBLANES]
        packed.append(acc)
    d_all = jnp.concatenate(packed, axis=0).astype(BF16)
    sub = lax.broadcasted_iota(jnp.int32, (SUBLANES, n), 0)
    lane = lax.broadcasted_iota(jnp.int32, (SUBLANES, n), 1)
    eye8 = jnp.where(sub == lane % SUBLANES, 1.0, 0.0)
    ts = [eye8 for _ in mats]
    for j in range(SUBLANES - 1):
        spread = _dot(d_all, lane_bcast[j])
        ts = [t + spread[i * SUBLANES:(i + 1) * SUBLANES] * jnp.broadcast_to(t[j:j + 1, :], (SUBLANES, n))
              for i, t in enumerate(ts)]
    blk = lax.broadcasted_iota(jnp.int32, (SUBLANES, n), 1) // SUBLANES
    out = []
    for t in ts:
        tm1 = t - eye8
        out.append(jnp.concatenate([jnp.where(blk == m, tm1, 0.0) for m in range(nb)], axis=0))
    return out


def _unit_lower_inverse_minus_eye(mats, lane_bcast, between_levels=lambda: None):
    n = mats[0].shape[0]
    row = lax.broadcasted_iota(jnp.int32, (n, n), 0)
    col = lax.broadcasted_iota(jnp.int32, (n, n), 1)
    xs = _diag_block_inverses_minus_eye(mats, lane_bcast)
    size = 2 * SUBLANES
    while size <= CHUNK:
        half = size // 2
        sel = ((row // size) == (col // size)) & ((row % size) >= half) & ((col % size) < half)
        als = [jnp.where(sel, a, 0.0) for a in mats]
        ps = [al + _bdot(x, al) for x, al in zip(xs, als)]
        xs = [x + p + _bdot(p, x) for x, p in zip(xs, ps)]
        between_levels()
        size *= 2
    return xs


def _chunk_operators(ld, strict, incl, lane_bcast, side_work=()):
    side_work = list(side_work)

    def breathe():
        if side_work:
            side_work.pop(0)()

    cat0 = lambda xs: jnp.concatenate(xs, axis=0)
    la = [_stack_heads(x) for x in ld("la")]
    lr = [_stack_heads(x) for x in ld("lr")]
    n2 = 2 * CHUNK
    amat = [_dot_nt(cat0([a, b]).astype(BF16), cat0([c, d]).astype(BF16))
            for a, b, c, d in zip(la, lr, ld("rb"), ld("rk"))]

    def block_diag_pair(m):
        swapped = pltpu.roll(m, CHUNK, axis=1)
        first = cat0([m[0:CHUNK], swapped[CHUNK:n2]])
        second = cat0([swapped[0:CHUNK], m[CHUNK:n2]])
        return first, second

    breathe()
    top = [block_diag_pair(m[0:n2]) for m in amat]
    a_ab = [jnp.where(strict, t[0], 0.0) for t in top]
    xs = _unit_lower_inverse_minus_eye(a_ab, lane_bcast, breathe)
    vst = [_stack_heads_bf16(x) for x in ld("v")]
    av = [_dot(jnp.where(strict, t[1], 0.0).astype(BF16), v) for t, v in zip(top, vst)]
    breathe()
    uv0 = [y + _bdot(x, y) for x, y in zip(xs, av)]
    tla = [y + _bdot(x, y) for x, y in zip(xs, la)]
    breathe()
    bottom = [block_diag_pair(m[n2:2 * n2]) for m in amat]
    a_rb = [jnp.where(incl, t[0], 0.0) for t in bottom]
    a_rk = [jnp.where(incl, t[1], 0.0) for t in bottom]
    gop = [y + _bdot(a, t) for y, a, t in zip(lr, a_rb, tla)]
    uv = [cat0([u.astype(BF16), v]) for u, v in zip(uv0, vst)]
    y0 = [_dot(jnp.concatenate([a, b], axis=1).astype(BF16), w) for a, b, w in zip(a_rb, a_rk, uv)]
    breathe()
    bt = [_stack_heads_bf16(x) for x in ld("bt")]
    kt = [_stack_heads_bf16(x) for x in ld("kt")]
    pop = [_dot_tn(t.astype(BF16), b) for t, b in zip(tla, bt)]
    qop = [_dot_tn(w, cat0([b, k])) for w, b, k in zip(uv, bt, kt)]
    while side_work:
        breathe()
    return gop, y0, pop, qop


def _prompt_kernel(xh_ref, xt_ref, pt_ref, ng_ref, win_ref, wo_f32_ref, wg_f32_ref, wp_f32_ref, fg_ref, *refs,
                   tc, n_t, n_tiles):
    prm_refs = refs[:len(_PRM_NAMES)]
    headones_ref, tri_ref, bcast_ref, st_ref, vecs_ref = refs[len(_PRM_NAMES):len(_PRM_NAMES) + 5]
    smp_refs = refs[len(_PRM_NAMES) + 5:len(_PRM_NAMES) + 10]
    (y_ref, conv_out_ref, shift_out_ref, wkv_out_ref, st1_ref, y_smp_ref) = refs[len(_PRM_NAMES) + 10:len(_PRM_NAMES) + 16]
    (z_s, ycat_s, ubuf, zsbuf, s_ref, la_s, lr_s, rb_s, rk_s, bt_s, kt_s, v_s, gam_s, o_s,
     g_sc, y0_sc, p_sc, q_sc, bonus_p, gr_p, ot_s, wo_ref, wg_ref, wp_ref) = refs[len(_PRM_NAMES) + 16:]

    s = pl.program_id(0)
    head_t = lax.rem(s, jnp.int32(n_t))
    tail_t = lax.rem(s + (n_t - 1), jnp.int32(n_t))
    n2 = 2 * CHUNK
    n_chunks = tc // CHUNK
    tail = {}

    def load_parameters():
        tail["prm"] = _load_prm(prm_refs)
        tail["headones"] = headones_ref[...]

    @pl.when(s == 0)
    def _():
        wo_ref[...] = wo_f32_ref[...].astype(BF16)
        wg_ref[...] = wg_f32_ref[...].astype(BF16)
        wp_ref[...] = wp_f32_ref[...].astype(BF16)

    @pl.when(head_t == 0)
    def _():
        ubuf[0:CARRY_ROWS, :] = jnp.zeros((CARRY_ROWS, CONV_CH), F32)
        zsbuf[0:CARRY_ROWS, :] = jnp.zeros((CARRY_ROWS, SHIFT_W), F32)

    @pl.when(tail_t == 0)
    def _():
        s_ref[...] = jnp.zeros(s_ref.shape, F32)

    def tail_state_pass(c):
        def run():
            for j in range(N_PAIRS):
                idx = c * N_PAIRS + j
                lanes = slice(j * LANES, (j + 1) * LANES)
                s_old = s_ref[j]
                s_bf = s_old.astype(BF16)
                yst = _dot_nt(g_sc[idx], s_bf) + y0_sc[idx]
                o_s[c * CHUNK:(c + 1) * CHUNK, lanes] = yst[0:CHUNK] + yst[CHUNK:n2]
                gam = gam_s[c * CHUNK:c * CHUNK + 1, lanes]
                s_ref[j] = s_old * gam + _dot(s_bf, p_sc[idx]) + q_sc[idx]
        return run

    def tail_post():
        ycat_s[:, CONV_CH:2 * CONV_CH] = _rwkv_post(o_s[...], bonus_p[...], gr_p[...], tail["prm"], tail["headones"])

    def tail_out_proj():
        tail["h1"] = xt_ref[...] + _dot(ycat_s[...].astype(BF16), wo_ref[...])

    def tail_gate():
        tail["gate"] = _sigmoid(_dot(tail["h1"].astype(BF16), wg_ref[...]))
        tail["pe"] = _dot(pt_ref[...].astype(BF16), wp_ref[...])

    def tail_norm():
        h2 = tail["h1"] + tail["gate"] * tail["pe"]
        ms = jnp.mean(h2 * h2, axis=-1, keepdims=True)
        y_ref[...] = h2 * lax.rsqrt(ms + RMS_EPS) * fg_ref[...]

    def head(passes, rest):
        prm, headones = tail["prm"], tail["headones"]
        x = xh_ref[...]
        xn = (x * lax.rsqrt(jnp.mean(x * x, axis=-1, keepdims=True) + RMS_EPS) * ng_ref[...]).astype(BF16)

        def project(c0):
            def run():
                z_s[:, c0:c0 + CONV_CH] = _dot(xn, win_ref[:, c0:c0 + CONV_CH])
            return run

        blocks = [project(c0) for c0 in (OFF_GR, 0, CONV_CH, 2 * CONV_CH, 3 * CONV_CH)]
        side_queue = []
        late_blocks, blocks = blocks[-1:], blocks[:-1]
        while passes or blocks:
            side_queue += passes[:1] + blocks[:1]
            passes, blocks = passes[1:], blocks[1:]
        side_queue += rest

        def project_next(count=2):
            for _ in range(count):
                if side_queue:
                    side_queue.pop(0)()

        z_s[:, OFF_ZS:OFF_ZS + SHIFT_W] = _dot(xn, win_ref[:, OFF_ZS:OFF_ZS + SHIFT_W])

        blocks_per_head = HEAD_DIM // STATE_ROWS
        v_row0 = lax.rem(s, jnp.int32(blocks_per_head)) * STATE_ROWS
        _sample_state_rows(st_ref, vecs_ref, st1_ref, ot_s, v_row0, s * STATE_ROWS)

        zs = z_s[:, OFF_ZS:OFF_ZS + SHIFT_W]
        zsbuf[CARRY_ROWS:CARRY_ROWS + tc, :] = zs
        zprev = zsbuf[CARRY_ROWS - 1:CARRY_ROWS - 1 + tc, :]
        zm = zs + (zprev - zs) * prm["mu"]
        zsbuf[CARRY_ROWS - 1:CARRY_ROWS, :] = zsbuf[CARRY_ROWS + tc - 1:CARRY_ROWS + tc, :]

        project_next()
        r = zm[:, 0:RW]
        k = zm[:, RW:2 * RW]
        v = zm[:, 2 * RW:3 * RW]
        wa = zm[:, 3 * RW:3 * RW + 2 * LORA]
        lw, a, kk, kmod, bonus = _rwkv_tokens(r, k, v, wa, prm, headones, project_next)

        g = _const_dot(tri_ref[...], lw, CUM_PARTS)
        gc = jnp.concatenate(
            [jnp.broadcast_to(g[(c + 1) * CHUNK - 1:(c + 1) * CHUNK, :], (CHUNK, RW)) for c in range(tc // CHUNK)],
            axis=0)
        project_next()
        eng = jnp.exp(-g)
        etail = jnp.exp(gc - g)
        b = kk * a
        la_s[...] = -kk * jnp.exp(g - lw)
        lr_s[...] = r * jnp.exp(g)
        project_next()
        rb_s[...] = b * eng
        rk_s[...] = kmod * eng
        bt_s[...] = b * etail
        kt_s[...] = kmod * etail
        v_s[...] = v
        while side_queue:
            project_next()
        gam_s[...] = jnp.exp(gc)

        row = lax.broadcasted_iota(jnp.int32, (n2, n2), 0)
        col = lax.broadcasted_iota(jnp.int32, (n2, n2), 1)
        same = (row // CHUNK) == (col // CHUNK)
        strict = same & (row > col)
        incl = same & (row >= col)

        cw = prm["conv_w"]

        def conv_rows(r0):
            def run():
                rows = slice(r0, r0 + CHUNK)
                u = z_s[rows, CONV_CH:2 * CONV_CH] * z_s[rows, 2 * CONV_CH:3 * CONV_CH]
                ubuf[CARRY_ROWS + r0:CARRY_ROWS + r0 + CHUNK, :] = u
                um1 = ubuf[CARRY_ROWS - 1 + r0:CARRY_ROWS - 1 + r0 + CHUNK, :]
                um2 = ubuf[CARRY_ROWS - 2 + r0:CARRY_ROWS - 2 + r0 + CHUNK, :]
                conv = cw[0:1, :] * um2 + cw[1:2, :] * um1 + cw[2:3, :] * u
                ycat_s[rows, 0:CONV_CH] = z_s[rows, 0:CONV_CH] * conv * _silu(z_s[rows, 3 * CONV_CH:4 * CONV_CH])
            return run

        srcs = {"la": la_s, "lr": lr_s, "rb": rb_s, "rk": rk_s, "bt": bt_s, "kt": kt_s, "v": v_s}
        members = [(c, j) for c in range(n_chunks) for j in range(N_PAIRS)]

        def ld(name):
            ref = srcs[name]
            return [ref[c * CHUNK:(c + 1) * CHUNK, j * LANES:(j + 1) * LANES] for c, j in members]

        lane_bcast = [bcast_ref[j] for j in range(SUBLANES - 1)]
        gop, y0, pop, qop = _chunk_operators(ld, strict, incl, lane_bcast,
                                             late_blocks + [conv_rows(c * CHUNK) for c in range(n_chunks)])
        for (c, j), g_, y_, p_, q_ in zip(members, gop, y0, pop, qop):
            idx = c * N_PAIRS + j
            g_sc[idx] = g_.astype(BF16)
            y0_sc[idx] = y_
            p_sc[idx] = p_.astype(BF16)
            q_sc[idx] = q_

        ubuf[CARRY_ROWS - 2:CARRY_ROWS, :] = ubuf[CARRY_ROWS + tc - 2:CARRY_ROWS + tc, :]

        bonus_p[...] = bonus
        gr_p[...] = z_s[:, OFF_GR:OFF_GR + RW]

    tail_passes = [tail_state_pass(c) for c in range(n_chunks)]
    tail_rest = [tail_post, tail_out_proj, tail_gate, tail_norm]

    @pl.when(s == 0)
    def _():
        load_parameters()
        head([], [])

    @pl.when((s > 0) & (s < n_tiles))
    def _():
        load_parameters()
        head(tail_passes, tail_rest)

    @pl.when(s == n_tiles)
    def _():
        load_parameters()
        for piece in tail_passes + tail_rest:
            piece()

    @pl.when((head_t == n_t - 1) & (s < n_tiles))
    def _():
        seq = lax.div(s, jnp.int32(n_t))
        shift_out_ref[pl.ds(seq, 1), :] = zsbuf[CARRY_ROWS - 1:CARRY_ROWS, :]
        conv_out_ref[0] = ubuf[CARRY_ROWS - 2:CARRY_ROWS, :]

    @pl.when((tail_t == n_t - 1) & (s > 0))
    def _():
        for h in range(N_HEADS):
            j, i = divmod(h, 2)
            blk = s_ref[j]
            wkv_out_ref[0, h] = blk[i * HEAD_DIM:(i + 1) * HEAD_DIM, i * HEAD_DIM:(i + 1) * HEAD_DIM]

    @pl.when(s == n_tiles)
    def _():
        bonus_ref, gr_ref, ya_ref, xs_ref, ps_ref = smp_refs
        y_r = _rwkv_post(ot_s[...].T, bonus_ref[...], gr_ref[...], _load_prm(prm_refs), headones_ref[...])
        ycat = jnp.concatenate([ya_ref[...], y_r], axis=1).astype(BF16)
        h1 = xs_ref[:, 0, :] + _dot(ycat, wo_ref[...])
        gate = _sigmoid(_dot(h1.astype(BF16), wg_ref[...]))
        pe = _dot(ps_ref[:, 0, :].astype(BF16), wp_ref[...])
        h2 = h1 + gate * pe
        ms = jnp.mean(h2 * h2, axis=-1, keepdims=True)
        y_smp_ref[:, 0, :] = h2 * lax.rsqrt(ms + RMS_EPS) * fg_ref[...]


def _prompt_layer(x2d, p2d, w, bsz, t_len, tc, state_hvkb, step_vecs, sample_parts):
    n_t = t_len // tc
    n_seq = state_hvkb.shape[-1]
    once3 = lambda width: pl.BlockSpec((n_seq, 1, width), lambda s: (0, 0, 0), pipeline_mode=pl.Buffered(1))
    blocks_per_head = HEAD_DIM // STATE_ROWS
    assert bsz * n_t == N_HEADS * blocks_per_head, "one state block per prompt tile"
    state_blk = lambda s: jnp.minimum(s, bsz * n_t - 1)
    tok = np.arange(tc)
    blockones = (tok[:, None] // CHUNK) == (tok[None, :] // CHUNK)
    tri = jnp.asarray(blockones & (tok[:, None] >= tok[None, :]), BF16)
    n_tiles = bsz * n_t
    n2 = 2 * CHUNK
    idx = np.arange(n2)
    same_blk = (idx[:, None] // SUBLANES) == (idx[None, :] // SUBLANES)
    lane_bcast = jnp.asarray(
        np.stack([same_blk & (idx[:, None] % SUBLANES == j) for j in range(SUBLANES - 1)]), BF16)
    kern = functools.partial(_prompt_kernel, tc=tc, n_t=n_t, n_tiles=n_tiles)
    big = lambda: pltpu.VMEM((tc, RW), F32)
    n_blk = (tc // CHUNK) * N_PAIRS
    op = lambda dt: pltpu.VMEM((n_blk, LANES, LANES), dt)
    head = lambda s: jnp.minimum(s, n_tiles - 1)
    tail = lambda s: jnp.maximum(s - 1, 0)
    head_tile = lambda width: pl.BlockSpec((tc, width), lambda s: (head(s), 0))
    tail_tile = lambda width: pl.BlockSpec((tc, width), lambda s: (tail(s), 0))
    const = lambda shape: pl.BlockSpec(shape, lambda s: (0, 0), pipeline_mode=pl.Buffered(1))
    return pl.pallas_call(
        kern,
        grid=(n_tiles + 1,),
        in_specs=[head_tile(D_MODEL), tail_tile(D_MODEL), tail_tile(D_PLE), const((1, D_MODEL)),
                  const((D_MODEL, IN_W)), const((D_MODEL, D_MODEL)), const((D_MODEL, D_MODEL)),
                  const((D_PLE, D_MODEL)), const((1, D_MODEL))]
        + _prm_specs()
        + [const((HEADSUM_W, HEADSUM_W)), const((tc, tc)),
           pl.BlockSpec((SUBLANES - 1, n2, n2), lambda s: (0, 0, 0), pipeline_mode=pl.Buffered(1)),
           pl.BlockSpec((1, STATE_ROWS, HEAD_DIM, n_seq),
                        lambda s: (state_blk(s) // blocks_per_head, state_blk(s) % blocks_per_head, 0, 0)),
           pl.BlockSpec((len(_STEP_VECS), 1, HEAD_DIM, n_seq),
                        lambda s: (0, state_blk(s) // blocks_per_head, 0, 0)),
           const((n_seq, RW)), const((n_seq, RW)), const((n_seq, CONV_CH)), once3(D_MODEL), once3(D_PLE)],
        out_specs=[
            tail_tile(D_MODEL),
            pl.BlockSpec((1, 2, CONV_CH), lambda s: (head(s) // n_t, 0, 0)),
            pl.BlockSpec((bsz, SHIFT_W), lambda s: (0, 0)),
            pl.BlockSpec((1, N_HEADS, HEAD_DIM, HEAD_DIM), lambda s: (tail(s) // n_t, 0, 0, 0)),
            pl.BlockSpec((1, STATE_ROWS, HEAD_DIM, n_seq),
                         lambda s: (state_blk(s) // blocks_per_head, state_blk(s) % blocks_per_head, 0, 0)),
            pl.BlockSpec((n_seq, 1, D_MODEL), lambda s: (0, 0, 0)),
        ],
        out_shape=[
            jax.ShapeDtypeStruct((bsz * t_len, D_MODEL), F32),
            jax.ShapeDtypeStruct((bsz, 2, CONV_CH), F32),
            jax.ShapeDtypeStruct((bsz, SHIFT_W), F32),
            jax.ShapeDtypeStruct((bsz, N_HEADS, HEAD_DIM, HEAD_DIM), F32),
            jax.ShapeDtypeStruct(state_hvkb.shape, F32),
            jax.ShapeDtypeStruct((n_seq, 1, D_MODEL), F32),
        ],
        scratch_shapes=[
            pltpu.VMEM((tc, IN_W), F32),
            pltpu.VMEM((tc, D_MODEL), F32),
            pltpu.VMEM((CARRY_ROWS + tc, CONV_CH), F32),
            pltpu.VMEM((CARRY_ROWS + tc, SHIFT_W), F32),
            pltpu.VMEM((N_PAIRS, LANES, LANES), F32),
            big(), big(), big(), big(), big(), big(), big(), big(), big(),
            op(BF16), op(F32), op(BF16), op(F32),
            big(), big(),
            pltpu.VMEM((RW, n_seq), F32),
            pltpu.VMEM((D_MODEL, D_MODEL), BF16), pltpu.VMEM((D_MODEL, D_MODEL), BF16),
            pltpu.VMEM((D_PLE, D_MODEL), BF16),
        ],
        compiler_params=pltpu.CompilerParams(
            dimension_semantics=("arbitrary",), vmem_limit_bytes=VMEM_LIMIT),
        name="prompt_layer",
    )(x2d, x2d, p2d, w["norm_g"].reshape(1, D_MODEL), w["w_in"], w["w_out_f32"], w["w_pg_f32"], w["w_pp_f32"],
      w["final_g"].reshape(1, D_MODEL), *w["prm"], w["headones"], tri, lane_bcast, state_hvkb, step_vecs,
      *sample_parts)


def _sample_front_kernel(x_ref, ng_ref, win_ref, cb_ref, sb_ref, *refs):
    prm_refs = refs[:len(_PRM_NAMES)]
    headones_ref = refs[len(_PRM_NAMES)]
    (ya_ref, conv_out_ref, shift_out_ref, gr_ref, bonus_ref, vecs_ref,
     wbf_ref, cw_rows_ref, rk_row_ref, xn_s, z_s) = refs[len(_PRM_NAMES) + 1:]
    j = pl.program_id(0)

    @pl.when(j == 0)
    def _():
        x = x_ref[:, 0, :]
        xn_s[...] = (x * lax.rsqrt(jnp.mean(x * x, axis=-1, keepdims=True) + RMS_EPS) * ng_ref[...]).astype(BF16)

    w_blk = win_ref[...].astype(BF16)
    wbf_ref[...] = w_blk
    z_s[j] = _dot(xn_s[...], w_blk)

    @pl.when(j == pl.num_programs(0) - 1)
    def _():
        _sample_front_tokens(z_s, cb_ref, sb_ref, prm_refs, headones_ref, ya_ref, conv_out_ref, shift_out_ref,
                             gr_ref, bonus_ref, vecs_ref, cw_rows_ref, rk_row_ref)


def _sample_front_tokens(z_s, cb_ref, sb_ref, prm_refs, headones_ref, ya_ref, conv_out_ref, shift_out_ref,
                         gr_ref, bonus_ref, vecs_ref, cw_rows_ref, rk_row_ref):
    prm = _load_prm(prm_refs)
    prm["conv_w"] = jnp.concatenate([prm["conv_w"][t] for t in range(3)], axis=0)
    prm["r_k"] = jnp.concatenate([prm["r_k"][h:h + 1, :] for h in range(N_HEADS)], axis=1)
    cw_rows_ref[...] = prm["conv_w"]
    rk_row_ref[...] = prm["r_k"]
    headones = headones_ref[...]
    z = jnp.concatenate([z_s[i] for i in range(z_s.shape[0])], axis=1)

    u = z[:, CONV_CH:2 * CONV_CH] * z[:, 2 * CONV_CH:3 * CONV_CH]
    cb0 = cb_ref[:, 0, :]
    cb1 = cb_ref[:, 1, :]
    cw = prm["conv_w"]
    conv = cw[0:1, :] * cb0 + cw[1:2, :] * cb1 + cw[2:3, :] * u
    ya_ref[...] = z[:, 0:CONV_CH] * conv * _silu(z[:, 3 * CONV_CH:4 * CONV_CH])
    conv_out_ref[:, 0, :] = cb1
    conv_out_ref[:, 1, :] = u

    zs = z[:, OFF_ZS:OFF_ZS + SHIFT_W]
    shift_out_ref[...] = zs
    gr_ref[...] = z[:, OFF_GR:OFF_GR + RW]
    zm = zs + (sb_ref[...] - zs) * prm["mu"]
    r = zm[:, 0:RW]
    k = zm[:, RW:2 * RW]
    v = zm[:, 2 * RW:3 * RW]
    wa = zm[:, 3 * RW:3 * RW + 2 * LORA]
    lw, a, kk, kmod, bonus = _rwkv_tokens(r, k, v, wa, prm, headones)
    bonus_ref[...] = bonus
    for i, vec in enumerate((-kk, jnp.exp(lw), kk * a, kmod, v, r)):
        vecs_ref[i] = vec.T


_STEP_VECS = ("nkk", "decay", "b", "kmod", "v", "r")
STATE_ROWS = 8


def _sample_state_rows(st_ref, vecs_ref, s1_ref, y_ref, v_row0, y_row0):
    nkk, w, b, km = (vecs_ref[i, 0] for i in range(4))
    r = vecs_ref[5, 0]
    for vi in range(STATE_ROWS):
        s0 = st_ref[0, vi]
        sa = jnp.sum(s0 * nkk, axis=0, keepdims=True)
        v_row = vecs_ref[4, 0, pl.ds(v_row0 + vi, 1), :]
        s1 = s0 * w + sa * b + v_row * km
        s1_ref[0, vi] = s1
        y_ref[pl.ds(y_row0 + vi, 1), :] = jnp.sum(s1 * r, axis=0, keepdims=True)


def _full(shape):
    nd = len(shape)
    return pl.BlockSpec(shape, lambda *_: (0,) * nd)


def _sample_front(x3d, state_conv, state_shift, w):
    n = x3d.shape[0]
    row = jax.ShapeDtypeStruct((n, RW), F32)
    wblk = IN_W // W_IN_BLOCKS
    raw_shapes = {"conv_w": (3, 1, CONV_CH), "r_k": (N_HEADS, HEAD_DIM)}
    prm_specs = [_full(raw_shapes[name]) if name in raw_shapes else spec
                 for name, spec in zip(_PRM_NAMES, _prm_specs())]
    outs = pl.pallas_call(
        _sample_front_kernel,
        grid=(W_IN_BLOCKS,),
        in_specs=[_full((n, 1, D_MODEL)), _full((1, D_MODEL)), pl.BlockSpec((D_MODEL, wblk), lambda j: (0, j)),
                  _full((n, 2, CONV_CH)), _full((n, SHIFT_W))]
        + prm_specs + [_full((HEADSUM_W, HEADSUM_W))],
        out_specs=[_full((n, CONV_CH)), _full((n, 2, CONV_CH)), _full((n, SHIFT_W)),
                   _full((n, RW)), _full((n, RW)), _full((len(_STEP_VECS), RW, n)),
                   pl.BlockSpec((D_MODEL, wblk), lambda j: (0, j)),
                   _full((3, CONV_CH)), _full((1, RW))],
        out_shape=[jax.ShapeDtypeStruct((n, CONV_CH), F32), jax.ShapeDtypeStruct((n, 2, CONV_CH), F32),
                   jax.ShapeDtypeStruct((n, SHIFT_W), F32), row, row,
                   jax.ShapeDtypeStruct((len(_STEP_VECS), RW, n), F32),
                   jax.ShapeDtypeStruct((D_MODEL, IN_W), BF16),
                   jax.ShapeDtypeStruct((3, CONV_CH), F32), jax.ShapeDtypeStruct((1, RW), F32)],
        scratch_shapes=[pltpu.VMEM((n, D_MODEL), BF16), pltpu.VMEM((W_IN_BLOCKS, n, wblk), F32)],
        compiler_params=pltpu.CompilerParams(dimension_semantics=("arbitrary",), vmem_limit_bytes=VMEM_LIMIT),
        name="sample_front",
    )(x3d, w["norm_g"].reshape(1, D_MODEL), w["w_in_f32"], state_conv, state_shift, *w["prm_raw"], w["headones"])
    ya, conv_new, shift_new, g_r, bonus, vecs, w_in, cw_rows, rk_row = outs
    prm = tuple({"conv_w": cw_rows, "r_k": rk_row}.get(name, arr) for name, arr in zip(_PRM_NAMES, w["prm_raw"]))
    derived = {"w_in": w_in, "prm": prm}
    return (ya, g_r, bonus), conv_new, shift_new, vecs.reshape(len(_STEP_VECS), N_HEADS, HEAD_DIM, n), derived


def _layer(x_prompt, p_prompt, x_sample, p_sample, state_conv, state_shift, state_wkv, w, tc=256):
    bsz, t_len, _ = x_prompt.shape
    (ya, g_r, bonus), conv_s, shift_s, vecs, derived = _sample_front(x_sample, state_conv, state_shift, w)
    w = {**w, **derived}
    s_hvkb = jnp.transpose(state_wkv, (1, 2, 3, 0))
    y_p, conv_p, shift_p, wkv_p, s1, y_s = _prompt_layer(
        x_prompt.reshape(bsz * t_len, D_MODEL), p_prompt.reshape(bsz * t_len, D_PLE), w, bsz, t_len, tc,
        s_hvkb, vecs, (bonus, g_r, ya, x_sample, p_sample))
    prompt_out = (y_p.reshape(bsz, t_len, D_MODEL), conv_p, shift_p, wkv_p)
    sample_out = (y_s, conv_s, shift_s, jnp.transpose(s1, (3, 0, 1, 2)))
    return prompt_out, sample_out


def kernel(x_prompt, x_sample, p_prompt, p_sample, state_conv, state_shift, state_wkv, norm_g, w_in, conv_w, mu_shift, w0, w_up, a0, a_up, k_k, k_a, r_k, ln_w, ln_b, w_out, w_pg, w_pp, final_g):
    depth = norm_g.shape[0]
    assert depth == 1
    i = 0
    head_id = np.arange(HEADSUM_W) // HEAD_DIM
    w = {
        "norm_g": norm_g[i],
        "w_in_f32": w_in[i],
        "w_out_f32": w_out[i],
        "w_pg_f32": w_pg[i],
        "w_pp_f32": w_pp[i],
        "final_g": final_g,
        "prm_raw": _prm_arrays(jnp.transpose(conv_w[i:i + 1], (1, 0, 2)), mu_shift[i], w0[i], w_up[i], a0[i],
                               a_up[i], k_k[i], k_a[i], r_k[i], ln_w[i], ln_b[i]),
        "headones": jnp.asarray(head_id[:, None] == head_id[None, :], BF16),
    }
    (yp, cp, sp, wp), (ys, cs, ss, ws) = _layer(x_prompt, p_prompt[i], x_sample, p_sample[i], state_conv[i],
                                                state_shift[i], state_wkv[i], w)
    return (yp, ys, cp[None], sp[None], wp[None], cs[None], ss[None], ws[None])
```

```python
import functools
import math

import jax
import jax.numpy as jnp
import numpy as np
from jax import lax
from jax.experimental import pallas as pl
from jax.experimental.pallas import tpu as pltpu

F32 = jnp.float32
BF16 = jnp.bfloat16

D_MODEL = 1024
CONV_CH = 512
RW = 512
HEAD_DIM = 64
N_HEADS = 8
N_PAIRS = N_HEADS // 2
LORA = 64
D_PLE = 256
SHIFT_W = 3 * RW + 2 * LORA
IN_W = 4 * CONV_CH + SHIFT_W + RW
OFF_ZS = 4 * CONV_CH
OFF_GR = OFF_ZS + SHIFT_W
RMS_EPS = 1e-6
GN_EPS = 64e-5
DECAY_SCALE = math.exp(-0.5)

LANES = 128
SUBLANES = 8
MXU_DIM = 256
HEADSUM_W = MXU_DIM
SUM_PARTS = 1
CUM_PARTS = 1
LORA_PASSES = 1
CHUNK = 64
W_IN_BLOCKS = 3
CARRY_ROWS = 8
VMEM_LIMIT = 56 * 1024 * 1024


def _dot(a, b):
    return jnp.dot(a, b, preferred_element_type=F32)


def _dot_nt(a, b):
    return lax.dot_general(a, b, (((1,), (1,)), ((), ())), preferred_element_type=F32)


def _dot_tn(a, b):
    return lax.dot_general(a, b, (((0,), (0,)), ((), ())), preferred_element_type=F32)


def _split(x, parts):
    out = []
    rem = x
    for i in range(parts):
        t = rem.astype(BF16)
        out.append(t)
        if i + 1 < parts:
            rem = rem - t.astype(F32)
    return out


def _const_dot(c_bf16, x, parts):
    acc = None
    for t in _split(x, parts):
        d = _dot(c_bf16, t)
        acc = d if acc is None else acc + d
    return acc


def _x_dot_const(x, c_bf16, parts):
    acc = None
    for t in _split(x, parts):
        d = _dot(t, c_bf16)
        acc = d if acc is None else acc + d
    return acc


def _head_sum(x, headones):
    w = headones.shape[0]
    halves = [_x_dot_const(x[:, i * w:(i + 1) * w], headones, SUM_PARTS) for i in range(x.shape[1] // w)]
    return jnp.concatenate(halves, axis=1)


def _mm(a, b, passes):
    if passes == 1:
        return _dot(a.astype(BF16), b.astype(BF16))
    ah, al = _split(a, 2)
    bh, bl = _split(b, 2)
    return _dot(ah, bh) + _dot(al, bh) + _dot(ah, bl)


def _sigmoid(x):
    return 1.0 / (1.0 + jnp.exp(-x))


def _silu(x):
    return x * _sigmoid(x)


def _rwkv_tokens(r, k, v, wa, prm, headones, between_steps=lambda: None):
    n = r.shape[0]
    lane = lax.broadcasted_iota(jnp.int32, (n, LANES), 1)
    th = jnp.where(lane < LORA, jnp.tanh(wa), wa)
    lora = _mm(th, prm["wlora"], LORA_PASSES)
    lw = -DECAY_SCALE * _sigmoid(prm["w0"] + lora[:, :RW])
    a = _sigmoid(prm["a0"] + lora[:, RW:])
    between_steps()
    kk = k * prm["k_k"]
    ss = _head_sum(kk * kk, headones)
    kk = kk / jnp.maximum(jnp.sqrt(ss), 1e-12)
    between_steps()
    kmod = k * (1.0 + (a - 1.0) * prm["k_a"])
    bonus = _head_sum(r * kmod * prm["r_k"], headones) * v
    between_steps()
    return lw, a, kk, kmod, bonus


def _rwkv_post(o, bonus, g_r, prm, headones):
    mean = _head_sum(o, headones) * (1.0 / HEAD_DIM)
    d = o - mean
    var = _head_sum(d * d, headones) * (1.0 / HEAD_DIM)
    on = d * lax.rsqrt(var + GN_EPS)
    on = on * prm["ln_w"] + prm["ln_b"]
    return (on + bonus) * _silu(g_r)


_PRM_NAMES = ("conv_w", "mu", "w0", "a0", "w_up", "a_up", "k_k", "k_a", "r_k", "ln_w", "ln_b")


def _prm_arrays(conv_w, mu_shift, w0, w_up, a0, a_up, k_k, k_a, r_k, ln_w, ln_b):
    row = lambda x: x.reshape(1, -1)
    return (conv_w, row(mu_shift), row(w0), row(a0), w_up, a_up, row(k_k), row(k_a),
            r_k, row(ln_w), row(ln_b))


def _prm_specs():
    zero = (lambda *_: (0, 0))
    shapes = ((3, CONV_CH), (1, SHIFT_W), (1, RW), (1, RW), (LORA, RW), (LORA, RW),
              (1, RW), (1, RW), (1, RW), (1, RW), (1, RW))
    return [pl.BlockSpec(s, zero) for s in shapes]


def _load_prm(refs):
    prm = {n: r[...] for n, r in zip(_PRM_NAMES, refs)}
    zero = jnp.zeros((LORA, RW), F32)
    prm["wlora"] = jnp.concatenate([jnp.concatenate([prm["w_up"], zero], axis=1),
                                    jnp.concatenate([zero, prm["a_up"]], axis=1)], axis=0)
    return prm


def _stack_heads(x):
    lane = lax.broadcasted_iota(jnp.int32, x.shape, 1)
    lo = jnp.where(lane < HEAD_DIM, x, 0.0)
    hi = jnp.where(lane >= HEAD_DIM, x, 0.0)
    return jnp.concatenate([lo, hi], axis=0)


def _stack_heads_bf16(x):
    lane = lax.broadcasted_iota(jnp.int32, x.shape, 1)
    lo = jnp.where(lane < HEAD_DIM, 1.0, 0.0).astype(BF16)
    xb = x.astype(BF16)
    return jnp.concatenate([xb * lo, xb * (1.0 - lo)], axis=0)


def _bdot(a, b):
    return _dot(a.astype(BF16), b.astype(BF16))


def _diag_block_inverses_minus_eye(mats, lane_bcast):
    n = mats[0].shape[0]
    nb = n // SUBLANES
    row = lax.broadcasted_iota(jnp.int32, (n, n), 0)
    col = lax.broadcasted_iota(jnp.int32, (n, n), 1)
    in_diag = ((row // SUBLANES) == (col // SUBLANES)) & (row > col)
    packed = []
    for a in mats:
        d = jnp.where(in_diag, a, 0.0)
        acc = d[0:SUBLANES]
        for m in range(1, nb):
            acc = acc + d[m * SUBLANES:(m + 1) * SUBLANES]
        packed.append(acc)
    d_all = jnp.concatenate(packed, axis=0).astype(BF16)
    sub = lax.broadcasted_iota(jnp.int32, (SUBLANES, n), 0)
    lane = lax.broadcasted_iota(jnp.int32, (SUBLANES, n), 1)
    eye8 = jnp.where(sub == lane % SUBLANES, 1.0, 0.0)
    ts = [eye8 for _ in mats]
    for j in range(SUBLANES - 1):
        spread = _dot(d_all, lane_bcast[j])
        ts = [t + spread[i * SUBLANES:(i + 1) * SUBLANES] * jnp.broadcast_to(t[j:j + 1, :], (SUBLANES, n))
              for i, t in enumerate(ts)]
    blk = lax.broadcasted_iota(jnp.int32, (SUBLANES, n), 1) // SUBLANES
    out = []
    for t in ts:
        tm1 = t - eye8
        out.append(jnp.concatenate([jnp.where(blk == m, tm1, 0.0) for m in range(nb)], axis=0))
    return out


def _unit_lower_inverse_minus_eye(mats, lane_bcast, between_levels=lambda: None):
    n = mats[0].shape[0]
    row = lax.broadcasted_iota(jnp.int32, (n, n), 0)
    col = lax.broadcasted_iota(jnp.int32, (n, n), 1)
    xs = _diag_block_inverses_minus_eye(mats, lane_bcast)
    size = 2 * SUBLANES
    while size <= CHUNK:
        half = size // 2
        sel = ((row // size) == (col // size)) & ((row % size) >= half) & ((col % size) < half)
        als = [jnp.where(sel, a, 0.0) for a in mats]
        ps = [al + _bdot(x, al) for x, al in zip(xs, als)]
        xs = [x + p + _bdot(p, x) for x, p in zip(xs, ps)]
        between_levels()
        size *= 2
    return xs


def _chunk_operators(ld, strict, incl, lane_bcast, side_work=()):
    side_work = list(side_work)

    def breathe():
        if side_work:
            side_work.pop(0)()

    cat0 = lambda xs: jnp.concatenate(xs, axis=0)
    la = [_stack_heads(x) for x in ld("la")]
    lr = [_stack_heads(x) for x in ld("lr")]
    n2 = 2 * CHUNK
    amat = [_dot_nt(cat0([a, b]).astype(BF16), cat0([c, d]).astype(BF16))
            for a, b, c, d in zip(la, lr, ld("rb"), ld("rk"))]

    def block_diag_pair(m):
        swapped = pltpu.roll(m, CHUNK, axis=1)
        first = cat0([m[0:CHUNK], swapped[CHUNK:n2]])
        second = cat0([swapped[0:CHUNK], m[CHUNK:n2]])
        return first, second

    breathe()
    top = [block_diag_pair(m[0:n2]) for m in amat]
    a_ab = [jnp.where(strict, t[0], 0.0) for t in top]
    xs = _unit_lower_inverse_minus_eye(a_ab, lane_bcast, breathe)
    vst = [_stack_heads_bf16(x) for x in ld("v")]
    av = [_dot(jnp.where(strict, t[1], 0.0).astype(BF16), v) for t, v in zip(top, vst)]
    breathe()
    uv0 = [y + _bdot(x, y) for x, y in zip(xs, av)]
    tla = [y + _bdot(x, y) for x, y in zip(xs, la)]
    breathe()
    bottom = [block_diag_pair(m[n2:2 * n2]) for m in amat]
    a_rb = [jnp.where(incl, t[0], 0.0) for t in bottom]
    a_rk = [jnp.where(incl, t[1], 0.0) for t in bottom]
    gop = [y + _bdot(a, t) for y, a, t in zip(lr, a_rb, tla)]
    uv = [cat0([u.astype(BF16), v]) for u, v in zip(uv0, vst)]
    y0 = [_dot(jnp.concatenate([a, b], axis=1).astype(BF16), w) for a, b, w in zip(a_rb, a_rk, uv)]
    breathe()
    bt = [_stack_heads_bf16(x) for x in ld("bt")]
    kt = [_stack_heads_bf16(x) for x in ld("kt")]
    pop = [_dot_tn(t.astype(BF16), b) for t, b in zip(tla, bt)]
    qop = [_dot_tn(w, cat0([b, k])) for w, b, k in zip(uv, bt, kt)]
    while side_work:
        breathe()
    return gop, y0, pop, qop


def _prompt_kernel(xh_ref, xt_ref, pt_ref, ng_ref, win_ref, wo_f32_ref, wg_f32_ref, wp_f32_ref, fg_ref, *refs,
                   tc, n_t, n_tiles):
    prm_refs = refs[:len(_PRM_NAMES)]
    headones_ref, tri_ref, bcast_ref, st_ref, vecs_ref = refs[len(_PRM_NAMES):len(_PRM_NAMES) + 5]
    smp_refs = refs[len(_PRM_NAMES) + 5:len(_PRM_NAMES) + 10]
    (y_ref, conv_out_ref, shift_out_ref, wkv_out_ref, st1_ref, y_smp_ref) = refs[len(_PRM_NAMES) + 10:len(_PRM_NAMES) + 16]
    (z_s, ycat_s, ubuf, zsbuf, s_ref, la_s, lr_s, rb_s, rk_s, bt_s, kt_s, v_s, gam_s, o_s,
     g_sc, y0_sc, p_sc, q_sc, bonus_p, gr_p, ot_s, wo_ref, wg_ref, wp_ref) = refs[len(_PRM_NAMES) + 16:]

    s = pl.program_id(0)
    head_t = lax.rem(s, jnp.int32(n_t))
    tail_t = lax.rem(s + (n_t - 1), jnp.int32(n_t))
    n2 = 2 * CHUNK
    n_chunks = tc // CHUNK
    tail = {}

    def load_parameters():
        tail["prm"] = _load_prm(prm_refs)
        tail["headones"] = headones_ref[...]

    @pl.when(s == 0)
    def _():
        for ref in (g_sc, y0_sc, p_sc, q_sc, gam_s, bonus_p, gr_p, ycat_s, s_ref):
            ref[...] = jnp.zeros(ref.shape, ref.dtype)
        wo_ref[...] = wo_f32_ref[...].astype(BF16)
        wg_ref[...] = wg_f32_ref[...].astype(BF16)
        wp_ref[...] = wp_f32_ref[...].astype(BF16)

    @pl.when(head_t == 0)
    def _():
        ubuf[0:CARRY_ROWS, :] = jnp.zeros((CARRY_ROWS, CONV_CH), F32)
        zsbuf[0:CARRY_ROWS, :] = jnp.zeros((CARRY_ROWS, SHIFT_W), F32)

    @pl.when(tail_t == 0)
    def _():
        s_ref[...] = jnp.zeros(s_ref.shape, F32)

    def tail_state_pass(c):
        def run():
            for j in range(N_PAIRS):
                idx = c * N_PAIRS + j
                lanes = slice(j * LANES, (j + 1) * LANES)
                s_old = s_ref[j]
                s_bf = s_old.astype(BF16)
                yst = _dot_nt(g_sc[idx], s_bf) + y0_sc[idx]
                o_s[c * CHUNK:(c + 1) * CHUNK, lanes] = yst[0:CHUNK] + yst[CHUNK:n2]
                gam = gam_s[c * CHUNK:c * CHUNK + 1, lanes]
                s_ref[j] = s_old * gam + _dot(s_bf, p_sc[idx]) + q_sc[idx]
        return run

    def tail_post():
        ycat_s[:, CONV_CH:2 * CONV_CH] = _rwkv_post(o_s[...], bonus_p[...], gr_p[...], tail["prm"], tail["headones"])

    def tail_out_proj():
        tail["h1"] = xt_ref[...] + _dot(ycat_s[...].astype(BF16), wo_ref[...])

    def tail_gate():
        tail["gate"] = _sigmoid(_dot(tail["h1"].astype(BF16), wg_ref[...]))
        tail["pe"] = _dot(pt_ref[...].astype(BF16), wp_ref[...])

    def tail_norm():
        h2 = tail["h1"] + tail["gate"] * tail["pe"]
        ms = jnp.mean(h2 * h2, axis=-1, keepdims=True)
        y_ref[...] = h2 * lax.rsqrt(ms + RMS_EPS) * fg_ref[...]

    def head(passes, rest):
        prm, headones = tail["prm"], tail["headones"]
        x = xh_ref[...]
        xn = (x * lax.rsqrt(jnp.mean(x * x, axis=-1, keepdims=True) + RMS_EPS) * ng_ref[...]).astype(BF16)

        def project(c0):
            def run():
                z_s[:, c0:c0 + CONV_CH] = _dot(xn, win_ref[:, c0:c0 + CONV_CH])
            return run

        blocks = [project(c0) for c0 in (OFF_GR, 0, CONV_CH, 2 * CONV_CH, 3 * CONV_CH)]
        side_queue = []
        late_blocks, blocks = blocks[-1:], blocks[:-1]
        while passes or blocks:
            side_queue += passes[:1] + blocks[:1]
            passes, blocks = passes[1:], blocks[1:]
        side_queue += rest

        def project_next(count=2):
            for _ in range(count):
                if side_queue:
                    side_queue.pop(0)()

        z_s[:, OFF_ZS:OFF_ZS + SHIFT_W] = _dot(xn, win_ref[:, OFF_ZS:OFF_ZS + SHIFT_W])

        blocks_per_head = HEAD_DIM // STATE_ROWS
        v_row0 = lax.rem(s, jnp.int32(blocks_per_head)) * STATE_ROWS
        _sample_state_rows(st_ref, vecs_ref, st1_ref, ot_s, v_row0, s * STATE_ROWS)

        zs = z_s[:, OFF_ZS:OFF_ZS + SHIFT_W]
        zsbuf[CARRY_ROWS:CARRY_ROWS + tc, :] = zs
        zprev = zsbuf[CARRY_ROWS - 1:CARRY_ROWS - 1 + tc, :]
        zm = zs + (zprev - zs) * prm["mu"]
        zsbuf[CARRY_ROWS - 1:CARRY_ROWS, :] = zsbuf[CARRY_ROWS + tc - 1:CARRY_ROWS + tc, :]

        project_next()
        r = zm[:, 0:RW]
        k = zm[:, RW:2 * RW]
        v = zm[:, 2 * RW:3 * RW]
        wa = zm[:, 3 * RW:3 * RW + 2 * LORA]
        lw, a, kk, kmod, bonus = _rwkv_tokens(r, k, v, wa, prm, headones, project_next)

        g = _const_dot(tri_ref[...], lw, CUM_PARTS)
        gc = jnp.concatenate(
            [jnp.broadcast_to(g[(c + 1) * CHUNK - 1:(c + 1) * CHUNK, :], (CHUNK, RW)) for c in range(tc // CHUNK)],
            axis=0)
        project_next()
        eng = jnp.exp(-g)
        etail = jnp.exp(gc - g)
        b = kk * a
        la_s[...] = -kk * jnp.exp(g - lw)
        lr_s[...] = r * jnp.exp(g)
        project_next()
        rb_s[...] = b * eng
        rk_s[...] = kmod * eng
        bt_s[...] = b * etail
        kt_s[...] = kmod * etail
        v_s[...] = v
        while side_queue:
            project_next()
        gam_s[...] = jnp.exp(gc)

        row = lax.broadcasted_iota(jnp.int32, (n2, n2), 0)
        col = lax.broadcasted_iota(jnp.int32, (n2, n2), 1)
        same = (row // CHUNK) == (col // CHUNK)
        strict = same & (row > col)
        incl = same & (row >= col)

        cw = prm["conv_w"]

        def conv_rows(r0):
            def run():
                rows = slice(r0, r0 + CHUNK)
                u = z_s[rows, CONV_CH:2 * CONV_CH] * z_s[rows, 2 * CONV_CH:3 * CONV_CH]
                ubuf[CARRY_ROWS + r0:CARRY_ROWS + r0 + CHUNK, :] = u
                um1 = ubuf[CARRY_ROWS - 1 + r0:CARRY_ROWS - 1 + r0 + CHUNK, :]
                um2 = ubuf[CARRY_ROWS - 2 + r0:CARRY_ROWS - 2 + r0 + CHUNK, :]
                conv = cw[0:1, :] * um2 + cw[1:2, :] * um1 + cw[2:3, :] * u
                ycat_s[rows, 0:CONV_CH] = z_s[rows, 0:CONV_CH] * conv * _silu(z_s[rows, 3 * CONV_CH:4 * CONV_CH])
            return run

        srcs = {"la": la_s, "lr": lr_s, "rb": rb_s, "rk": rk_s, "bt": bt_s, "kt": kt_s, "v": v_s}
        members = [(c, j) for c in range(n_chunks) for j in range(N_PAIRS)]

        def ld(name):
            ref = srcs[name]
            return [ref[c * CHUNK:(c + 1) * CHUNK, j * LANES:(j + 1) * LANES] for c, j in members]

        lane_bcast = [bcast_ref[j] for j in range(SUBLANES - 1)]
        gop, y0, pop, qop = _chunk_operators(ld, strict, incl, lane_bcast,
                                             late_blocks + [conv_rows(c * CHUNK) for c in range(n_chunks)])
        for (c, j), g_, y_, p_, q_ in zip(members, gop, y0, pop, qop):
            idx = c * N_PAIRS + j
            g_sc[idx] = g_.astype(BF16)
            y0_sc[idx] = y_
            p_sc[idx] = p_.astype(BF16)
            q_sc[idx] = q_

        ubuf[CARRY_ROWS - 2:CARRY_ROWS, :] = ubuf[CARRY_ROWS + tc - 2:CARRY_ROWS + tc, :]

        bonus_p[...] = bonus
        gr_p[...] = z_s[:, OFF_GR:OFF_GR + RW]

    tail_passes = [tail_state_pass(c) for c in range(n_chunks)]
    tail_rest = [tail_post, tail_out_proj, tail_gate, tail_norm]

    @pl.when(s < n_tiles)
    def _():
        load_parameters()
        head(tail_passes, tail_rest)

    @pl.when(s == n_tiles)
    def _():
        load_parameters()
        for piece in tail_passes + tail_rest:
            piece()

    @pl.when((head_t == n_t - 1) & (s < n_tiles))
    def _():
        seq = lax.div(s, jnp.int32(n_t))
        shift_out_ref[pl.ds(seq, 1), :] = zsbuf[CARRY_ROWS - 1:CARRY_ROWS, :]
        conv_out_ref[0] = ubuf[CARRY_ROWS - 2:CARRY_ROWS, :]

    @pl.when((tail_t == n_t - 1) & (s > 0))
    def _():
        for h in range(N_HEADS):
            j, i = divmod(h, 2)
            blk = s_ref[j]
            wkv_out_ref[0, h] = blk[i * HEAD_DIM:(i + 1) * HEAD_DIM, i * HEAD_DIM:(i + 1) * HEAD_DIM]

    @pl.when(s == n_tiles)
    def _():
        bonus_ref, gr_ref, ya_ref, xs_ref, ps_ref = smp_refs
        y_r = _rwkv_post(ot_s[...].T, bonus_ref[...], gr_ref[...], _load_prm(prm_refs), headones_ref[...])
        ycat = jnp.concatenate([ya_ref[...], y_r], axis=1).astype(BF16)
        h1 = xs_ref[:, 0, :] + _dot(ycat, wo_ref[...])
        gate = _sigmoid(_dot(h1.astype(BF16), wg_ref[...]))
        pe = _dot(ps_ref[:, 0, :].astype(BF16), wp_ref[...])
        h2 = h1 + gate * pe
        ms = jnp.mean(h2 * h2, axis=-1, keepdims=True)
        y_smp_ref[:, 0, :] = h2 * lax.rsqrt(ms + RMS_EPS) * fg_ref[...]


def _prompt_layer(x2d, p2d, w, bsz, t_len, tc, state_hvkb, step_vecs, sample_parts):
    n_t = t_len // tc
    n_seq = state_hvkb.shape[-1]
    once3 = lambda width: pl.BlockSpec((n_seq, 1, width), lambda s: (0, 0, 0), pipeline_mode=pl.Buffered(1))
    blocks_per_head = HEAD_DIM // STATE_ROWS
    assert bsz * n_t == N_HEADS * blocks_per_head, "one state block per prompt tile"
    state_blk = lambda s: jnp.minimum(s, bsz * n_t - 1)
    tok = np.arange(tc)
    blockones = (tok[:, None] // CHUNK) == (tok[None, :] // CHUNK)
    tri = jnp.asarray(blockones & (tok[:, None] >= tok[None, :]), BF16)
    n_tiles = bsz * n_t
    n2 = 2 * CHUNK
    idx = np.arange(n2)
    same_blk = (idx[:, None] // SUBLANES) == (idx[None, :] // SUBLANES)
    lane_bcast = jnp.asarray(
        np.stack([same_blk & (idx[:, None] % SUBLANES == j) for j in range(SUBLANES - 1)]), BF16)
    kern = functools.partial(_prompt_kernel, tc=tc, n_t=n_t, n_tiles=n_tiles)
    big = lambda: pltpu.VMEM((tc, RW), F32)
    n_blk = (tc // CHUNK) * N_PAIRS
    op = lambda dt: pltpu.VMEM((n_blk, LANES, LANES), dt)
    head = lambda s: jnp.minimum(s, n_tiles - 1)
    tail = lambda s: jnp.maximum(s - 1, 0)
    head_tile = lambda width: pl.BlockSpec((tc, width), lambda s: (head(s), 0))
    tail_tile = lambda width: pl.BlockSpec((tc, width), lambda s: (tail(s), 0))
    const = lambda shape: pl.BlockSpec(shape, lambda s: (0, 0), pipeline_mode=pl.Buffered(1))
    return pl.pallas_call(
        kern,
        grid=(n_tiles + 1,),
        in_specs=[head_tile(D_MODEL), tail_tile(D_MODEL), tail_tile(D_PLE), const((1, D_MODEL)),
                  const((D_MODEL, IN_W)), const((D_MODEL, D_MODEL)), const((D_MODEL, D_MODEL)),
                  const((D_PLE, D_MODEL)), const((1, D_MODEL))]
        + _prm_specs()
        + [const((HEADSUM_W, HEADSUM_W)), const((tc, tc)),
           pl.BlockSpec((SUBLANES - 1, n2, n2), lambda s: (0, 0, 0), pipeline_mode=pl.Buffered(1)),
           pl.BlockSpec((1, STATE_ROWS, HEAD_DIM, n_seq),
                        lambda s: (state_blk(s) // blocks_per_head, state_blk(s) % blocks_per_head, 0, 0)),
           pl.BlockSpec((len(_STEP_VECS), 1, HEAD_DIM, n_seq),
                        lambda s: (0, state_blk(s) // blocks_per_head, 0, 0)),
           const((n_seq, RW)), const((n_seq, RW)), const((n_seq, CONV_CH)), once3(D_MODEL), once3(D_PLE)],
        out_specs=[
            tail_tile(D_MODEL),
            pl.BlockSpec((1, 2, CONV_CH), lambda s: (head(s) // n_t, 0, 0)),
            pl.BlockSpec((bsz, SHIFT_W), lambda s: (0, 0)),
            pl.BlockSpec((1, N_HEADS, HEAD_DIM, HEAD_DIM), lambda s: (tail(s) // n_t, 0, 0, 0)),
            pl.BlockSpec((1, STATE_ROWS, HEAD_DIM, n_seq),
                         lambda s: (state_blk(s) // blocks_per_head, state_blk(s) % blocks_per_head, 0, 0)),
            pl.BlockSpec((n_seq, 1, D_MODEL), lambda s: (0, 0, 0)),
        ],
        out_shape=[
            jax.ShapeDtypeStruct((bsz * t_len, D_MODEL), F32),
            jax.ShapeDtypeStruct((bsz, 2, CONV_CH), F32),
            jax.ShapeDtypeStruct((bsz, SHIFT_W), F32),
            jax.ShapeDtypeStruct((bsz, N_HEADS, HEAD_DIM, HEAD_DIM), F32),
            jax.ShapeDtypeStruct(state_hvkb.shape, F32),
            jax.ShapeDtypeStruct((n_seq, 1, D_MODEL), F32),
        ],
        scratch_shapes=[
            pltpu.VMEM((tc, IN_W), F32),
            pltpu.VMEM((tc, D_MODEL), F32),
            pltpu.VMEM((CARRY_ROWS + tc, CONV_CH), F32),
            pltpu.VMEM((CARRY_ROWS + tc, SHIFT_W), F32),
            pltpu.VMEM((N_PAIRS, LANES, LANES), F32),
            big(), big(), big(), big(), big(), big(), big(), big(), big(),
            op(BF16), op(F32), op(BF16), op(F32),
            big(), big(),
            pltpu.VMEM((RW, n_seq), F32),
            pltpu.VMEM((D_MODEL, D_MODEL), BF16), pltpu.VMEM((D_MODEL, D_MODEL), BF16),
            pltpu.VMEM((D_PLE, D_MODEL), BF16),
        ],
        compiler_params=pltpu.CompilerParams(
            dimension_semantics=("arbitrary",), vmem_limit_bytes=VMEM_LIMIT),
        name="prompt_layer",
    )(x2d, x2d, p2d, w["norm_g"].reshape(1, D_MODEL), w["w_in"], w["w_out_f32"], w["w_pg_f32"], w["w_pp_f32"],
      w["final_g"].reshape(1, D_MODEL), *w["prm"], w["headones"], tri, lane_bcast, state_hvkb, step_vecs,
      *sample_parts)


def _sample_front_kernel(x_ref, ng_ref, win_ref, cb_ref, sb_ref, *refs):
    prm_refs = refs[:len(_PRM_NAMES)]
    headones_ref = refs[len(_PRM_NAMES)]
    (ya_ref, conv_out_ref, shift_out_ref, gr_ref, bonus_ref, vecs_ref,
     wbf_ref, cw_rows_ref, rk_row_ref, xn_s, z_s) = refs[len(_PRM_NAMES) + 1:]
    j = pl.program_id(0)

    @pl.when(j == 0)
    def _():
        x = x_ref[:, 0, :]
        xn_s[...] = (x * lax.rsqrt(jnp.mean(x * x, axis=-1, keepdims=True) + RMS_EPS) * ng_ref[...]).astype(BF16)

    w_blk = win_ref[...].astype(BF16)
    wbf_ref[...] = w_blk
    z_s[j] = _dot(xn_s[...], w_blk)

    @pl.when(j == pl.num_programs(0) - 1)
    def _():
        _sample_front_tokens(z_s, cb_ref, sb_ref, prm_refs, headones_ref, ya_ref, conv_out_ref, shift_out_ref,
                             gr_ref, bonus_ref, vecs_ref, cw_rows_ref, rk_row_ref)


def _sample_front_tokens(z_s, cb_ref, sb_ref, prm_refs, headones_ref, ya_ref, conv_out_ref, shift_out_ref,
                         gr_ref, bonus_ref, vecs_ref, cw_rows_ref, rk_row_ref):
    prm = _load_prm(prm_refs)
    prm["conv_w"] = jnp.concatenate([prm["conv_w"][t] for t in range(3)], axis=0)
    prm["r_k"] = jnp.concatenate([prm["r_k"][h:h + 1, :] for h in range(N_HEADS)], axis=1)
    cw_rows_ref[...] = prm["conv_w"]
    rk_row_ref[...] = prm["r_k"]
    headones = headones_ref[...]
    z = jnp.concatenate([z_s[i] for i in range(z_s.shape[0])], axis=1)

    u = z[:, CONV_CH:2 * CONV_CH] * z[:, 2 * CONV_CH:3 * CONV_CH]
    cb0 = cb_ref[:, 0, :]
    cb1 = cb_ref[:, 1, :]
    cw = prm["conv_w"]
    conv = cw[0:1, :] * cb0 + cw[1:2, :] * cb1 + cw[2:3, :] * u
    ya_ref[...] = z[:, 0:CONV_CH] * conv * _silu(z[:, 3 * CONV_CH:4 * CONV_CH])
    conv_out_ref[:, 0, :] = cb1
    conv_out_ref[:, 1, :] = u

    zs = z[:, OFF_ZS:OFF_ZS + SHIFT_W]
    shift_out_ref[...] = zs
    gr_ref[...] = z[:, OFF_GR:OFF_GR + RW]
    zm = zs + (sb_ref[...] - zs) * prm["mu"]
    r = zm[:, 0:RW]
    k = zm[:, RW:2 * RW]
    v = zm[:, 2 * RW:3 * RW]
    wa = zm[:, 3 * RW:3 * RW + 2 * LORA]
    lw, a, kk, kmod, bonus = _rwkv_tokens(r, k, v, wa, prm, headones)
    bonus_ref[...] = bonus
    for i, vec in enumerate((-kk, jnp.exp(lw), kk * a, kmod, v, r)):
        vecs_ref[i] = vec.T


_STEP_VECS = ("nkk", "decay", "b", "kmod", "v", "r")
STATE_ROWS = 8


def _sample_state_rows(st_ref, vecs_ref, s1_ref, y_ref, v_row0, y_row0):
    nkk, w, b, km = (vecs_ref[i, 0] for i in range(4))
    r = vecs_ref[5, 0]
    for vi in range(STATE_ROWS):
        s0 = st_ref[0, vi]
        sa = jnp.sum(s0 * nkk, axis=0, keepdims=True)
        v_row = vecs_ref[4, 0, pl.ds(v_row0 + vi, 1), :]
        s1 = s0 * w + sa * b + v_row * km
        s1_ref[0, vi] = s1
        y_ref[pl.ds(y_row0 + vi, 1), :] = jnp.sum(s1 * r, axis=0, keepdims=True)


def _full(shape):
    nd = len(shape)
    return pl.BlockSpec(shape, lambda *_: (0,) * nd)


def _sample_front(x3d, state_conv, state_shift, w):
    n = x3d.shape[0]
    row = jax.ShapeDtypeStruct((n, RW), F32)
    wblk = IN_W // W_IN_BLOCKS
    raw_shapes = {"conv_w": (3, 1, CONV_CH), "r_k": (N_HEADS, HEAD_DIM)}
    prm_specs = [_full(raw_shapes[name]) if name in raw_shapes else spec
                 for name, spec in zip(_PRM_NAMES, _prm_specs())]
    outs = pl.pallas_call(
        _sample_front_kernel,
        grid=(W_IN_BLOCKS,),
        in_specs=[_full((n, 1, D_MODEL)), _full((1, D_MODEL)), pl.BlockSpec((D_MODEL, wblk), lambda j: (0, j)),
                  _full((n, 2, CONV_CH)), _full((n, SHIFT_W))]
        + prm_specs + [_full((HEADSUM_W, HEADSUM_W))],
        out_specs=[_full((n, CONV_CH)), _full((n, 2, CONV_CH)), _full((n, SHIFT_W)),
                   _full((n, RW)), _full((n, RW)), _full((len(_STEP_VECS), RW, n)),
                   pl.BlockSpec((D_MODEL, wblk), lambda j: (0, j)),
                   _full((3, CONV_CH)), _full((1, RW))],
        out_shape=[jax.ShapeDtypeStruct((n, CONV_CH), F32), jax.ShapeDtypeStruct((n, 2, CONV_CH), F32),
                   jax.ShapeDtypeStruct((n, SHIFT_W), F32), row, row,
                   jax.ShapeDtypeStruct((len(_STEP_VECS), RW, n), F32),
                   jax.ShapeDtypeStruct((D_MODEL, IN_W), BF16),
                   jax.ShapeDtypeStruct((3, CONV_CH), F32), jax.ShapeDtypeStruct((1, RW), F32)],
        scratch_shapes=[pltpu.VMEM((n, D_MODEL), BF16), pltpu.VMEM((W_IN_BLOCKS, n, wblk), F32)],
        compiler_params=pltpu.CompilerParams(dimension_semantics=("arbitrary",), vmem_limit_bytes=VMEM_LIMIT),
        name="sample_front",
    )(x3d, w["norm_g"].reshape(1, D_MODEL), w["w_in_f32"], state_conv, state_shift, *w["prm_raw"], w["headones"])
    ya, conv_new, shift_new, g_r, bonus, vecs, w_in, cw_rows, rk_row = outs
    prm = tuple({"conv_w": cw_rows, "r_k": rk_row}.get(name, arr) for name, arr in zip(_PRM_NAMES, w["prm_raw"]))
    derived = {"w_in": w_in, "prm": prm}
    return (ya, g_r, bonus), conv_new, shift_new, vecs.reshape(len(_STEP_VECS), N_HEADS, HEAD_DIM, n), derived


def _layer(x_prompt, p_prompt, x_sample, p_sample, state_conv, state_shift, state_wkv, w, tc=256):
    bsz, t_len, _ = x_prompt.shape
    (ya, g_r, bonus), conv_s, shift_s, vecs, derived = _sample_front(x_sample, state_conv, state_shift, w)
    w = {**w, **derived}
    s_hvkb = jnp.transpose(state_wkv, (1, 2, 3, 0))
    y_p, conv_p, shift_p, wkv_p, s1, y_s = _prompt_layer(
        x_prompt.reshape(bsz * t_len, D_MODEL), p_prompt.reshape(bsz * t_len, D_PLE), w, bsz, t_len, tc,
        s_hvkb, vecs, (bonus, g_r, ya, x_sample, p_sample))
    prompt_out = (y_p.reshape(bsz, t_len, D_MODEL), conv_p, shift_p, wkv_p)
    sample_out = (y_s, conv_s, shift_s, jnp.transpose(s1, (3, 0, 1, 2)))
    return prompt_out, sample_out


def kernel(x_prompt, x_sample, p_prompt, p_sample, state_conv, state_shift, state_wkv, norm_g, w_in, conv_w, mu_shift, w0, w_up, a0, a_up, k_k, k_a, r_k, ln_w, ln_b, w_out, w_pg, w_pp, final_g):
    depth = norm_g.shape[0]
    assert depth == 1
    i = 0
    head_id = np.arange(HEADSUM_W) // HEAD_DIM
    w = {
        "norm_g": norm_g[i],
        "w_in_f32": w_in[i],
        "w_out_f32": w_out[i],
        "w_pg_f32": w_pg[i],
        "w_pp_f32": w_pp[i],
        "final_g": final_g,
        "prm_raw": _prm_arrays(jnp.transpose(conv_w[i:i + 1], (1, 0, 2)), mu_shift[i], w0[i], w_up[i], a0[i],
                               a_up[i], k_k[i], k_a[i], r_k[i], ln_w[i], ln_b[i]),
        "headones": jnp.asarray(head_id[:, None] == head_id[None, :], BF16),
    }
    (yp, cp, sp, wp), (ys, cs, ss, ws) = _layer(x_prompt, p_prompt[i], x_sample, p_sample[i], state_conv[i],
                                                state_shift[i], state_wkv[i], w)
    return (yp, ys, cp[None], sp[None], wp[None], cs[None], ss[None], ws[None])
```

```python
import functools
import itertools
import math

import jax
import jax.numpy as jnp
import numpy as np
from jax import lax
from jax.experimental import pallas as pl
from jax.experimental.pallas import tpu as pltpu

F32 = jnp.float32
BF16 = jnp.bfloat16

D_MODEL = 1024
CONV_CH = 512
RW = 512
HEAD_DIM = 64
N_HEADS = 8
N_PAIRS = N_HEADS // 2
LORA = 64
D_PLE = 256
SHIFT_W = 3 * RW + 2 * LORA
IN_W = 4 * CONV_CH + SHIFT_W + RW
OFF_ZS = 4 * CONV_CH
OFF_GR = OFF_ZS + SHIFT_W
RMS_EPS = 1e-6
GN_EPS = 64e-5
DECAY_SCALE = math.exp(-0.5)

LANES = 128
SUBLANES = 8
MXU_DIM = 256
HEADSUM_W = MXU_DIM
SUM_PARTS = 1
CUM_PARTS = 1
LORA_PASSES = 1
CHUNK = 64
W_IN_BLOCKS = 3
CARRY_ROWS = 8
VMEM_LIMIT = 56 * 1024 * 1024


def _dot(a, b):
    return jnp.dot(a, b, preferred_element_type=F32)


def _dot_nt(a, b):
    return lax.dot_general(a, b, (((1,), (1,)), ((), ())), preferred_element_type=F32)


def _dot_tn(a, b):
    return lax.dot_general(a, b, (((0,), (0,)), ((), ())), preferred_element_type=F32)


def _split(x, parts):
    out = []
    rem = x
    for i in range(parts):
        t = rem.astype(BF16)
        out.append(t)
        if i + 1 < parts:
            rem = rem - t.astype(F32)
    return out


def _const_dot(c_bf16, x, parts):
    acc = None
    for t in _split(x, parts):
        d = _dot(c_bf16, t)
        acc = d if acc is None else acc + d
    return acc


def _x_dot_const(x, c_bf16, parts):
    acc = None
    for t in _split(x, parts):
        d = _dot(t, c_bf16)
        acc = d if acc is None else acc + d
    return acc


def _head_sum(x, headones):
    w = headones.shape[0]
    halves = [_x_dot_const(x[:, i * w:(i + 1) * w], headones, SUM_PARTS) for i in range(x.shape[1] // w)]
    return jnp.concatenate(halves, axis=1)


def _mm(a, b, passes):
    if passes == 1:
        return _dot(a.astype(BF16), b.astype(BF16))
    ah, al = _split(a, 2)
    bh, bl = _split(b, 2)
    return _dot(ah, bh) + _dot(al, bh) + _dot(ah, bl)


def _sigmoid(x):
    return 1.0 / (1.0 + jnp.exp(-x))


def _silu(x):
    return x * _sigmoid(x)


def _rwkv_tokens(r, k, v, wa, prm, headones, between_steps=lambda: None):
    n = r.shape[0]
    lane = lax.broadcasted_iota(jnp.int32, (n, LANES), 1)
    th = jnp.where(lane < LORA, jnp.tanh(wa), wa)
    lora = _mm(th, prm["wlora"], LORA_PASSES)
    lw = -DECAY_SCALE * _sigmoid(prm["w0"] + lora[:, :RW])
    a = _sigmoid(prm["a0"] + lora[:, RW:])
    between_steps()
    kk = k * prm["k_k"]
    ss = _head_sum(kk * kk, headones)
    kk = kk / jnp.maximum(jnp.sqrt(ss), 1e-12)
    between_steps()
    kmod = k * (1.0 + (a - 1.0) * prm["k_a"])
    bonus = _head_sum(r * kmod * prm["r_k"], headones) * v
    between_steps()
    return lw, a, kk, kmod, bonus


def _rwkv_post(o, bonus, g_r, prm, headones):
    mean = _head_sum(o, headones) * (1.0 / HEAD_DIM)
    d = o - mean
    var = _head_sum(d * d, headones) * (1.0 / HEAD_DIM)
    on = d * lax.rsqrt(var + GN_EPS)
    on = on * prm["ln_w"] + prm["ln_b"]
    return (on + bonus) * _silu(g_r)


_PRM_NAMES = ("conv_w", "mu", "w0", "a0", "w_up", "a_up", "k_k", "k_a", "r_k", "ln_w", "ln_b")


def _prm_arrays(conv_w, mu_shift, w0, w_up, a0, a_up, k_k, k_a, r_k, ln_w, ln_b):
    row = lambda x: x.reshape(1, -1)
    return (conv_w, row(mu_shift), row(w0), row(a0), w_up, a_up, row(k_k), row(k_a),
            r_k, row(ln_w), row(ln_b))


def _prm_specs():
    zero = (lambda *_: (0, 0))
    shapes = ((3, CONV_CH), (1, SHIFT_W), (1, RW), (1, RW), (LORA, RW), (LORA, RW),
              (1, RW), (1, RW), (1, RW), (1, RW), (1, RW))
    return [pl.BlockSpec(s, zero) for s in shapes]


def _load_prm(refs):
    prm = {n: r[...] for n, r in zip(_PRM_NAMES, refs)}
    zero = jnp.zeros((LORA, RW), F32)
    prm["wlora"] = jnp.concatenate([jnp.concatenate([prm["w_up"], zero], axis=1),
                                    jnp.concatenate([zero, prm["a_up"]], axis=1)], axis=0)
    return prm


def _stack_heads(x):
    lane = lax.broadcasted_iota(jnp.int32, x.shape, 1)
    lo = jnp.where(lane < HEAD_DIM, x, 0.0)
    hi = jnp.where(lane >= HEAD_DIM, x, 0.0)
    return jnp.concatenate([lo, hi], axis=0)


def _stack_heads_bf16(x):
    lane = lax.broadcasted_iota(jnp.int32, x.shape, 1)
    lo = jnp.where(lane < HEAD_DIM, 1.0, 0.0).astype(BF16)
    xb = x.astype(BF16)
    return jnp.concatenate([xb * lo, xb * (1.0 - lo)], axis=0)


def _bdot(a, b):
    return _dot(a.astype(BF16), b.astype(BF16))


def _diag_block_inverses_minus_eye(mats, lane_bcast):
    n = mats[0].shape[0]
    nb = n // SUBLANES
    row = lax.broadcasted_iota(jnp.int32, (n, n), 0)
    col = lax.broadcasted_iota(jnp.int32, (n, n), 1)
    in_diag = ((row // SUBLANES) == (col // SUBLANES)) & (row > col)
    packed = []
    for a in mats:
        d = jnp.where(in_diag, a, 0.0)
        acc = d[0:SUBLANES]
        for m in range(1, nb):
            acc = acc + d[m * SUBLANES:(m + 1) * SUBLANES]
        packed.append(acc)
    d_all = jnp.concatenate(packed, axis=0).astype(BF16)
    sub = lax.broadcasted_iota(jnp.int32, (SUBLANES, n), 0)
    lane = lax.broadcasted_iota(jnp.int32, (SUBLANES, n), 1)
    eye8 = jnp.where(sub == lane % SUBLANES, 1.0, 0.0)
    ts = [eye8 for _ in mats]
    for j in range(SUBLANES - 1):
        spread = _dot(d_all, lane_bcast[j])
        ts = [t + spread[i * SUBLANES:(i + 1) * SUBLANES] * jnp.broadcast_to(t[j:j + 1, :], (SUBLANES, n))
              for i, t in enumerate(ts)]
    blk = lax.broadcasted_iota(jnp.int32, (SUBLANES, n), 1) // SUBLANES
    out = []
    for t in ts:
        tm1 = t - eye8
        out.append(jnp.concatenate([jnp.where(blk == m, tm1, 0.0) for m in range(nb)], axis=0))
    return out


def _unit_lower_inverse_minus_eye(mats, lane_bcast, between_levels=lambda: None):
    n = mats[0].shape[0]
    row = lax.broadcasted_iota(jnp.int32, (n, n), 0)
    col = lax.broadcasted_iota(jnp.int32, (n, n), 1)
    xs = _diag_block_inverses_minus_eye(mats, lane_bcast)
    size = 2 * SUBLANES
    while size <= CHUNK:
        half = size // 2
        sel = ((row // size) == (col // size)) & ((row % size) >= half) & ((col % size) < half)
        als = [jnp.where(sel, a, 0.0) for a in mats]
        ps = [al + _bdot(x, al) for x, al in zip(xs, als)]
        xs = [x + p + _bdot(p, x) for x, p in zip(xs, ps)]
        between_levels()
        size *= 2
    return xs


def _chunk_operators(ld, strict, incl, lane_bcast, side_work=()):
    side_work = list(side_work)

    def breathe():
        if side_work:
            side_work.pop(0)()

    cat0 = lambda xs: jnp.concatenate(xs, axis=0)
    la = [_stack_heads(x) for x in ld("la")]
    lr = [_stack_heads(x) for x in ld("lr")]
    n2 = 2 * CHUNK
    amat = [_dot_nt(cat0([a, b]).astype(BF16), cat0([c, d]).astype(BF16))
            for a, b, c, d in zip(la, lr, ld("rb"), ld("rk"))]

    def block_diag_pair(m):
        swapped = pltpu.roll(m, CHUNK, axis=1)
        first = cat0([m[0:CHUNK], swapped[CHUNK:n2]])
        second = cat0([swapped[0:CHUNK], m[CHUNK:n2]])
        return first, second

    breathe()
    top = [block_diag_pair(m[0:n2]) for m in amat]
    a_ab = [jnp.where(strict, t[0], 0.0) for t in top]
    xs = _unit_lower_inverse_minus_eye(a_ab, lane_bcast, breathe)
    vst = [_stack_heads_bf16(x) for x in ld("v")]
    av = [_dot(jnp.where(strict, t[1], 0.0).astype(BF16), v) for t, v in zip(top, vst)]
    breathe()
    uv0 = [y + _bdot(x, y) for x, y in zip(xs, av)]
    tla = [y + _bdot(x, y) for x, y in zip(xs, la)]
    breathe()
    bottom = [block_diag_pair(m[n2:2 * n2]) for m in amat]
    a_rb = [jnp.where(incl, t[0], 0.0) for t in bottom]
    a_rk = [jnp.where(incl, t[1], 0.0) for t in bottom]
    gop = [y + _bdot(a, t) for y, a, t in zip(lr, a_rb, tla)]
    uv = [cat0([u.astype(BF16), v]) for u, v in zip(uv0, vst)]
    y0 = [_dot(jnp.concatenate([a, b], axis=1).astype(BF16), w) for a, b, w in zip(a_rb, a_rk, uv)]
    breathe()
    bt = [_stack_heads_bf16(x) for x in ld("bt")]
    kt = [_stack_heads_bf16(x) for x in ld("kt")]
    pop = [_dot_tn(t.astype(BF16), b) for t, b in zip(tla, bt)]
    qop = [_dot_tn(w, cat0([b, k])) for w, b, k in zip(uv, bt, kt)]
    while side_work:
        breathe()
    return gop, y0, pop, qop


def _prompt_kernel(xh_ref, xt_ref, pt_ref, ng_ref, win_ref, wo_f32_ref, wg_f32_ref, wp_f32_ref, fg_ref, *refs,
                   tc, n_t, n_tiles):
    prm_refs = refs[:len(_PRM_NAMES)]
    headones_ref, tri_ref, bcast_ref, st_ref, vecs_ref = refs[len(_PRM_NAMES):len(_PRM_NAMES) + 5]
    smp_refs = refs[len(_PRM_NAMES) + 5:len(_PRM_NAMES) + 10]
    (y_ref, conv_out_ref, shift_out_ref, wkv_out_ref, st1_ref, y_smp_ref) = refs[len(_PRM_NAMES) + 10:len(_PRM_NAMES) + 16]
    (z_s, ycat_s, ubuf, zsbuf, s_ref, la_s, lr_s, rb_s, rk_s, bt_s, kt_s, v_s, gam_s, o_s,
     g_sc, y0_sc, p_sc, q_sc, bonus_p, gr_p, ot_s, wo_ref, wg_ref, wp_ref) = refs[len(_PRM_NAMES) + 16:]

    s = pl.program_id(0)
    head_t = lax.rem(s, jnp.int32(n_t))
    tail_t = lax.rem(s + (n_t - 1), jnp.int32(n_t))
    n2 = 2 * CHUNK
    n_chunks = tc // CHUNK
    tail = {}

    def load_parameters():
        tail["prm"] = _load_prm(prm_refs)
        tail["headones"] = headones_ref[...]

    @pl.when(s == 0)
    def _():
        for ref in (g_sc, y0_sc, p_sc, q_sc, gam_s, bonus_p, gr_p, ycat_s, s_ref):
            ref[...] = jnp.zeros(ref.shape, ref.dtype)
        wo_ref[...] = wo_f32_ref[...].astype(BF16)
        wg_ref[...] = wg_f32_ref[...].astype(BF16)
        wp_ref[...] = wp_f32_ref[...].astype(BF16)

    @pl.when(head_t == 0)
    def _():
        ubuf[0:CARRY_ROWS, :] = jnp.zeros((CARRY_ROWS, CONV_CH), F32)
        zsbuf[0:CARRY_ROWS, :] = jnp.zeros((CARRY_ROWS, SHIFT_W), F32)

    @pl.when(tail_t == 0)
    def _():
        s_ref[...] = jnp.zeros(s_ref.shape, F32)

    def tail_state_pass(c):
        def run():
            for j in range(N_PAIRS):
                idx = c * N_PAIRS + j
                lanes = slice(j * LANES, (j + 1) * LANES)
                s_old = s_ref[j]
                s_bf = s_old.astype(BF16)
                yst = _dot_nt(g_sc[idx], s_bf) + y0_sc[idx]
                o_s[c * CHUNK:(c + 1) * CHUNK, lanes] = yst[0:CHUNK] + yst[CHUNK:n2]
                gam = gam_s[c * CHUNK:c * CHUNK + 1, lanes]
                s_ref[j] = s_old * gam + _dot(s_bf, p_sc[idx]) + q_sc[idx]
        return run

    def tail_post():
        ycat_s[:, CONV_CH:2 * CONV_CH] = _rwkv_post(o_s[...], bonus_p[...], gr_p[...], tail["prm"], tail["headones"])

    def tail_out_proj():
        tail["h1"] = xt_ref[...] + _dot(ycat_s[...].astype(BF16), wo_ref[...])

    def tail_gate():
        tail["gate"] = _sigmoid(_dot(tail["h1"].astype(BF16), wg_ref[...]))
        tail["pe"] = _dot(pt_ref[...].astype(BF16), wp_ref[...])

    def tail_norm():
        h2 = tail["h1"] + tail["gate"] * tail["pe"]
        ms = jnp.mean(h2 * h2, axis=-1, keepdims=True)
        y_ref[...] = h2 * lax.rsqrt(ms + RMS_EPS) * fg_ref[...]

    def head(passes, rest):
        prm, headones = tail["prm"], tail["headones"]
        x = xh_ref[...]
        xn = (x * lax.rsqrt(jnp.mean(x * x, axis=-1, keepdims=True) + RMS_EPS) * ng_ref[...]).astype(BF16)

        def project(c0):
            def run():
                z_s[:, c0:c0 + CONV_CH] = _dot(xn, win_ref[:, c0:c0 + CONV_CH])
            return run

        blocks = [project(c0) for c0 in (OFF_GR, 0, CONV_CH, 2 * CONV_CH, 3 * CONV_CH)]
        side_queue = []
        late_blocks, blocks = blocks[-1:], blocks[:-1]
        while passes or blocks:
            side_queue += passes[:1] + blocks[:1]
            passes, blocks = passes[1:], blocks[1:]
        side_queue += rest

        def project_next(count=2):
            for _ in range(count):
                if side_queue:
                    side_queue.pop(0)()

        z_s[:, OFF_ZS:OFF_ZS + SHIFT_W] = _dot(xn, win_ref[:, OFF_ZS:OFF_ZS + SHIFT_W])

        blocks_per_head = HEAD_DIM // STATE_ROWS
        v_row0 = lax.rem(s, jnp.int32(blocks_per_head)) * STATE_ROWS
        _sample_state_rows(st_ref, vecs_ref, st1_ref, ot_s, v_row0, s * STATE_ROWS)

        zs = z_s[:, OFF_ZS:OFF_ZS + SHIFT_W]
        zsbuf[CARRY_ROWS:CARRY_ROWS + tc, :] = zs
        zprev = zsbuf[CARRY_ROWS - 1:CARRY_ROWS - 1 + tc, :]
        zm = zs + (zprev - zs) * prm["mu"]
        zsbuf[CARRY_ROWS - 1:CARRY_ROWS, :] = zsbuf[CARRY_ROWS + tc - 1:CARRY_ROWS + tc, :]

        project_next()
        r = zm[:, 0:RW]
        k = zm[:, RW:2 * RW]
        v = zm[:, 2 * RW:3 * RW]
        wa = zm[:, 3 * RW:3 * RW + 2 * LORA]
        lw, a, kk, kmod, bonus = _rwkv_tokens(r, k, v, wa, prm, headones, project_next)

        g = _const_dot(tri_ref[...], lw, CUM_PARTS)
        gc = jnp.concatenate(
            [jnp.broadcast_to(g[(c + 1) * CHUNK - 1:(c + 1) * CHUNK, :], (CHUNK, RW)) for c in range(tc // CHUNK)],
            axis=0)
        project_next()
        eng = jnp.exp(-g)
        etail = jnp.exp(gc - g)
        b = kk * a
        la_s[...] = -kk * jnp.exp(g - lw)
        lr_s[...] = r * jnp.exp(g)
        project_next()
        rb_s[...] = b * eng
        rk_s[...] = kmod * eng
        bt_s[...] = b * etail
        kt_s[...] = kmod * etail
        v_s[...] = v
        while side_queue:
            project_next()
        gam_s[...] = jnp.exp(gc)

        row = lax.broadcasted_iota(jnp.int32, (n2, n2), 0)
        col = lax.broadcasted_iota(jnp.int32, (n2, n2), 1)
        same = (row // CHUNK) == (col // CHUNK)
        strict = same & (row > col)
        incl = same & (row >= col)

        cw = prm["conv_w"]

        def conv_rows(r0):
            def run():
                rows = slice(r0, r0 + CHUNK)
                u = z_s[rows, CONV_CH:2 * CONV_CH] * z_s[rows, 2 * CONV_CH:3 * CONV_CH]
                ubuf[CARRY_ROWS + r0:CARRY_ROWS + r0 + CHUNK, :] = u
                um1 = ubuf[CARRY_ROWS - 1 + r0:CARRY_ROWS - 1 + r0 + CHUNK, :]
                um2 = ubuf[CARRY_ROWS - 2 + r0:CARRY_ROWS - 2 + r0 + CHUNK, :]
                conv = cw[0:1, :] * um2 + cw[1:2, :] * um1 + cw[2:3, :] * u
                ycat_s[rows, 0:CONV_CH] = z_s[rows, 0:CONV_CH] * conv * _silu(z_s[rows, 3 * CONV_CH:4 * CONV_CH])
            return run

        srcs = {"la": la_s, "lr": lr_s, "rb": rb_s, "rk": rk_s, "bt": bt_s, "kt": kt_s, "v": v_s}
        members = [(c, j) for c in range(n_chunks) for j in range(N_PAIRS)]

        def ld(name):
            ref = srcs[name]
            return [ref[c * CHUNK:(c + 1) * CHUNK, j * LANES:(j + 1) * LANES] for c, j in members]

        lane_bcast = [bcast_ref[j] for j in range(SUBLANES - 1)]
        gop, y0, pop, qop = _chunk_operators(ld, strict, incl, lane_bcast,
                                             late_blocks + [conv_rows(c * CHUNK) for c in range(n_chunks)])
        for (c, j), g_, y_, p_, q_ in zip(members, gop, y0, pop, qop):
            idx = c * N_PAIRS + j
            g_sc[idx] = g_.astype(BF16)
            y0_sc[idx] = y_
            p_sc[idx] = p_.astype(BF16)
            q_sc[idx] = q_

        ubuf[CARRY_ROWS - 2:CARRY_ROWS, :] = ubuf[CARRY_ROWS + tc - 2:CARRY_ROWS + tc, :]

        bonus_p[...] = bonus
        gr_p[...] = z_s[:, OFF_GR:OFF_GR + RW]

    tail_passes = [tail_state_pass(c) for c in range(n_chunks)]
    tail_rest = [tail_post, tail_out_proj, tail_gate, tail_norm]

    @pl.when(s < n_tiles)
    def _():
        load_parameters()
        head(tail_passes, tail_rest)

    def sample_post():
        bonus_ref, gr_ref, ya_ref, _, _ = smp_refs
        y_r = _rwkv_post(ot_s[...].T, bonus_ref[...], gr_ref[...], tail["prm"], tail["headones"])
        tail["smp_ycat"] = jnp.concatenate([ya_ref[...], y_r], axis=1).astype(BF16)

    def sample_out_proj():
        tail["smp_h1"] = smp_refs[3][:, 0, :] + _dot(tail["smp_ycat"], wo_ref[...])

    def sample_gate():
        tail["smp_gate"] = _sigmoid(_dot(tail["smp_h1"].astype(BF16), wg_ref[...]))
        tail["smp_pe"] = _dot(smp_refs[4][:, 0, :].astype(BF16), wp_ref[...])

    def sample_norm():
        h2 = tail["smp_h1"] + tail["smp_gate"] * tail["smp_pe"]
        ms = jnp.mean(h2 * h2, axis=-1, keepdims=True)
        y_smp_ref[:, 0, :] = h2 * lax.rsqrt(ms + RMS_EPS) * fg_ref[...]

    @pl.when(s == n_tiles)
    def _():
        load_parameters()
        sample_pieces = [sample_post, sample_out_proj, sample_gate, sample_norm]
        for pair in itertools.zip_longest(tail_passes, sample_pieces):
            for piece in filter(None, pair):
                piece()
        for piece in tail_rest:
            piece()

    @pl.when((head_t == n_t - 1) & (s < n_tiles))
    def _():
        seq = lax.div(s, jnp.int32(n_t))
        shift_out_ref[pl.ds(seq, 1), :] = zsbuf[CARRY_ROWS - 1:CARRY_ROWS, :]
        conv_out_ref[0] = ubuf[CARRY_ROWS - 2:CARRY_ROWS, :]

    @pl.when((tail_t == n_t - 1) & (s > 0))
    def _():
        for h in range(N_HEADS):
            j, i = divmod(h, 2)
            blk = s_ref[j]
            wkv_out_ref[0, h] = blk[i * HEAD_DIM:(i + 1) * HEAD_DIM, i * HEAD_DIM:(i + 1) * HEAD_DIM]


def _prompt_layer(x2d, p2d, w, bsz, t_len, tc, state_hvkb, step_vecs, sample_parts):
    n_t = t_len // tc
    n_seq = state_hvkb.shape[-1]
    once3 = lambda width: pl.BlockSpec((n_seq, 1, width), lambda s: (0, 0, 0), pipeline_mode=pl.Buffered(1))
    blocks_per_head = HEAD_DIM // STATE_ROWS
    assert bsz * n_t == N_HEADS * blocks_per_head, "one state block per prompt tile"
    state_blk = lambda s: jnp.minimum(s, bsz * n_t - 1)
    tok = np.arange(tc)
    blockones = (tok[:, None] // CHUNK) == (tok[None, :] // CHUNK)
    tri = jnp.asarray(blockones & (tok[:, None] >= tok[None, :]), BF16)
    n_tiles = bsz * n_t
    n2 = 2 * CHUNK
    idx = np.arange(n2)
    same_blk = (idx[:, None] // SUBLANES) == (idx[None, :] // SUBLANES)
    lane_bcast = jnp.asarray(
        np.stack([same_blk & (idx[:, None] % SUBLANES == j) for j in range(SUBLANES - 1)]), BF16)
    kern = functools.partial(_prompt_kernel, tc=tc, n_t=n_t, n_tiles=n_tiles)
    big = lambda: pltpu.VMEM((tc, RW), F32)
    n_blk = (tc // CHUNK) * N_PAIRS
    op = lambda dt: pltpu.VMEM((n_blk, LANES, LANES), dt)
    head = lambda s: jnp.minimum(s, n_tiles - 1)
    tail = lambda s: jnp.maximum(s - 1, 0)
    head_tile = lambda width: pl.BlockSpec((tc, width), lambda s: (head(s), 0))
    tail_tile = lambda width: pl.BlockSpec((tc, width), lambda s: (tail(s), 0))
    const = lambda shape: pl.BlockSpec(shape, lambda s: (0, 0), pipeline_mode=pl.Buffered(1))
    return pl.pallas_call(
        kern,
        grid=(n_tiles + 1,),
        in_specs=[head_tile(D_MODEL), tail_tile(D_MODEL), tail_tile(D_PLE), const((1, D_MODEL)),
                  const((D_MODEL, IN_W)), const((D_MODEL, D_MODEL)), const((D_MODEL, D_MODEL)),
                  const((D_PLE, D_MODEL)), const((1, D_MODEL))]
        + _prm_specs()
        + [const((HEADSUM_W, HEADSUM_W)), const((tc, tc)),
           pl.BlockSpec((SUBLANES - 1, n2, n2), lambda s: (0, 0, 0), pipeline_mode=pl.Buffered(1)),
           pl.BlockSpec((1, STATE_ROWS, HEAD_DIM, n_seq),
                        lambda s: (state_blk(s) // blocks_per_head, state_blk(s) % blocks_per_head, 0, 0)),
           pl.BlockSpec((len(_STEP_VECS), 1, HEAD_DIM, n_seq),
                        lambda s: (0, state_blk(s) // blocks_per_head, 0, 0)),
           const((n_seq, RW)), const((n_seq, RW)), const((n_seq, CONV_CH)), once3(D_MODEL), once3(D_PLE)],
        out_specs=[
            tail_tile(D_MODEL),
            pl.BlockSpec((1, 2, CONV_CH), lambda s: (head(s) // n_t, 0, 0)),
            pl.BlockSpec((bsz, SHIFT_W), lambda s: (0, 0)),
            pl.BlockSpec((1, N_HEADS, HEAD_DIM, HEAD_DIM), lambda s: (tail(s) // n_t, 0, 0, 0)),
            pl.BlockSpec((1, STATE_ROWS, HEAD_DIM, n_seq),
                         lambda s: (state_blk(s) // blocks_per_head, state_blk(s) % blocks_per_head, 0, 0)),
            pl.BlockSpec((n_seq, 1, D_MODEL), lambda s: (0, 0, 0)),
        ],
        out_shape=[
            jax.ShapeDtypeStruct((bsz * t_len, D_MODEL), F32),
            jax.ShapeDtypeStruct((bsz, 2, CONV_CH), F32),
            jax.ShapeDtypeStruct((bsz, SHIFT_W), F32),
            jax.ShapeDtypeStruct((bsz, N_HEADS, HEAD_DIM, HEAD_DIM), F32),
            jax.ShapeDtypeStruct(state_hvkb.shape, F32),
            jax.ShapeDtypeStruct((n_seq, 1, D_MODEL), F32),
        ],
        scratch_shapes=[
            pltpu.VMEM((tc, IN_W), F32),
            pltpu.VMEM((tc, D_MODEL), F32),
            pltpu.VMEM((CARRY_ROWS + tc, CONV_CH), F32),
            pltpu.VMEM((CARRY_ROWS + tc, SHIFT_W), F32),
            pltpu.VMEM((N_PAIRS, LANES, LANES), F32),
            big(), big(), big(), big(), big(), big(), big(), big(), big(),
            op(BF16), op(F32), op(BF16), op(F32),
            big(), big(),
            pltpu.VMEM((RW, n_seq), F32),
            pltpu.VMEM((D_MODEL, D_MODEL), BF16), pltpu.VMEM((D_MODEL, D_MODEL), BF16),
            pltpu.VMEM((D_PLE, D_MODEL), BF16),
        ],
        compiler_params=pltpu.CompilerParams(
            dimension_semantics=("arbitrary",), vmem_limit_bytes=VMEM_LIMIT),
        name="prompt_layer",
    )(x2d, x2d, p2d, w["norm_g"].reshape(1, D_MODEL), w["w_in"], w["w_out_f32"], w["w_pg_f32"], w["w_pp_f32"],
      w["final_g"].reshape(1, D_MODEL), *w["prm"], w["headones"], tri, lane_bcast, state_hvkb, step_vecs,
      *sample_parts)


def _sample_front_kernel(x_ref, ng_ref, win_ref, cb_ref, sb_ref, *refs):
    prm_refs = refs[:len(_PRM_NAMES)]
    headones_ref = refs[len(_PRM_NAMES)]
    (ya_ref, conv_out_ref, shift_out_ref, gr_ref, bonus_ref, vecs_ref,
     wbf_ref, cw_rows_ref, rk_row_ref, xn_s, z_s) = refs[len(_PRM_NAMES) + 1:]
    j = pl.program_id(0)

    @pl.when(j == 0)
    def _():
        x = x_ref[:, 0, :]
        xn_s[...] = (x * lax.rsqrt(jnp.mean(x * x, axis=-1, keepdims=True) + RMS_EPS) * ng_ref[...]).astype(BF16)

    w_blk = win_ref[...].astype(BF16)
    wbf_ref[...] = w_blk
    z_s[j] = _dot(xn_s[...], w_blk)

    @pl.when(j == pl.num_programs(0) - 1)
    def _():
        _sample_front_tokens(z_s, cb_ref, sb_ref, prm_refs, headones_ref, ya_ref, conv_out_ref, shift_out_ref,
                             gr_ref, bonus_ref, vecs_ref, cw_rows_ref, rk_row_ref)


def _sample_front_tokens(z_s, cb_ref, sb_ref, prm_refs, headones_ref, ya_ref, conv_out_ref, shift_out_ref,
                         gr_ref, bonus_ref, vecs_ref, cw_rows_ref, rk_row_ref):
    prm = _load_prm(prm_refs)
    prm["conv_w"] = jnp.concatenate([prm["conv_w"][t] for t in range(3)], axis=0)
    prm["r_k"] = jnp.concatenate([prm["r_k"][h:h + 1, :] for h in range(N_HEADS)], axis=1)
    cw_rows_ref[...] = prm["conv_w"]
    rk_row_ref[...] = prm["r_k"]
    headones = headones_ref[...]
    z = jnp.concatenate([z_s[i] for i in range(z_s.shape[0])], axis=1)

    u = z[:, CONV_CH:2 * CONV_CH] * z[:, 2 * CONV_CH:3 * CONV_CH]
    cb0 = cb_ref[:, 0, :]
    cb1 = cb_ref[:, 1, :]
    cw = prm["conv_w"]
    conv = cw[0:1, :] * cb0 + cw[1:2, :] * cb1 + cw[2:3, :] * u
    ya_ref[...] = z[:, 0:CONV_CH] * conv * _silu(z[:, 3 * CONV_CH:4 * CONV_CH])
    conv_out_ref[:, 0, :] = cb1
    conv_out_ref[:, 1, :] = u

    zs = z[:, OFF_ZS:OFF_ZS + SHIFT_W]
    shift_out_ref[...] = zs
    gr_ref[...] = z[:, OFF_GR:OFF_GR + RW]
    zm = zs + (sb_ref[...] - zs) * prm["mu"]
    r = zm[:, 0:RW]
    k = zm[:, RW:2 * RW]
    v = zm[:, 2 * RW:3 * RW]
    wa = zm[:, 3 * RW:3 * RW + 2 * LORA]
    lw, a, kk, kmod, bonus = _rwkv_tokens(r, k, v, wa, prm, headones)
    bonus_ref[...] = bonus
    for i, vec in enumerate((-kk, jnp.exp(lw), kk * a, kmod, v, r)):
        vecs_ref[i] = vec.T


_STEP_VECS = ("nkk", "decay", "b", "kmod", "v", "r")
STATE_ROWS = 8


def _sample_state_rows(st_ref, vecs_ref, s1_ref, y_ref, v_row0, y_row0):
    nkk, w, b, km = (vecs_ref[i, 0] for i in range(4))
    r = vecs_ref[5, 0]
    for vi in range(STATE_ROWS):
        s0 = st_ref[0, vi]
        sa = jnp.sum(s0 * nkk, axis=0, keepdims=True)
        v_row = vecs_ref[4, 0, pl.ds(v_row0 + vi, 1), :]
        s1 = s0 * w + sa * b + v_row * km
        s1_ref[0, vi] = s1
        y_ref[pl.ds(y_row0 + vi, 1), :] = jnp.sum(s1 * r, axis=0, keepdims=True)


def _full(shape):
    nd = len(shape)
    return pl.BlockSpec(shape, lambda *_: (0,) * nd)


def _sample_front(x3d, state_conv, state_shift, w):
    n = x3d.shape[0]
    row = jax.ShapeDtypeStruct((n, RW), F32)
    wblk = IN_W // W_IN_BLOCKS
    raw_shapes = {"conv_w": (3, 1, CONV_CH), "r_k": (N_HEADS, HEAD_DIM)}
    prm_specs = [_full(raw_shapes[name]) if name in raw_shapes else spec
                 for name, spec in zip(_PRM_NAMES, _prm_specs())]
    outs = pl.pallas_call(
        _sample_front_kernel,
        grid=(W_IN_BLOCKS,),
        in_specs=[_full((n, 1, D_MODEL)), _full((1, D_MODEL)), pl.BlockSpec((D_MODEL, wblk), lambda j: (0, j)),
                  _full((n, 2, CONV_CH)), _full((n, SHIFT_W))]
        + prm_specs + [_full((HEADSUM_W, HEADSUM_W))],
        out_specs=[_full((n, CONV_CH)), _full((n, 2, CONV_CH)), _full((n, SHIFT_W)),
                   _full((n, RW)), _full((n, RW)), _full((len(_STEP_VECS), RW, n)),
                   pl.BlockSpec((D_MODEL, wblk), lambda j: (0, j)),
                   _full((3, CONV_CH)), _full((1, RW))],
        out_shape=[jax.ShapeDtypeStruct((n, CONV_CH), F32), jax.ShapeDtypeStruct((n, 2, CONV_CH), F32),
                   jax.ShapeDtypeStruct((n, SHIFT_W), F32), row, row,
                   jax.ShapeDtypeStruct((len(_STEP_VECS), RW, n), F32),
                   jax.ShapeDtypeStruct((D_MODEL, IN_W), BF16),
                   jax.ShapeDtypeStruct((3, CONV_CH), F32), jax.ShapeDtypeStruct((1, RW), F32)],
        scratch_shapes=[pltpu.VMEM((n, D_MODEL), BF16), pltpu.VMEM((W_IN_BLOCKS, n, wblk), F32)],
        compiler_params=pltpu.CompilerParams(dimension_semantics=("arbitrary",), vmem_limit_bytes=VMEM_LIMIT),
        name="sample_front",
    )(x3d, w["norm_g"].reshape(1, D_MODEL), w["w_in_f32"], state_conv, state_shift, *w["prm_raw"], w["headones"])
    ya, conv_new, shift_new, g_r, bonus, vecs, w_in, cw_rows, rk_row = outs
    prm = tuple({"conv_w": cw_rows, "r_k": rk_row}.get(name, arr) for name, arr in zip(_PRM_NAMES, w["prm_raw"]))
    derived = {"w_in": w_in, "prm": prm}
    return (ya, g_r, bonus), conv_new, shift_new, vecs.reshape(len(_STEP_VECS), N_HEADS, HEAD_DIM, n), derived


def _layer(x_prompt, p_prompt, x_sample, p_sample, state_conv, state_shift, state_wkv, w, tc=256):
    bsz, t_len, _ = x_prompt.shape
    (ya, g_r, bonus), conv_s, shift_s, vecs, derived = _sample_front(x_sample, state_conv, state_shift, w)
    w = {**w, **derived}
    s_hvkb = jnp.transpose(state_wkv, (1, 2, 3, 0))
    y_p, conv_p, shift_p, wkv_p, s1, y_s = _prompt_layer(
        x_prompt.reshape(bsz * t_len, D_MODEL), p_prompt.reshape(bsz * t_len, D_PLE), w, bsz, t_len, tc,
        s_hvkb, vecs, (bonus, g_r, ya, x_sample, p_sample))
    prompt_out = (y_p.reshape(bsz, t_len, D_MODEL), conv_p, shift_p, wkv_p)
    sample_out = (y_s, conv_s, shift_s, jnp.transpose(s1, (3, 0, 1, 2)))
    return prompt_out, sample_out


def kernel(x_prompt, x_sample, p_prompt, p_sample, state_conv, state_shift, state_wkv, norm_g, w_in, conv_w, mu_shift, w0, w_up, a0, a_up, k_k, k_a, r_k, ln_w, ln_b, w_out, w_pg, w_pp, final_g):
    depth = norm_g.shape[0]
    assert depth == 1
    i = 0
    head_id = np.arange(HEADSUM_W) // HEAD_DIM
    w = {
        "norm_g": norm_g[i],
        "w_in_f32": w_in[i],
        "w_out_f32": w_out[i],
        "w_pg_f32": w_pg[i],
        "w_pp_f32": w_pp[i],
        "final_g": final_g,
        "prm_raw": _prm_arrays(jnp.transpose(conv_w[i:i + 1], (1, 0, 2)), mu_shift[i], w0[i], w_up[i], a0[i],
                               a_up[i], k_k[i], k_a[i], r_k[i], ln_w[i], ln_b[i]),
        "headones": jnp.asarray(head_id[:, None] == head_id[None, :], BF16),
    }
    (yp, cp, sp, wp), (ys, cs, ss, ws) = _layer(x_prompt, p_prompt[i], x_sample, p_sample[i], state_conv[i],
                                                state_shift[i], state_wkv[i], w)
    return (yp, ys, cp[None], sp[None], wp[None], cs[None], ss[None], ws[None])
```

```python
import functools
import math

import jax
import jax.numpy as jnp
import numpy as np
from jax import lax
from jax.experimental import pallas as pl
from jax.experimental.pallas import tpu as pltpu

F32 = jnp.float32
BF16 = jnp.bfloat16

D_MODEL = 1024
CONV_CH = 512
RW = 512
HEAD_DIM = 64
N_HEADS = 8
N_PAIRS = N_HEADS // 2
LORA = 64
D_PLE = 256
SHIFT_W = 3 * RW + 2 * LORA
IN_W = 4 * CONV_CH + SHIFT_W + RW
OFF_ZS = 4 * CONV_CH
OFF_GR = OFF_ZS + SHIFT_W
RMS_EPS = 1e-6
GN_EPS = 64e-5
DECAY_SCALE = math.exp(-0.5)

LANES = 128
SUBLANES = 8
MXU_DIM = 256
HEADSUM_W = MXU_DIM
SUM_PARTS = 1
CUM_PARTS = 1
LORA_PASSES = 1
CHUNK = 64
W_IN_BLOCKS = 3
CARRY_ROWS = 8
VMEM_LIMIT = 56 * 1024 * 1024


def _dot(a, b):
    return jnp.dot(a, b, preferred_element_type=F32)


def _dot_nt(a, b):
    return lax.dot_general(a, b, (((1,), (1,)), ((), ())), preferred_element_type=F32)


def _dot_tn(a, b):
    return lax.dot_general(a, b, (((0,), (0,)), ((), ())), preferred_element_type=F32)


def _split(x, parts):
    out = []
    rem = x
    for i in range(parts):
        t = rem.astype(BF16)
        out.append(t)
        if i + 1 < parts:
            rem = rem - t.astype(F32)
    return out


def _const_dot(c_bf16, x, parts):
    acc = None
    for t in _split(x, parts):
        d = _dot(c_bf16, t)
        acc = d if acc is None else acc + d
    return acc


def _x_dot_const(x, c_bf16, parts):
    acc = None
    for t in _split(x, parts):
        d = _dot(t, c_bf16)
        acc = d if acc is None else acc + d
    return acc


def _head_sum(x, headones):
    w = headones.shape[0]
    halves = [_x_dot_const(x[:, i * w:(i + 1) * w], headones, SUM_PARTS) for i in range(x.shape[1] // w)]
    return jnp.concatenate(halves, axis=1)


def _mm(a, b, passes):
    if passes == 1:
        return _dot(a.astype(BF16), b.astype(BF16))
    ah, al = _split(a, 2)
    bh, bl = _split(b, 2)
    return _dot(ah, bh) + _dot(al, bh) + _dot(ah, bl)


def _sigmoid(x):
    return 1.0 / (1.0 + jnp.exp(-x))


def _silu(x):
    return x * _sigmoid(x)


def _rwkv_tokens(r, k, v, wa, prm, headones, between_steps=lambda: None):
    n = r.shape[0]
    lane = lax.broadcasted_iota(jnp.int32, (n, LANES), 1)
    th = jnp.where(lane < LORA, jnp.tanh(wa), wa)
    lora = _mm(th, prm["wlora"], LORA_PASSES)
    lw = -DECAY_SCALE * _sigmoid(prm["w0"] + lora[:, :RW])
    a = _sigmoid(prm["a0"] + lora[:, RW:])
    between_steps()
    kk = k * prm["k_k"]
    ss = _head_sum(kk * kk, headones)
    kk = kk / jnp.maximum(jnp.sqrt(ss), 1e-12)
    between_steps()
    kmod = k * (1.0 + (a - 1.0) * prm["k_a"])
    bonus = _head_sum(r * kmod * prm["r_k"], headones) * v
    between_steps()
    return lw, a, kk, kmod, bonus


def _rwkv_post(o, bonus, g_r, prm, headones):
    mean = _head_sum(o, headones) * (1.0 / HEAD_DIM)
    d = o - mean
    var = _head_sum(d * d, headones) * (1.0 / HEAD_DIM)
    on = d * lax.rsqrt(var + GN_EPS)
    on = on * prm["ln_w"] + prm["ln_b"]
    return (on + bonus) * _silu(g_r)


_PRM_NAMES = ("conv_w", "mu", "w0", "a0", "w_up", "a_up", "k_k", "k_a", "r_k", "ln_w", "ln_b")


def _prm_arrays(conv_w, mu_shift, w0, w_up, a0, a_up, k_k, k_a, r_k, ln_w, ln_b):
    row = lambda x: x.reshape(1, -1)
    return (conv_w, row(mu_shift), row(w0), row(a0), w_up, a_up, row(k_k), row(k_a),
            r_k, row(ln_w), row(ln_b))


def _prm_specs():
    zero = (lambda *_: (0, 0))
    shapes = ((3, CONV_CH), (1, SHIFT_W), (1, RW), (1, RW), (LORA, RW), (LORA, RW),
              (1, RW), (1, RW), (1, RW), (1, RW), (1, RW))
    return [pl.BlockSpec(s, zero) for s in shapes]


def _load_prm(refs):
    prm = {n: r[...] for n, r in zip(_PRM_NAMES, refs)}
    zero = jnp.zeros((LORA, RW), F32)
    prm["wlora"] = jnp.concatenate([jnp.concatenate([prm["w_up"], zero], axis=1),
                                    jnp.concatenate([zero, prm["a_up"]], axis=1)], axis=0)
    return prm


def _stack_heads(x):
    lane = lax.broadcasted_iota(jnp.int32, x.shape, 1)
    lo = jnp.where(lane < HEAD_DIM, x, 0.0)
    hi = jnp.where(lane >= HEAD_DIM, x, 0.0)
    return jnp.concatenate([lo, hi], axis=0)


def _stack_heads_bf16(x):
    lane = lax.broadcasted_iota(jnp.int32, x.shape, 1)
    lo = jnp.where(lane < HEAD_DIM, 1.0, 0.0).astype(BF16)
    xb = x.astype(BF16)
    return jnp.concatenate([xb * lo, xb * (1.0 - lo)], axis=0)


def _bdot(a, b):
    return _dot(a.astype(BF16), b.astype(BF16))


def _diag_block_inverses_minus_eye(mats, lane_bcast):
    n = mats[0].shape[0]
    nb = n // SUBLANES
    row = lax.broadcasted_iota(jnp.int32, (n, n), 0)
    col = lax.broadcasted_iota(jnp.int32, (n, n), 1)
    in_diag = ((row // SUBLANES) == (col // SUBLANES)) & (row > col)
    packed = []
    for a in mats:
        d = jnp.where(in_diag, a, 0.0)
        acc = d[0:SUBLANES]
        for m in range(1, nb):
            acc = acc + d[m * SUBLANES:(m + 1) * SUBLANES]
        packed.append(acc)
    d_all = jnp.concatenate(packed, axis=0).astype(BF16)
    sub = lax.broadcasted_iota(jnp.int32, (SUBLANES, n), 0)
    lane = lax.broadcasted_iota(jnp.int32, (SUBLANES, n), 1)
    eye8 = jnp.where(sub == lane % SUBLANES, 1.0, 0.0)
    ts = [eye8 for _ in mats]
    for j in range(SUBLANES - 1):
        spread = _dot(d_all, lane_bcast[j])
        ts = [t + spread[i * SUBLANES:(i + 1) * SUBLANES] * jnp.broadcast_to(t[j:j + 1, :], (SUBLANES, n))
              for i, t in enumerate(ts)]
    blk = lax.broadcasted_iota(jnp.int32, (SUBLANES, n), 1) // SUBLANES
    out = []
    for t in ts:
        tm1 = t - eye8
        out.append(jnp.concatenate([jnp.where(blk == m, tm1, 0.0) for m in range(nb)], axis=0))
    return out


def _unit_lower_inverse_minus_eye(mats, lane_bcast, between_levels=lambda: None):
    n = mats[0].shape[0]
    row = lax.broadcasted_iota(jnp.int32, (n, n), 0)
    col = lax.broadcasted_iota(jnp.int32, (n, n), 1)
    xs = _diag_block_inverses_minus_eye(mats, lane_bcast)
    size = 2 * SUBLANES
    while size <= CHUNK:
        half = size // 2
        sel = ((row // size) == (col // size)) & ((row % size) >= half) & ((col % size) < half)
        als = [jnp.where(sel, a, 0.0) for a in mats]
        ps = [al + _bdot(x, al) for x, al in zip(xs, als)]
        xs = [x + p + _bdot(p, x) for x, p in zip(xs, ps)]
        between_levels()
        size *= 2
    return xs


def _chunk_operators(ld, strict, incl, lane_bcast, side_work=()):
    side_work = list(side_work)

    def breathe():
        if side_work:
            side_work.pop(0)()

    cat0 = lambda xs: jnp.concatenate(xs, axis=0)
    la = [_stack_heads(x) for x in ld("la")]
    lr = [_stack_heads(x) for x in ld("lr")]
    n2 = 2 * CHUNK
    amat = [_dot_nt(cat0([a, b]).astype(BF16), cat0([c, d]).astype(BF16))
            for a, b, c, d in zip(la, lr, ld("rb"), ld("rk"))]

    def block_diag_pair(m):
        swapped = pltpu.roll(m, CHUNK, axis=1)
        first = cat0([m[0:CHUNK], swapped[CHUNK:n2]])
        second = cat0([swapped[0:CHUNK], m[CHUNK:n2]])
        return first, second

    breathe()
    top = [block_diag_pair(m[0:n2]) for m in amat]
    a_ab = [jnp.where(strict, t[0], 0.0) for t in top]
    xs = _unit_lower_inverse_minus_eye(a_ab, lane_bcast, breathe)
    vst = [_stack_heads_bf16(x) for x in ld("v")]
    av = [_dot(jnp.where(strict, t[1], 0.0).astype(BF16), v) for t, v in zip(top, vst)]
    breathe()
    uv0 = [y + _bdot(x, y) for x, y in zip(xs, av)]
    tla = [y + _bdot(x, y) for x, y in zip(xs, la)]
    breathe()
    bottom = [block_diag_pair(m[n2:2 * n2]) for m in amat]
    a_rb = [jnp.where(incl, t[0], 0.0) for t in bottom]
    a_rk = [jnp.where(incl, t[1], 0.0) for t in bottom]
    gop = [y + _bdot(a, t) for y, a, t in zip(lr, a_rb, tla)]
    uv = [cat0([u.astype(BF16), v]) for u, v in zip(uv0, vst)]
    y0 = [_dot(jnp.concatenate([a, b], axis=1).astype(BF16), w) for a, b, w in zip(a_rb, a_rk, uv)]
    breathe()
    bt = [_stack_heads_bf16(x) for x in ld("bt")]
    kt = [_stack_heads_bf16(x) for x in ld("kt")]
    pop = [_dot_tn(t.astype(BF16), b) for t, b in zip(tla, bt)]
    qop = [_dot_tn(w, cat0([b, k])) for w, b, k in zip(uv, bt, kt)]
    while side_work:
        breathe()
    return gop, y0, pop, qop


def _prompt_kernel(xh_ref, pt_ref, ng_ref, win_ref, wo_f32_ref, wg_f32_ref, wp_f32_ref, fg_ref, *refs,
                   tc, n_t, n_tiles):
    prm_refs = refs[:len(_PRM_NAMES)]
    headones_ref, tri_ref, bcast_ref, st_ref, vecs_ref = refs[len(_PRM_NAMES):len(_PRM_NAMES) + 5]
    smp_refs = refs[len(_PRM_NAMES) + 5:len(_PRM_NAMES) + 10]
    (y_ref, conv_out_ref, shift_out_ref, wkv_out_ref, st1_ref, y_smp_ref) = refs[len(_PRM_NAMES) + 10:len(_PRM_NAMES) + 16]
    (z_s, ycat_s, ubuf, zsbuf, s_ref, la_s, lr_s, rb_s, rk_s, bt_s, kt_s, v_s, gam_s, o_s,
     g_sc, y0_sc, p_sc, q_sc, bonus_p, gr_p, ot_s, wo_ref, wg_ref, wp_ref, x_p) = refs[len(_PRM_NAMES) + 16:]

    s = pl.program_id(0)
    head_t = lax.rem(s, jnp.int32(n_t))
    tail_t = lax.rem(s + (n_t - 1), jnp.int32(n_t))
    n2 = 2 * CHUNK
    n_chunks = tc // CHUNK
    tail = {}

    def load_parameters():
        tail["prm"] = _load_prm(prm_refs)
        tail["headones"] = headones_ref[...]

    @pl.when(s == 0)
    def _():
        for ref in (g_sc, y0_sc, p_sc, q_sc, gam_s, bonus_p, gr_p, x_p, ycat_s, s_ref):
            ref[...] = jnp.zeros(ref.shape, ref.dtype)
        wo_ref[...] = wo_f32_ref[...].astype(BF16)
        wg_ref[...] = wg_f32_ref[...].astype(BF16)
        wp_ref[...] = wp_f32_ref[...].astype(BF16)

    @pl.when(head_t == 0)
    def _():
        ubuf[0:CARRY_ROWS, :] = jnp.zeros((CARRY_ROWS, CONV_CH), F32)
        zsbuf[0:CARRY_ROWS, :] = jnp.zeros((CARRY_ROWS, SHIFT_W), F32)

    @pl.when(tail_t == 0)
    def _():
        s_ref[...] = jnp.zeros(s_ref.shape, F32)

    def tail_state_pass(c):
        def run():
            for j in range(N_PAIRS):
                idx = c * N_PAIRS + j
                lanes = slice(j * LANES, (j + 1) * LANES)
                s_old = s_ref[j]
                s_bf = s_old.astype(BF16)
                yst = _dot_nt(g_sc[idx], s_bf) + y0_sc[idx]
                o_s[c * CHUNK:(c + 1) * CHUNK, lanes] = yst[0:CHUNK] + yst[CHUNK:n2]
                gam = gam_s[c * CHUNK:c * CHUNK + 1, lanes]
                s_ref[j] = s_old * gam + _dot(s_bf, p_sc[idx]) + q_sc[idx]
        return run

    def tail_post():
        ycat_s[:, CONV_CH:2 * CONV_CH] = _rwkv_post(o_s[...], bonus_p[...], gr_p[...], tail["prm"], tail["headones"])

    def tail_out_proj():
        tail["h1"] = x_p[...] + _dot(ycat_s[...].astype(BF16), wo_ref[...])

    def tail_gate():
        tail["gate"] = _sigmoid(_dot(tail["h1"].astype(BF16), wg_ref[...]))
        tail["pe"] = _dot(pt_ref[...].astype(BF16), wp_ref[...])

    def tail_norm():
        h2 = tail["h1"] + tail["gate"] * tail["pe"]
        ms = jnp.mean(h2 * h2, axis=-1, keepdims=True)
        y_ref[...] = h2 * lax.rsqrt(ms + RMS_EPS) * fg_ref[...]

    def head(passes, rest):
        prm, headones = tail["prm"], tail["headones"]
        x = xh_ref[...]
        xn = (x * lax.rsqrt(jnp.mean(x * x, axis=-1, keepdims=True) + RMS_EPS) * ng_ref[...]).astype(BF16)

        def project(c0):
            def run():
                z_s[:, c0:c0 + CONV_CH] = _dot(xn, win_ref[:, c0:c0 + CONV_CH])
            return run

        blocks = [project(c0) for c0 in (OFF_GR, 0, CONV_CH, 2 * CONV_CH, 3 * CONV_CH)]
        side_queue = []
        late_blocks, blocks = blocks[-1:], blocks[:-1]
        while passes or blocks:
            side_queue += passes[:1] + blocks[:1]
            passes, blocks = passes[1:], blocks[1:]
        side_queue += rest

        def project_next(count=2):
            for _ in range(count):
                if side_queue:
                    side_queue.pop(0)()

        z_s[:, OFF_ZS:OFF_ZS + SHIFT_W] = _dot(xn, win_ref[:, OFF_ZS:OFF_ZS + SHIFT_W])

        blocks_per_head = HEAD_DIM // STATE_ROWS
        v_row0 = lax.rem(s, jnp.int32(blocks_per_head)) * STATE_ROWS
        _sample_state_rows(st_ref, vecs_ref, st1_ref, ot_s, v_row0, s * STATE_ROWS)

        zs = z_s[:, OFF_ZS:OFF_ZS + SHIFT_W]
        zsbuf[CARRY_ROWS:CARRY_ROWS + tc, :] = zs
        zprev = zsbuf[CARRY_ROWS - 1:CARRY_ROWS - 1 + tc, :]
        zm = zs + (zprev - zs) * prm["mu"]
        zsbuf[CARRY_ROWS - 1:CARRY_ROWS, :] = zsbuf[CARRY_ROWS + tc - 1:CARRY_ROWS + tc, :]

        project_next()
        r = zm[:, 0:RW]
        k = zm[:, RW:2 * RW]
        v = zm[:, 2 * RW:3 * RW]
        wa = zm[:, 3 * RW:3 * RW + 2 * LORA]
        lw, a, kk, kmod, bonus = _rwkv_tokens(r, k, v, wa, prm, headones, project_next)

        g = _const_dot(tri_ref[...], lw, CUM_PARTS)
        gc = jnp.concatenate(
            [jnp.broadcast_to(g[(c + 1) * CHUNK - 1:(c + 1) * CHUNK, :], (CHUNK, RW)) for c in range(tc // CHUNK)],
            axis=0)
        project_next()
        eng = jnp.exp(-g)
        etail = jnp.exp(gc - g)
        b = kk * a
        la_s[...] = -kk * jnp.exp(g - lw)
        lr_s[...] = r * jnp.exp(g)
        project_next()
        rb_s[...] = b * eng
        rk_s[...] = kmod * eng
        bt_s[...] = b * etail
        kt_s[...] = kmod * etail
        v_s[...] = v
        while side_queue:
            project_next()
        gam_s[...] = jnp.exp(gc)

        row = lax.broadcasted_iota(jnp.int32, (n2, n2), 0)
        col = lax.broadcasted_iota(jnp.int32, (n2, n2), 1)
        same = (row // CHUNK) == (col // CHUNK)
        strict = same & (row > col)
        incl = same & (row >= col)

        cw = prm["conv_w"]

        def conv_rows(r0):
            def run():
                rows = slice(r0, r0 + CHUNK)
                u = z_s[rows, CONV_CH:2 * CONV_CH] * z_s[rows, 2 * CONV_CH:3 * CONV_CH]
                ubuf[CARRY_ROWS + r0:CARRY_ROWS + r0 + CHUNK, :] = u
                um1 = ubuf[CARRY_ROWS - 1 + r0:CARRY_ROWS - 1 + r0 + CHUNK, :]
                um2 = ubuf[CARRY_ROWS - 2 + r0:CARRY_ROWS - 2 + r0 + CHUNK, :]
                conv = cw[0:1, :] * um2 + cw[1:2, :] * um1 + cw[2:3, :] * u
                ycat_s[rows, 0:CONV_CH] = z_s[rows, 0:CONV_CH] * conv * _silu(z_s[rows, 3 * CONV_CH:4 * CONV_CH])
            return run

        srcs = {"la": la_s, "lr": lr_s, "rb": rb_s, "rk": rk_s, "bt": bt_s, "kt": kt_s, "v": v_s}
        members = [(c, j) for c in range(n_chunks) for j in range(N_PAIRS)]

        def ld(name):
            ref = srcs[name]
            return [ref[c * CHUNK:(c + 1) * CHUNK, j * LANES:(j + 1) * LANES] for c, j in members]

        lane_bcast = [bcast_ref[j] for j in range(SUBLANES - 1)]
        gop, y0, pop, qop = _chunk_operators(ld, strict, incl, lane_bcast,
                                             late_blocks + [conv_rows(c * CHUNK) for c in range(n_chunks)])
        for (c, j), g_, y_, p_, q_ in zip(members, gop, y0, pop, qop):
            idx = c * N_PAIRS + j
            g_sc[idx] = g_.astype(BF16)
            y0_sc[idx] = y_
            p_sc[idx] = p_.astype(BF16)
            q_sc[idx] = q_

        ubuf[CARRY_ROWS - 2:CARRY_ROWS, :] = ubuf[CARRY_ROWS + tc - 2:CARRY_ROWS + tc, :]

        bonus_p[...] = bonus
        gr_p[...] = z_s[:, OFF_GR:OFF_GR + RW]
        x_p[...] = xh_ref[...]

    tail_passes = [tail_state_pass(c) for c in range(n_chunks)]
    tail_rest = [tail_post, tail_out_proj, tail_gate, tail_norm]

    @pl.when(s < n_tiles)
    def _():
        load_parameters()
        head(tail_passes, tail_rest)

    @pl.when(s == n_tiles)
    def _():
        load_parameters()
        for piece in tail_passes + tail_rest:
            piece()

    @pl.when((head_t == n_t - 1) & (s < n_tiles))
    def _():
        seq = lax.div(s, jnp.int32(n_t))
        shift_out_ref[pl.ds(seq, 1), :] = zsbuf[CARRY_ROWS - 1:CARRY_ROWS, :]
        conv_out_ref[0] = ubuf[CARRY_ROWS - 2:CARRY_ROWS, :]

    @pl.when((tail_t == n_t - 1) & (s > 0))
    def _():
        for h in range(N_HEADS):
            j, i = divmod(h, 2)
            blk = s_ref[j]
            wkv_out_ref[0, h] = blk[i * HEAD_DIM:(i + 1) * HEAD_DIM, i * HEAD_DIM:(i + 1) * HEAD_DIM]

    @pl.when(s == n_tiles)
    def _():
        bonus_ref, gr_ref, ya_ref, xs_ref, ps_ref = smp_refs
        y_r = _rwkv_post(ot_s[...].T, bonus_ref[...], gr_ref[...], _load_prm(prm_refs), headones_ref[...])
        ycat = jnp.concatenate([ya_ref[...], y_r], axis=1).astype(BF16)
        h1 = xs_ref[:, 0, :] + _dot(ycat, wo_ref[...])
        gate = _sigmoid(_dot(h1.astype(BF16), wg_ref[...]))
        pe = _dot(ps_ref[:, 0, :].astype(BF16), wp_ref[...])
        h2 = h1 + gate * pe
        ms = jnp.mean(h2 * h2, axis=-1, keepdims=True)
        y_smp_ref[:, 0, :] = h2 * lax.rsqrt(ms + RMS_EPS) * fg_ref[...]


def _prompt_layer(x2d, p2d, w, bsz, t_len, tc, state_hvkb, step_vecs, sample_parts):
    n_t = t_len // tc
    n_seq = state_hvkb.shape[-1]
    once3 = lambda width: pl.BlockSpec((n_seq, 1, width), lambda s: (0, 0, 0), pipeline_mode=pl.Buffered(1))
    blocks_per_head = HEAD_DIM // STATE_ROWS
    assert bsz * n_t == N_HEADS * blocks_per_head, "one state block per prompt tile"
    state_blk = lambda s: jnp.minimum(s, bsz * n_t - 1)
    tok = np.arange(tc)
    blockones = (tok[:, None] // CHUNK) == (tok[None, :] // CHUNK)
    tri = jnp.asarray(blockones & (tok[:, None] >= tok[None, :]), BF16)
    n_tiles = bsz * n_t
    n2 = 2 * CHUNK
    idx = np.arange(n2)
    same_blk = (idx[:, None] // SUBLANES) == (idx[None, :] // SUBLANES)
    lane_bcast = jnp.asarray(
        np.stack([same_blk & (idx[:, None] % SUBLANES == j) for j in range(SUBLANES - 1)]), BF16)
    kern = functools.partial(_prompt_kernel, tc=tc, n_t=n_t, n_tiles=n_tiles)
    big = lambda: pltpu.VMEM((tc, RW), F32)
    n_blk = (tc // CHUNK) * N_PAIRS
    op = lambda dt: pltpu.VMEM((n_blk, LANES, LANES), dt)
    head = lambda s: jnp.minimum(s, n_tiles - 1)
    tail = lambda s: jnp.maximum(s - 1, 0)
    head_tile = lambda width: pl.BlockSpec((tc, width), lambda s: (head(s), 0))
    tail_tile = lambda width: pl.BlockSpec((tc, width), lambda s: (tail(s), 0))
    const = lambda shape: pl.BlockSpec(shape, lambda s: (0, 0), pipeline_mode=pl.Buffered(1))
    return pl.pallas_call(
        kern,
        grid=(n_tiles + 1,),
        in_specs=[head_tile(D_MODEL), tail_tile(D_PLE), const((1, D_MODEL)),
                  const((D_MODEL, IN_W)), const((D_MODEL, D_MODEL)), const((D_MODEL, D_MODEL)),
                  const((D_PLE, D_MODEL)), const((1, D_MODEL))]
        + _prm_specs()
        + [const((HEADSUM_W, HEADSUM_W)), const((tc, tc)),
           pl.BlockSpec((SUBLANES - 1, n2, n2), lambda s: (0, 0, 0), pipeline_mode=pl.Buffered(1)),
           pl.BlockSpec((1, STATE_ROWS, HEAD_DIM, n_seq),
                        lambda s: (state_blk(s) // blocks_per_head, state_blk(s) % blocks_per_head, 0, 0)),
           pl.BlockSpec((len(_STEP_VECS), 1, HEAD_DIM, n_seq),
                        lambda s: (0, state_blk(s) // blocks_per_head, 0, 0)),
           const((n_seq, RW)), const((n_seq, RW)), const((n_seq, CONV_CH)), once3(D_MODEL), once3(D_PLE)],
        out_specs=[
            tail_tile(D_MODEL),
            pl.BlockSpec((1, 2, CONV_CH), lambda s: (head(s) // n_t, 0, 0)),
            pl.BlockSpec((bsz, SHIFT_W), lambda s: (0, 0)),
            pl.BlockSpec((1, N_HEADS, HEAD_DIM, HEAD_DIM), lambda s: (tail(s) // n_t, 0, 0, 0)),
            pl.BlockSpec((1, STATE_ROWS, HEAD_DIM, n_seq),
                         lambda s: (state_blk(s) // blocks_per_head, state_blk(s) % blocks_per_head, 0, 0)),
            pl.BlockSpec((n_seq, 1, D_MODEL), lambda s: (0, 0, 0)),
        ],
        out_shape=[
            jax.ShapeDtypeStruct((bsz * t_len, D_MODEL), F32),
            jax.ShapeDtypeStruct((bsz, 2, CONV_CH), F32),
            jax.ShapeDtypeStruct((bsz, SHIFT_W), F32),
            jax.ShapeDtypeStruct((bsz, N_HEADS, HEAD_DIM, HEAD_DIM), F32),
            jax.ShapeDtypeStruct(state_hvkb.shape, F32),
            jax.ShapeDtypeStruct((n_seq, 1, D_MODEL), F32),
        ],
        scratch_shapes=[
            pltpu.VMEM((tc, IN_W), F32),
            pltpu.VMEM((tc, D_MODEL), F32),
            pltpu.VMEM((CARRY_ROWS + tc, CONV_CH), F32),
            pltpu.VMEM((CARRY_ROWS + tc, SHIFT_W), F32),
            pltpu.VMEM((N_PAIRS, LANES, LANES), F32),
            big(), big(), big(), big(), big(), big(), big(), big(), big(),
            op(BF16), op(F32), op(BF16), op(F32),
            big(), big(),
            pltpu.VMEM((RW, n_seq), F32),
            pltpu.VMEM((D_MODEL, D_MODEL), BF16), pltpu.VMEM((D_MODEL, D_MODEL), BF16),
            pltpu.VMEM((D_PLE, D_MODEL), BF16),
            pltpu.VMEM((tc, D_MODEL), F32),
        ],
        compiler_params=pltpu.CompilerParams(
            dimension_semantics=("arbitrary",), vmem_limit_bytes=VMEM_LIMIT),
        name="prompt_layer",
    )(x2d, p2d, w["norm_g"].reshape(1, D_MODEL), w["w_in"], w["w_out_f32"], w["w_pg_f32"], w["w_pp_f32"],
      w["final_g"].reshape(1, D_MODEL), *w["prm"], w["headones"], tri, lane_bcast, state_hvkb, step_vecs,
      *sample_parts)


def _sample_front_kernel(x_ref, ng_ref, win_ref, cb_ref, sb_ref, *refs):
    prm_refs = refs[:len(_PRM_NAMES)]
    headones_ref = refs[len(_PRM_NAMES)]
    (ya_ref, conv_out_ref, shift_out_ref, gr_ref, bonus_ref, vecs_ref,
     wbf_ref, cw_rows_ref, rk_row_ref, xn_s, z_s) = refs[len(_PRM_NAMES) + 1:]
    j = pl.program_id(0)

    @pl.when(j == 0)
    def _():
        x = x_ref[:, 0, :]
        xn_s[...] = (x * lax.rsqrt(jnp.mean(x * x, axis=-1, keepdims=True) + RMS_EPS) * ng_ref[...]).astype(BF16)

    w_blk = win_ref[...].astype(BF16)
    wbf_ref[...] = w_blk
    z_s[j] = _dot(xn_s[...], w_blk)

    @pl.when(j == pl.num_programs(0) - 1)
    def _():
        _sample_front_tokens(z_s, cb_ref, sb_ref, prm_refs, headones_ref, ya_ref, conv_out_ref, shift_out_ref,
                             gr_ref, bonus_ref, vecs_ref, cw_rows_ref, rk_row_ref)


def _sample_front_tokens(z_s, cb_ref, sb_ref, prm_refs, headones_ref, ya_ref, conv_out_ref, shift_out_ref,
                         gr_ref, bonus_ref, vecs_ref, cw_rows_ref, rk_row_ref):
    prm = _load_prm(prm_refs)
    prm["conv_w"] = jnp.concatenate([prm["conv_w"][t] for t in range(3)], axis=0)
    prm["r_k"] = jnp.concatenate([prm["r_k"][h:h + 1, :] for h in range(N_HEADS)], axis=1)
    cw_rows_ref[...] = prm["conv_w"]
    rk_row_ref[...] = prm["r_k"]
    headones = headones_ref[...]
    z = jnp.concatenate([z_s[i] for i in range(z_s.shape[0])], axis=1)

    u = z[:, CONV_CH:2 * CONV_CH] * z[:, 2 * CONV_CH:3 * CONV_CH]
    cb0 = cb_ref[:, 0, :]
    cb1 = cb_ref[:, 1, :]
    cw = prm["conv_w"]
    conv = cw[0:1, :] * cb0 + cw[1:2, :] * cb1 + cw[2:3, :] * u
    ya_ref[...] = z[:, 0:CONV_CH] * conv * _silu(z[:, 3 * CONV_CH:4 * CONV_CH])
    conv_out_ref[:, 0, :] = cb1
    conv_out_ref[:, 1, :] = u

    zs = z[:, OFF_ZS:OFF_ZS + SHIFT_W]
    shift_out_ref[...] = zs
    gr_ref[...] = z[:, OFF_GR:OFF_GR + RW]
    zm = zs + (sb_ref[...] - zs) * prm["mu"]
    r = zm[:, 0:RW]
    k = zm[:, RW:2 * RW]
    v = zm[:, 2 * RW:3 * RW]
    wa = zm[:, 3 * RW:3 * RW + 2 * LORA]
    lw, a, kk, kmod, bonus = _rwkv_tokens(r, k, v, wa, prm, headones)
    bonus_ref[...] = bonus
    for i, vec in enumerate((-kk, jnp.exp(lw), kk * a, kmod, v, r)):
        vecs_ref[i] = vec.T


_STEP_VECS = ("nkk", "decay", "b", "kmod", "v", "r")
STATE_ROWS = 8


def _sample_state_rows(st_ref, vecs_ref, s1_ref, y_ref, v_row0, y_row0):
    nkk, w, b, km = (vecs_ref[i, 0] for i in range(4))
    r = vecs_ref[5, 0]
    for vi in range(STATE_ROWS):
        s0 = st_ref[0, vi]
        sa = jnp.sum(s0 * nkk, axis=0, keepdims=True)
        v_row = vecs_ref[4, 0, pl.ds(v_row0 + vi, 1), :]
        s1 = s0 * w + sa * b + v_row * km
        s1_ref[0, vi] = s1
        y_ref[pl.ds(y_row0 + vi, 1), :] = jnp.sum(s1 * r, axis=0, keepdims=True)


def _full(shape):
    nd = len(shape)
    return pl.BlockSpec(shape, lambda *_: (0,) * nd)


def _sample_front(x3d, state_conv, state_shift, w):
    n = x3d.shape[0]
    row = jax.ShapeDtypeStruct((n, RW), F32)
    wblk = IN_W // W_IN_BLOCKS
    raw_shapes = {"conv_w": (3, 1, CONV_CH), "r_k": (N_HEADS, HEAD_DIM)}
    prm_specs = [_full(raw_shapes[name]) if name in raw_shapes else spec
                 for name, spec in zip(_PRM_NAMES, _prm_specs())]
    outs = pl.pallas_call(
        _sample_front_kernel,
        grid=(W_IN_BLOCKS,),
        in_specs=[_full((n, 1, D_MODEL)), _full((1, D_MODEL)), pl.BlockSpec((D_MODEL, wblk), lambda j: (0, j)),
                  _full((n, 2, CONV_CH)), _full((n, SHIFT_W))]
        + prm_specs + [_full((HEADSUM_W, HEADSUM_W))],
        out_specs=[_full((n, CONV_CH)), _full((n, 2, CONV_CH)), _full((n, SHIFT_W)),
                   _full((n, RW)), _full((n, RW)), _full((len(_STEP_VECS), RW, n)),
                   pl.BlockSpec((D_MODEL, wblk), lambda j: (0, j)),
                   _full((3, CONV_CH)), _full((1, RW))],
        out_shape=[jax.ShapeDtypeStruct((n, CONV_CH), F32), jax.ShapeDtypeStruct((n, 2, CONV_CH), F32),
                   jax.ShapeDtypeStruct((n, SHIFT_W), F32), row, row,
                   jax.ShapeDtypeStruct((len(_STEP_VECS), RW, n), F32),
                   jax.ShapeDtypeStruct((D_MODEL, IN_W), BF16),
                   jax.ShapeDtypeStruct((3, CONV_CH), F32), jax.ShapeDtypeStruct((1, RW), F32)],
        scratch_shapes=[pltpu.VMEM((n, D_MODEL), BF16), pltpu.VMEM((W_IN_BLOCKS, n, wblk), F32)],
        compiler_params=pltpu.CompilerParams(dimension_semantics=("arbitrary",), vmem_limit_bytes=VMEM_LIMIT),
        name="sample_front",
    )(x3d, w["norm_g"].reshape(1, D_MODEL), w["w_in_f32"], state_conv, state_shift, *w["prm_raw"], w["headones"])
    ya, conv_new, shift_new, g_r, bonus, vecs, w_in, cw_rows, rk_row = outs
    prm = tuple({"conv_w": cw_rows, "r_k": rk_row}.get(name, arr) for name, arr in zip(_PRM_NAMES, w["prm_raw"]))
    derived = {"w_in": w_in, "prm": prm}
    return (ya, g_r, bonus), conv_new, shift_new, vecs.reshape(len(_STEP_VECS), N_HEADS, HEAD_DIM, n), derived


def _layer(x_prompt, p_prompt, x_sample, p_sample, state_conv, state_shift, state_wkv, w, tc=256):
    bsz, t_len, _ = x_prompt.shape
    (ya, g_r, bonus), conv_s, shift_s, vecs, derived = _sample_front(x_sample, state_conv, state_shift, w)
    w = {**w, **derived}
    s_hvkb = jnp.transpose(state_wkv, (1, 2, 3, 0))
    y_p, conv_p, shift_p, wkv_p, s1, y_s = _prompt_layer(
        x_prompt.reshape(bsz * t_len, D_MODEL), p_prompt.reshape(bsz * t_len, D_PLE), w, bsz, t_len, tc,
        s_hvkb, vecs, (bonus, g_r, ya, x_sample, p_sample))
    prompt_out = (y_p.reshape(bsz, t_len, D_MODEL), conv_p, shift_p, wkv_p)
    sample_out = (y_s, conv_s, shift_s, jnp.transpose(s1, (3, 0, 1, 2)))
    return prompt_out, sample_out


def kernel(x_prompt, x_sample, p_prompt, p_sample, state_conv, state_shift, state_wkv, norm_g, w_in, conv_w, mu_shift, w0, w_up, a0, a_up, k_k, k_a, r_k, ln_w, ln_b, w_out, w_pg, w_pp, final_g):
    depth = norm_g.shape[0]
    assert depth == 1
    i = 0
    head_id = np.arange(HEADSUM_W) // HEAD_DIM
    w = {
        "norm_g": norm_g[i],
        "w_in_f32": w_in[i],
        "w_out_f32": w_out[i],
        "w_pg_f32": w_pg[i],
        "w_pp_f32": w_pp[i],
        "final_g": final_g,
        "prm_raw": _prm_arrays(jnp.transpose(conv_w[i:i + 1], (1, 0, 2)), mu_shift[i], w0[i], w_up[i], a0[i],
                               a_up[i], k_k[i], k_a[i], r_k[i], ln_w[i], ln_b[i]),
        "headones": jnp.asarray(head_id[:, None] == head_id[None, :], BF16),
    }
    (yp, cp, sp, wp), (ys, cs, ss, ws) = _layer(x_prompt, p_prompt[i], x_sample, p_sample[i], state_conv[i],
                                                state_shift[i], state_wkv[i], w)
    return (yp, ys, cp[None], sp[None], wp[None], cs[None], ss[None], ws[None])
```
